```python
import math, functools
import jax, jax.numpy as jnp
from jax import lax
import numpy as np

D_MODEL = 1024
BATCH = 2
SEQ = 8192
DEPTH = 2
DEC_BATCH = 32
DEC_SEQ = 1
PAST_LEN = 8192
PAGE_SIZE = 128

HEAD_DIM = 64
MIX_W = D_MODEL // 4
N_HEADS_MIX = MIX_W // HEAD_DIM
NORM_EPS = 1e-6
NEG_INF = -1e30
MLSTM_CHUNK = 64
CONV_W = 31
CONV_GROUPS = N_HEADS_MIX
NSA_KV = 2
NSA_REP = N_HEADS_MIX // NSA_KV
KV_W = NSA_KV * HEAD_DIM
CMP_LEN = 32
CMP_STRIDE = 16
SEL_BLOCK = 64
SEL_PER = SEL_BLOCK // CMP_STRIDE
N_SELECT = 16
SEL_FORCE = 1e9
WINDOW = 512
Q_BLOCK = 128
ROPE_THETA = 10000.0
ATTN_SCALE = HEAD_DIM ** -0.5
GMLP_CHUNK = 128
GMLP_HEADS = N_HEADS_MIX
PEER_HEADS = 8
PEER_TOPK = 16
N_KEYS = 128
N_EXPERTS = N_KEYS * N_KEYS
PEER_DQ = 256
PEER_BLOCK = 128
IN_SIZES = (MIX_W,) * 4 + (N_HEADS_MIX,) * 2 + (MIX_W,) * 2 + (MIX_W,) + (KV_W,) * 6 + (3 * N_HEADS_MIX,) + (MIX_W,) * 2
IN_W = sum(IN_SIZES)

kernel_name = 'hymba_style_mlstm_conv_nsa_gmlp_peer_step'


def rmsnorm(x, g):
    xf = x.astype(jnp.float32)
    y = xf * lax.rsqrt(jnp.mean(xf * xf, axis=-1, keepdims=True) + NORM_EPS)
    return (y * g.astype(jnp.float32)).astype(x.dtype)


def group_layernorm(x, n_groups):
    xf = x.astype(jnp.float32).reshape(x.shape[:-1] + (n_groups, x.shape[-1] // n_groups))
    mu = jnp.mean(xf, axis=-1, keepdims=True)
    var = jnp.mean(jnp.square(xf - mu), axis=-1, keepdims=True)
    return ((xf - mu) * lax.rsqrt(var + NORM_EPS)).reshape(x.shape)


def rope(x, pos):
    half = HEAD_DIM // 2
    inv_freq = ROPE_THETA ** (-jnp.arange(half, dtype=jnp.float32) / half)
    ang = pos.astype(jnp.float32)[:, None] * inv_freq[None, :]
    cos = jnp.cos(ang)[None, :, None, :]
    sin = jnp.sin(ang)[None, :, None, :]
    xf = x.astype(jnp.float32)
    x1, x2 = xf[..., :half], xf[..., half:]
    return jnp.concatenate([x1 * cos - x2 * sin, x1 * sin + x2 * cos], axis=-1).astype(x.dtype)


def masked_softmax(s, valid):
    s = jnp.where(valid, s, NEG_INF)
    p = jnp.where(valid, jnp.exp(s - jnp.max(s, axis=-1, keepdims=True)), 0.0)
    return p / jnp.maximum(jnp.sum(p, axis=-1, keepdims=True), 1e-30)


def mlstm_chunk(carry, xs):
    c_prev, n_prev, m_prev = carry
    q, k, v, ig, lf = xs
    L = q.shape[2]
    F = jnp.cumsum(lf, axis=-1)
    causal = jnp.tril(jnp.ones((L, L), dtype=bool))
    D = jnp.where(causal, F[..., :, None] - F[..., None, :] + ig[..., None, :], NEG_INF)
    m_inter = m_prev[..., None] + F
    m_t = jnp.maximum(m_inter, jnp.max(D, axis=-1))
    d_exp = jnp.exp(D - m_t[..., None])
    decay = jnp.exp(m_inter - m_t)
    s = jnp.einsum('bhtd,bhsd->bhts', q, k) * d_exp
    num = jnp.einsum('bhts,bhsd->bhtd', s, v) + decay[..., None] * jnp.einsum('bhvk,bhtk->bhtv', c_prev, q)
    den = jnp.sum(s, axis=-1) + decay * jnp.einsum('bhk,bhtk->bht', n_prev, q)
    h = num / jnp.maximum(jnp.abs(den), jnp.exp(-m_t))[..., None]
    m_new = m_t[..., -1]
    w = jnp.exp(F[..., -1:] - F + ig - m_new[..., None])
    carry_decay = jnp.exp(m_prev + F[..., -1] - m_new)
    c_new = carry_decay[..., None, None] * c_prev + jnp.einsum('bhs,bhsv,bhsk->bhvk', w, v, k)
    n_new = carry_decay[..., None] * n_prev + jnp.einsum('bhs,bhsk->bhk', w, k)
    return (c_new, n_new, m_new), h


def mlstm_mixer(q, k, v, o, ig, fg, b_i, b_f, norm_g, state, chunk):
    b_, L, _ = q.shape
    nc = L // chunk

    def heads(t):
        return t.astype(jnp.float32).reshape(b_, nc, chunk, N_HEADS_MIX, HEAD_DIM).transpose(1, 0, 3, 2, 4)

    def gates(t, b):
        return (t.astype(jnp.float32) + b.astype(jnp.float32)).reshape(b_, nc, chunk, N_HEADS_MIX).transpose(1, 0, 3, 2)

    state = (state[0].astype(jnp.float32), state[1].astype(jnp.float32), state[2].astype(jnp.float32))
    xs = (heads(q), heads(k) * HEAD_DIM ** -0.5, heads(v), gates(ig, b_i), jax.nn.log_sigmoid(gates(fg, b_f)))
    new_state, h = lax.scan(mlstm_chunk, state, xs)
    h = h.transpose(1, 0, 3, 2, 4).reshape(b_, L, MIX_W)
    hn = group_layernorm(h, N_HEADS_MIX) * norm_g.astype(jnp.float32)
    return (hn * jax.nn.sigmoid(o.astype(jnp.float32))).astype(q.dtype), new_state


def conv_mixer(a, b, prefix, w, bias, g, beta):
    u = a * jax.nn.sigmoid(b)
    z = jnp.concatenate([prefix.astype(u.dtype), u], axis=1)
    y = lax.conv_general_dilated(z, w[:, None, :].astype(u.dtype), window_strides=(1,), padding='VALID',
                                 dimension_numbers=('NWC', 'WIO', 'NWC'), feature_group_count=MIX_W)
    y = group_layernorm(y + bias.astype(y.dtype), CONV_GROUPS) * g.astype(jnp.float32) + beta.astype(jnp.float32)
    return jax.nn.silu(y).astype(a.dtype), z[:, -(CONV_W - 1):]


def nsa_inputs(q, kvs, gates, pos):
    b_, L = q.shape[:2]
    q = rope(q.reshape(b_, L, N_HEADS_MIX, HEAD_DIM), pos).astype(jnp.float32) * ATTN_SCALE
    q = q.reshape(b_, L, NSA_KV, NSA_REP, HEAD_DIM)
    kc, vc, ks, vs, kw, vw = [t.reshape(b_, L, NSA_KV, HEAD_DIM) for t in kvs]
    kc, ks, kw = rope(kc, pos), rope(ks, pos), rope(kw, pos)
    g = jax.nn.sigmoid(gates.astype(jnp.float32)).reshape(b_, L, NSA_KV, NSA_REP, 3)
    return q, kc, vc, ks, vs, kw, vw, g


def nsa_compress(x, w):
    b_, lp = x.shape[:2]
    c = x.astype(jnp.float32).reshape(b_, lp // CMP_STRIDE, CMP_STRIDE, NSA_KV, HEAD_DIM)
    w = w.astype(jnp.float32)
    lo = jnp.einsum('bncgd,cg->bngd', c, w[:CMP_STRIDE])
    hi = jnp.einsum('bncgd,cg->bngd', c, w[CMP_STRIDE:])
    return lo[:, :-1] + hi[:, 1:]


def sel_blocks(x):
    b_, lp = x.shape[:2]
    return x.reshape(b_, lp // SEL_BLOCK, SEL_BLOCK, NSA_KV, HEAD_DIM).transpose(0, 3, 1, 2, 4)


def nsa_cmp_sel(q, t, kc, vc, ks_t, vs_t):
    b_, tq = q.shape[:2]
    n_cmp = kc.shape[1]
    n_sel = ks_t.shape[2]
    s = jnp.einsum('btgrd,bngd->btgrn', q, kc)
    blk_end = jnp.arange(n_cmp) * CMP_STRIDE + CMP_LEN
    valid = (blk_end[None, :] <= t[:, None] + 1)[None, :, None, None, :]
    p = masked_softmax(s, valid)
    o_cmp = jnp.einsum('btgrn,bngd->btgrd', p, vc)
    imp = jnp.pad(jnp.sum(p, axis=3), ((0, 0), (0, 0), (0, 0), (0, n_sel * SEL_PER - n_cmp)))
    imp = jnp.sum(imp.reshape(b_, tq, NSA_KV, n_sel, SEL_PER), axis=-1)
    blk = jnp.arange(n_sel)[None, :]
    imp = jnp.where((blk == t[:, None] // SEL_BLOCK)[None, :, None, :], SEL_FORCE, imp)
    imp = jnp.where((blk * SEL_BLOCK <= t[:, None])[None, :, None, :], imp, -1.0)
    n_top = min(N_SELECT, n_sel)
    _, idx = lax.top_k(imp, n_top)
    bi = jnp.arange(b_)[:, None, None, None]
    gi = jnp.arange(NSA_KV)[None, None, :, None]
    kg = ks_t[bi, gi, idx].reshape(b_, tq, NSA_KV, n_top * SEL_BLOCK, HEAD_DIM).astype(jnp.float32)
    vg = vs_t[bi, gi, idx].reshape(b_, tq, NSA_KV, n_top * SEL_BLOCK, HEAD_DIM).astype(jnp.float32)
    pos = (idx[..., None] * SEL_BLOCK + jnp.arange(SEL_BLOCK)).reshape(b_, tq, NSA_KV, n_top * SEL_BLOCK)
    valid = (pos <= t[None, :, None, None])[:, :, :, None, :]
    s = jnp.einsum('btgrd,btgkd->btgrk', q, kg)
    o_sel = jnp.einsum('btgrk,btgkd->btgrd', masked_softmax(s, valid), vg)
    return o_cmp, o_sel


def nsa_window(q, t, kw, vw, kpos):
    s = jnp.einsum('btgrd,bkgd->btgrk', q, kw.astype(jnp.float32))
    valid = (kpos[None, :] <= t[:, None]) & (kpos[None, :] > t[:, None] - WINDOW) & (kpos[None, :] >= 0)
    return jnp.einsum('btgrk,bkgd->btgrd', masked_softmax(s, valid[None, :, None, None, :]), vw.astype(jnp.float32))


def nsa_combine(o_c, o_s, o_w, g):
    return o_c * g[..., 0:1] + o_s * g[..., 1:2] + o_w * g[..., 2:3]


def nsa_prompt(q, kvs, gates, wk, wv):
    b_, S = q.shape[:2]
    pos = jnp.arange(S)
    q, kc, vc, ks, vs, kw, vw, g = nsa_inputs(q, kvs, gates, pos)
    kc_b, vc_b = nsa_compress(kc, wk), nsa_compress(vc, wv)
    ks_t, vs_t = sel_blocks(ks), sel_blocks(vs)
    pad = ((0, 0), (WINDOW, 0), (0, 0), (0, 0))
    kw_p, vw_p = jnp.pad(kw, pad), jnp.pad(vw, pad)

    def block(i):
        start = i * Q_BLOCK
        qi = lax.dynamic_slice_in_dim(q, start, Q_BLOCK, 1)
        gi = lax.dynamic_slice_in_dim(g, start, Q_BLOCK, 1)
        t = start + jnp.arange(Q_BLOCK)
        o_c, o_s = nsa_cmp_sel(qi, t, kc_b, vc_b, ks_t, vs_t)
        kpos = start - WINDOW + jnp.arange(WINDOW + Q_BLOCK)
        o_w = nsa_window(qi, t, lax.dynamic_slice_in_dim(kw_p, start, WINDOW + Q_BLOCK, 1),
                         lax.dynamic_slice_in_dim(vw_p, start, WINDOW + Q_BLOCK, 1), kpos)
        return nsa_combine(o_c, o_s, o_w, gi)

    o = lax.map(block, jnp.arange(S // Q_BLOCK))
    o = jnp.moveaxis(o, 0, 1).reshape(b_, S, MIX_W)
    new_kv = jnp.stack([kc, vc, ks, vs], axis=2)
    n_win = min(WINDOW, S)
    new_win = jnp.stack([kw, vw], axis=2)[:, S - n_win:]
    return o, new_kv, new_win


def nsa_sample(q, kvs, gates, wk, wv, pool, page_table, win_buf):
    b_, T = q.shape[:2]
    P = page_table.shape[1] * PAGE_SIZE
    pos = P + jnp.arange(T)
    q, kc, vc, ks, vs, kw, vw, g = nsa_inputs(q, kvs, gates, pos)
    new_kv = jnp.stack([kc, vc, ks, vs], axis=2)
    past = pool[page_table].reshape(b_, P, 4, NSA_KV, HEAD_DIM)
    L = P + T
    lp = -(-L // SEL_BLOCK) * SEL_BLOCK
    full = jnp.pad(jnp.concatenate([past.astype(new_kv.dtype), new_kv], axis=1),
                   ((0, 0), (0, lp - L), (0, 0), (0, 0), (0, 0)))
    kc_b, vc_b = nsa_compress(full[:, :, 0], wk), nsa_compress(full[:, :, 1], wv)
    ks_t, vs_t = sel_blocks(full[:, :, 2]), sel_blocks(full[:, :, 3])
    o_c, o_s = nsa_cmp_sel(q, pos, kc_b, vc_b, ks_t, vs_t)
    n_win = win_buf.shape[1]
    win_all = jnp.concatenate([win_buf.astype(kw.dtype), jnp.stack([kw, vw], axis=2)], axis=1)
    kpos = P - n_win + jnp.arange(n_win + T)
    o_w = nsa_window(q, pos, win_all[:, :, 0], win_all[:, :, 1], kpos)
    o = nsa_combine(o_c, o_s, o_w, g).reshape(b_, T, MIX_W)
    return o, new_kv, win_all[:, T:]


def gmlp_mixer(u_pre, v_pre, g, beta, ws, bs, chunk):
    b_, L, _ = u_pre.shape
    u = jax.nn.gelu(u_pre.astype(jnp.float32))
    v = group_layernorm(jax.nn.gelu(v_pre.astype(jnp.float32)), GMLP_HEADS) * g.astype(jnp.float32) + beta.astype(jnp.float32)
    v5 = v.reshape(b_, L // chunk, chunk, GMLP_HEADS, MIX_W // GMLP_HEADS)
    w = (ws.astype(jnp.float32) * jnp.tril(jnp.ones((GMLP_CHUNK, GMLP_CHUNK), jnp.float32)))[:, :chunk, :chunk]
    mixed = jnp.einsum('hts,bcshd->bcthd', w, v5) + bs.astype(jnp.float32)[:, :chunk].T[None, None, :, :, None]
    return (u * mixed.reshape(b_, L, MIX_W)).astype(u_pre.dtype), v.astype(u_pre.dtype)


def peer(h, wq, k1, k2, U, V):
    n = h.shape[0]
    hf = h.astype(jnp.float32)
    q = (hf @ wq.astype(jnp.float32)).reshape(n, PEER_HEADS, PEER_DQ)
    half = PEER_DQ // 2
    s1 = jnp.einsum('nhd,hkd->nhk', q[..., :half], k1.astype(jnp.float32))
    s2 = jnp.einsum('nhd,hkd->nhk', q[..., half:], k2.astype(jnp.float32))
    v1, i1 = lax.top_k(s1, PEER_TOPK)
    v2, i2 = lax.top_k(s2, PEER_TOPK)
    cand = (v1[..., :, None] + v2[..., None, :]).reshape(n, PEER_HEADS, PEER_TOPK * PEER_TOPK)
    sc, ci = lax.top_k(cand, PEER_TOPK)
    e = jnp.take_along_axis(i1, ci // PEER_TOPK, axis=-1) * N_KEYS + jnp.take_along_axis(i2, ci % PEER_TOPK, axis=-1)
    gate = jax.nn.softmax(sc, axis=-1)
    act = jax.nn.gelu(jnp.einsum('nhkd,nd->nhk', U[e].astype(jnp.float32), hf))
    out = jnp.einsum('nhk,nhkd->nd', gate * act, V[e].astype(jnp.float32))
    return out.astype(h.dtype)


def peer_tokens(h, wq, k1, k2, U, V):
    b_, L, d = h.shape
    n = b_ * L
    flat = h.reshape(n, d)
    if n % PEER_BLOCK == 0 and n > PEER_BLOCK:
        out = lax.map(lambda hb: peer(hb, wq, k1, k2, U, V), flat.reshape(n // PEER_BLOCK, PEER_BLOCK, d))
        out = out.reshape(n, d)
    else:
        out = peer(flat, wq, k1, k2, U, V)
    return out.reshape(b_, L, d)


def trunk_layer(x, lw, mlstm_state, conv_prefix, chunk_a, chunk_d, nsa_fn):
    hn = rmsnorm(x, lw['norm1'])
    parts = jnp.split(hn @ lw['w_in'], np.cumsum(IN_SIZES)[:-1].tolist(), axis=-1)
    qa, ka, va, oa, ia, fa, ga, gb, qc, kc, vc, ks, vs, kw, vw, gc, ud, vd = parts
    y_a, (c_new, n_new, m_new) = mlstm_mixer(qa, ka, va, oa, ia, fa, lw['mlstm_b_i'], lw['mlstm_b_f'],
                                             lw['mlstm_norm_g'], mlstm_state, chunk_a)
    y_b, conv_new = conv_mixer(ga, gb, conv_prefix, lw['conv_w'], lw['conv_b'], lw['conv_norm_g'], lw['conv_norm_b'])
    y_c, kv_new, win_new = nsa_fn(qc, (kc, vc, ks, vs, kw, vw), gc, lw['nsa_cmp_wk'], lw['nsa_cmp_wv'])
    y_d, v_new = gmlp_mixer(ud, vd, lw['gmlp_norm_g'], lw['gmlp_norm_b'], lw['gmlp_ws'], lw['gmlp_bs'], chunk_d)
    mixed = jnp.concatenate([y_a.astype(x.dtype), y_b.astype(x.dtype), y_c.astype(x.dtype), y_d.astype(x.dtype)], axis=-1)
    x = x + mixed @ lw['w_out']
    x = x + peer_tokens(rmsnorm(x, lw['norm2']), lw['peer_wq'], lw['peer_k1'], lw['peer_k2'], lw['peer_u'], lw['peer_v'])
    return x, (kv_new, win_new, c_new, n_new, m_new, conv_new, v_new)


def setup_inputs(seed: int = 0) -> dict:
    key = jax.random.key(seed)
    keys = iter(jax.random.split(key, 40))

    def nrm(shape, scale=1.0):
        return scale * jax.random.normal(next(keys), shape, jnp.float32)

    n_pages = PAST_LEN // PAGE_SIZE
    n_used = DEC_BATCH * n_pages
    n_pool = n_used + n_used // 4
    n_win = min(WINDOW, PAST_LEN)
    return {
        'x_prompt': nrm((BATCH, SEQ, D_MODEL)),
        'x_sample': nrm((DEC_BATCH, DEC_SEQ, D_MODEL)),
        'cache_nsa_kv': nrm((DEPTH, n_pool, PAGE_SIZE, 4, NSA_KV, HEAD_DIM)),
        'state_nsa_win': nrm((DEPTH, DEC_BATCH, n_win, 2, NSA_KV, HEAD_DIM)),
        'state_mlstm_C': nrm((DEPTH, DEC_BATCH, N_HEADS_MIX, HEAD_DIM, HEAD_DIM), 0.1),
        'state_mlstm_n': nrm((DEPTH, DEC_BATCH, N_HEADS_MIX, HEAD_DIM), 0.1),
        'state_mlstm_m': nrm((DEPTH, DEC_BATCH, N_HEADS_MIX)),
        'state_conv': nrm((DEPTH, DEC_BATCH, CONV_W - 1, MIX_W), 0.5),
        'page_table': jax.random.permutation(next(keys), n_pool)[:n_used].reshape(DEC_BATCH, n_pages).astype(jnp.int32),
        'norm1_g': 1.0 + nrm((DEPTH, D_MODEL), 0.01),
        'norm2_g': 1.0 + nrm((DEPTH, D_MODEL), 0.01),
        'final_norm_g': 1.0 + nrm((D_MODEL,), 0.01),
        'w_in': nrm((DEPTH, D_MODEL, IN_W), D_MODEL ** -0.5),
        'w_out': nrm((DEPTH, 4 * MIX_W, D_MODEL), (4 * MIX_W) ** -0.5),
        'mlstm_b_i': nrm((DEPTH, N_HEADS_MIX), 0.1),
        'mlstm_b_f': 3.0 + nrm((DEPTH, N_HEADS_MIX), 0.1),
        'mlstm_norm_g': 1.0 + nrm((DEPTH, MIX_W), 0.01),
        'conv_w': nrm((DEPTH, CONV_W, MIX_W), CONV_W ** -0.5),
        'conv_b': nrm((DEPTH, MIX_W), 0.01),
        'conv_norm_g': 1.0 + nrm((DEPTH, MIX_W), 0.01),
        'conv_norm_b': nrm((DEPTH, MIX_W), 0.01),
        'nsa_cmp_wk': (1.0 + nrm((DEPTH, CMP_LEN, NSA_KV), 0.1)) / CMP_LEN,
        'nsa_cmp_wv': (1.0 + nrm((DEPTH, CMP_LEN, NSA_KV), 0.1)) / CMP_LEN,
        'gmlp_norm_g': 1.0 + nrm((DEPTH, MIX_W), 0.01),
        'gmlp_norm_b': nrm((DEPTH, MIX_W), 0.01),
        'gmlp_ws': nrm((DEPTH, GMLP_HEADS, GMLP_CHUNK, GMLP_CHUNK), GMLP_CHUNK ** -0.5),
        'gmlp_bs': 1.0 + nrm((DEPTH, GMLP_HEADS, GMLP_CHUNK), 0.1),
        'peer_wq': nrm((DEPTH, D_MODEL, PEER_HEADS * PEER_DQ), D_MODEL ** -0.5),
        'peer_k1': nrm((DEPTH, PEER_HEADS, N_KEYS, PEER_DQ // 2), (PEER_DQ // 2) ** -0.5),
        'peer_k2': nrm((DEPTH, PEER_HEADS, N_KEYS, PEER_DQ // 2), (PEER_DQ // 2) ** -0.5),
        'peer_u': nrm((DEPTH, N_EXPERTS, D_MODEL), D_MODEL ** -0.5),
        'peer_v': nrm((DEPTH, N_EXPERTS, D_MODEL), PEER_HEADS ** -0.5),
    }


def reference(x_prompt, x_sample, cache_nsa_kv, state_nsa_win, state_mlstm_C, state_mlstm_n, state_mlstm_m,
              state_conv, page_table, norm1_g, norm2_g, final_norm_g, w_in, w_out, mlstm_b_i, mlstm_b_f,
              mlstm_norm_g, conv_w, conv_b, conv_norm_g, conv_norm_b, nsa_cmp_wk, nsa_cmp_wv, gmlp_norm_g,
              gmlp_norm_b, gmlp_ws, gmlp_bs, peer_wq, peer_k1, peer_k2, peer_u, peer_v):
    xp, xs = x_prompt, x_sample
    bp = xp.shape[0]
    new_p, new_s = [], []
    for l in range(DEPTH):
        lw = {'norm1': norm1_g[l], 'norm2': norm2_g[l], 'w_in': w_in[l], 'w_out': w_out[l],
              'mlstm_b_i': mlstm_b_i[l], 'mlstm_b_f': mlstm_b_f[l], 'mlstm_norm_g': mlstm_norm_g[l],
              'conv_w': conv_w[l], 'conv_b': conv_b[l], 'conv_norm_g': conv_norm_g[l], 'conv_norm_b': conv_norm_b[l],
              'nsa_cmp_wk': nsa_cmp_wk[l], 'nsa_cmp_wv': nsa_cmp_wv[l],
              'gmlp_norm_g': gmlp_norm_g[l], 'gmlp_norm_b': gmlp_norm_b[l], 'gmlp_ws': gmlp_ws[l], 'gmlp_bs': gmlp_bs[l],
              'peer_wq': peer_wq[l], 'peer_k1': peer_k1[l], 'peer_k2': peer_k2[l], 'peer_u': peer_u[l], 'peer_v': peer_v[l]}
        zero_mem = (jnp.zeros((bp, N_HEADS_MIX, HEAD_DIM, HEAD_DIM), jnp.float32),
                    jnp.zeros((bp, N_HEADS_MIX, HEAD_DIM), jnp.float32),
                    jnp.zeros((bp, N_HEADS_MIX), jnp.float32))
        xp, st_p = trunk_layer(xp, lw, zero_mem, jnp.zeros((bp, CONV_W - 1, MIX_W), xp.dtype),
                               MLSTM_CHUNK, GMLP_CHUNK, nsa_prompt)
        nsa_fn = functools.partial(nsa_sample, pool=cache_nsa_kv[l], page_table=page_table, win_buf=state_nsa_win[l])
        xs, st_s = trunk_layer(xs, lw, (state_mlstm_C[l], state_mlstm_n[l], state_mlstm_m[l]), state_conv[l],
                               xs.shape[1], xs.shape[1], nsa_fn)
        new_p.append(st_p)
        new_s.append(st_s)

    def stack(states, i):
        return jnp.stack([s[i] for s in states])

    y_prompt = rmsnorm(xp, final_norm_g)
    y_sample = rmsnorm(xs, final_norm_g)
    return (y_prompt, y_sample,
            stack(new_p, 0), stack(new_s, 0),
            stack(new_p, 1), stack(new_s, 1),
            stack(new_p, 2), stack(new_s, 2),
            stack(new_p, 3), stack(new_s, 3),
            stack(new_p, 4), stack(new_s, 4),
            stack(new_p, 5), stack(new_s, 5),
            stack(new_s, 6))
```

```python
import functools

import jax
import jax.numpy as jnp
from jax import lax
from jax.experimental import pallas as pl
from jax.experimental.pallas import tpu as pltpu

F32 = jnp.float32
BF16 = jnp.bfloat16
HIGHEST = lax.Precision.HIGHEST

HEAD_DIM = 64
N_HEADS = 4
MIX_W = 256
NSA_KV = 2
NORM_EPS = 1e-6
NEG = -1e30
MLSTM_CHUNK = 64
CONV_W = 31
CONV_PAD = 32
CMP_STRIDE = 16
CMP_LEN = 32
SEL_BLOCK = 64
N_SELECT = 16
SEL_FORCE = 1e9
WINDOW = 512
ROPE_THETA = 10000.0
ATTN_SCALE = HEAD_DIM ** -0.5
GMLP_CHUNK = 128
PEER_TOPK = 16
LANES = 128
VMEM_LIMIT = 48 * 1024 * 1024

NT = (((1,), (1,)), ((), ()))
TN = (((0,), (0,)), ((), ()))


def _cparams(sem):
    return pltpu.CompilerParams(dimension_semantics=sem, vmem_limit_bytes=VMEM_LIMIT)


def _iota(shape, dim):
    return lax.broadcasted_iota(jnp.int32, shape, dim)


def _sigmoid(x):
    return 1.0 / (1.0 + jnp.exp(-x))


def _gelu(x):
    return 0.5 * x * (1.0 + jnp.tanh(0.7978845608028654 * (x + 0.044715 * (x * x * x))))


def _dot(a, b, dims=None, precision=None):
    if dims is None:
        return jnp.dot(a, b, preferred_element_type=F32, precision=precision)
    return lax.dot_general(a, b, dims, preferred_element_type=F32, precision=precision)


def _block_ones(n, blk):
    return (_iota((n, n), 0) // blk == _iota((n, n), 1) // blk).astype(F32)


def _group_layernorm(x, ee, width):
    mu = _dot(x, ee, precision=HIGHEST) * (1.0 / width)
    d = x - mu
    var = _dot(d * d, ee, precision=HIGHEST) * (1.0 / width)
    return d * lax.rsqrt(var + NORM_EPS)


def _rmsnorm(x, g):
    return x * lax.rsqrt(jnp.mean(x * x, axis=-1, keepdims=True) + NORM_EPS) * g


def _masked_softmax(s, valid):
    sm = jnp.where(valid, s, NEG)
    p = jnp.where(valid, jnp.exp(sm - jnp.max(sm, axis=-1, keepdims=True)), 0.0)
    return p / jnp.maximum(jnp.sum(p, axis=-1, keepdims=True), 1e-30)


def _rope(x, cos, sin_signed):
    w = x.shape[1]
    fwd = pltpu.roll(x, w - HEAD_DIM // 2, 1)
    bwd = pltpu.roll(x, HEAD_DIM // 2, 1)
    first = (_iota(x.shape, 1) % HEAD_DIM) < HEAD_DIM // 2
    return x * cos + jnp.where(first, fwd, bwd) * sin_signed


def _in_kernel(x_ref, g_ref, w_ref, cos_ref, sin_ref,
               oa_ref, ob_ref, oq_ref, okv_ref, owin_ref, oselb_ref, owinb_ref, od_ref, os_ref):
    y = _rmsnorm(x_ref[...], g_ref[...])
    z = _dot(y.astype(BF16), w_ref[...])
    oa_ref[...] = z[:, 0:1024]
    ob_ref[...] = z[:, 1024:1536]
    cos = cos_ref[...]
    sin = sin_ref[...]
    cos2 = jnp.concatenate([cos, cos], axis=1)
    sin2 = jnp.concatenate([sin, sin], axis=1)
    oq_ref[...] = _rope(z[:, 1536:1792], cos2, sin2) * ATTN_SCALE
    kc = _rope(z[:, 1792:1920], cos, sin)
    vc = z[:, 1920:2048]
    ks = _rope(z[:, 2048:2176], cos, sin)
    vs = z[:, 2176:2304]
    kw = _rope(z[:, 2304:2432], cos, sin)
    vw = z[:, 2432:2560]
    okv_ref[...] = jnp.concatenate([kc, vc, ks, vs], axis=1)
    owin = jnp.concatenate([kw, vw], axis=1)
    owin_ref[...] = owin
    oselb_ref[...] = jnp.concatenate([ks, vs], axis=1).astype(BF16)
    owinb_ref[...] = owin.astype(BF16)
    od_ref[...] = z[:, 2560:3072]
    os_ref[...] = z[:, 3072:3200]


def _in_proj(x, g, w, cos_t, sin_t, tm):
    n, d = x.shape
    npb = cos_t.shape[0] // tm
    widths = (1024, 512, 256, 512, 256, 256, 256, 512, 128)
    dtypes = (F32, F32, F32, F32, F32, BF16, BF16, F32, F32)
    row = lambda i: (i, 0)
    return pl.pallas_call(
        _in_kernel,
        grid=(n // tm,),
        in_specs=[pl.BlockSpec((tm, d), row),
                  pl.BlockSpec((1, d), lambda i: (0, 0)),
                  pl.BlockSpec(w.shape, lambda i: (0, 0)),
                  pl.BlockSpec((tm, LANES), lambda i: (i % npb, 0)),
                  pl.BlockSpec((tm, LANES), lambda i: (i % npb, 0))],
        out_specs=[pl.BlockSpec((tm, wd), row) for wd in widths],
        out_shape=[jax.ShapeDtypeStruct((n, wd), dt) for wd, dt in zip(widths, dtypes)],
        compiler_params=_cparams(("parallel",)),
    )(x, g, w, cos_t, sin_t)


def _mlstm_kernel(za_ref, zs_ref, bias_ref, ng_ref, c0_ref, n0_ref, m0_ref,
                  y_ref, c_ref, n_ref, m_ref, *, L, t_valid, bb):
    @pl.when(pl.program_id(1) == 0)
    def _():
        c_ref[...] = c0_ref[...]
        n_ref[...] = n0_ref[...]
        m_ref[...] = m0_ref[...]

    head_of_lane = _iota((1, MIX_W), 1) // HEAD_DIM
    causal = _iota((L, L), 0) >= _iota((L, L), 1)
    tril = causal.astype(F32)
    ee = _block_ones(MIX_W, HEAD_DIM)
    bdiag = ee > 0.5
    row128 = _iota((L, LANES), 0)
    lane128 = _iota((L, LANES), 1)
    for bi in range(bb):
        za = za_ref[bi]
        q = za[:, 0:256]
        k = za[:, 256:512] * (HEAD_DIM ** -0.5)
        v = za[:, 512:768]
        o = za[:, 768:1024]
        gi = zs_ref[bi] + bias_ref[...]
        ls = jnp.minimum(gi, 0.0) - jnp.log(1.0 + jnp.exp(-jnp.abs(gi)))
        if t_valid < L:
            live = row128 < t_valid
            ig = jnp.where(live, gi, NEG)
            ls = jnp.where(live, ls, 0.0)
        else:
            ig = gi
        gmat = jnp.where(lane128 < N_HEADS, ig, 0.0)
        lfm = jnp.where((lane128 >= N_HEADS) & (lane128 < 2 * N_HEADS), ls, 0.0)
        fc = _dot(tril, lfm, precision=HIGHEST)
        g_t = gmat.T
        f_t = fc.T
        mrow = m_ref[bi]
        nrow = n_ref[bi]
        cm = c_ref[bi]
        qb = q.astype(BF16)
        kb = k.astype(BF16)
        vb = v.astype(BF16)
        num = jnp.zeros((L, MIX_W), F32)
        decay_b = jnp.zeros((L, MIX_W), F32)
        dens_b = jnp.zeros((L, MIX_W), F32)
        mt_b = jnp.zeros((L, MIX_W), F32)
        w_b = jnp.zeros((L, MIX_W), F32)
        cd_b = jnp.zeros((1, MIX_W), F32)
        mnew_b = jnp.zeros((1, MIX_W), F32)
        for h in range(N_HEADS):
            hm = head_of_lane == h
            f_col = fc[:, N_HEADS + h:N_HEADS + h + 1]
            ig_col = gmat[:, h:h + 1]
            f_row = f_t[N_HEADS + h:N_HEADS + h + 1, :]
            ig_row = g_t[h:h + 1, :]
            mp = mrow[:, h * HEAD_DIM:h * HEAD_DIM + 1]
            dmat = jnp.where(causal, (f_col - f_row) + ig_row, NEG)
            m_inter = mp + f_col
            m_t = jnp.maximum(m_inter, jnp.max(dmat, axis=1, keepdims=True))
            d_exp = jnp.exp(dmat - m_t)
            decay = jnp.exp(m_inter - m_t)
            qh = jnp.where(hm, q, 0.0).astype(BF16)
            s = _dot(qh, kb, NT) * d_exp
            num = jnp.where(hm, _dot(s.astype(BF16), vb), num)
            dens_b = jnp.where(hm, jnp.sum(s, axis=1, keepdims=True), dens_b)
            decay_b = jnp.where(hm, decay, decay_b)
            mt_b = jnp.where(hm, m_t, mt_b)
            f_last = f_col[L - 1:L, :]
            m_new = m_t[L - 1:L, :]
            w_b = jnp.where(hm, jnp.exp((f_last - f_col) + ig_col - m_new), w_b)
            cd_b = jnp.where(hm, jnp.exp(mp + f_last - m_new), cd_b)
            mnew_b = jnp.where(hm, m_new, mnew_b)
        inter = _dot(qb, cm.astype(BF16), NT)
        nq_b = _dot(q * nrow, ee, precision=HIGHEST)
        hnum = num + decay_b * inter
        den_b = dens_b + decay_b * nq_b
        hh = hnum / jnp.maximum(jnp.abs(den_b), jnp.exp(-mt_b))
        hn = _group_layernorm(hh, ee, HEAD_DIM) * ng_ref[...]
        y_ref[bi] = hn * _sigmoid(o)
        kw = k * w_b
        upd = _dot(vb, kw.astype(BF16), TN)
        c_ref[bi] = jnp.where(bdiag, cd_b * cm + upd, 0.0)
        n_ref[bi] = cd_b * nrow + jnp.sum(kw, axis=0, keepdims=True)
        m_ref[bi] = mnew_b


def _mlstm(za, zs, bias, ng, c0, n0, m0, L, t_valid, bb):
    b, t, _ = za.shape
    nc = t // L
    st = lambda i, c: (i, 0, 0)
    return pl.pallas_call(
        functools.partial(_mlstm_kernel, L=L, t_valid=t_valid, bb=bb),
        grid=(b // bb, nc),
        in_specs=[pl.BlockSpec((bb, L, 1024), lambda i, c: (i, c, 0)),
                  pl.BlockSpec((bb, L, LANES), lambda i, c: (i, c, 0)),
                  pl.BlockSpec((1, LANES), lambda i, c: (0, 0)),
                  pl.BlockSpec((1, MIX_W), lambda i, c: (0, 0)),
                  pl.BlockSpec((bb, MIX_W, MIX_W), st),
                  pl.BlockSpec((bb, 1, MIX_W), st),
                  pl.BlockSpec((bb, 1, MIX_W), st)],
        out_specs=[pl.BlockSpec((bb, L, MIX_W), lambda i, c: (i, c, 0)),
                   pl.BlockSpec((bb, MIX_W, MIX_W), st),
                   pl.BlockSpec((bb, 1, MIX_W), st),
                   pl.BlockSpec((bb, 1, MIX_W), st)],
        out_shape=[jax.ShapeDtypeStruct((b, t, MIX_W), F32),
                   jax.ShapeDtypeStruct((b, MIX_W, MIX_W), F32),
                   jax.ShapeDtypeStruct((b, 1, MIX_W), F32),
                   jax.ShapeDtypeStruct((b, 1, MIX_W), F32)],
        compiler_params=_cparams(("parallel", "arbitrary")),
    )(za, zs, bias, ng, c0, n0, m0)


def _conv_kernel(zb_ref, pre_ref, w_ref, cb_ref, g_ref, beta_ref, y_ref, st_ref, ext_ref, *, tt, tv):
    @pl.when(pl.program_id(1) == 0)
    def _():
        ext_ref[0:CONV_PAD, :] = pre_ref[0]

    z = zb_ref[0]
    u = z[:, :MIX_W] * _sigmoid(z[:, MIX_W:])
    ext_ref[CONV_PAD:CONV_PAD + tt, :] = u
    off = CONV_PAD - (CONV_W - 1)
    acc = jnp.zeros((tt, MIX_W), F32)
    for j in range(CONV_W):
        acc = acc + w_ref[j:j + 1, :] * ext_ref[off + j:off + j + tt, :]
    ee = _block_ones(MIX_W, HEAD_DIM)
    y = _group_layernorm(acc + cb_ref[...], ee, HEAD_DIM) * g_ref[...] + beta_ref[...]
    y_ref[0] = y * _sigmoid(y)
    st_ref[0] = ext_ref[tv:tv + CONV_PAD, :]
    ext_ref[0:CONV_PAD, :] = ext_ref[tt:tt + CONV_PAD, :]


def _conv(zb, prefix, w, cb, g, beta, tt, tv):
    b, t, _ = zb.shape
    vec = lambda i, j: (0, 0)
    return pl.pallas_call(
        functools.partial(_conv_kernel, tt=tt, tv=tv),
        grid=(b, t // tt),
        in_specs=[pl.BlockSpec((1, tt, 2 * MIX_W), lambda i, j: (i, j, 0)),
                  pl.BlockSpec((1, CONV_PAD, MIX_W), lambda i, j: (i, 0, 0)),
                  pl.BlockSpec((CONV_W, MIX_W), vec),
                  pl.BlockSpec((1, MIX_W), vec),
                  pl.BlockSpec((1, MIX_W), vec),
                  pl.BlockSpec((1, MIX_W), vec)],
        out_specs=[pl.BlockSpec((1, tt, MIX_W), lambda i, j: (i, j, 0)),
                   pl.BlockSpec((1, CONV_PAD, MIX_W), lambda i, j: (i, 0, 0))],
        out_shape=[jax.ShapeDtypeStruct((b, t, MIX_W), F32),
                   jax.ShapeDtypeStruct((b, CONV_PAD, MIX_W), F32)],
        scratch_shapes=[pltpu.VMEM((CONV_PAD + tt, MIX_W), F32)],
        compiler_params=_cparams(("parallel", "arbitrary")),
    )(zb, prefix, w, cb, g, beta)


def _gmlp_kernel(zd_ref, g_ref, beta_ref, ws_ref, bsb_ref, y_ref, v_ref, *, cpb):
    z = zd_ref[...]
    ee = _block_ones(MIX_W, HEAD_DIM)
    u = _gelu(z[:, :MIX_W])
    vv = _group_layernorm(_gelu(z[:, MIX_W:]), ee, HEAD_DIM) * g_ref[...] + beta_ref[...]
    v_ref[...] = vv
    tril = _iota((GMLP_CHUNK, GMLP_CHUNK), 0) >= _iota((GMLP_CHUNK, GMLP_CHUNK), 1)
    head_of_lane = _iota((1, MIX_W), 1) // HEAD_DIM
    wm = [jnp.where(tril, ws_ref[h], 0.0).astype(BF16) for h in range(N_HEADS)]
    for c in range(cpb):
        rows = slice(c * GMLP_CHUNK, (c + 1) * GMLP_CHUNK)
        vc = vv[rows].astype(BF16)
        mixed = jnp.zeros((GMLP_CHUNK, MIX_W), F32)
        for h in range(N_HEADS):
            mixed = jnp.where(head_of_lane == h, _dot(wm[h], vc), mixed)
        y_ref[rows, :] = u[rows] * (mixed + bsb_ref[...])


def _gmlp(zd, g, beta, ws, bsb, cpb):
    n = zd.shape[0]
    rows = cpb * GMLP_CHUNK
    vec = lambda i: (0, 0)
    return pl.pallas_call(
        functools.partial(_gmlp_kernel, cpb=cpb),
        grid=(n // rows,),
        in_specs=[pl.BlockSpec((rows, 2 * MIX_W), lambda i: (i, 0)),
                  pl.BlockSpec((1, MIX_W), vec),
                  pl.BlockSpec((1, MIX_W), vec),
                  pl.BlockSpec((N_HEADS, GMLP_CHUNK, GMLP_CHUNK), lambda i: (0, 0, 0)),
                  pl.BlockSpec((GMLP_CHUNK, MIX_W), vec)],
        out_specs=[pl.BlockSpec((rows, MIX_W), lambda i: (i, 0)),
                   pl.BlockSpec((rows, MIX_W), lambda i: (i, 0))],
        out_shape=[jax.ShapeDtypeStruct((n, MIX_W), F32),
                   jax.ShapeDtypeStruct((n, MIX_W), F32)],
        compiler_params=_cparams(("parallel",)),
    )(zd, g, beta, ws, bsb)


def _compress_rows(x, wlo, whi):
    r = x.shape[0] // CMP_STRIDE
    x3 = x.reshape(r, CMP_STRIDE, x.shape[1])
    return jnp.sum(x3 * wlo[None], axis=1), jnp.sum(x3 * whi[None], axis=1)


def _combine_lo_hi(lo, hi):
    r = lo.shape[0]
    nxt = pltpu.roll(hi, r - 1, 0)
    return lo + jnp.where(_iota(lo.shape, 0) < r - 1, nxt, 0.0)


def _cmp_kernel(kv_ref, wlo_ref, whi_ref, o_ref):
    lo, hi = _compress_rows(kv_ref[0], wlo_ref[...], whi_ref[...])
    o_ref[0] = _combine_lo_hi(lo, hi)


def _cmp(okv3, wlo, whi):
    b, s, _ = okv3.shape
    return pl.pallas_call(
        _cmp_kernel,
        grid=(b,),
        in_specs=[pl.BlockSpec((1, s, 2 * LANES), lambda i: (i, 0, 0)),
                  pl.BlockSpec((CMP_STRIDE, 2 * LANES), lambda i: (0, 0)),
                  pl.BlockSpec((CMP_STRIDE, 2 * LANES), lambda i: (0, 0))],
        out_specs=pl.BlockSpec((1, s // CMP_STRIDE, 2 * LANES), lambda i: (i, 0, 0)),
        out_shape=jax.ShapeDtypeStruct((b, s // CMP_STRIDE, 2 * LANES), F32),
        compiler_params=_cparams(("parallel",)),
    )(okv3, wlo, whi)


def _nsa_prompt_kernel(q_ref, gs_ref, cb_ref, sel_ref, win_ref, y_ref, imp_s, *, tq, s_len, n_top, tk):
    start = pl.program_id(1) * tq
    nc = s_len // CMP_STRIDE
    ns = s_len // SEL_BLOCK
    q = q_ref[0]
    gl = _sigmoid(gs_ref[0])
    cb = cb_ref[0]
    trow = start + _iota((tq, 1), 0)
    t2 = jnp.concatenate([trow, trow], axis=0)
    pool4 = (_iota((nc, ns), 0) // (SEL_BLOCK // CMP_STRIDE) == _iota((nc, ns), 1)).astype(F32)
    outs = []
    for g in range(NSA_KV):
        qa = q[:, (2 * g) * HEAD_DIM:(2 * g + 1) * HEAD_DIM]
        qb = q[:, (2 * g + 1) * HEAD_DIM:(2 * g + 2) * HEAD_DIM]
        q2 = jnp.concatenate([qa, qb], axis=0).astype(BF16)
        kcb = cb[:, g * HEAD_DIM:(g + 1) * HEAD_DIM].astype(BF16)
        vcb = cb[:, LANES + g * HEAD_DIM:LANES + (g + 1) * HEAD_DIM].astype(BF16)
        sc = _dot(q2, kcb, NT)
        validc = (_iota((1, nc), 1) * CMP_STRIDE + CMP_LEN) <= (t2 + 1)
        p = _masked_softmax(sc, validc)
        o_c = _dot(p.astype(BF16), vcb)
        imp = _dot(p[:tq] + p[tq:], pool4, precision=HIGHEST)
        jidx = _iota((1, ns), 1)
        imp = jnp.where(jidx == trow // SEL_BLOCK, SEL_FORCE, imp)
        imp = jnp.where(jidx * SEL_BLOCK <= trow, imp, -1.0)
        if n_top < ns:
            imp_s[...] = imp.T
            x = imp_s[...]
            jcol = _iota((ns, tq), 0)

            def rank_body(i, cnt):
                row = imp_s[pl.ds(i, 1), :]
                beats = (row > x) | ((row == x) & (i < jcol))
                return cnt + jnp.where(beats, 1.0, 0.0)

            cnt = lax.fori_loop(0, ns, rank_body, jnp.zeros((ns, tq), F32))
            sel = jnp.where(cnt < n_top, 1.0, 0.0).T
        else:
            sel = jnp.ones((tq, ns), F32)
        selb = sel.astype(BF16)

        def tile_body(c, carry):
            m, l, acc = carry
            k0 = pl.multiple_of(c * tk, tk)
            kt = sel_ref[0, pl.ds(k0, tk), g * HEAD_DIM:(g + 1) * HEAD_DIM]
            vt = sel_ref[0, pl.ds(k0, tk), LANES + g * HEAD_DIM:LANES + (g + 1) * HEAD_DIM]
            s = _dot(q2, kt, NT)
            kpos = k0 + _iota((1, tk), 1)
            expand = jnp.where(_iota((ns, tk), 0) == kpos // SEL_BLOCK, 1.0, 0.0).astype(BF16)
            chosen = _dot(selb, expand)
            chosen2 = jnp.concatenate([chosen, chosen], axis=0)
            valid = (chosen2 > 0.5) & (kpos <= t2)
            sm = jnp.where(valid, s, NEG)
            m_new = jnp.maximum(m, jnp.max(sm, axis=1, keepdims=True))
            pt = jnp.where(valid, jnp.exp(sm - m_new), 0.0)
            alpha = jnp.exp(m - m_new)
            l = alpha * l + jnp.sum(pt, axis=1, keepdims=True)
            acc = alpha * acc + _dot(pt.astype(BF16), vt)
            return m_new, l, acc

        n_tiles = (start + tq + tk - 1) // tk
        m0 = jnp.full((2 * tq, 1), NEG, F32)
        l0 = jnp.zeros((2 * tq, 1), F32)
        a0 = jnp.zeros((2 * tq, HEAD_DIM), F32)
        _, l, acc = lax.fori_loop(0, n_tiles, tile_body, (m0, l0, a0))
        o_s = acc / jnp.maximum(l, 1e-30)
        wl = WINDOW + tq
        w0 = pl.multiple_of(jnp.maximum(start - WINDOW, 0), tq)
        kwt = win_ref[0, pl.ds(w0, wl), g * HEAD_DIM:(g + 1) * HEAD_DIM]
        vwt = win_ref[0, pl.ds(w0, wl), LANES + g * HEAD_DIM:LANES + (g + 1) * HEAD_DIM]
        sw = _dot(q2, kwt, NT)
        kposw = w0 + _iota((1, wl), 1)
        validw = (kposw <= t2) & (kposw > t2 - WINDOW)
        o_w = _dot(_masked_softmax(sw, validw).astype(BF16), vwt)
        for r in range(2):
            col = 2 * N_HEADS + (2 * g + r) * 3
            rows = slice(r * tq, (r + 1) * tq)
            outs.append(o_c[rows] * gl[:, col:col + 1] + o_s[rows] * gl[:, col + 1:col + 2]
                        + o_w[rows] * gl[:, col + 2:col + 3])
    y_ref[0] = jnp.concatenate(outs, axis=1)


def _nsa_prompt(oq3, os3, cb, oselb3, owinb3, tq):
    b, s, _ = oq3.shape
    ns = s // SEL_BLOCK
    n_top = min(N_SELECT, ns)
    tk = min(512, s)
    full = lambda i, j: (i, 0, 0)
    return pl.pallas_call(
        functools.partial(_nsa_prompt_kernel, tq=tq, s_len=s, n_top=n_top, tk=tk),
        grid=(b, s // tq),
        in_specs=[pl.BlockSpec((1, tq, MIX_W), lambda i, j: (i, j, 0)),
                  pl.BlockSpec((1, tq, LANES), lambda i, j: (i, j, 0)),
                  pl.BlockSpec((1, s // CMP_STRIDE, 2 * LANES), full),
                  pl.BlockSpec((1, s, 2 * LANES), full),
                  pl.BlockSpec((1, s, 2 * LANES), full)],
        out_specs=pl.BlockSpec((1, tq, MIX_W), lambda i, j: (i, j, 0)),
        out_shape=jax.ShapeDtypeStruct((b, s, MIX_W), F32),
        scratch_shapes=[pltpu.VMEM((ns, tq), F32)],
        compiler_params=_cparams(("parallel", "arbitrary")),
    )(oq3, os3, cb, oselb3, owinb3)


def _nsa_sample_kernel(pt_ref, q4_ref, gq_ref, newkv_ref, newwin_ref, win_ref, wlo_ref, whi_ref, *rest,
                       pp, npages, n_top):
    pool_refs = rest[:pp]
    o_ref, wo_ref, lo_s, hi_s, m_s, l_s, a_s = rest[pp:]
    i = pl.program_id(1)
    nb = 2 * npages
    nbp = m_s.shape[1]
    ncp = npages * (LANES // CMP_STRIDE)
    past = npages * LANES
    q4 = q4_ref[0]
    q4b = q4.astype(BF16)

    @pl.when(i == 0)
    def _():
        m_s[...] = jnp.zeros(m_s.shape, F32)
        l_s[...] = jnp.zeros(l_s.shape, F32)
        if nb < nbp:
            a_s[...] = jnp.zeros(a_s.shape, F32)

    lane8 = _iota((8, LANES), 1)
    lanej = _iota((8, nbp), 1)
    for kk in range(pp):
        page = pool_refs[kk][0]
        pidx = i * pp + kk
        lo8, hi8 = _compress_rows(page[:, 0:2 * LANES], wlo_ref[...], whi_ref[...])
        r0 = pl.multiple_of(pidx * 8, 8)
        lo_s[pl.ds(r0, 8), :] = lo8
        hi_s[pl.ds(r0, 8), :] = hi8
        ks = page[:, 2 * LANES:3 * LANES].astype(BF16)
        vs = page[:, 3 * LANES:4 * LANES].astype(BF16)
        s = _dot(q4b, ks, NT)
        for jb in range(LANES // SEL_BLOCK):
            inb = (lane8 >= jb * SEL_BLOCK) & (lane8 < (jb + 1) * SEL_BLOCK)
            sm = jnp.where(inb, s, NEG)
            m = jnp.max(sm, axis=1, keepdims=True)
            p = jnp.where(inb, jnp.exp(sm - m), 0.0)
            lsum = jnp.sum(p, axis=1, keepdims=True)
            a = _dot(p.astype(BF16), vs)
            j = pidx * (LANES // SEL_BLOCK) + jb
            m_s[...] = jnp.where(lanej == j, m, m_s[...])
            l_s[...] = jnp.where(lanej == j, lsum, l_s[...])
            for hh in range(N_HEADS):
                a_s[hh, pl.ds(j, 1), :] = a[hh:hh + 1, :]

    @pl.when(i == pl.num_programs(1) - 1)
    def _():
        row8 = _iota((8, 1), 0)
        cbm = _combine_lo_hi(lo_s[...], hi_s[...])
        kcb = cbm[:, 0:LANES].astype(BF16)
        vcb = cbm[:, LANES:2 * LANES].astype(BF16)
        sc = _dot(q4b, kcb, NT)
        validc = (_iota((1, ncp), 1) * CMP_STRIDE + CMP_LEN) <= past + 1
        p = _masked_softmax(sc, validc)
        o_c = _dot(p.astype(BF16), vcb)
        pg = jnp.where(row8 == 0, p[0:1] + p[1:2], jnp.where(row8 == 1, p[2:3] + p[3:4], 0.0))
        pool4 = (_iota((ncp, nbp), 0) // (SEL_BLOCK // CMP_STRIDE) == _iota((ncp, nbp), 1)).astype(F32)
        imp2 = _dot(pg, pool4, precision=HIGHEST)
        ii = _iota((nbp, nbp), 0)
        jj = _iota((nbp, nbp), 1)
        sels = []
        for g in range(NSA_KV):
            mx = jnp.broadcast_to(imp2[g:g + 1, :], (nbp, nbp))
            mt = mx.T
            beats = ((mt > mx) | ((mt == mx) & (ii < jj))) & (ii < nb)
            rank = jnp.sum(jnp.where(beats, 1.0, 0.0), axis=0, keepdims=True)
            sels.append(jnp.where((rank < n_top - 1) & (_iota((1, nbp), 1) < nb), 1.0, 0.0))
        sel8 = jnp.where(row8 < 2, sels[0], sels[1]) > 0.5
        newkv = newkv_ref[0]
        ksn = newkv[:, 2 * LANES:3 * LANES]
        vsn = newkv[:, 3 * LANES:4 * LANES]
        s_new = jnp.sum(q4 * ksn, axis=1, keepdims=True)
        mrow = m_s[...]
        m_all = jnp.maximum(jnp.max(jnp.where(sel8, mrow, NEG), axis=1, keepdims=True), s_new)
        wj = jnp.where(sel8, jnp.exp(mrow - m_all), 0.0)
        w_new = jnp.exp(s_new - m_all)
        ltot = jnp.sum(wj * l_s[...], axis=1, keepdims=True) + w_new
        osum = w_new * vsn
        for hh in range(N_HEADS):
            osum = osum + jnp.where(row8 == hh, _dot(wj, a_s[hh], precision=HIGHEST), 0.0)
        o_s = osum / jnp.maximum(ltot, 1e-30)
        win = win_ref[0]
        nw = win.shape[0]
        sw = _dot(q4b, win[:, 0:LANES].astype(BF16), NT)
        validw = (past - nw + _iota((1, nw), 1)) > past - WINDOW
        neww = newwin_ref[0]
        sw_new = jnp.sum(q4 * neww[:, 0:LANES], axis=1, keepdims=True)
        mw = jnp.maximum(jnp.max(jnp.where(validw, sw, NEG), axis=1, keepdims=True), sw_new)
        pw = jnp.where(validw, jnp.exp(sw - mw), 0.0)
        pn = jnp.exp(sw_new - mw)
        zw = jnp.sum(pw, axis=1, keepdims=True) + pn
        o_w = (_dot(pw.astype(BF16), win[:, LANES:2 * LANES].astype(BF16)) + pn * neww[:, LANES:2 * LANES]) / zw
        gg = _sigmoid(gq_ref[0])
        o_ref[0] = o_c * gg[:, 0:1] + o_s * gg[:, 1:2] + o_w * gg[:, 2:3]
        wo_ref[0, 0:nw - 1, :] = win[1:nw, :]
        wo_ref[0, nw - 1:nw, :] = neww


def _nsa_sample(pt, q4, gq, newkv, newwin, win, wlo, whi, pool, pp):
    bd, npages = pt.shape
    nw = win.shape[1]
    nb = 2 * npages
    nbp = -(-nb // LANES) * LANES
    n_top = min(N_SELECT, nb + 1)
    per_b = lambda b, i, pt_ref: (b, 0, 0)
    const = lambda b, i, pt_ref: (0, 0)

    def page_map(kk):
        return lambda b, i, pt_ref: (pt_ref[b, i * pp + kk], 0, 0)

    grid_spec = pltpu.PrefetchScalarGridSpec(
        num_scalar_prefetch=1,
        grid=(bd, npages // pp),
        in_specs=[pl.BlockSpec((1, 8, LANES), per_b),
                  pl.BlockSpec((1, 8, LANES), per_b),
                  pl.BlockSpec((1, 1, 4 * LANES), per_b),
                  pl.BlockSpec((1, 1, 2 * LANES), per_b),
                  pl.BlockSpec((1, nw, 2 * LANES), per_b),
                  pl.BlockSpec((CMP_STRIDE, 2 * LANES), const),
                  pl.BlockSpec((CMP_STRIDE, 2 * LANES), const)]
                 + [pl.BlockSpec((1, LANES, 4 * LANES), page_map(kk)) for kk in range(pp)],
        out_specs=[pl.BlockSpec((1, 8, LANES), per_b),
                   pl.BlockSpec((1, nw, 2 * LANES), per_b)],
        scratch_shapes=[pltpu.VMEM((npages * 8, 2 * LANES), F32),
                        pltpu.VMEM((npages * 8, 2 * LANES), F32),
                        pltpu.VMEM((8, nbp), F32),
                        pltpu.VMEM((8, nbp), F32),
                        pltpu.VMEM((N_HEADS, nbp, LANES), F32)])
    return pl.pallas_call(
        functools.partial(_nsa_sample_kernel, pp=pp, npages=npages, n_top=n_top),
        grid_spec=grid_spec,
        out_shape=[jax.ShapeDtypeStruct((bd, 8, LANES), F32),
                   jax.ShapeDtypeStruct((bd, nw, 2 * LANES), F32)],
        compiler_params=_cparams(("parallel", "arbitrary")),
    )(pt, q4, gq, newkv, newwin, win, wlo, whi, *([pool] * pp))


def _out_kernel(x_ref, ya_ref, yb_ref, yc_ref, yd_ref, wo_ref, g2_ref, wq_ref, k1_ref, k2_ref,
                xn_ref, ht_ref, s1_ref, s2_ref, *, nh, dq):
    acc = x_ref[...]
    for idx, y_ref in enumerate((ya_ref, yb_ref, yc_ref, yd_ref)):
        acc = acc + _dot(y_ref[...].astype(BF16), wo_ref[idx * MIX_W:(idx + 1) * MIX_W, :])
    xn_ref[...] = acc
    h2 = _rmsnorm(acc, g2_ref[...])
    ht_ref[...] = h2.T.astype(BF16)
    q = _dot(h2.astype(BF16), wq_ref[...])
    half = dq // 2
    for h in range(nh):
        s1_ref[h] = _dot(k1_ref[h], q[:, h * dq:h * dq + half].astype(BF16), NT)
        s2_ref[h] = _dot(k2_ref[h], q[:, h * dq + half:(h + 1) * dq].astype(BF16), NT)


def _out_proj(x, ya, yb, yc, yd, wo, g2, wq, k1, k2, tm):
    n, d = x.shape
    nh, nk, half = k1.shape
    row = lambda i: (i, 0)
    c2 = lambda i: (0, 0)
    return pl.pallas_call(
        functools.partial(_out_kernel, nh=nh, dq=2 * half),
        grid=(n // tm,),
        in_specs=[pl.BlockSpec((tm, d), row)] + [pl.BlockSpec((tm, MIX_W), row)] * 4
                 + [pl.BlockSpec(wo.shape, c2), pl.BlockSpec((1, d), c2), pl.BlockSpec(wq.shape, c2),
                    pl.BlockSpec(k1.shape, lambda i: (0, 0, 0)), pl.BlockSpec(k2.shape, lambda i: (0, 0, 0))],
        out_specs=[pl.BlockSpec((tm, d), row),
                   pl.BlockSpec((d, tm), lambda i: (0, i)),
                   pl.BlockSpec((nh, nk, tm), lambda i: (0, 0, i)),
                   pl.BlockSpec((nh, nk, tm), lambda i: (0, 0, i))],
        out_shape=[jax.ShapeDtypeStruct((n, d), F32),
                   jax.ShapeDtypeStruct((d, n), BF16),
                   jax.ShapeDtypeStruct((nh, nk, n), F32),
                   jax.ShapeDtypeStruct((nh, nk, n), F32)],
        compiler_params=_cparams(("parallel",)),
    )(x, ya, yb, yc, yd, wo, g2, wq, k1, k2)


def _gate_kernel(s1_ref, s2_ref, o_ref, v1_s, v2_s, c_s):
    x1 = s1_ref[0]
    x2 = s2_ref[0]
    for i in range(PEER_TOPK):
        m1 = jnp.max(x1, axis=0, keepdims=True)
        m2 = jnp.max(x2, axis=0, keepdims=True)
        v1_s[i:i + 1, :] = m1
        v2_s[i:i + 1, :] = m2
        x1 = jnp.where(x1 == m1, NEG, x1)
        x2 = jnp.where(x2 == m2, NEG, x2)
    v1 = v1_s[...]
    v2 = v2_s[...]
    for i in range(PEER_TOPK):
        c_s[i * PEER_TOPK:(i + 1) * PEER_TOPK, :] = v1[i:i + 1, :] + v2
    c = c_s[...]
    mx = v1[0:1, :] + v2[0:1, :]
    z = jnp.zeros_like(mx)
    m = mx
    for i in range(PEER_TOPK):
        m = jnp.max(c, axis=0, keepdims=True)
        z = z + jnp.exp(m - mx)
        c = jnp.where(c == m, NEG, c)
    o_ref[0, 0:1, :] = m
    o_ref[0, 1:2, :] = v1[0:1, :]
    o_ref[0, 2:3, :] = v2[0:1, :]
    o_ref[0, 3:4, :] = 1.0 / z
    o_ref[0, 4:8, :] = jnp.zeros((4, mx.shape[1]), F32)


def _gate(s1t, s2t, tn):
    nh, nk, n = s1t.shape
    blk = lambda h, i: (h, 0, i)
    return pl.pallas_call(
        _gate_kernel,
        grid=(nh, n // tn),
        in_specs=[pl.BlockSpec((1, nk, tn), blk), pl.BlockSpec((1, nk, tn), blk)],
        out_specs=pl.BlockSpec((1, 8, tn), blk),
        out_shape=jax.ShapeDtypeStruct((nh, 8, n), F32),
        scratch_shapes=[pltpu.VMEM((PEER_TOPK, tn), F32), pltpu.VMEM((PEER_TOPK, tn), F32),
                        pltpu.VMEM((PEER_TOPK * PEER_TOPK, tn), F32)],
        compiler_params=_cparams(("parallel", "parallel")),
    )(s1t, s2t)


def _peer_kernel(ht_ref, u_ref, vt_ref, s1_ref, s2_ref, sm_ref, xn_ref, fg_ref, o_ref,
                 acc_ref, e2_ref, f_ref, *, nh, nk, apc, final):
    j = pl.program_id(1)

    @pl.when(j == 0)
    def _():
        acc_ref[...] = jnp.zeros(acc_ref.shape, F32)
        for h in range(nh):
            e2_ref[h] = jnp.exp(s2_ref[h] - sm_ref[h, 2:3, :])

    ht = ht_ref[...]
    for aa in range(apc):
        a = j * apc + aa
        act = _gelu(_dot(u_ref[aa * nk:(aa + 1) * nk, :], ht))
        wgt = jnp.zeros(act.shape, F32)
        for h in range(nh):
            s1row = s1_ref[h, pl.ds(a, 1), :]
            crow = jnp.exp(s1row - sm_ref[h, 1:2, :]) * sm_ref[h, 3:4, :]
            score = s1row + s2_ref[h]
            wgt = wgt + jnp.where(score >= sm_ref[h, 0:1, :], e2_ref[h] * crow, 0.0)
        f_ref[aa * nk:(aa + 1) * nk, :] = (wgt * act).astype(BF16)
    acc_ref[...] += _dot(vt_ref[...], f_ref[...])

    @pl.when(j == pl.num_programs(1) - 1)
    def _():
        out = xn_ref[...] + acc_ref[...].T
        if final:
            out = _rmsnorm(out, fg_ref[...])
        o_ref[...] = out


def _peer(ht, u, vt, s1t, s2t, sm, xn, fg, tm, apc, final):
    d, n = ht.shape
    nh, nk, _ = s1t.shape
    ne = u.shape[0]
    te = apc * nk
    tok3 = lambda i, j: (0, 0, i)
    return pl.pallas_call(
        functools.partial(_peer_kernel, nh=nh, nk=nk, apc=apc, final=final),
        grid=(n // tm, ne // te),
        in_specs=[pl.BlockSpec((d, tm), lambda i, j: (0, i)),
                  pl.BlockSpec((te, d), lambda i, j: (j, 0)),
                  pl.BlockSpec((d, te), lambda i, j: (0, j)),
                  pl.BlockSpec((nh, nk, tm), tok3),
                  pl.BlockSpec((nh, nk, tm), tok3),
                  pl.BlockSpec((nh, 8, tm), tok3),
                  pl.BlockSpec((tm, d), lambda i, j: (i, 0)),
                  pl.BlockSpec((1, d), lambda i, j: (0, 0))],
        out_specs=pl.BlockSpec((tm, d), lambda i, j: (i, 0)),
        out_shape=jax.ShapeDtypeStruct((n, d), F32),
        scratch_shapes=[pltpu.VMEM((d, tm), F32), pltpu.VMEM((nh, nk, tm), F32), pltpu.VMEM((te, tm), BF16)],
        compiler_params=_cparams(("parallel", "arbitrary")),
    )(ht, u, vt, s1t, s2t, sm, xn, fg)


def _rope_tables(pos):
    half = HEAD_DIM // 2
    inv_freq = ROPE_THETA ** (-jnp.arange(half, dtype=F32) / half)
    ang = pos.astype(F32)[:, None] * inv_freq[None, :]
    cos = jnp.cos(ang)
    sin = jnp.sin(ang)
    return jnp.tile(jnp.concatenate([cos, cos], axis=1), (1, 2)), jnp.tile(jnp.concatenate([-sin, sin], axis=1), (1, 2))


def _pad_rows(x, rows):
    return jnp.pad(x, ((0, rows - x.shape[0]),) + ((0, 0),) * (x.ndim - 1))


def _token_tile(n, pref):
    return pref if n % pref == 0 else n


def kernel(x_prompt, x_sample, cache_nsa_kv, state_nsa_win, state_mlstm_C, state_mlstm_n, state_mlstm_m, state_conv, page_table, norm1_g, norm2_g, final_norm_g, w_in, w_out, mlstm_b_i, mlstm_b_f, mlstm_norm_g, conv_w, conv_b, conv_norm_g, conv_norm_b, nsa_cmp_wk, nsa_cmp_wv, gmlp_norm_g, gmlp_norm_b, gmlp_ws, gmlp_bs, peer_wq, peer_k1, peer_k2, peer_u, peer_v):
    depth = w_in.shape[0]
    bp, s_len, d = x_prompt.shape
    bd, t_dec, _ = x_sample.shape
    n_pool = cache_nsa_kv.shape[1]
    npages = page_table.shape[1]
    past = npages * cache_nsa_kv.shape[2]
    assert t_dec == 1 and cache_nsa_kv.shape[2] == LANES and past % SEL_BLOCK == 0
    assert s_len % (WINDOW + 128) == 0 or s_len >= WINDOW + 128
    np_tok = bp * s_len
    ns_pad = LANES
    assert bd <= ns_pad

    xp = x_prompt.reshape(np_tok, d)
    xs = _pad_rows(x_sample.reshape(bd, d), ns_pad)
    cos_p, sin_p = _rope_tables(jnp.arange(s_len))
    cos_s, sin_s = _rope_tables(jnp.full((ns_pad,), past))
    pool = cache_nsa_kv.reshape(depth * n_pool, LANES, 4 * LANES)
    eye_h = jnp.eye(N_HEADS, dtype=F32)

    tm_in = _token_tile(s_len, 256)
    tm_out = _token_tile(np_tok, 256)
    tm_peer = _token_tile(np_tok, 512)
    nk = peer_k1.shape[2]
    apc = min(8, nk)

    new_p, new_s = [], []
    for l in range(depth):
        wi = w_in[l]
        w_perm = jnp.concatenate([wi[:, 0:1024], wi[:, 1032:2568], wi[:, 2580:3092], wi[:, 1024:1032],
                                  wi[:, 2568:2580], jnp.zeros((d, 108), F32)], axis=1).astype(BF16)
        g1 = norm1_g[l].reshape(1, d)
        g2 = norm2_g[l].reshape(1, d)
        gate_bias = jnp.concatenate([mlstm_b_i[l], mlstm_b_f[l], jnp.zeros((LANES - 2 * N_HEADS,), F32)]).reshape(1, LANES)
        mng = mlstm_norm_g[l].reshape(1, MIX_W)
        cw = conv_w[l]
        cbias = conv_b[l].reshape(1, MIX_W)
        cg = conv_norm_g[l].reshape(1, MIX_W)
        cbeta = conv_norm_b[l].reshape(1, MIX_W)
        w32 = jnp.concatenate([jnp.repeat(nsa_cmp_wk[l], HEAD_DIM, axis=1), jnp.repeat(nsa_cmp_wv[l], HEAD_DIM, axis=1)], axis=1)
        wlo, whi = w32[:CMP_STRIDE], w32[CMP_STRIDE:]
        gg = gmlp_norm_g[l].reshape(1, MIX_W)
        gbeta = gmlp_norm_b[l].reshape(1, MIX_W)
        gws = gmlp_ws[l]
        gbsb = jnp.repeat(gmlp_bs[l].T, HEAD_DIM, axis=1)
        wo = w_out[l].astype(BF16)
        wq = peer_wq[l].astype(BF16)
        k1 = peer_k1[l].astype(BF16)
        k2 = peer_k2[l].astype(BF16)
        ub = peer_u[l].astype(BF16)
        vtb = peer_v[l].astype(BF16).T
        fg = final_norm_g.reshape(1, d)
        final = l == depth - 1

        oa, ob, oq, okv, owin, oselb, owinb, od, osm = _in_proj(xp, g1, w_perm, cos_p, sin_p, tm_in)
        ya, c_p, n_p, m_p = _mlstm(oa.reshape(bp, s_len, 1024), osm.reshape(bp, s_len, LANES), gate_bias, mng,
                                   jnp.zeros((bp, MIX_W, MIX_W), F32), jnp.zeros((bp, 1, MIX_W), F32),
                                   jnp.zeros((bp, 1, MIX_W), F32), MLSTM_CHUNK, MLSTM_CHUNK, bp)
        yb, conv_p = _conv(ob.reshape(bp, s_len, 2 * MIX_W), jnp.zeros((bp, CONV_PAD, MIX_W), F32),
                           cw, cbias, cg, cbeta, 512, 512)
        cb = _cmp(okv.reshape(bp, s_len, 4 * LANES), wlo, whi)
        yc = _nsa_prompt(oq.reshape(bp, s_len, MIX_W), osm.reshape(bp, s_len, LANES), cb,
                         oselb.reshape(bp, s_len, 2 * LANES), owinb.reshape(bp, s_len, 2 * LANES), 128)
        yd, _ = _gmlp(od, gg, gbeta, gws, gbsb, 4)
        xn, ht, s1t, s2t = _out_proj(xp, ya.reshape(np_tok, MIX_W), yb.reshape(np_tok, MIX_W),
                                     yc.reshape(np_tok, MIX_W), yd, wo, g2, wq, k1, k2, tm_out)
        sm = _gate(s1t, s2t, _token_tile(np_tok, 512))
        xp = _peer(ht, ub, vtb, s1t, s2t, sm, xn, fg, tm_peer, apc, final)
        n_win = min(WINDOW, s_len)
        new_p.append((okv.reshape(bp, s_len, 4, NSA_KV, HEAD_DIM),
                      owin.reshape(bp, s_len, 2, NSA_KV, HEAD_DIM)[:, s_len - n_win:],
                      jnp.stack([c_p[:, h * HEAD_DIM:(h + 1) * HEAD_DIM, h * HEAD_DIM:(h + 1) * HEAD_DIM]
                                 for h in range(N_HEADS)], axis=1),
                      n_p.reshape(bp, N_HEADS, HEAD_DIM),
                      m_p[:, 0, ::HEAD_DIM],
                      conv_p[:, CONV_PAD - (CONV_W - 1):]))

        sa, sb, sq, skv, swin, _, _, sd, ssm = _in_proj(xs, g1, w_perm, cos_s, sin_s, ns_pad)
        rows8 = lambda t: jnp.pad(t[:bd, None, :], ((0, 0), (0, 7), (0, 0)))
        c0 = jnp.einsum('bhvk,hg->bhvgk', state_mlstm_C[l], eye_h).reshape(bd, MIX_W, MIX_W)
        n0 = state_mlstm_n[l].reshape(bd, 1, MIX_W)
        m0 = jnp.repeat(state_mlstm_m[l], HEAD_DIM, axis=-1).reshape(bd, 1, MIX_W)
        bb_s = 2 if bd % 2 == 0 else 1
        ya_s, c_s, n_s, m_s = _mlstm(rows8(sa), rows8(ssm), gate_bias, mng, c0, n0, m0, 8, 1, bb_s)
        prefix = jnp.pad(state_conv[l], ((0, 0), (CONV_PAD - (CONV_W - 1), 0), (0, 0)))
        yb_s, conv_s = _conv(rows8(sb), prefix, cw, cbias, cg, cbeta, 8, 1)
        zd_s = jnp.pad(sd[:bd, None, :], ((0, 0), (0, GMLP_CHUNK - 1), (0, 0))).reshape(bd * GMLP_CHUNK, 2 * MIX_W)
        yd_s, v_s = _gmlp(zd_s, gg, gbeta, gws, gbsb, 1)
        q_heads = sq[:bd].reshape(bd, NSA_KV, 2, 1, HEAD_DIM)
        q4 = (q_heads * jnp.eye(NSA_KV, dtype=F32)[None, :, None, :, None]).reshape(bd, N_HEADS, LANES)
        q4 = jnp.pad(q4, ((0, 0), (0, 8 - N_HEADS), (0, 0)))
        gq = jnp.pad(ssm[:bd, 2 * N_HEADS:2 * N_HEADS + 3 * N_HEADS].reshape(bd, N_HEADS, 3),
                     ((0, 0), (0, 8 - N_HEADS), (0, LANES - 3)))
        win_state = state_nsa_win[l].reshape(bd, -1, 2 * LANES)
        pp = 4 if npages % 4 == 0 else 1
        o8, win_s = _nsa_sample(page_table + l * n_pool, q4, gq, skv[:bd, None, :], swin[:bd, None, :],
                                win_state, wlo, whi, pool, pp)
        yc_s = jnp.concatenate([o8[:, h, (h // 2) * HEAD_DIM:(h // 2 + 1) * HEAD_DIM] for h in range(N_HEADS)], axis=1)
        xn_s, ht_s, s1t_s, s2t_s = _out_proj(xs, _pad_rows(ya_s[:, 0], ns_pad), _pad_rows(yb_s[:, 0], ns_pad),
                                             _pad_rows(yc_s, ns_pad),
                                             _pad_rows(yd_s.reshape(bd, GMLP_CHUNK, MIX_W)[:, 0], ns_pad),
                                             wo, g2, wq, k1, k2, ns_pad)
        sm_s = _gate(s1t_s, s2t_s, ns_pad)
        xs = _peer(ht_s, ub, vtb, s1t_s, s2t_s, sm_s, xn_s, fg, ns_pad, apc, final)
        new_s.append((skv[:bd].reshape(bd, 1, 4, NSA_KV, HEAD_DIM),
                      win_s.reshape(bd, -1, 2, NSA_KV, HEAD_DIM),
                      jnp.stack([c_s[:, h * HEAD_DIM:(h + 1) * HEAD_DIM, h * HEAD_DIM:(h + 1) * HEAD_DIM]
                                 for h in range(N_HEADS)], axis=1),
                      n_s.reshape(bd, N_HEADS, HEAD_DIM),
                      m_s[:, 0, ::HEAD_DIM],
                      conv_s[:, CONV_PAD - (CONV_W - 1):],
                      v_s.reshape(bd, GMLP_CHUNK, MIX_W)[:, 0:1]))

    def stack(states, i):
        return jnp.stack([st[i] for st in states])

    y_prompt = xp.reshape(bp, s_len, d)
    y_sample = xs[:bd].reshape(bd, 1, d)
    return (y_prompt, y_sample,
            stack(new_p, 0), stack(new_s, 0),
            stack(new_p, 1), stack(new_s, 1),
            stack(new_p, 2), stack(new_s, 2),
            stack(new_p, 3), stack(new_s, 3),
            stack(new_p, 4), stack(new_s, 4),
            stack(new_p, 5), stack(new_s, 5),
            stack(new_s, 6))
```

```python
import functools

import jax
import jax.numpy as jnp
from jax import lax
from jax.experimental import pallas as pl
from jax.experimental.pallas import tpu as pltpu

F32 = jnp.float32
BF16 = jnp.bfloat16
HIGHEST = lax.Precision.HIGHEST

HEAD_DIM = 64
N_HEADS = 4
MIX_W = 256
NSA_KV = 2
NORM_EPS = 1e-6
NEG = -1e30
MLSTM_CHUNK = 64
CONV_W = 31
CONV_PAD = 32
CMP_STRIDE = 16
CMP_LEN = 32
SEL_BLOCK = 64
N_SELECT = 16
SEL_FORCE = 1e9
WINDOW = 512
ROPE_THETA = 10000.0
ATTN_SCALE = HEAD_DIM ** -0.5
GMLP_CHUNK = 128
PEER_TOPK = 16
LANES = 128
VMEM_LIMIT = 48 * 1024 * 1024

NT = (((1,), (1,)), ((), ()))
TN = (((0,), (0,)), ((), ()))


def _cparams(sem):
    return pltpu.CompilerParams(dimension_semantics=sem, vmem_limit_bytes=VMEM_LIMIT)


def _iota(shape, dim):
    return lax.broadcasted_iota(jnp.int32, shape, dim)


def _sigmoid(x):
    return 1.0 / (1.0 + jnp.exp(-x))


def _gelu(x):
    c1 = -2.0 * 0.7978845608028654
    c2 = c1 * 0.044715
    return x / (1.0 + jnp.exp(x * (c1 + c2 * (x * x))))


def _dot(a, b, dims=None, precision=None):
    if dims is None:
        return jnp.dot(a, b, preferred_element_type=F32, precision=precision)
    return lax.dot_general(a, b, dims, preferred_element_type=F32, precision=precision)


def _block_ones(n, blk):
    return (_iota((n, n), 0) // blk == _iota((n, n), 1) // blk).astype(F32)


def _group_layernorm(x, ee, width):
    mu = _dot(x, ee, precision=HIGHEST) * (1.0 / width)
    d = x - mu
    var = _dot(d * d, ee, precision=HIGHEST) * (1.0 / width)
    return d * lax.rsqrt(var + NORM_EPS)


def _rmsnorm(x, g):
    return x * lax.rsqrt(jnp.mean(x * x, axis=-1, keepdims=True) + NORM_EPS) * g


def _masked_softmax(s, valid):
    sm = jnp.where(valid, s, NEG)
    p = jnp.where(valid, jnp.exp(sm - jnp.max(sm, axis=-1, keepdims=True)), 0.0)
    return p / jnp.maximum(jnp.sum(p, axis=-1, keepdims=True), 1e-30)


def _rope(x, cos, sin_signed):
    w = x.shape[1]
    fwd = pltpu.roll(x, w - HEAD_DIM // 2, 1)
    bwd = pltpu.roll(x, HEAD_DIM // 2, 1)
    first = (_iota(x.shape, 1) % HEAD_DIM) < HEAD_DIM // 2
    return x * cos + jnp.where(first, fwd, bwd) * sin_signed


def _in_kernel(x_ref, g_ref, w_ref, cos_ref, sin_ref,
               oa_ref, ob_ref, oq_ref, okv_ref, owin_ref, okaug_ref, ovsb_ref, owinb_ref, od_ref, os_ref, *, npb):
    tm = x_ref.shape[0]
    y = _rmsnorm(x_ref[...], g_ref[...])
    z = _dot(y.astype(BF16), w_ref[...])
    oa_ref[...] = z[:, 0:1024]
    ob_ref[...] = z[:, 1024:1536]
    cos = cos_ref[...]
    sin = sin_ref[...]
    cos2 = jnp.concatenate([cos, cos], axis=1)
    sin2 = jnp.concatenate([sin, sin], axis=1)
    oq_ref[...] = _rope(z[:, 1536:1792], cos2, sin2) * ATTN_SCALE
    kc = _rope(z[:, 1792:1920], cos, sin)
    vc = z[:, 1920:2048]
    ks = _rope(z[:, 2048:2176], cos, sin)
    vs = z[:, 2176:2304]
    kw = _rope(z[:, 2304:2432], cos, sin)
    vw = z[:, 2432:2560]
    okv_ref[...] = jnp.concatenate([kc, vc, ks, vs], axis=1)
    owin = jnp.concatenate([kw, vw], axis=1)
    owin_ref[...] = owin
    pos = (pl.program_id(0) % npb) * tm + _iota((tm, LANES), 0)
    onehot = jnp.where(_iota((tm, LANES), 1) == pos // SEL_BLOCK, 1.0, 0.0)
    okaug_ref[...] = jnp.concatenate([ks, onehot], axis=1).astype(BF16)
    ovsb_ref[...] = vs.astype(BF16)
    owinb_ref[...] = owin.astype(BF16)
    od_ref[...] = z[:, 2560:3072]
    os_ref[...] = z[:, 3072:3200]


def _in_proj(x, g, w, cos_t, sin_t, tm):
    n, d = x.shape
    npb = cos_t.shape[0] // tm
    widths = (1024, 512, 256, 512, 256, 256, 128, 256, 512, 128)
    dtypes = (F32, F32, F32, F32, F32, BF16, BF16, BF16, F32, F32)
    row = lambda i: (i, 0)
    return pl.pallas_call(
        functools.partial(_in_kernel, npb=npb),
        grid=(n // tm,),
        in_specs=[pl.BlockSpec((tm, d), row),
                  pl.BlockSpec((1, d), lambda i: (0, 0)),
                  pl.BlockSpec(w.shape, lambda i: (0, 0)),
                  pl.BlockSpec((tm, LANES), lambda i: (i % npb, 0)),
                  pl.BlockSpec((tm, LANES), lambda i: (i % npb, 0))],
        out_specs=[pl.BlockSpec((tm, wd), row) for wd in widths],
        out_shape=[jax.ShapeDtypeStruct((n, wd), dt) for wd, dt in zip(widths, dtypes)],
        compiler_params=_cparams(("parallel",)), name="in_proj",
    )(x, g, w, cos_t, sin_t)


def _mlstm_kernel(za_ref, zs_ref, bias_ref, ng_ref, c0_ref, n0_ref, m0_ref,
                  y_ref, c_ref, n_ref, m_ref, *, L, t_valid, bb):
    @pl.when(pl.program_id(1) == 0)
    def _():
        c_ref[...] = c0_ref[...]
        n_ref[...] = n0_ref[...]
        m_ref[...] = m0_ref[...]

    head_of_lane = _iota((1, MIX_W), 1) // HEAD_DIM
    causal = _iota((L, L), 0) >= _iota((L, L), 1)
    tril = causal.astype(F32)
    ee = _block_ones(MIX_W, HEAD_DIM)
    bdiag = ee > 0.5
    row128 = _iota((L, LANES), 0)
    lane128 = _iota((L, LANES), 1)
    for bi in range(bb):
        za = za_ref[bi]
        q = za[:, 0:256]
        k = za[:, 256:512] * (HEAD_DIM ** -0.5)
        v = za[:, 512:768]
        o = za[:, 768:1024]
        gi = zs_ref[bi] + bias_ref[...]
        ls = jnp.minimum(gi, 0.0) - jnp.log(1.0 + jnp.exp(-jnp.abs(gi)))
        if t_valid < L:
            live = row128 < t_valid
            ig = jnp.where(live, gi, NEG)
            ls = jnp.where(live, ls, 0.0)
        else:
            ig = gi
        gmat = jnp.where(lane128 < N_HEADS, ig, 0.0)
        lfm = jnp.where((lane128 >= N_HEADS) & (lane128 < 2 * N_HEADS), ls, 0.0)
        fc = _dot(tril, lfm, precision=HIGHEST)
        g_t = gmat.T
        f_t = fc.T
        mrow = m_ref[bi]
        nrow = n_ref[bi]
        cm = c_ref[bi]
        qb = q.astype(BF16)
        kb = k.astype(BF16)
        vb = v.astype(BF16)
        num = jnp.zeros((L, MIX_W), F32)
        decay_b = jnp.zeros((L, MIX_W), F32)
        dens_b = jnp.zeros((L, MIX_W), F32)
        mt_b = jnp.zeros((L, MIX_W), F32)
        w_b = jnp.zeros((L, MIX_W), F32)
        cd_b = jnp.zeros((1, MIX_W), F32)
        mnew_b = jnp.zeros((1, MIX_W), F32)
        for h in range(N_HEADS):
            hm = head_of_lane == h
            f_col = fc[:, N_HEADS + h:N_HEADS + h + 1]
            ig_col = gmat[:, h:h + 1]
            f_row = f_t[N_HEADS + h:N_HEADS + h + 1, :]
            ig_row = g_t[h:h + 1, :]
            mp = mrow[:, h * HEAD_DIM:h * HEAD_DIM + 1]
            dmat = jnp.where(causal, (f_col - f_row) + ig_row, NEG)
            m_inter = mp + f_col
            m_t = jnp.maximum(m_inter, jnp.max(dmat, axis=1, keepdims=True))
            d_exp = jnp.exp(dmat - m_t)
            decay = jnp.exp(m_inter - m_t)
            qh = jnp.where(hm, q, 0.0).astype(BF16)
            s = _dot(qh, kb, NT) * d_exp
            num = jnp.where(hm, _dot(s.astype(BF16), vb), num)
            dens_b = jnp.where(hm, jnp.sum(s, axis=1, keepdims=True), dens_b)
            decay_b = jnp.where(hm, decay, decay_b)
            mt_b = jnp.where(hm, m_t, mt_b)
            f_last = f_col[L - 1:L, :]
            m_new = m_t[L - 1:L, :]
            w_b = jnp.where(hm, jnp.exp((f_last - f_col) + ig_col - m_new), w_b)
            cd_b = jnp.where(hm, jnp.exp(mp + f_last - m_new), cd_b)
            mnew_b = jnp.where(hm, m_new, mnew_b)
        inter = _dot(qb, cm.astype(BF16), NT)
        nq_b = _dot(q * nrow, ee, precision=HIGHEST)
        hnum = num + decay_b * inter
        den_b = dens_b + decay_b * nq_b
        hh = hnum / jnp.maximum(jnp.abs(den_b), jnp.exp(-mt_b))
        hn = _group_layernorm(hh, ee, HEAD_DIM) * ng_ref[...]
        y_ref[bi] = hn * _sigmoid(o)
        kw = k * w_b
        upd = _dot(vb, kw.astype(BF16), TN)
        c_ref[bi] = jnp.where(bdiag, cd_b * cm + upd, 0.0)
        n_ref[bi] = cd_b * nrow + jnp.sum(kw, axis=0, keepdims=True)
        m_ref[bi] = mnew_b


def _mlstm(za, zs, bias, ng, c0, n0, m0, L, t_valid, bb):
    b, t, _ = za.shape
    nc = t // L
    st = lambda i, c: (i, 0, 0)
    return pl.pallas_call(
        functools.partial(_mlstm_kernel, L=L, t_valid=t_valid, bb=bb),
        grid=(b // bb, nc),
        in_specs=[pl.BlockSpec((bb, L, 1024), lambda i, c: (i, c, 0)),
                  pl.BlockSpec((bb, L, LANES), lambda i, c: (i, c, 0)),
                  pl.BlockSpec((1, LANES), lambda i, c: (0, 0)),
                  pl.BlockSpec((1, MIX_W), lambda i, c: (0, 0)),
                  pl.BlockSpec((bb, MIX_W, MIX_W), st),
                  pl.BlockSpec((bb, 1, MIX_W), st),
                  pl.BlockSpec((bb, 1, MIX_W), st)],
        out_specs=[pl.BlockSpec((bb, L, MIX_W), lambda i, c: (i, c, 0)),
                   pl.BlockSpec((bb, MIX_W, MIX_W), st),
                   pl.BlockSpec((bb, 1, MIX_W), st),
                   pl.BlockSpec((bb, 1, MIX_W), st)],
        out_shape=[jax.ShapeDtypeStruct((b, t, MIX_W), F32),
                   jax.ShapeDtypeStruct((b, MIX_W, MIX_W), F32),
                   jax.ShapeDtypeStruct((b, 1, MIX_W), F32),
                   jax.ShapeDtypeStruct((b, 1, MIX_W), F32)],
        compiler_params=_cparams(("parallel", "arbitrary")), name="mlstm",
    )(za, zs, bias, ng, c0, n0, m0)


def _conv_kernel(zb_ref, pre_ref, w_ref, cb_ref, g_ref, beta_ref, y_ref, st_ref, ext_ref, *, tt, tv):
    @pl.when(pl.program_id(1) == 0)
    def _():
        ext_ref[0:CONV_PAD, :] = pre_ref[0]

    z = zb_ref[0]
    u = z[:, :MIX_W] * _sigmoid(z[:, MIX_W:])
    ext_ref[CONV_PAD:CONV_PAD + tt, :] = u
    off = CONV_PAD - (CONV_W - 1)
    acc = jnp.zeros((tt, MIX_W), F32)
    for j in range(CONV_W):
        acc = acc + w_ref[j:j + 1, :] * ext_ref[off + j:off + j + tt, :]
    ee = _block_ones(MIX_W, HEAD_DIM)
    y = _group_layernorm(acc + cb_ref[...], ee, HEAD_DIM) * g_ref[...] + beta_ref[...]
    y_ref[0] = y * _sigmoid(y)
    st_ref[0] = ext_ref[tv:tv + CONV_PAD, :]
    ext_ref[0:CONV_PAD, :] = ext_ref[tt:tt + CONV_PAD, :]


def _conv(zb, prefix, w, cb, g, beta, tt, tv):
    b, t, _ = zb.shape
    vec = lambda i, j: (0, 0)
    return pl.pallas_call(
        functools.partial(_conv_kernel, tt=tt, tv=tv),
        grid=(b, t // tt),
        in_specs=[pl.BlockSpec((1, tt, 2 * MIX_W), lambda i, j: (i, j, 0)),
                  pl.BlockSpec((1, CONV_PAD, MIX_W), lambda i, j: (i, 0, 0)),
                  pl.BlockSpec((CONV_W, MIX_W), vec),
                  pl.BlockSpec((1, MIX_W), vec),
                  pl.BlockSpec((1, MIX_W), vec),
                  pl.BlockSpec((1, MIX_W), vec)],
        out_specs=[pl.BlockSpec((1, tt, MIX_W), lambda i, j: (i, j, 0)),
                   pl.BlockSpec((1, CONV_PAD, MIX_W), lambda i, j: (i, 0, 0))],
        out_shape=[jax.ShapeDtypeStruct((b, t, MIX_W), F32),
                   jax.ShapeDtypeStruct((b, CONV_PAD, MIX_W), F32)],
        scratch_shapes=[pltpu.VMEM((CONV_PAD + tt, MIX_W), F32)],
        compiler_params=_cparams(("parallel", "arbitrary")), name="conv",
    )(zb, prefix, w, cb, g, beta)


def _gmlp_kernel(zd_ref, g_ref, beta_ref, ws_ref, bsb_ref, y_ref, v_ref, *, cpb):
    z = zd_ref[...]
    ee = _block_ones(MIX_W, HEAD_DIM)
    u = _gelu(z[:, :MIX_W])
    vv = _group_layernorm(_gelu(z[:, MIX_W:]), ee, HEAD_DIM) * g_ref[...] + beta_ref[...]
    v_ref[...] = vv
    tril = _iota((GMLP_CHUNK, GMLP_CHUNK), 0) >= _iota((GMLP_CHUNK, GMLP_CHUNK), 1)
    head_of_lane = _iota((1, MIX_W), 1) // HEAD_DIM
    wm = [jnp.where(tril, ws_ref[h], 0.0).astype(BF16) for h in range(N_HEADS)]
    for c in range(cpb):
        rows = slice(c * GMLP_CHUNK, (c + 1) * GMLP_CHUNK)
        vc = vv[rows].astype(BF16)
        mixed = jnp.zeros((GMLP_CHUNK, MIX_W), F32)
        for h in range(N_HEADS):
            mixed = jnp.where(head_of_lane == h, _dot(wm[h], vc), mixed)
        y_ref[rows, :] = u[rows] * (mixed + bsb_ref[...])


def _gmlp(zd, g, beta, ws, bsb, cpb):
    n = zd.shape[0]
    rows = cpb * GMLP_CHUNK
    vec = lambda i: (0, 0)
    return pl.pallas_call(
        functools.partial(_gmlp_kernel, cpb=cpb),
        grid=(n // rows,),
        in_specs=[pl.BlockSpec((rows, 2 * MIX_W), lambda i: (i, 0)),
                  pl.BlockSpec((1, MIX_W), vec),
                  pl.BlockSpec((1, MIX_W), vec),
                  pl.BlockSpec((N_HEADS, GMLP_CHUNK, GMLP_CHUNK), lambda i: (0, 0, 0)),
                  pl.BlockSpec((GMLP_CHUNK, MIX_W), vec)],
        out_specs=[pl.BlockSpec((rows, MIX_W), lambda i: (i, 0)),
                   pl.BlockSpec((rows, MIX_W), lambda i: (i, 0))],
        out_shape=[jax.ShapeDtypeStruct((n, MIX_W), F32),
                   jax.ShapeDtypeStruct((n, MIX_W), F32)],
        compiler_params=_cparams(("parallel",)), name="gmlp",
    )(zd, g, beta, ws, bsb)


def _combine_lo_hi(lo, hi):
    r = lo.shape[0]
    nxt = pltpu.roll(hi, r - 1, 0)
    return lo + jnp.where(_iota(lo.shape, 0) < r - 1, nxt, 0.0)


def _cmp_kernel(kv_ref, wlo_ref, whi_ref, o_ref):
    x = kv_ref[0]
    x3 = x.reshape(x.shape[0] // CMP_STRIDE, CMP_STRIDE, x.shape[1])
    lo = jnp.sum(x3 * wlo_ref[...][None], axis=1)
    hi = jnp.sum(x3 * whi_ref[...][None], axis=1)
    o_ref[0] = _combine_lo_hi(lo, hi)


def _cmp(okv3, wlo, whi):
    b, s, _ = okv3.shape
    return pl.pallas_call(
        _cmp_kernel,
        grid=(b,),
        in_specs=[pl.BlockSpec((1, s, 2 * LANES), lambda i: (i, 0, 0)),
                  pl.BlockSpec((CMP_STRIDE, 2 * LANES), lambda i: (0, 0)),
                  pl.BlockSpec((CMP_STRIDE, 2 * LANES), lambda i: (0, 0))],
        out_specs=pl.BlockSpec((1, s // CMP_STRIDE, 2 * LANES), lambda i: (i, 0, 0)),
        out_shape=jax.ShapeDtypeStruct((b, s // CMP_STRIDE, 2 * LANES), F32),
        compiler_params=_cparams(("parallel",)), name="nsa_cmp",
    )(okv3, wlo, whi)


def _nsa_prompt_kernel(q_ref, gs_ref, cb_ref, ka_ref, vs_ref, win_ref, y_ref, *, tq, s_len, n_top, tk):
    start = pl.program_id(1) * tq
    nc = s_len // CMP_STRIDE
    ns = s_len // SEL_BLOCK
    q = q_ref[0]
    gl = _sigmoid(gs_ref[0])
    lo_half = _iota((tq, LANES), 1) < HEAD_DIM
    q_rows = []
    for g in range(NSA_KV):
        qpair = q[:, g * LANES:(g + 1) * LANES]
        swapped = pltpu.roll(qpair, HEAD_DIM, 1)
        if g == 0:
            q_rows += [jnp.where(lo_half, qpair, 0.0), jnp.where(lo_half, swapped, 0.0)]
        else:
            q_rows += [jnp.where(lo_half, 0.0, swapped), jnp.where(lo_half, 0.0, qpair)]
    q128 = jnp.concatenate(q_rows, axis=0)
    q128b = q128.astype(BF16)
    trow = start + _iota((tq, 1), 0)
    t2 = jnp.concatenate([trow, trow], axis=0)
    t4 = jnp.concatenate([t2, t2], axis=0)
    cb = cb_ref[0]
    sc = _dot(q128b, cb[:, 0:LANES].astype(BF16), NT)
    validc = (_iota((1, nc), 1) * CMP_STRIDE + CMP_LEN) <= (t4 + 1)
    p = _masked_softmax(sc, validc)
    o_c = _dot(p.astype(BF16), cb[:, LANES:2 * LANES].astype(BF16))
    psum = jnp.concatenate([p[0:tq] + p[tq:2 * tq], p[2 * tq:3 * tq] + p[3 * tq:4 * tq]], axis=0)
    pool4 = (_iota((nc, ns), 0) // (SEL_BLOCK // CMP_STRIDE) == _iota((nc, ns), 1)).astype(F32)
    imp = _dot(psum, pool4, precision=HIGHEST)
    jidx = _iota((1, ns), 1)
    imp = jnp.where(jidx == t2 // SEL_BLOCK, SEL_FORCE, imp)
    imp = jnp.where(jidx * SEL_BLOCK <= t2, imp, -1.0)
    if n_top < ns:
        x = jnp.concatenate([imp[0:tq].T, imp[tq:2 * tq].T], axis=1)
        jrow = _iota((ns, 2 * tq), 0).astype(F32)
        sel_t = jnp.zeros((ns, 2 * tq), F32)
        for _ in range(n_top):
            m = jnp.max(x, axis=0, keepdims=True)
            first = jnp.min(jnp.where(x == m, jrow, float(ns)), axis=0, keepdims=True)
            hit = jrow == first
            sel_t = jnp.where(hit, 1.0, sel_t)
            x = jnp.where(hit, -3.0, x)
        sel = jnp.concatenate([sel_t[:, 0:tq].T, sel_t[:, tq:2 * tq].T], axis=0)
    else:
        sel = jnp.ones((2 * tq, ns), F32)
    selneg = (sel - 1.0) * (-NEG)
    if ns < LANES:
        selneg = jnp.concatenate([selneg, jnp.zeros((2 * tq, LANES - ns), F32)], axis=1)
    selneg4 = jnp.concatenate([selneg[0:tq], selneg[0:tq], selneg[tq:2 * tq], selneg[tq:2 * tq]], axis=0)
    qaug = jnp.concatenate([q128, selneg4], axis=1).astype(BF16)

    def tile(c, carry, diagonal):
        m, l, acc = carry
        k0 = pl.multiple_of(c * tk, tk)
        s = _dot(qaug, ka_ref[0, pl.ds(k0, tk), :], NT)
        if diagonal:
            s = jnp.where(k0 + _iota((1, tk), 1) <= t4, s, NEG)
        m_new = jnp.maximum(m, jnp.max(s, axis=1, keepdims=True))
        pt = jnp.exp(s - m_new)
        alpha = jnp.exp(m - m_new)
        l = alpha * l + jnp.sum(pt, axis=1, keepdims=True)
        acc = alpha * acc + _dot(pt.astype(BF16), vs_ref[0, pl.ds(k0, tk), :])
        return m_new, l, acc

    n_tiles = (start + tq + tk - 1) // tk
    carry = (jnp.full((4 * tq, 1), NEG, F32), jnp.zeros((4 * tq, 1), F32), jnp.zeros((4 * tq, LANES), F32))
    carry = lax.fori_loop(0, n_tiles - 1, lambda c, cr: tile(c, cr, False), carry)
    _, l, acc = tile(n_tiles - 1, carry, True)
    o_s = acc / jnp.maximum(l, 1e-30)
    wl = WINDOW + tq
    w0 = pl.multiple_of(jnp.maximum(start - WINDOW, 0), tq)
    sw = _dot(q128b, win_ref[0, pl.ds(w0, wl), 0:LANES], NT)
    kposw = w0 + _iota((1, wl), 1)
    validw = (kposw <= t4) & (kposw > t4 - WINDOW)
    o_w = _dot(_masked_softmax(sw, validw).astype(BF16), win_ref[0, pl.ds(w0, wl), LANES:2 * LANES])
    heads = []
    for hh in range(N_HEADS):
        col = 2 * N_HEADS + hh * 3
        rows = slice(hh * tq, (hh + 1) * tq)
        heads.append(o_c[rows] * gl[:, col:col + 1] + o_s[rows] * gl[:, col + 1:col + 2]
                     + o_w[rows] * gl[:, col + 2:col + 3])
    y_ref[0] = jnp.concatenate([jnp.where(lo_half, heads[0], pltpu.roll(heads[1], HEAD_DIM, 1)),
                                jnp.where(lo_half, pltpu.roll(heads[2], HEAD_DIM, 1), heads[3])], axis=1)


def _nsa_prompt(oq3, os3, cb, okaug3, ovsb3, owinb3, tq):
    b, s, _ = oq3.shape
    ns = s // SEL_BLOCK
    assert ns <= LANES
    n_top = min(N_SELECT, ns)
    tk = min(512, s)
    full = lambda i, j: (i, 0, 0)
    return pl.pallas_call(
        functools.partial(_nsa_prompt_kernel, tq=tq, s_len=s, n_top=n_top, tk=tk),
        grid=(b, s // tq),
        in_specs=[pl.BlockSpec((1, tq, MIX_W), lambda i, j: (i, j, 0)),
                  pl.BlockSpec((1, tq, LANES), lambda i, j: (i, j, 0)),
                  pl.BlockSpec((1, s // CMP_STRIDE, 2 * LANES), full),
                  pl.BlockSpec((1, s, 2 * LANES), full),
                  pl.BlockSpec((1, s, LANES), full),
                  pl.BlockSpec((1, s, 2 * LANES), full)],
        out_specs=pl.BlockSpec((1, tq, MIX_W), lambda i, j: (i, j, 0)),
        out_shape=jax.ShapeDtypeStruct((b, s, MIX_W), F32),
        compiler_params=_cparams(("parallel", "arbitrary")), name="nsa_prompt",
    )(oq3, os3, cb, okaug3, ovsb3, owinb3)


def _nsa_sample_kernel(pt_ref, q4_ref, gq_ref, newkv_ref, newwin_ref, newcol_ref, win_ref, wall_ref, *rest,
                       pp, npages, n_top):
    pool_refs = rest[:pp]
    o_ref, wo_ref, lo_s, hi_s, m_s, l_s, a_s = rest[pp:]
    i = pl.program_id(1)
    nb = 2 * npages
    nbp = m_s.shape[1]
    ncp = npages * (LANES // CMP_STRIDE)
    past = npages * LANES
    q4 = q4_ref[0]
    q4b = q4.astype(BF16)

    @pl.when(i == 0)
    def _():
        m_s[...] = jnp.zeros(m_s.shape, F32)
        l_s[...] = jnp.zeros(l_s.shape, F32)
        if nb < nbp:
            a_s[...] = jnp.zeros(a_s.shape, F32)

    lane8 = _iota((8, LANES), 1)
    lanej = _iota((8, nbp), 1)
    set_of_lane = _iota((8, 2 * LANES), 1) // HEAD_DIM
    for kk in range(pp):
        page = pool_refs[kk][0]
        pidx = i * pp + kk
        pooled = _dot(wall_ref[...], page[0:2 * LANES, :], NT, precision=HIGHEST)
        lo8 = jnp.zeros((8, 2 * LANES), F32)
        hi8 = jnp.zeros((8, 2 * LANES), F32)
        for st in range(4):
            lo8 = jnp.where(set_of_lane == st, pooled[st * 16:st * 16 + 8], lo8)
            hi8 = jnp.where(set_of_lane == st, pooled[st * 16 + 8:st * 16 + 16], hi8)
        r0 = pl.multiple_of(pidx * 8, 8)
        lo_s[pl.ds(r0, 8), :] = lo8
        hi_s[pl.ds(r0, 8), :] = hi8
        ks_t = page[2 * LANES:3 * LANES, :].astype(BF16)
        vs_t = page[3 * LANES:4 * LANES, :].astype(BF16)
        s = _dot(q4b, ks_t)
        for jb in range(LANES // SEL_BLOCK):
            inb = (lane8 >= jb * SEL_BLOCK) & (lane8 < (jb + 1) * SEL_BLOCK)
            sm = jnp.where(inb, s, NEG)
            m = jnp.max(sm, axis=1, keepdims=True)
            p = jnp.where(inb, jnp.exp(sm - m), 0.0)
            lsum = jnp.sum(p, axis=1, keepdims=True)
            a = _dot(p.astype(BF16), vs_t, NT)
            j = pidx * (LANES // SEL_BLOCK) + jb
            m_s[...] = jnp.where(lanej == j, m, m_s[...])
            l_s[...] = jnp.where(lanej == j, lsum, l_s[...])
            for hh in range(N_HEADS):
                a_s[hh, pl.ds(j, 1), :] = a[hh:hh + 1, :]

    @pl.when(i == pl.num_programs(1) - 1)
    def _():
        row8 = _iota((8, 1), 0)
        cbm = _combine_lo_hi(lo_s[...], hi_s[...])
        kcb = cbm[:, 0:LANES].astype(BF16)
        vcb = cbm[:, LANES:2 * LANES].astype(BF16)
        sc = _dot(q4b, kcb, NT)
        validc = (_iota((1, ncp), 1) * CMP_STRIDE + CMP_LEN) <= past + 1
        p = _masked_softmax(sc, validc)
        o_c = _dot(p.astype(BF16), vcb)
        pg = jnp.where(row8 == 0, p[0:1] + p[1:2], jnp.where(row8 == 1, p[2:3] + p[3:4], 0.0))
        pool4 = (_iota((ncp, nbp), 0) // (SEL_BLOCK // CMP_STRIDE) == _iota((ncp, nbp), 1)).astype(F32)
        imp2 = _dot(pg, pool4, precision=HIGHEST)
        ii = _iota((nbp, nbp), 0)
        jj = _iota((nbp, nbp), 1)
        sels = []
        for g in range(NSA_KV):
            mx = jnp.broadcast_to(imp2[g:g + 1, :], (nbp, nbp))
            mt = mx.T
            beats = ((mt > mx) | ((mt == mx) & (ii < jj))) & (ii < nb)
            rank = jnp.sum(jnp.where(beats, 1.0, 0.0), axis=0, keepdims=True)
            sels.append(jnp.where((rank < n_top - 1) & (_iota((1, nbp), 1) < nb), 1.0, 0.0))
        sel8 = jnp.where(row8 < 2, sels[0], sels[1]) > 0.5
        newkv = newkv_ref[0]
        ksn = newkv[:, 2 * LANES:3 * LANES]
        vsn = newkv[:, 3 * LANES:4 * LANES]
        s_new = jnp.sum(q4 * ksn, axis=1, keepdims=True)
        mrow = m_s[...]
        m_all = jnp.maximum(jnp.max(jnp.where(sel8, mrow, NEG), axis=1, keepdims=True), s_new)
        wj = jnp.where(sel8, jnp.exp(mrow - m_all), 0.0)
        w_new = jnp.exp(s_new - m_all)
        ltot = jnp.sum(wj * l_s[...], axis=1, keepdims=True) + w_new
        osum = w_new * vsn
        for hh in range(N_HEADS):
            osum = osum + jnp.where(row8 == hh, _dot(wj, a_s[hh], precision=HIGHEST), 0.0)
        o_s = osum / jnp.maximum(ltot, 1e-30)
        win_t = win_ref[0]
        nw = win_t.shape[1]
        lane_w = _iota((1, nw), 1)
        sw = _dot(q4b, win_t[0:LANES, :].astype(BF16))
        validw = (past - nw + lane_w) > past - WINDOW
        neww = newwin_ref[0]
        sw_new = jnp.sum(q4 * neww[:, 0:LANES], axis=1, keepdims=True)
        mw = jnp.maximum(jnp.max(jnp.where(validw, sw, NEG), axis=1, keepdims=True), sw_new)
        pw = jnp.where(validw, jnp.exp(sw - mw), 0.0)
        pn = jnp.exp(sw_new - mw)
        zw = jnp.sum(pw, axis=1, keepdims=True) + pn
        o_w = (_dot(pw.astype(BF16), win_t[LANES:2 * LANES, :].astype(BF16), NT) + pn * neww[:, LANES:2 * LANES]) / zw
        gg = _sigmoid(gq_ref[0])
        o_ref[0] = o_c * gg[:, 0:1] + o_s * gg[:, 1:2] + o_w * gg[:, 2:3]
        wo_ref[0] = jnp.where(lane_w == nw - 1, newcol_ref[0], pltpu.roll(win_t, nw - 1, 1))


def _nsa_sample(pt, q4, gq, newkv, newwin, newcol, win_t, win_off, wall, pool_t, pp):
    bd, npages = pt.shape
    nw = win_t.shape[2]
    nb = 2 * npages
    nbp = -(-nb // LANES) * LANES
    n_top = min(N_SELECT, nb + 1)
    per_b = lambda b, i, pt_ref: (b, 0, 0)
    const = lambda b, i, pt_ref: (0, 0)

    def page_map(kk):
        return lambda b, i, pt_ref: (pt_ref[b, i * pp + kk], 0, 0)

    grid_spec = pltpu.PrefetchScalarGridSpec(
        num_scalar_prefetch=1,
        grid=(bd, npages // pp),
        in_specs=[pl.BlockSpec((1, 8, LANES), per_b),
                  pl.BlockSpec((1, 8, LANES), per_b),
                  pl.BlockSpec((1, 1, 4 * LANES), per_b),
                  pl.BlockSpec((1, 1, 2 * LANES), per_b),
                  pl.BlockSpec((1, 2 * LANES, 1), per_b),
                  pl.BlockSpec((1, 2 * LANES, nw), lambda b, i, pt_ref: (win_off + b, 0, 0)),
                  pl.BlockSpec(wall.shape, const)]
                 + [pl.BlockSpec((1, 4 * LANES, LANES), page_map(kk)) for kk in range(pp)],
        out_specs=[pl.BlockSpec((1, 8, LANES), per_b),
                   pl.BlockSpec((1, 2 * LANES, nw), per_b)],
        scratch_shapes=[pltpu.VMEM((npages * 8, 2 * LANES), F32),
                        pltpu.VMEM((npages * 8, 2 * LANES), F32),
                        pltpu.VMEM((8, nbp), F32),
                        pltpu.VMEM((8, nbp), F32),
                        pltpu.VMEM((N_HEADS, nbp, LANES), F32)])
    return pl.pallas_call(
        functools.partial(_nsa_sample_kernel, pp=pp, npages=npages, n_top=n_top),
        grid_spec=grid_spec,
        out_shape=[jax.ShapeDtypeStruct((bd, 8, LANES), F32),
                   jax.ShapeDtypeStruct((bd, 2 * LANES, nw), F32)],
        compiler_params=_cparams(("parallel", "arbitrary")), name="nsa_sample",
    )(pt, q4, gq, newkv, newwin, newcol, win_t, wall, *([pool_t] * pp))


def _out_kernel(x_ref, ya_ref, yb_ref, yc_ref, yd_ref, wo_ref, g2_ref, wq_ref, k1_ref, k2_ref,
                xn_ref, ht_ref, s1_ref, s2_ref, *, nh, dq):
    acc = x_ref[...]
    for idx, y_ref in enumerate((ya_ref, yb_ref, yc_ref, yd_ref)):
        acc = acc + _dot(y_ref[...].astype(BF16), wo_ref[idx * MIX_W:(idx + 1) * MIX_W, :])
    xn_ref[...] = acc
    h2 = _rmsnorm(acc, g2_ref[...])
    ht_ref[...] = h2.T.astype(BF16)
    q = _dot(h2.astype(BF16), wq_ref[...])
    half = dq // 2
    for h in range(nh):
        s1_ref[h] = _dot(k1_ref[h], q[:, h * dq:h * dq + half].astype(BF16), NT)
        s2_ref[h] = _dot(k2_ref[h], q[:, h * dq + half:(h + 1) * dq].astype(BF16), NT)


def _out_proj(x, ya, yb, yc, yd, wo, g2, wq, k1, k2, tm):
    n, d = x.shape
    nh, nk, half = k1.shape
    row = lambda i: (i, 0)
    c2 = lambda i: (0, 0)
    return pl.pallas_call(
        functools.partial(_out_kernel, nh=nh, dq=2 * half),
        grid=(n // tm,),
        in_specs=[pl.BlockSpec((tm, d), row)] + [pl.BlockSpec((tm, MIX_W), row)] * 4
                 + [pl.BlockSpec(wo.shape, c2), pl.BlockSpec((1, d), c2), pl.BlockSpec(wq.shape, c2),
                    pl.BlockSpec(k1.shape, lambda i: (0, 0, 0)), pl.BlockSpec(k2.shape, lambda i: (0, 0, 0))],
        out_specs=[pl.BlockSpec((tm, d), row),
                   pl.BlockSpec((d, tm), lambda i: (0, i)),
                   pl.BlockSpec((nh, nk, tm), lambda i: (0, 0, i)),
                   pl.BlockSpec((nh, nk, tm), lambda i: (0, 0, i))],
        out_shape=[jax.ShapeDtypeStruct((n, d), F32),
                   jax.ShapeDtypeStruct((d, n), BF16),
                   jax.ShapeDtypeStruct((nh, nk, n), F32),
                   jax.ShapeDtypeStruct((nh, nk, n), F32)],
        compiler_params=_cparams(("parallel",)), name="out_proj",
    )(x, ya, yb, yc, yd, wo, g2, wq, k1, k2)


_PAIR_COUNTS = tuple(PEER_TOPK // (i + 1) for i in range(PEER_TOPK))
_PAIR_ROWS = -(-sum(_PAIR_COUNTS) // 8) * 8


def _gate_kernel(s1_ref, s2_ref, o_ref, v1_s, v2_s, c_s):
    x1 = s1_ref[0]
    x2 = s2_ref[0]
    for i in range(PEER_TOPK):
        m1 = jnp.max(x1, axis=0, keepdims=True)
        m2 = jnp.max(x2, axis=0, keepdims=True)
        v1_s[i:i + 1, :] = m1
        v2_s[i:i + 1, :] = m2
        x1 = jnp.where(x1 == m1, NEG, x1)
        x2 = jnp.where(x2 == m2, NEG, x2)
    v1 = v1_s[...]
    v2 = v2_s[...]
    off = 0
    for i, cnt in enumerate(_PAIR_COUNTS):
        c_s[off:off + cnt, :] = v1[i:i + 1, :] + v2[0:cnt, :]
        off += cnt
    if off < _PAIR_ROWS:
        c_s[off:_PAIR_ROWS, :] = jnp.full((_PAIR_ROWS - off, v1.shape[1]), NEG, F32)
    c = c_s[...]
    mx = v1[0:1, :] + v2[0:1, :]
    z = jnp.zeros_like(mx)
    m = mx
    for i in range(PEER_TOPK):
        m = jnp.max(c, axis=0, keepdims=True)
        z = z + jnp.exp(m - mx)
        c = jnp.where(c == m, NEG, c)
    o_ref[0, 0:1, :] = m
    o_ref[0, 1:2, :] = v1[0:1, :]
    o_ref[0, 2:3, :] = v2[0:1, :]
    o_ref[0, 3:4, :] = 1.0 / z
    o_ref[0, 4:8, :] = jnp.zeros((4, mx.shape[1]), F32)


def _gate(s1t, s2t, tn):
    nh, nk, n = s1t.shape
    blk = lambda h, i: (h, 0, i)
    return pl.pallas_call(
        _gate_kernel,
        grid=(nh, n // tn),
        in_specs=[pl.BlockSpec((1, nk, tn), blk), pl.BlockSpec((1, nk, tn), blk)],
        out_specs=pl.BlockSpec((1, 8, tn), blk),
        out_shape=jax.ShapeDtypeStruct((nh, 8, n), F32),
        scratch_shapes=[pltpu.VMEM((PEER_TOPK, tn), F32), pltpu.VMEM((PEER_TOPK, tn), F32),
                        pltpu.VMEM((_PAIR_ROWS, tn), F32)],
        compiler_params=_cparams(("parallel", "parallel")), name="peer_gate",
    )(s1t, s2t)


def _peer_kernel(ht_ref, u_ref, vt_ref, s1_ref, s2_ref, sm_ref, xn_ref, fg_ref, o_ref,
                 acc_ref, e2_ref, c_ref, a_ref, f_ref, row_ref, *, nh, nk, apc, final):
    j = pl.program_id(1)
    tm = ht_ref.shape[1]

    @pl.when(j == 0)
    def _():
        acc_ref[...] = jnp.zeros(acc_ref.shape, F32)
        for h in range(nh):
            e2_ref[h] = jnp.exp(s2_ref[h] - sm_ref[h, 2:3, :])
            c_ref[h] = jnp.exp(s1_ref[h] - sm_ref[h, 1:2, :]) * sm_ref[h, 3:4, :]

    a_ref[...] = _dot(u_ref[...], ht_ref[...])
    for aa in range(apc):
        a = j * apc + aa
        rows = slice(aa * nk, (aa + 1) * nk)
        for h in range(nh):
            row_ref[h:h + 1, :] = s1_ref[h, pl.ds(a, 1), :]
            row_ref[nh + h:nh + h + 1, :] = c_ref[h, pl.ds(a, 1), :]
        for ts in range(tm // LANES):
            cols = slice(ts * LANES, (ts + 1) * LANES)
            wgt = jnp.zeros((nk, LANES), F32)
            for h in range(nh):
                score = row_ref[h:h + 1, cols] + s2_ref[h, :, cols]
                wgt = wgt + jnp.where(score >= sm_ref[h, 0:1, cols], e2_ref[h, :, cols] * row_ref[nh + h:nh + h + 1, cols], 0.0)
            f_ref[rows, cols] = (wgt * _gelu(a_ref[rows, cols])).astype(BF16)
    acc_ref[...] += _dot(vt_ref[...], f_ref[...])

    @pl.when(j == pl.num_programs(1) - 1)
    def _():
        out = xn_ref[...] + acc_ref[...].T
        if final:
            out = _rmsnorm(out, fg_ref[...])
        o_ref[...] = out


def _peer(ht, u, vt, s1t, s2t, sm, xn, fg, tm, apc, final):
    d, n = ht.shape
    nh, nk, _ = s1t.shape
    ne = u.shape[0]
    te = apc * nk
    tok3 = lambda i, j: (0, 0, i)
    return pl.pallas_call(
        functools.partial(_peer_kernel, nh=nh, nk=nk, apc=apc, final=final),
        grid=(n // tm, ne // te),
        in_specs=[pl.BlockSpec((d, tm), lambda i, j: (0, i)),
                  pl.BlockSpec((te, d), lambda i, j: (j, 0)),
                  pl.BlockSpec((d, te), lambda i, j: (0, j)),
                  pl.BlockSpec((nh, nk, tm), tok3),
                  pl.BlockSpec((nh, nk, tm), tok3),
                  pl.BlockSpec((nh, 8, tm), tok3),
                  pl.BlockSpec((tm, d), lambda i, j: (i, 0)),
                  pl.BlockSpec((1, d), lambda i, j: (0, 0))],
        out_specs=pl.BlockSpec((tm, d), lambda i, j: (i, 0)),
        out_shape=jax.ShapeDtypeStruct((n, d), F32),
        scratch_shapes=[pltpu.VMEM((d, tm), F32), pltpu.VMEM((nh, nk, tm), F32), pltpu.VMEM((nh, nk, tm), F32),
                        pltpu.VMEM((te, tm), F32), pltpu.VMEM((te, tm), BF16), pltpu.VMEM((2 * nh, tm), F32)],
        compiler_params=_cparams(("parallel", "arbitrary")), name="peer",
    )(ht, u, vt, s1t, s2t, sm, xn, fg)


def _rope_tables(pos):
    half = HEAD_DIM // 2
    inv_freq = ROPE_THETA ** (-jnp.arange(half, dtype=F32) / half)
    ang = pos.astype(F32)[:, None] * inv_freq[None, :]
    cos = jnp.cos(ang)
    sin = jnp.sin(ang)
    return jnp.tile(jnp.concatenate([cos, cos], axis=1), (1, 2)), jnp.tile(jnp.concatenate([-sin, sin], axis=1), (1, 2))


def _pad_rows(x, rows):
    return jnp.pad(x, ((0, rows - x.shape[0]),) + ((0, 0),) * (x.ndim - 1))


def _token_tile(n, pref):
    return pref if n % pref == 0 else n


def _page_pool_weights(wk, wv):
    wset = jnp.stack([wk[:, 0], wk[:, 1], wv[:, 0], wv[:, 1]]).reshape(4, 2, CMP_STRIDE)
    per_pos = jnp.tile(wset, (1, 1, LANES // CMP_STRIDE))
    chunk_of_pos = (jnp.arange(LANES) // CMP_STRIDE)[None, :] == jnp.arange(LANES // CMP_STRIDE)[:, None]
    return (per_pos[:, :, None, :] * chunk_of_pos[None, None].astype(F32)).reshape(64, LANES)


def kernel(x_prompt, x_sample, cache_nsa_kv, state_nsa_win, state_mlstm_C, state_mlstm_n, state_mlstm_m, state_conv, page_table, norm1_g, norm2_g, final_norm_g, w_in, w_out, mlstm_b_i, mlstm_b_f, mlstm_norm_g, conv_w, conv_b, conv_norm_g, conv_norm_b, nsa_cmp_wk, nsa_cmp_wv, gmlp_norm_g, gmlp_norm_b, gmlp_ws, gmlp_bs, peer_wq, peer_k1, peer_k2, peer_u, peer_v):
    depth = w_in.shape[0]
    bp, s_len, d = x_prompt.shape
    bd, t_dec, _ = x_sample.shape
    n_pool = cache_nsa_kv.shape[1]
    npages = page_table.shape[1]
    past = npages * cache_nsa_kv.shape[2]
    n_win = state_nsa_win.shape[2]
    assert t_dec == 1 and cache_nsa_kv.shape[2] == LANES and past % SEL_BLOCK == 0
    assert s_len >= WINDOW + 128 and n_win == WINDOW
    np_tok = bp * s_len
    ns_pad = LANES
    assert bd <= ns_pad

    xp = x_prompt.reshape(np_tok, d)
    xs = _pad_rows(x_sample.reshape(bd, d), ns_pad)
    cos_p, sin_p = _rope_tables(jnp.arange(s_len))
    cos_s, sin_s = _rope_tables(jnp.full((ns_pad,), past))
    pool_t = cache_nsa_kv.reshape(depth * n_pool, LANES, 4 * LANES).transpose(0, 2, 1)
    win_t = state_nsa_win.reshape(depth * bd, n_win, 2 * LANES).transpose(0, 2, 1)
    eye_h = jnp.eye(N_HEADS, dtype=F32)

    tm_in = _token_tile(s_len, 256)
    tm_out = _token_tile(np_tok, 256)
    tm_peer = _token_tile(np_tok, 512)
    nk = peer_k1.shape[2]
    apc = min(8, nk)

    new_p, new_s = [], []
    for l in range(depth):
        wi = w_in[l]
        w_perm = jnp.concatenate([wi[:, 0:1024], wi[:, 1032:2568], wi[:, 2580:3092], wi[:, 1024:1032],
                                  wi[:, 2568:2580], jnp.zeros((d, 108), F32)], axis=1).astype(BF16)
        g1 = norm1_g[l].reshape(1, d)
        g2 = norm2_g[l].reshape(1, d)
        gate_bias = jnp.concatenate([mlstm_b_i[l], mlstm_b_f[l], jnp.zeros((LANES - 2 * N_HEADS,), F32)]).reshape(1, LANES)
        mng = mlstm_norm_g[l].reshape(1, MIX_W)
        cw = conv_w[l]
        cbias = conv_b[l].reshape(1, MIX_W)
        cg = conv_norm_g[l].reshape(1, MIX_W)
        cbeta = conv_norm_b[l].reshape(1, MIX_W)
        w32 = jnp.concatenate([jnp.repeat(nsa_cmp_wk[l], HEAD_DIM, axis=1), jnp.repeat(nsa_cmp_wv[l], HEAD_DIM, axis=1)], axis=1)
        wlo, whi = w32[:CMP_STRIDE], w32[CMP_STRIDE:]
        wall = _page_pool_weights(nsa_cmp_wk[l], nsa_cmp_wv[l])
        gg = gmlp_norm_g[l].reshape(1, MIX_W)
        gbeta = gmlp_norm_b[l].reshape(1, MIX_W)
        gws = gmlp_ws[l]
        gbsb = jnp.repeat(gmlp_bs[l].T, HEAD_DIM, axis=1)
        wo = w_out[l].astype(BF16)
        wq = peer_wq[l].astype(BF16)
        k1 = peer_k1[l].astype(BF16)
        k2 = peer_k2[l].astype(BF16)
        ub = peer_u[l].astype(BF16)
        vtb = peer_v[l].astype(BF16).T
        fg = final_norm_g.reshape(1, d)
        final = l == depth - 1

        oa, ob, oq, okv, owin, okaug, ovsb, owinb, od, osm = _in_proj(xp, g1, w_perm, cos_p, sin_p, tm_in)
        ya, c_p, n_p, m_p = _mlstm(oa.reshape(bp, s_len, 1024), osm.reshape(bp, s_len, LANES), gate_bias, mng,
                                   jnp.zeros((bp, MIX_W, MIX_W), F32), jnp.zeros((bp, 1, MIX_W), F32),
                                   jnp.zeros((bp, 1, MIX_W), F32), MLSTM_CHUNK, MLSTM_CHUNK, bp)
        yb, conv_p = _conv(ob.reshape(bp, s_len, 2 * MIX_W), jnp.zeros((bp, CONV_PAD, MIX_W), F32),
                           cw, cbias, cg, cbeta, 512, 512)
        cb = _cmp(okv.reshape(bp, s_len, 4 * LANES), wlo, whi)
        yc = _nsa_prompt(oq.reshape(bp, s_len, MIX_W), osm.reshape(bp, s_len, LANES), cb,
                         okaug.reshape(bp, s_len, 2 * LANES), ovsb.reshape(bp, s_len, LANES),
                         owinb.reshape(bp, s_len, 2 * LANES), 128)
        yd, _ = _gmlp(od, gg, gbeta, gws, gbsb, 4)
        xn, ht, s1t, s2t = _out_proj(xp, ya.reshape(np_tok, MIX_W), yb.reshape(np_tok, MIX_W),
                                     yc.reshape(np_tok, MIX_W), yd, wo, g2, wq, k1, k2, tm_out)
        sm = _gate(s1t, s2t, _token_tile(np_tok, 512))
        xp = _peer(ht, ub, vtb, s1t, s2t, sm, xn, fg, tm_peer, apc, final)
        new_p.append((okv.reshape(bp, s_len, 4, NSA_KV, HEAD_DIM),
                      owin.reshape(bp, s_len, 2, NSA_KV, HEAD_DIM)[:, s_len - n_win:],
                      jnp.stack([c_p[:, h * HEAD_DIM:(h + 1) * HEAD_DIM, h * HEAD_DIM:(h + 1) * HEAD_DIM]
                                 for h in range(N_HEADS)], axis=1),
                      n_p.reshape(bp, N_HEADS, HEAD_DIM),
                      m_p[:, 0, ::HEAD_DIM],
                      conv_p[:, CONV_PAD - (CONV_W - 1):]))

        sa, sb, sq, skv, swin, _, _, _, sd, ssm = _in_proj(xs, g1, w_perm, cos_s, sin_s, ns_pad)
        rows8 = lambda t: jnp.pad(t[:bd, None, :], ((0, 0), (0, 7), (0, 0)))
        c0 = jnp.einsum('bhvk,hg->bhvgk', state_mlstm_C[l], eye_h).reshape(bd, MIX_W, MIX_W)
        n0 = state_mlstm_n[l].reshape(bd, 1, MIX_W)
        m0 = jnp.repeat(state_mlstm_m[l], HEAD_DIM, axis=-1).reshape(bd, 1, MIX_W)
        bb_s = 2 if bd % 2 == 0 else 1
        ya_s, c_s, n_s, m_s = _mlstm(rows8(sa), rows8(ssm), gate_bias, mng, c0, n0, m0, 8, 1, bb_s)
        prefix = jnp.pad(state_conv[l], ((0, 0), (CONV_PAD - (CONV_W - 1), 0), (0, 0)))
        yb_s, conv_s = _conv(rows8(sb), prefix, cw, cbias, cg, cbeta, 8, 1)
        zd_s = jnp.pad(sd[:bd, None, :], ((0, 0), (0, GMLP_CHUNK - 1), (0, 0))).reshape(bd * GMLP_CHUNK, 2 * MIX_W)
        yd_s, v_s = _gmlp(zd_s, gg, gbeta, gws, gbsb, 1)
        q_heads = sq[:bd].reshape(bd, NSA_KV, 2, 1, HEAD_DIM)
        q4 = (q_heads * jnp.eye(NSA_KV, dtype=F32)[None, :, None, :, None]).reshape(bd, N_HEADS, LANES)
        q4 = jnp.pad(q4, ((0, 0), (0, 8 - N_HEADS), (0, 0)))
        gq = jnp.pad(ssm[:bd, 2 * N_HEADS:2 * N_HEADS + 3 * N_HEADS].reshape(bd, N_HEADS, 3),
                     ((0, 0), (0, 8 - N_HEADS), (0, LANES - 3)))
        pp = 8 if npages % 8 == 0 else 1
        o8, win_s = _nsa_sample(page_table + l * n_pool, q4, gq, skv[:bd, None, :], swin[:bd, None, :],
                                swin[:bd, :, None], win_t, l * bd, wall, pool_t, pp)
        yc_s = jnp.concatenate([o8[:, h, (h // 2) * HEAD_DIM:(h // 2 + 1) * HEAD_DIM] for h in range(N_HEADS)], axis=1)
        xn_s, ht_s, s1t_s, s2t_s = _out_proj(xs, _pad_rows(ya_s[:, 0], ns_pad), _pad_rows(yb_s[:, 0], ns_pad),
                                             _pad_rows(yc_s, ns_pad),
                                             _pad_rows(yd_s.reshape(bd, GMLP_CHUNK, MIX_W)[:, 0], ns_pad),
                                             wo, g2, wq, k1, k2, ns_pad)
        sm_s = _gate(s1t_s, s2t_s, ns_pad)
        xs = _peer(ht_s, ub, vtb, s1t_s, s2t_s, sm_s, xn_s, fg, ns_pad, apc, final)
        new_s.append((skv[:bd].reshape(bd, 1, 4, NSA_KV, HEAD_DIM),
                      win_s.transpose(0, 2, 1).reshape(bd, n_win, 2, NSA_KV, HEAD_DIM),
                      jnp.stack([c_s[:, h * HEAD_DIM:(h + 1) * HEAD_DIM, h * HEAD_DIM:(h + 1) * HEAD_DIM]
                                 for h in range(N_HEADS)], axis=1),
                      n_s.reshape(bd, N_HEADS, HEAD_DIM),
                      m_s[:, 0, ::HEAD_DIM],
                      conv_s[:, CONV_PAD - (CONV_W - 1):],
                      v_s.reshape(bd, GMLP_CHUNK, MIX_W)[:, 0:1]))

    def stack(states, i):
        return jnp.stack([st[i] for st in states])

    y_prompt = xp.reshape(bp, s_len, d)
    y_sample = xs[:bd].reshape(bd, 1, d)
    return (y_prompt, y_sample,
            stack(new_p, 0), stack(new_s, 0),
            stack(new_p, 1), stack(new_s, 1),
            stack(new_p, 2), stack(new_s, 2),
            stack(new_p, 3), stack(new_s, 3),
            stack(new_p, 4), stack(new_s, 4),
            stack(new_p, 5), stack(new_s, 5),
            stack(new_s, 6))
```

```python
import functools

import jax
import jax.numpy as jnp
from jax import lax
from jax.experimental import pallas as pl
from jax.experimental.pallas import tpu as pltpu

F32 = jnp.float32
BF16 = jnp.bfloat16
HIGHEST = lax.Precision.HIGHEST

HEAD_DIM = 64
N_HEADS = 4
MIX_W = 256
NSA_KV = 2
NORM_EPS = 1e-6
NEG = -1e30
MLSTM_CHUNK = 128
CONV_W = 31
CONV_PAD = 32
CMP_STRIDE = 16
CMP_LEN = 32
SEL_BLOCK = 64
N_SELECT = 16
SEL_FORCE = 1e9
WINDOW = 512
ROPE_THETA = 10000.0
ATTN_SCALE = HEAD_DIM ** -0.5
GMLP_CHUNK = 128
PEER_TOPK = 16
PEER_ROWS_PER_ITER = 1
LANES = 128
VMEM_LIMIT = 48 * 1024 * 1024

NT = (((1,), (1,)), ((), ()))
TN = (((0,), (0,)), ((), ()))


def _cparams(sem):
    return pltpu.CompilerParams(dimension_semantics=sem, vmem_limit_bytes=VMEM_LIMIT)


def _iota(shape, dim):
    return lax.broadcasted_iota(jnp.int32, shape, dim)


def _sigmoid(x):
    return 1.0 / (1.0 + jnp.exp(-x))


def _gelu(x):
    c1 = -2.0 * 0.7978845608028654
    c2 = c1 * 0.044715
    return x / (1.0 + jnp.exp(x * (c1 + c2 * (x * x))))


def _dot(a, b, dims=None, precision=None):
    if dims is None:
        return jnp.dot(a, b, preferred_element_type=F32, precision=precision)
    return lax.dot_general(a, b, dims, preferred_element_type=F32, precision=precision)


def _block_ones(n, blk):
    return (_iota((n, n), 0) // blk == _iota((n, n), 1) // blk).astype(F32)


def _group_layernorm(x, ee, width):
    mu = _dot(x, ee, precision=HIGHEST) * (1.0 / width)
    d = x - mu
    var = _dot(d * d, ee, precision=HIGHEST) * (1.0 / width)
    return d * lax.rsqrt(var + NORM_EPS)


def _rmsnorm(x, g):
    return x * lax.rsqrt(jnp.mean(x * x, axis=-1, keepdims=True) + NORM_EPS) * g


def _masked_softmax(s, valid):
    sm = jnp.where(valid, s, NEG)
    p = jnp.where(valid, jnp.exp(sm - jnp.max(sm, axis=-1, keepdims=True)), 0.0)
    return p / jnp.maximum(jnp.sum(p, axis=-1, keepdims=True), 1e-30)


def _rope(x, cos, sin_signed):
    w = x.shape[1]
    fwd = pltpu.roll(x, w - HEAD_DIM // 2, 1)
    bwd = pltpu.roll(x, HEAD_DIM // 2, 1)
    first = (_iota(x.shape, 1) % HEAD_DIM) < HEAD_DIM // 2
    return x * cos + jnp.where(first, fwd, bwd) * sin_signed


def _in_kernel(x_ref, g_ref, w_ref, wsmall_ref, cos_ref, sin_ref,
               oa_ref, ob_ref, oq_ref, okv_ref, owin_ref, okaug_ref, ovsb_ref, owinb_ref, od_ref, os_ref, *, npb):
    tm = x_ref.shape[0]
    y = _rmsnorm(x_ref[...], g_ref[...])
    z = _dot(y.astype(BF16), w_ref[...])
    os_ref[...] = _dot(y, wsmall_ref[...], precision=HIGHEST)
    oa_ref[...] = z[:, 0:1024]
    ob_ref[...] = z[:, 1024:1536]
    cos = cos_ref[...]
    sin = sin_ref[...]
    cos2 = jnp.concatenate([cos, cos], axis=1)
    sin2 = jnp.concatenate([sin, sin], axis=1)
    oq_ref[...] = _rope(z[:, 1536:1792], cos2, sin2) * ATTN_SCALE
    kc = _rope(z[:, 1792:1920], cos, sin)
    vc = z[:, 1920:2048]
    ks = _rope(z[:, 2048:2176], cos, sin)
    vs = z[:, 2176:2304]
    kw = _rope(z[:, 2304:2432], cos, sin)
    vw = z[:, 2432:2560]
    okv_ref[...] = jnp.concatenate([kc, vc, ks, vs], axis=1)
    owin = jnp.concatenate([kw, vw], axis=1)
    owin_ref[...] = owin
    pos = (pl.program_id(0) % npb) * tm + _iota((tm, LANES), 0)
    onehot = jnp.where(_iota((tm, LANES), 1) == pos // SEL_BLOCK, 1.0, 0.0)
    okaug_ref[...] = jnp.concatenate([ks, onehot], axis=1).astype(BF16)
    ovsb_ref[...] = vs.astype(BF16)
    owinb_ref[...] = owin.astype(BF16)
    od_ref[...] = z[:, 2560:3072]


def _in_proj(x, g, w, wsmall, cos_t, sin_t, tm):
    n, d = x.shape
    npb = cos_t.shape[0] // tm
    widths = (1024, 512, 256, 512, 256, 256, 128, 256, 512, 128)
    dtypes = (F32, F32, F32, F32, F32, BF16, BF16, BF16, F32, F32)
    row = lambda i: (i, 0)
    return pl.pallas_call(
        functools.partial(_in_kernel, npb=npb),
        grid=(n // tm,),
        in_specs=[pl.BlockSpec((tm, d), row),
                  pl.BlockSpec((1, d), lambda i: (0, 0)),
                  pl.BlockSpec(w.shape, lambda i: (0, 0)),
                  pl.BlockSpec(wsmall.shape, lambda i: (0, 0)),
                  pl.BlockSpec((tm, LANES), lambda i: (i % npb, 0)),
                  pl.BlockSpec((tm, LANES), lambda i: (i % npb, 0))],
        out_specs=[pl.BlockSpec((tm, wd), row) for wd in widths],
        out_shape=[jax.ShapeDtypeStruct((n, wd), dt) for wd, dt in zip(widths, dtypes)],
        compiler_params=_cparams(("parallel",)), name="in_proj",
    )(x, g, w, wsmall, cos_t, sin_t)


def _mlstm_kernel(za_ref, zs_ref, bias_ref, ng_ref, c0_ref, n0_ref, m0_ref,
                  y_ref, c_ref, n_ref, m_ref, *, L, t_valid, bb):
    @pl.when(pl.program_id(1) == 0)
    def _():
        c_ref[...] = c0_ref[...]
        n_ref[...] = n0_ref[...]
        m_ref[...] = m0_ref[...]

    head_of_lane = _iota((1, MIX_W), 1) // HEAD_DIM
    causal = _iota((L, L), 0) >= _iota((L, L), 1)
    tril = causal.astype(F32)
    ee = _block_ones(MIX_W, HEAD_DIM)
    bdiag = ee > 0.5
    row128 = _iota((L, LANES), 0)
    lane128 = _iota((L, LANES), 1)
    for bi in range(bb):
        za = za_ref[bi]
        q = za[:, 0:256]
        k = za[:, 256:512] * (HEAD_DIM ** -0.5)
        v = za[:, 512:768]
        o = za[:, 768:1024]
        gi = zs_ref[bi] + bias_ref[...]
        ls = jnp.minimum(gi, 0.0) - jnp.log(1.0 + jnp.exp(-jnp.abs(gi)))
        if t_valid < L:
            live = row128 < t_valid
            ig = jnp.where(live, gi, NEG)
            ls = jnp.where(live, ls, 0.0)
        else:
            ig = gi
        gmat = jnp.where(lane128 < N_HEADS, ig, 0.0)
        lfm = jnp.where((lane128 >= N_HEADS) & (lane128 < 2 * N_HEADS), ls, 0.0)
        fc = _dot(tril, lfm, precision=HIGHEST)
        g_t = gmat.T
        f_t = fc.T
        mrow = m_ref[bi]
        nrow = n_ref[bi]
        cm = c_ref[bi]
        qb = q.astype(BF16)
        kb = k.astype(BF16)
        vb = v.astype(BF16)
        num = jnp.zeros((L, MIX_W), F32)
        decay_b = jnp.zeros((L, MIX_W), F32)
        dens_b = jnp.zeros((L, MIX_W), F32)
        mt_b = jnp.zeros((L, MIX_W), F32)
        w_b = jnp.zeros((L, MIX_W), F32)
        cd_b = jnp.zeros((1, MIX_W), F32)
        mnew_b = jnp.zeros((1, MIX_W), F32)
        for h in range(N_HEADS):
            hm = head_of_lane == h
            f_col = fc[:, N_HEADS + h:N_HEADS + h + 1]
            ig_col = gmat[:, h:h + 1]
            f_row = f_t[N_HEADS + h:N_HEADS + h + 1, :]
            ig_row = g_t[h:h + 1, :]
            mp = mrow[:, h * HEAD_DIM:h * HEAD_DIM + 1]
            dmat = jnp.where(causal, (f_col - f_row) + ig_row, NEG)
            m_inter = mp + f_col
            m_t = jnp.maximum(m_inter, jnp.max(dmat, axis=1, keepdims=True))
            d_exp = jnp.exp(dmat - m_t)
            decay = jnp.exp(m_inter - m_t)
            qh = jnp.where(hm, q, 0.0).astype(BF16)
            s = _dot(qh, kb, NT) * d_exp
            num = jnp.where(hm, _dot(s.astype(BF16), vb), num)
            dens_b = jnp.where(hm, jnp.sum(s, axis=1, keepdims=True), dens_b)
            decay_b = jnp.where(hm, decay, decay_b)
            mt_b = jnp.where(hm, m_t, mt_b)
            f_last = f_col[L - 1:L, :]
            m_new = m_t[L - 1:L, :]
            w_b = jnp.where(hm, jnp.exp((f_last - f_col) + ig_col - m_new), w_b)
            cd_b = jnp.where(hm, jnp.exp(mp + f_last - m_new), cd_b)
            mnew_b = jnp.where(hm, m_new, mnew_b)
        inter = _dot(qb, cm.astype(BF16), NT)
        nq_b = _dot(q * nrow, ee, precision=HIGHEST)
        hnum = num + decay_b * inter
        den_b = dens_b + decay_b * nq_b
        hh = hnum / jnp.maximum(jnp.abs(den_b), jnp.exp(-mt_b))
        hn = _group_layernorm(hh, ee, HEAD_DIM) * ng_ref[...]
        y_ref[bi] = hn * _sigmoid(o)
        kw = k * w_b
        upd = _dot(vb, kw.astype(BF16), TN)
        c_ref[bi] = jnp.where(bdiag, cd_b * cm + upd, 0.0)
        n_ref[bi] = cd_b * nrow + jnp.sum(kw, axis=0, keepdims=True)
        m_ref[bi] = mnew_b


def _mlstm(za, zs, bias, ng, c0, n0, m0, L, t_valid, bb):
    b, t, _ = za.shape
    nc = t // L
    st = lambda i, c: (i, 0, 0)
    return pl.pallas_call(
        functools.partial(_mlstm_kernel, L=L, t_valid=t_valid, bb=bb),
        grid=(b // bb, nc),
        in_specs=[pl.BlockSpec((bb, L, 1024), lambda i, c: (i, c, 0)),
                  pl.BlockSpec((bb, L, LANES), lambda i, c: (i, c, 0)),
                  pl.BlockSpec((1, LANES), lambda i, c: (0, 0)),
                  pl.BlockSpec((1, MIX_W), lambda i, c: (0, 0)),
                  pl.BlockSpec((bb, MIX_W, MIX_W), st),
                  pl.BlockSpec((bb, 1, MIX_W), st),
                  pl.BlockSpec((bb, 1, MIX_W), st)],
        out_specs=[pl.BlockSpec((bb, L, MIX_W), lambda i, c: (i, c, 0)),
                   pl.BlockSpec((bb, MIX_W, MIX_W), st),
                   pl.BlockSpec((bb, 1, MIX_W), st),
                   pl.BlockSpec((bb, 1, MIX_W), st)],
        out_shape=[jax.ShapeDtypeStruct((b, t, MIX_W), F32),
                   jax.ShapeDtypeStruct((b, MIX_W, MIX_W), F32),
                   jax.ShapeDtypeStruct((b, 1, MIX_W), F32),
                   jax.ShapeDtypeStruct((b, 1, MIX_W), F32)],
        compiler_params=_cparams(("parallel", "arbitrary")), name="mlstm",
    )(za, zs, bias, ng, c0, n0, m0)


def _conv_kernel(zb_ref, pre_ref, w_ref, cb_ref, g_ref, beta_ref, y_ref, st_ref, ext_ref, *, tt, tv):
    @pl.when(pl.program_id(1) == 0)
    def _():
        ext_ref[0:CONV_PAD, :] = pre_ref[0]

    z = zb_ref[0]
    u = z[:, :MIX_W] * _sigmoid(z[:, MIX_W:])
    ext_ref[CONV_PAD:CONV_PAD + tt, :] = u
    off = CONV_PAD - (CONV_W - 1)
    acc = jnp.zeros((tt, MIX_W), F32)
    for j in range(CONV_W):
        acc = acc + w_ref[j:j + 1, :] * ext_ref[off + j:off + j + tt, :]
    ee = _block_ones(MIX_W, HEAD_DIM)
    y = _group_layernorm(acc + cb_ref[...], ee, HEAD_DIM) * g_ref[...] + beta_ref[...]
    y_ref[0] = y * _sigmoid(y)
    st_ref[0] = ext_ref[tv:tv + CONV_PAD, :]
    ext_ref[0:CONV_PAD, :] = ext_ref[tt:tt + CONV_PAD, :]


def _conv(zb, prefix, w, cb, g, beta, tt, tv):
    b, t, _ = zb.shape
    vec = lambda i, j: (0, 0)
    return pl.pallas_call(
        functools.partial(_conv_kernel, tt=tt, tv=tv),
        grid=(b, t // tt),
        in_specs=[pl.BlockSpec((1, tt, 2 * MIX_W), lambda i, j: (i, j, 0)),
                  pl.BlockSpec((1, CONV_PAD, MIX_W), lambda i, j: (i, 0, 0)),
                  pl.BlockSpec((CONV_W, MIX_W), vec),
                  pl.BlockSpec((1, MIX_W), vec),
                  pl.BlockSpec((1, MIX_W), vec),
                  pl.BlockSpec((1, MIX_W), vec)],
        out_specs=[pl.BlockSpec((1, tt, MIX_W), lambda i, j: (i, j, 0)),
                   pl.BlockSpec((1, CONV_PAD, MIX_W), lambda i, j: (i, 0, 0))],
        out_shape=[jax.ShapeDtypeStruct((b, t, MIX_W), F32),
                   jax.ShapeDtypeStruct((b, CONV_PAD, MIX_W), F32)],
        scratch_shapes=[pltpu.VMEM((CONV_PAD + tt, MIX_W), F32)],
        compiler_params=_cparams(("parallel", "arbitrary")), name="conv",
    )(zb, prefix, w, cb, g, beta)


def _gmlp_kernel(zd_ref, g_ref, beta_ref, ws_ref, bsb_ref, y_ref, v_ref, *, cpb):
    z = zd_ref[...]
    ee = _block_ones(MIX_W, HEAD_DIM)
    u = _gelu(z[:, :MIX_W])
    vv = _group_layernorm(_gelu(z[:, MIX_W:]), ee, HEAD_DIM) * g_ref[...] + beta_ref[...]
    v_ref[...] = vv
    tril = _iota((GMLP_CHUNK, GMLP_CHUNK), 0) >= _iota((GMLP_CHUNK, GMLP_CHUNK), 1)
    head_of_lane = _iota((1, MIX_W), 1) // HEAD_DIM
    wm = [jnp.where(tril, ws_ref[h], 0.0).astype(BF16) for h in range(N_HEADS)]
    for c in range(cpb):
        rows = slice(c * GMLP_CHUNK, (c + 1) * GMLP_CHUNK)
        vc = vv[rows].astype(BF16)
        mixed = jnp.zeros((GMLP_CHUNK, MIX_W), F32)
        for h in range(N_HEADS):
            mixed = jnp.where(head_of_lane == h, _dot(wm[h], vc), mixed)
        y_ref[rows, :] = u[rows] * (mixed + bsb_ref[...])


def _gmlp(zd, g, beta, ws, bsb, cpb):
    n = zd.shape[0]
    rows = cpb * GMLP_CHUNK
    vec = lambda i: (0, 0)
    return pl.pallas_call(
        functools.partial(_gmlp_kernel, cpb=cpb),
        grid=(n // rows,),
        in_specs=[pl.BlockSpec((rows, 2 * MIX_W), lambda i: (i, 0)),
                  pl.BlockSpec((1, MIX_W), vec),
                  pl.BlockSpec((1, MIX_W), vec),
                  pl.BlockSpec((N_HEADS, GMLP_CHUNK, GMLP_CHUNK), lambda i: (0, 0, 0)),
                  pl.BlockSpec((GMLP_CHUNK, MIX_W), vec)],
        out_specs=[pl.BlockSpec((rows, MIX_W), lambda i: (i, 0)),
                   pl.BlockSpec((rows, MIX_W), lambda i: (i, 0))],
        out_shape=[jax.ShapeDtypeStruct((n, MIX_W), F32),
                   jax.ShapeDtypeStruct((n, MIX_W), F32)],
        compiler_params=_cparams(("parallel",)), name="gmlp",
    )(zd, g, beta, ws, bsb)


def _combine_lo_hi(lo, hi):
    r = lo.shape[0]
    nxt = pltpu.roll(hi, r - 1, 0)
    return lo + jnp.where(_iota(lo.shape, 0) < r - 1, nxt, 0.0)


def _cmp_kernel(kv_ref, wlo_ref, whi_ref, o_ref):
    x = kv_ref[0]
    x3 = x.reshape(x.shape[0] // CMP_STRIDE, CMP_STRIDE, x.shape[1])
    lo = jnp.sum(x3 * wlo_ref[...][None], axis=1)
    hi = jnp.sum(x3 * whi_ref[...][None], axis=1)
    o_ref[0] = _combine_lo_hi(lo, hi)


def _cmp(okv3, wlo, whi):
    b, s, _ = okv3.shape
    return pl.pallas_call(
        _cmp_kernel,
        grid=(b,),
        in_specs=[pl.BlockSpec((1, s, 2 * LANES), lambda i: (i, 0, 0)),
                  pl.BlockSpec((CMP_STRIDE, 2 * LANES), lambda i: (0, 0)),
                  pl.BlockSpec((CMP_STRIDE, 2 * LANES), lambda i: (0, 0))],
        out_specs=pl.BlockSpec((1, s // CMP_STRIDE, 2 * LANES), lambda i: (i, 0, 0)),
        out_shape=jax.ShapeDtypeStruct((b, s // CMP_STRIDE, 2 * LANES), F32),
        compiler_params=_cparams(("parallel",)), name="nsa_cmp",
    )(okv3, wlo, whi)


def _nsa_prompt_kernel(q_ref, gs_ref, cb_ref, ka_ref, vs_ref, win_ref, y_ref, *, tq, s_len, n_top, tk):
    start = pl.program_id(1) * tq
    nc = s_len // CMP_STRIDE
    ns = s_len // SEL_BLOCK
    q = q_ref[0]
    gl = _sigmoid(gs_ref[0])
    lo_half = _iota((tq, LANES), 1) < HEAD_DIM
    q_rows = []
    for g in range(NSA_KV):
        qpair = q[:, g * LANES:(g + 1) * LANES]
        swapped = pltpu.roll(qpair, HEAD_DIM, 1)
        if g == 0:
            q_rows += [jnp.where(lo_half, qpair, 0.0), jnp.where(lo_half, swapped, 0.0)]
        else:
            q_rows += [jnp.where(lo_half, 0.0, swapped), jnp.where(lo_half, 0.0, qpair)]
    q128 = jnp.concatenate(q_rows, axis=0)
    q128b = q128.astype(BF16)
    trow = start + _iota((tq, 1), 0)
    t2 = jnp.concatenate([trow, trow], axis=0)
    t4 = jnp.concatenate([t2, t2], axis=0)
    cb = cb_ref[0]
    sc = _dot(q128b, cb[:, 0:LANES].astype(BF16), NT)
    validc = (_iota((1, nc), 1) * CMP_STRIDE + CMP_LEN) <= (t4 + 1)
    p = _masked_softmax(sc, validc)
    o_c = _dot(p.astype(BF16), cb[:, LANES:2 * LANES].astype(BF16))
    psum = jnp.concatenate([p[0:tq] + p[tq:2 * tq], p[2 * tq:3 * tq] + p[3 * tq:4 * tq]], axis=0)
    pool4 = (_iota((nc, ns), 0) // (SEL_BLOCK // CMP_STRIDE) == _iota((nc, ns), 1)).astype(F32)
    imp = _dot(psum, pool4, precision=HIGHEST)
    jidx = _iota((1, ns), 1)
    imp = jnp.where(jidx == t2 // SEL_BLOCK, SEL_FORCE, imp)
    imp = jnp.where(jidx * SEL_BLOCK <= t2, imp, -1.0)
    if n_top < ns:
        x = jnp.concatenate([imp[0:tq].T, imp[tq:2 * tq].T], axis=1)
        jrow = _iota((ns, 2 * tq), 0).astype(F32)
        sel_t = jnp.zeros((ns, 2 * tq), F32)
        for _ in range(n_top):
            m = jnp.max(x, axis=0, keepdims=True)
            first = jnp.min(jnp.where(x == m, jrow, float(ns)), axis=0, keepdims=True)
            hit = jrow == first
            sel_t = jnp.where(hit, 1.0, sel_t)
            x = jnp.where(hit, -3.0, x)
        sel = jnp.concatenate([sel_t[:, 0:tq].T, sel_t[:, tq:2 * tq].T], axis=0)
    else:
        sel = jnp.ones((2 * tq, ns), F32)
    selneg = (sel - 1.0) * (-NEG)
    if ns < LANES:
        selneg = jnp.concatenate([selneg, jnp.zeros((2 * tq, LANES - ns), F32)], axis=1)
    selneg4 = jnp.concatenate([selneg[0:tq], selneg[0:tq], selneg[tq:2 * tq], selneg[tq:2 * tq]], axis=0)
    qaug = jnp.concatenate([q128, selneg4], axis=1).astype(BF16)

    def tile(c, carry, diagonal):
        m, l, acc = carry
        k0 = pl.multiple_of(c * tk, tk)
        s = _dot(qaug, ka_ref[0, pl.ds(k0, tk), :], NT)
        if diagonal:
            s = jnp.where(k0 + _iota((1, tk), 1) <= t4, s, NEG)
        m_new = jnp.maximum(m, jnp.max(s, axis=1, keepdims=True))
        pt = jnp.exp(s - m_new)
        alpha = jnp.exp(m - m_new)
        l = alpha * l + jnp.sum(pt, axis=1, keepdims=True)
        acc = alpha * acc + _dot(pt.astype(BF16), vs_ref[0, pl.ds(k0, tk), :])
        return m_new, l, acc

    n_tiles = (start + tq + tk - 1) // tk
    carry = (jnp.full((4 * tq, 1), NEG, F32), jnp.zeros((4 * tq, 1), F32), jnp.zeros((4 * tq, LANES), F32))
    carry = lax.fori_loop(0, n_tiles - 1, lambda c, cr: tile(c, cr, False), carry)
    _, l, acc = tile(n_tiles - 1, carry, True)
    o_s = acc / jnp.maximum(l, 1e-30)
    wl = WINDOW + tq
    w0 = pl.multiple_of(jnp.maximum(start - WINDOW, 0), tq)
    sw = _dot(q128b, win_ref[0, pl.ds(w0, wl), 0:LANES], NT)
    kposw = w0 + _iota((1, wl), 1)
    validw = (kposw <= t4) & (kposw > t4 - WINDOW)
    o_w = _dot(_masked_softmax(sw, validw).astype(BF16), win_ref[0, pl.ds(w0, wl), LANES:2 * LANES])
    heads = []
    for hh in range(N_HEADS):
        col = 2 * N_HEADS + hh * 3
        rows = slice(hh * tq, (hh + 1) * tq)
        heads.append(o_c[rows] * gl[:, col:col + 1] + o_s[rows] * gl[:, col + 1:col + 2]
                     + o_w[rows] * gl[:, col + 2:col + 3])
    y_ref[0] = jnp.concatenate([jnp.where(lo_half, heads[0], pltpu.roll(heads[1], HEAD_DIM, 1)),
                                jnp.where(lo_half, pltpu.roll(heads[2], HEAD_DIM, 1), heads[3])], axis=1)


def _nsa_prompt(oq3, os3, cb, okaug3, ovsb3, owinb3, tq):
    b, s, _ = oq3.shape
    ns = s // SEL_BLOCK
    assert ns <= LANES
    n_top = min(N_SELECT, ns)
    tk = min(512, s)
    full = lambda i, j: (i, 0, 0)
    return pl.pallas_call(
        functools.partial(_nsa_prompt_kernel, tq=tq, s_len=s, n_top=n_top, tk=tk),
        grid=(b, s // tq),
        in_specs=[pl.BlockSpec((1, tq, MIX_W), lambda i, j: (i, j, 0)),
                  pl.BlockSpec((1, tq, LANES), lambda i, j: (i, j, 0)),
                  pl.BlockSpec((1, s // CMP_STRIDE, 2 * LANES), full),
                  pl.BlockSpec((1, s, 2 * LANES), full),
                  pl.BlockSpec((1, s, LANES), full),
                  pl.BlockSpec((1, s, 2 * LANES), full)],
        out_specs=pl.BlockSpec((1, tq, MIX_W), lambda i, j: (i, j, 0)),
        out_shape=jax.ShapeDtypeStruct((b, s, MIX_W), F32),
        compiler_params=_cparams(("parallel", "arbitrary")), name="nsa_prompt",
    )(oq3, os3, cb, okaug3, ovsb3, owinb3)


def _nsa_sample_kernel(pt_ref, q4_ref, gq_ref, newkv_ref, newwin_ref, newcol_ref, win_ref, wall_ref, *rest,
                       pp, npages, n_top):
    pool_refs = rest[:pp]
    o_ref, wo_ref, lo_s, hi_s, m_s, l_s, a_s = rest[pp:]
    i = pl.program_id(1)
    nb = 2 * npages
    nbp = m_s.shape[1]
    ncp = npages * (LANES // CMP_STRIDE)
    past = npages * LANES
    q4 = q4_ref[0]
    q4b = q4.astype(BF16)

    @pl.when(i == 0)
    def _():
        m_s[...] = jnp.zeros(m_s.shape, F32)
        l_s[...] = jnp.zeros(l_s.shape, F32)
        if nb < nbp:
            a_s[...] = jnp.zeros(a_s.shape, F32)

    lane8 = _iota((8, LANES), 1)
    lanej = _iota((8, nbp), 1)
    set_of_lane = _iota((8, 2 * LANES), 1) // HEAD_DIM
    for kk in range(pp):
        page = pool_refs[kk][0]
        pidx = i * pp + kk
        pooled = _dot(wall_ref[...].astype(BF16), page[0:2 * LANES, :].astype(BF16), NT)
        lo8 = jnp.zeros((8, 2 * LANES), F32)
        hi8 = jnp.zeros((8, 2 * LANES), F32)
        for st in range(4):
            lo8 = jnp.where(set_of_lane == st, pooled[st * 16:st * 16 + 8], lo8)
            hi8 = jnp.where(set_of_lane == st, pooled[st * 16 + 8:st * 16 + 16], hi8)
        r0 = pl.multiple_of(pidx * 8, 8)
        lo_s[pl.ds(r0, 8), :] = lo8
        hi_s[pl.ds(r0, 8), :] = hi8
        ks_t = page[2 * LANES:3 * LANES, :].astype(BF16)
        vs_t = page[3 * LANES:4 * LANES, :].astype(BF16)
        s = _dot(q4b, ks_t)
        for jb in range(LANES // SEL_BLOCK):
            inb = (lane8 >= jb * SEL_BLOCK) & (lane8 < (jb + 1) * SEL_BLOCK)
            sm = jnp.where(inb, s, NEG)
            m = jnp.max(sm, axis=1, keepdims=True)
            p = jnp.where(inb, jnp.exp(sm - m), 0.0)
            lsum = jnp.sum(p, axis=1, keepdims=True)
            a = _dot(p.astype(BF16), vs_t, NT)
            j = pidx * (LANES // SEL_BLOCK) + jb
            m_s[...] = jnp.where(lanej == j, m, m_s[...])
            l_s[...] = jnp.where(lanej == j, lsum, l_s[...])
            for hh in range(N_HEADS):
                a_s[hh, pl.ds(j, 1), :] = a[hh:hh + 1, :]

    @pl.when(i == pl.num_programs(1) - 1)
    def _():
        row8 = _iota((8, 1), 0)
        cbm = _combine_lo_hi(lo_s[...], hi_s[...])
        kcb = cbm[:, 0:LANES].astype(BF16)
        vcb = cbm[:, LANES:2 * LANES].astype(BF16)
        sc = _dot(q4b, kcb, NT)
        validc = (_iota((1, ncp), 1) * CMP_STRIDE + CMP_LEN) <= past + 1
        p = _masked_softmax(sc, validc)
        o_c = _dot(p.astype(BF16), vcb)
        pg = jnp.where(row8 == 0, p[0:1] + p[1:2], jnp.where(row8 == 1, p[2:3] + p[3:4], 0.0))
        pool4 = (_iota((ncp, nbp), 0) // (SEL_BLOCK // CMP_STRIDE) == _iota((ncp, nbp), 1)).astype(F32)
        imp2 = _dot(pg, pool4, precision=HIGHEST)
        ii = _iota((nbp, nbp), 0)
        jj = _iota((nbp, nbp), 1)
        sels = []
        for g in range(NSA_KV):
            mx = jnp.broadcast_to(imp2[g:g + 1, :], (nbp, nbp))
            mt = mx.T
            beats = ((mt > mx) | ((mt == mx) & (ii < jj))) & (ii < nb)
            rank = jnp.sum(jnp.where(beats, 1.0, 0.0), axis=0, keepdims=True)
            sels.append(jnp.where((rank < n_top - 1) & (_iota((1, nbp), 1) < nb), 1.0, 0.0))
        sel8 = jnp.where(row8 < 2, sels[0], sels[1]) > 0.5
        newkv = newkv_ref[0]
        ksn = newkv[:, 2 * LANES:3 * LANES]
        vsn = newkv[:, 3 * LANES:4 * LANES]
        s_new = jnp.sum(q4 * ksn, axis=1, keepdims=True)
        mrow = m_s[...]
        m_all = jnp.maximum(jnp.max(jnp.where(sel8, mrow, NEG), axis=1, keepdims=True), s_new)
        wj = jnp.where(sel8, jnp.exp(mrow - m_all), 0.0)
        w_new = jnp.exp(s_new - m_all)
        ltot = jnp.sum(wj * l_s[...], axis=1, keepdims=True) + w_new
        osum = w_new * vsn
        for hh in range(N_HEADS):
            osum = osum + jnp.where(row8 == hh, _dot(wj, a_s[hh], precision=HIGHEST), 0.0)
        o_s = osum / jnp.maximum(ltot, 1e-30)
        win_t = win_ref[0]
        nw = win_t.shape[1]
        lane_w = _iota((1, nw), 1)
        sw = _dot(q4b, win_t[0:LANES, :].astype(BF16))
        validw = (past - nw + lane_w) > past - WINDOW
        neww = newwin_ref[0]
        sw_new = jnp.sum(q4 * neww[:, 0:LANES], axis=1, keepdims=True)
        mw = jnp.maximum(jnp.max(jnp.where(validw, sw, NEG), axis=1, keepdims=True), sw_new)
        pw = jnp.where(validw, jnp.exp(sw - mw), 0.0)
        pn = jnp.exp(sw_new - mw)
        zw = jnp.sum(pw, axis=1, keepdims=True) + pn
        o_w = (_dot(pw.astype(BF16), win_t[LANES:2 * LANES, :].astype(BF16), NT) + pn * neww[:, LANES:2 * LANES]) / zw
        gg = _sigmoid(gq_ref[0])
        o_ref[0] = o_c * gg[:, 0:1] + o_s * gg[:, 1:2] + o_w * gg[:, 2:3]
        wo_ref[0] = jnp.where(lane_w == nw - 1, newcol_ref[0], pltpu.roll(win_t, nw - 1, 1))


def _nsa_sample(pt, q4, gq, newkv, newwin, newcol, win_t, win_off, wall, pool_t, pp):
    bd, npages = pt.shape
    nw = win_t.shape[2]
    nb = 2 * npages
    nbp = -(-nb // LANES) * LANES
    n_top = min(N_SELECT, nb + 1)
    per_b = lambda b, i, pt_ref: (b, 0, 0)
    const = lambda b, i, pt_ref: (0, 0)

    def page_map(kk):
        return lambda b, i, pt_ref: (pt_ref[b, i * pp + kk], 0, 0)

    grid_spec = pltpu.PrefetchScalarGridSpec(
        num_scalar_prefetch=1,
        grid=(bd, npages // pp),
        in_specs=[pl.BlockSpec((1, 8, LANES), per_b),
                  pl.BlockSpec((1, 8, LANES), per_b),
                  pl.BlockSpec((1, 1, 4 * LANES), per_b),
                  pl.BlockSpec((1, 1, 2 * LANES), per_b),
                  pl.BlockSpec((1, 2 * LANES, 1), per_b),
                  pl.BlockSpec((1, 2 * LANES, nw), lambda b, i, pt_ref: (win_off + b, 0, 0)),
                  pl.BlockSpec(wall.shape, const)]
                 + [pl.BlockSpec((1, 4 * LANES, LANES), page_map(kk)) for kk in range(pp)],
        out_specs=[pl.BlockSpec((1, 8, LANES), per_b),
                   pl.BlockSpec((1, 2 * LANES, nw), per_b)],
        scratch_shapes=[pltpu.VMEM((npages * 8, 2 * LANES), F32),
                        pltpu.VMEM((npages * 8, 2 * LANES), F32),
                        pltpu.VMEM((8, nbp), F32),
                        pltpu.VMEM((8, nbp), F32),
                        pltpu.VMEM((N_HEADS, nbp, LANES), F32)])
    return pl.pallas_call(
        functools.partial(_nsa_sample_kernel, pp=pp, npages=npages, n_top=n_top),
        grid_spec=grid_spec,
        out_shape=[jax.ShapeDtypeStruct((bd, 8, LANES), F32),
                   jax.ShapeDtypeStruct((bd, 2 * LANES, nw), F32)],
        compiler_params=_cparams(("parallel", "arbitrary")), name="nsa_sample",
    )(pt, q4, gq, newkv, newwin, newcol, win_t, wall, *([pool_t] * pp))


def _out_kernel(x_ref, ya_ref, yb_ref, yc_ref, yd_ref, wo_ref, g2_ref, wq_ref, k1_ref, k2_ref,
                xn_ref, ht_ref, s1_ref, s2_ref, *, nh, dq):
    acc = x_ref[...]
    for idx, y_ref in enumerate((ya_ref, yb_ref, yc_ref, yd_ref)):
        acc = acc + _dot(y_ref[...].astype(BF16), wo_ref[idx * MIX_W:(idx + 1) * MIX_W, :])
    xn_ref[...] = acc
    h2 = _rmsnorm(acc, g2_ref[...])
    ht_ref[...] = h2.T.astype(BF16)
    q = _dot(h2.astype(BF16), wq_ref[...])
    half = dq // 2
    for h in range(nh):
        s1_ref[h] = _dot(k1_ref[h], q[:, h * dq:h * dq + half].astype(BF16), NT)
        s2_ref[h] = _dot(k2_ref[h], q[:, h * dq + half:(h + 1) * dq].astype(BF16), NT)


def _out_proj(x, ya, yb, yc, yd, wo, g2, wq, k1, k2, tm):
    n, d = x.shape
    nh, nk, half = k1.shape
    row = lambda i: (i, 0)
    c2 = lambda i: (0, 0)
    return pl.pallas_call(
        functools.partial(_out_kernel, nh=nh, dq=2 * half),
        grid=(n // tm,),
        in_specs=[pl.BlockSpec((tm, d), row)] + [pl.BlockSpec((tm, MIX_W), row)] * 4
                 + [pl.BlockSpec(wo.shape, c2), pl.BlockSpec((1, d), c2), pl.BlockSpec(wq.shape, c2),
                    pl.BlockSpec(k1.shape, lambda i: (0, 0, 0)), pl.BlockSpec(k2.shape, lambda i: (0, 0, 0))],
        out_specs=[pl.BlockSpec((tm, d), row),
                   pl.BlockSpec((d, tm), lambda i: (0, i)),
                   pl.BlockSpec((nh, nk, tm), lambda i: (0, 0, i)),
                   pl.BlockSpec((nh, nk, tm), lambda i: (0, 0, i))],
        out_shape=[jax.ShapeDtypeStruct((n, d), F32),
                   jax.ShapeDtypeStruct((d, n), BF16),
                   jax.ShapeDtypeStruct((nh, nk, n), F32),
                   jax.ShapeDtypeStruct((nh, nk, n), F32)],
        compiler_params=_cparams(("parallel",)), name="out_proj",
    )(x, ya, yb, yc, yd, wo, g2, wq, k1, k2)


_PAIR_COUNTS = tuple(PEER_TOPK // (i + 1) for i in range(PEER_TOPK))
_PAIR_ROWS = -(-sum(_PAIR_COUNTS) // 8) * 8


def _gate_kernel(s1_ref, s2_ref, o_ref, v1_s, v2_s, c_s):
    x1 = s1_ref[0]
    x2 = s2_ref[0]
    for i in range(PEER_TOPK):
        m1 = jnp.max(x1, axis=0, keepdims=True)
        m2 = jnp.max(x2, axis=0, keepdims=True)
        v1_s[i:i + 1, :] = m1
        v2_s[i:i + 1, :] = m2
        x1 = jnp.where(x1 == m1, NEG, x1)
        x2 = jnp.where(x2 == m2, NEG, x2)
    v1 = v1_s[...]
    v2 = v2_s[...]
    off = 0
    for i, cnt in enumerate(_PAIR_COUNTS):
        c_s[off:off + cnt, :] = v1[i:i + 1, :] + v2[0:cnt, :]
        off += cnt
    if off < _PAIR_ROWS:
        c_s[off:_PAIR_ROWS, :] = jnp.full((_PAIR_ROWS - off, v1.shape[1]), NEG, F32)
    c = c_s[...]
    mx = v1[0:1, :] + v2[0:1, :]
    z = jnp.zeros_like(mx)
    m = mx
    for i in range(PEER_TOPK):
        m = jnp.max(c, axis=0, keepdims=True)
        z = z + jnp.exp(m - mx)
        c = jnp.where(c == m, NEG, c)
    o_ref[0, 0:1, :] = m
    o_ref[0, 1:2, :] = v1[0:1, :]
    o_ref[0, 2:3, :] = v2[0:1, :]
    o_ref[0, 3:4, :] = 1.0 / z
    o_ref[0, 4:8, :] = jnp.zeros((4, mx.shape[1]), F32)


def _gate(s1t, s2t, tn):
    nh, nk, n = s1t.shape
    blk = lambda h, i: (h, 0, i)
    return pl.pallas_call(
        _gate_kernel,
        grid=(nh, n // tn),
        in_specs=[pl.BlockSpec((1, nk, tn), blk), pl.BlockSpec((1, nk, tn), blk)],
        out_specs=pl.BlockSpec((1, 8, tn), blk),
        out_shape=jax.ShapeDtypeStruct((nh, 8, n), F32),
        scratch_shapes=[pltpu.VMEM((PEER_TOPK, tn), F32), pltpu.VMEM((PEER_TOPK, tn), F32),
                        pltpu.VMEM((_PAIR_ROWS, tn), F32)],
        compiler_params=_cparams(("parallel", "parallel")), name="peer_gate",
    )(s1t, s2t)


def _peer_kernel(ht_ref, u_ref, vt_ref, s1_ref, s2_ref, sm_ref, xn_ref, fg_ref, o_ref,
                 acc_ref, e2_ref, c_ref, a_ref, f_ref, row_ref, *, nh, nk, apc, final):
    j = pl.program_id(1)
    tm = ht_ref.shape[1]

    @pl.when(j == 0)
    def _():
        acc_ref[...] = jnp.zeros(acc_ref.shape, F32)
        for h in range(nh):
            e2_ref[h] = jnp.exp(s2_ref[h] - sm_ref[h, 2:3, :])
            c_ref[h] = jnp.exp(s1_ref[h] - sm_ref[h, 1:2, :]) * sm_ref[h, 3:4, :]

    def gate_rows(it):
        for k in range(PEER_ROWS_PER_ITER):
            aa = it * PEER_ROWS_PER_ITER + k
            a = j * apc + aa
            rows = pl.ds(pl.multiple_of(aa * nk, nk), nk)
            for h in range(nh):
                row_ref[2 * nh * k + h:2 * nh * k + h + 1, :] = s1_ref[h, pl.ds(a, 1), :]
                row_ref[2 * nh * k + nh + h:2 * nh * k + nh + h + 1, :] = c_ref[h, pl.ds(a, 1), :]
            for ts in range(tm // LANES):
                cols = slice(ts * LANES, (ts + 1) * LANES)
                wgt = jnp.zeros((nk, LANES), F32)
                for h in range(nh):
                    score = row_ref[2 * nh * k + h:2 * nh * k + h + 1, cols] + s2_ref[h, :, cols]
                    wgt = wgt + jnp.where(score >= sm_ref[h, 0:1, cols],
                                          e2_ref[h, :, cols] * row_ref[2 * nh * k + nh + h:2 * nh * k + nh + h + 1, cols], 0.0)
                f_ref[rows, cols] = (wgt * _gelu(a_ref[rows, cols])).astype(BF16)

    a_ref[...] = _dot(u_ref[...], ht_ref[...])

    def body(it, carry):
        gate_rows(it)
        return carry

    lax.fori_loop(0, apc // PEER_ROWS_PER_ITER, body, 0)
    acc_ref[...] += _dot(vt_ref[...], f_ref[...])

    @pl.when(j == pl.num_programs(1) - 1)
    def _():
        out = xn_ref[...] + acc_ref[...].T
        if final:
            out = _rmsnorm(out, fg_ref[...])
        o_ref[...] = out


def _peer(ht, u, vt, s1t, s2t, sm, xn, fg, tm, apc, final):
    d, n = ht.shape
    nh, nk, _ = s1t.shape
    ne = u.shape[0]
    te = apc * nk
    assert apc % PEER_ROWS_PER_ITER == 0
    tok3 = lambda i, j: (0, 0, i)
    return pl.pallas_call(
        functools.partial(_peer_kernel, nh=nh, nk=nk, apc=apc, final=final),
        grid=(n // tm, ne // te),
        in_specs=[pl.BlockSpec((d, tm), lambda i, j: (0, i)),
                  pl.BlockSpec((te, d), lambda i, j: (j, 0)),
                  pl.BlockSpec((d, te), lambda i, j: (0, j)),
                  pl.BlockSpec((nh, nk, tm), tok3),
                  pl.BlockSpec((nh, nk, tm), tok3),
                  pl.BlockSpec((nh, 8, tm), tok3),
                  pl.BlockSpec((tm, d), lambda i, j: (i, 0)),
                  pl.BlockSpec((1, d), lambda i, j: (0, 0))],
        out_specs=pl.BlockSpec((tm, d), lambda i, j: (i, 0)),
        out_shape=jax.ShapeDtypeStruct((n, d), F32),
        scratch_shapes=[pltpu.VMEM((d, tm), F32), pltpu.VMEM((nh, nk, tm), F32), pltpu.VMEM((nh, nk, tm), F32),
                        pltpu.VMEM((te, tm), F32), pltpu.VMEM((te, tm), BF16),
                        pltpu.VMEM((2 * nh * PEER_ROWS_PER_ITER, tm), F32)],
        compiler_params=_cparams(("parallel", "arbitrary")), name="peer",
    )(ht, u, vt, s1t, s2t, sm, xn, fg)


def _rope_tables(pos):
    half = HEAD_DIM // 2
    inv_freq = ROPE_THETA ** (-jnp.arange(half, dtype=F32) / half)
    ang = pos.astype(F32)[:, None] * inv_freq[None, :]
    cos = jnp.cos(ang)
    sin = jnp.sin(ang)
    return jnp.tile(jnp.concatenate([cos, cos], axis=1), (1, 2)), jnp.tile(jnp.concatenate([-sin, sin], axis=1), (1, 2))


def _pad_rows(x, rows):
    return jnp.pad(x, ((0, rows - x.shape[0]),) + ((0, 0),) * (x.ndim - 1))


def _token_tile(n, pref):
    return pref if n % pref == 0 else n


def _page_pool_weights(wk, wv):
    wset = jnp.stack([wk[:, 0], wk[:, 1], wv[:, 0], wv[:, 1]]).reshape(4, 2, CMP_STRIDE)
    per_pos = jnp.tile(wset, (1, 1, LANES // CMP_STRIDE))
    chunk_of_pos = (jnp.arange(LANES) // CMP_STRIDE)[None, :] == jnp.arange(LANES // CMP_STRIDE)[:, None]
    return (per_pos[:, :, None, :] * chunk_of_pos[None, None].astype(F32)).reshape(64, LANES)


def kernel(x_prompt, x_sample, cache_nsa_kv, state_nsa_win, state_mlstm_C, state_mlstm_n, state_mlstm_m, state_conv, page_table, norm1_g, norm2_g, final_norm_g, w_in, w_out, mlstm_b_i, mlstm_b_f, mlstm_norm_g, conv_w, conv_b, conv_norm_g, conv_norm_b, nsa_cmp_wk, nsa_cmp_wv, gmlp_norm_g, gmlp_norm_b, gmlp_ws, gmlp_bs, peer_wq, peer_k1, peer_k2, peer_u, peer_v):
    depth = w_in.shape[0]
    bp, s_len, d = x_prompt.shape
    bd, t_dec, _ = x_sample.shape
    n_pool = cache_nsa_kv.shape[1]
    npages = page_table.shape[1]
    past = npages * cache_nsa_kv.shape[2]
    n_win = state_nsa_win.shape[2]
    assert t_dec == 1 and cache_nsa_kv.shape[2] == LANES and past % SEL_BLOCK == 0
    assert s_len >= WINDOW + 128 and n_win == WINDOW
    np_tok = bp * s_len
    ns_pad = LANES
    assert bd <= ns_pad

    xp = x_prompt.reshape(np_tok, d)
    xs = _pad_rows(x_sample.reshape(bd, d), ns_pad)
    cos_p, sin_p = _rope_tables(jnp.arange(s_len))
    cos_s, sin_s = _rope_tables(jnp.full((ns_pad,), past))
    pool_t = cache_nsa_kv.reshape(depth * n_pool, LANES, 4 * LANES).transpose(0, 2, 1)
    win_t = state_nsa_win.reshape(depth * bd, n_win, 2 * LANES).transpose(0, 2, 1)
    eye_h = jnp.eye(N_HEADS, dtype=F32)

    tm_in = _token_tile(s_len, 256)
    tm_out = _token_tile(np_tok, 256)
    tm_peer = _token_tile(np_tok, 512)
    nk = peer_k1.shape[2]
    apc = min(8, nk)

    new_p, new_s = [], []
    for l in range(depth):
        wi = w_in[l]
        w_perm = jnp.concatenate([wi[:, 0:1024], wi[:, 1032:2568], wi[:, 2580:3092]], axis=1).astype(BF16)
        w_small = jnp.concatenate([wi[:, 1024:1032], wi[:, 2568:2580], jnp.zeros((d, 108), F32)], axis=1)
        g1 = norm1_g[l].reshape(1, d)
        g2 = norm2_g[l].reshape(1, d)
        gate_bias = jnp.concatenate([mlstm_b_i[l], mlstm_b_f[l], jnp.zeros((LANES - 2 * N_HEADS,), F32)]).reshape(1, LANES)
        mng = mlstm_norm_g[l].reshape(1, MIX_W)
        cw = conv_w[l]
        cbias = conv_b[l].reshape(1, MIX_W)
        cg = conv_norm_g[l].reshape(1, MIX_W)
        cbeta = conv_norm_b[l].reshape(1, MIX_W)
        w32 = jnp.concatenate([jnp.repeat(nsa_cmp_wk[l], HEAD_DIM, axis=1), jnp.repeat(nsa_cmp_wv[l], HEAD_DIM, axis=1)], axis=1)
        wlo, whi = w32[:CMP_STRIDE], w32[CMP_STRIDE:]
        wall = _page_pool_weights(nsa_cmp_wk[l], nsa_cmp_wv[l])
        gg = gmlp_norm_g[l].reshape(1, MIX_W)
        gbeta = gmlp_norm_b[l].reshape(1, MIX_W)
        gws = gmlp_ws[l]
        gbsb = jnp.repeat(gmlp_bs[l].T, HEAD_DIM, axis=1)
        wo = w_out[l].astype(BF16)
        wq = peer_wq[l].astype(BF16)
        k1 = peer_k1[l].astype(BF16)
        k2 = peer_k2[l].astype(BF16)
        ub = peer_u[l].astype(BF16)
        vtb = peer_v[l].astype(BF16).T
        fg = final_norm_g.reshape(1, d)
        final = l == depth - 1

        oa, ob, oq, okv, owin, okaug, ovsb, owinb, od, osm = _in_proj(xp, g1, w_perm, w_small, cos_p, sin_p, tm_in)
        ya, c_p, n_p, m_p = _mlstm(oa.reshape(bp, s_len, 1024), osm.reshape(bp, s_len, LANES), gate_bias, mng,
                                   jnp.zeros((bp, MIX_W, MIX_W), F32), jnp.zeros((bp, 1, MIX_W), F32),
                                   jnp.zeros((bp, 1, MIX_W), F32), MLSTM_CHUNK, MLSTM_CHUNK, bp)
        yb, conv_p = _conv(ob.reshape(bp, s_len, 2 * MIX_W), jnp.zeros((bp, CONV_PAD, MIX_W), F32),
                           cw, cbias, cg, cbeta, 512, 512)
        cb = _cmp(okv.reshape(bp, s_len, 4 * LANES), wlo, whi)
        yc = _nsa_prompt(oq.reshape(bp, s_len, MIX_W), osm.reshape(bp, s_len, LANES), cb,
                         okaug.reshape(bp, s_len, 2 * LANES), ovsb.reshape(bp, s_len, LANES),
                         owinb.reshape(bp, s_len, 2 * LANES), 128)
        yd, _ = _gmlp(od, gg, gbeta, gws, gbsb, 4)
        xn, ht, s1t, s2t = _out_proj(xp, ya.reshape(np_tok, MIX_W), yb.reshape(np_tok, MIX_W),
                                     yc.reshape(np_tok, MIX_W), yd, wo, g2, wq, k1, k2, tm_out)
        sm = _gate(s1t, s2t, _token_tile(np_tok, 512))
        xp = _peer(ht, ub, vtb, s1t, s2t, sm, xn, fg, tm_peer, apc, final)
        new_p.append((okv.reshape(bp, s_len, 4, NSA_KV, HEAD_DIM),
                      owin.reshape(bp, s_len, 2, NSA_KV, HEAD_DIM)[:, s_len - n_win:],
                      jnp.stack([c_p[:, h * HEAD_DIM:(h + 1) * HEAD_DIM, h * HEAD_DIM:(h + 1) * HEAD_DIM]
                                 for h in range(N_HEADS)], axis=1),
                      n_p.reshape(bp, N_HEADS, HEAD_DIM),
                      m_p[:, 0, ::HEAD_DIM],
                      conv_p[:, CONV_PAD - (CONV_W - 1):]))

        sa, sb, sq, skv, swin, _, _, _, sd, ssm = _in_proj(xs, g1, w_perm, w_small, cos_s, sin_s, ns_pad)
        rows8 = lambda t: jnp.pad(t[:bd, None, :], ((0, 0), (0, 7), (0, 0)))
        c0 = jnp.einsum('bhvk,hg->bhvgk', state_mlstm_C[l], eye_h).reshape(bd, MIX_W, MIX_W)
        n0 = state_mlstm_n[l].reshape(bd, 1, MIX_W)
        m0 = jnp.repeat(state_mlstm_m[l], HEAD_DIM, axis=-1).reshape(bd, 1, MIX_W)
        bb_s = 2 if bd % 2 == 0 else 1
        ya_s, c_s, n_s, m_s = _mlstm(rows8(sa), rows8(ssm), gate_bias, mng, c0, n0, m0, 8, 1, bb_s)
        prefix = jnp.pad(state_conv[l], ((0, 0), (CONV_PAD - (CONV_W - 1), 0), (0, 0)))
        yb_s, conv_s = _conv(rows8(sb), prefix, cw, cbias, cg, cbeta, 8, 1)
        zd_s = jnp.pad(sd[:bd, None, :], ((0, 0), (0, GMLP_CHUNK - 1), (0, 0))).reshape(bd * GMLP_CHUNK, 2 * MIX_W)
        yd_s, v_s = _gmlp(zd_s, gg, gbeta, gws, gbsb, 1)
        q_heads = sq[:bd].reshape(bd, NSA_KV, 2, 1, HEAD_DIM)
        q4 = (q_heads * jnp.eye(NSA_KV, dtype=F32)[None, :, None, :, None]).reshape(bd, N_HEADS, LANES)
        q4 = jnp.pad(q4, ((0, 0), (0, 8 - N_HEADS), (0, 0)))
        gq = jnp.pad(ssm[:bd, 2 * N_HEADS:2 * N_HEADS + 3 * N_HEADS].reshape(bd, N_HEADS, 3),
                     ((0, 0), (0, 8 - N_HEADS), (0, LANES - 3)))
        pp = 16 if npages % 16 == 0 else (8 if npages % 8 == 0 else 1)
        o8, win_s = _nsa_sample(page_table + l * n_pool, q4, gq, skv[:bd, None, :], swin[:bd, None, :],
                                swin[:bd, :, None], win_t, l * bd, wall, pool_t, pp)
        yc_s = jnp.concatenate([o8[:, h, (h // 2) * HEAD_DIM:(h // 2 + 1) * HEAD_DIM] for h in range(N_HEADS)], axis=1)
        xn_s, ht_s, s1t_s, s2t_s = _out_proj(xs, _pad_rows(ya_s[:, 0], ns_pad), _pad_rows(yb_s[:, 0], ns_pad),
                                             _pad_rows(yc_s, ns_pad),
                                             _pad_rows(yd_s.reshape(bd, GMLP_CHUNK, MIX_W)[:, 0], ns_pad),
                                             wo, g2, wq, k1, k2, ns_pad)
        sm_s = _gate(s1t_s, s2t_s, ns_pad)
        xs = _peer(ht_s, ub, vtb, s1t_s, s2t_s, sm_s, xn_s, fg, ns_pad, apc, final)
        new_s.append((skv[:bd].reshape(bd, 1, 4, NSA_KV, HEAD_DIM),
                      win_s.transpose(0, 2, 1).reshape(bd, n_win, 2, NSA_KV, HEAD_DIM),
                      jnp.stack([c_s[:, h * HEAD_DIM:(h + 1) * HEAD_DIM, h * HEAD_DIM:(h + 1) * HEAD_DIM]
                                 for h in range(N_HEADS)], axis=1),
                      n_s.reshape(bd, N_HEADS, HEAD_DIM),
                      m_s[:, 0, ::HEAD_DIM],
                      conv_s[:, CONV_PAD - (CONV_W - 1):],
                      v_s.reshape(bd, GMLP_CHUNK, MIX_W)[:, 0:1]))

    def stack(states, i):
        return jnp.stack([st[i] for st in states])

    y_prompt = xp.reshape(bp, s_len, d)
    y_sample = xs[:bd].reshape(bd, 1, d)
    return (y_prompt, y_sample,
            stack(new_p, 0), stack(new_s, 0),
            stack(new_p, 1), stack(new_s, 1),
            stack(new_p, 2), stack(new_s, 2),
            stack(new_p, 3), stack(new_s, 3),
            stack(new_p, 4), stack(new_s, 4),
            stack(new_p, 5), stack(new_s, 5),
            stack(new_s, 6))
```

```python
import functools

import jax
import jax.numpy as jnp
from jax import lax
from jax.experimental import pallas as pl
from jax.experimental.pallas import tpu as pltpu

F32 = jnp.float32
BF16 = jnp.bfloat16
HIGHEST = lax.Precision.HIGHEST

HEAD_DIM = 64
N_HEADS = 4
MIX_W = 256
NSA_KV = 2
NORM_EPS = 1e-6
NEG = -1e30
MLSTM_CHUNK = 128
CONV_W = 31
CONV_PAD = 32
CMP_STRIDE = 16
CMP_LEN = 32
SEL_BLOCK = 64
N_SELECT = 16
SEL_FORCE = 1e9
WINDOW = 512
ROPE_THETA = 10000.0
ATTN_SCALE = HEAD_DIM ** -0.5
GMLP_CHUNK = 128
PEER_TOPK = 16
LANES = 128
VMEM_LIMIT = 48 * 1024 * 1024

NT = (((1,), (1,)), ((), ()))
TN = (((0,), (0,)), ((), ()))


def _cparams(sem):
    return pltpu.CompilerParams(dimension_semantics=sem, vmem_limit_bytes=VMEM_LIMIT)


def _iota(shape, dim):
    return lax.broadcasted_iota(jnp.int32, shape, dim)


def _sigmoid(x):
    return 1.0 / (1.0 + jnp.exp(-x))


def _gelu(x):
    c1 = -2.0 * 0.7978845608028654
    c2 = c1 * 0.044715
    return x / (1.0 + jnp.exp(x * (c1 + c2 * (x * x))))


def _dot(a, b, dims=None, precision=None):
    if dims is None:
        return jnp.dot(a, b, preferred_element_type=F32, precision=precision)
    return lax.dot_general(a, b, dims, preferred_element_type=F32, precision=precision)


def _block_ones(n, blk):
    return (_iota((n, n), 0) // blk == _iota((n, n), 1) // blk).astype(F32)


def _group_layernorm(x, ee, width):
    mu = _dot(x, ee, precision=HIGHEST) * (1.0 / width)
    d = x - mu
    var = _dot(d * d, ee, precision=HIGHEST) * (1.0 / width)
    return d * lax.rsqrt(var + NORM_EPS)


def _rmsnorm(x, g):
    return x * lax.rsqrt(jnp.mean(x * x, axis=-1, keepdims=True) + NORM_EPS) * g


def _masked_softmax(s, valid):
    sm = jnp.where(valid, s, NEG)
    p = jnp.where(valid, jnp.exp(sm - jnp.max(sm, axis=-1, keepdims=True)), 0.0)
    return p / jnp.maximum(jnp.sum(p, axis=-1, keepdims=True), 1e-30)


def _rope(x, cos, sin_signed):
    w = x.shape[1]
    fwd = pltpu.roll(x, w - HEAD_DIM // 2, 1)
    bwd = pltpu.roll(x, HEAD_DIM // 2, 1)
    first = (_iota(x.shape, 1) % HEAD_DIM) < HEAD_DIM // 2
    return x * cos + jnp.where(first, fwd, bwd) * sin_signed


def _in_kernel(x_ref, g_ref, w_ref, wsmall_ref, cos_ref, sin_ref,
               oa_ref, ob_ref, oq_ref, okv_ref, owin_ref, okaug_ref, ovsb_ref, owinb_ref, od_ref, os_ref, *, npb):
    tm = x_ref.shape[0]
    y = _rmsnorm(x_ref[...], g_ref[...])
    yb = y.astype(BF16)
    z = _dot(yb, w_ref[...])
    os_ref[...] = _dot(yb, wsmall_ref[...])
    oa_ref[...] = z[:, 0:1024]
    ob_ref[...] = z[:, 1024:1536]
    cos = cos_ref[...]
    sin = sin_ref[...]
    cos2 = jnp.concatenate([cos, cos], axis=1)
    sin2 = jnp.concatenate([sin, sin], axis=1)
    oq_ref[...] = _rope(z[:, 1536:1792], cos2, sin2) * ATTN_SCALE
    kc = _rope(z[:, 1792:1920], cos, sin)
    vc = z[:, 1920:2048]
    ks = _rope(z[:, 2048:2176], cos, sin)
    vs = z[:, 2176:2304]
    kw = _rope(z[:, 2304:2432], cos, sin)
    vw = z[:, 2432:2560]
    okv_ref[...] = jnp.concatenate([kc, vc, ks, vs], axis=1)
    owin = jnp.concatenate([kw, vw], axis=1)
    owin_ref[...] = owin
    pos = (pl.program_id(0) % npb) * tm + _iota((tm, LANES), 0)
    onehot = jnp.where(_iota((tm, LANES), 1) == pos // SEL_BLOCK, 1.0, 0.0)
    okaug_ref[...] = jnp.concatenate([ks, onehot], axis=1).astype(BF16)
    ovsb_ref[...] = vs.astype(BF16)
    owinb_ref[...] = owin.astype(BF16)
    od_ref[...] = z[:, 2560:3072]


def _in_proj(x, g, w, wsmall, cos_t, sin_t, tm):
    n, d = x.shape
    npb = cos_t.shape[0] // tm
    widths = (1024, 512, 256, 512, 256, 256, 128, 256, 512, 128)
    dtypes = (F32, F32, F32, F32, F32, BF16, BF16, BF16, F32, F32)
    row = lambda i: (i, 0)
    return pl.pallas_call(
        functools.partial(_in_kernel, npb=npb),
        grid=(n // tm,),
        in_specs=[pl.BlockSpec((tm, d), row),
                  pl.BlockSpec((1, d), lambda i: (0, 0)),
                  pl.BlockSpec(w.shape, lambda i: (0, 0)),
                  pl.BlockSpec(wsmall.shape, lambda i: (0, 0)),
                  pl.BlockSpec((tm, LANES), lambda i: (i % npb, 0)),
                  pl.BlockSpec((tm, LANES), lambda i: (i % npb, 0))],
        out_specs=[pl.BlockSpec((tm, wd), row) for wd in widths],
        out_shape=[jax.ShapeDtypeStruct((n, wd), dt) for wd, dt in zip(widths, dtypes)],
        compiler_params=_cparams(("parallel",)), name="in_proj",
    )(x, g, w, wsmall, cos_t, sin_t)


def _mlstm_kernel(za_ref, zs_ref, bias_ref, ng_ref, c0_ref, n0_ref, m0_ref,
                  y_ref, c_ref, n_ref, m_ref, *, L, t_valid, bb):
    @pl.when(pl.program_id(1) == 0)
    def _():
        c_ref[...] = c0_ref[...]
        n_ref[...] = n0_ref[...]
        m_ref[...] = m0_ref[...]

    head_of_lane = _iota((1, MIX_W), 1) // HEAD_DIM
    causal = _iota((L, L), 0) >= _iota((L, L), 1)
    tril = causal.astype(F32)
    ee = _block_ones(MIX_W, HEAD_DIM)
    bdiag = ee > 0.5
    row128 = _iota((L, LANES), 0)
    lane128 = _iota((L, LANES), 1)
    for bi in range(bb):
        za = za_ref[bi]
        q = za[:, 0:256]
        k = za[:, 256:512] * (HEAD_DIM ** -0.5)
        v = za[:, 512:768]
        o = za[:, 768:1024]
        gi = zs_ref[bi] + bias_ref[...]
        ls = jnp.minimum(gi, 0.0) - jnp.log(1.0 + jnp.exp(-jnp.abs(gi)))
        if t_valid < L:
            live = row128 < t_valid
            ig = jnp.where(live, gi, NEG)
            ls = jnp.where(live, ls, 0.0)
        else:
            ig = gi
        gmat = jnp.where(lane128 < N_HEADS, ig, 0.0)
        lfm = jnp.where((lane128 >= N_HEADS) & (lane128 < 2 * N_HEADS), ls, 0.0)
        fc = _dot(tril, lfm, precision=HIGHEST)
        g_t = gmat.T
        f_t = fc.T
        mrow = m_ref[bi]
        nrow = n_ref[bi]
        cm = c_ref[bi]
        qb = q.astype(BF16)
        kb = k.astype(BF16)
        vb = v.astype(BF16)
        num = jnp.zeros((L, MIX_W), F32)
        decay_b = jnp.zeros((L, MIX_W), F32)
        dens_b = jnp.zeros((L, MIX_W), F32)
        mt_b = jnp.zeros((L, MIX_W), F32)
        w_b = jnp.zeros((L, MIX_W), F32)
        cd_b = jnp.zeros((1, MIX_W), F32)
        mnew_b = jnp.zeros((1, MIX_W), F32)
        for h in range(N_HEADS):
            hm = head_of_lane == h
            f_col = fc[:, N_HEADS + h:N_HEADS + h + 1]
            ig_col = gmat[:, h:h + 1]
            f_row = f_t[N_HEADS + h:N_HEADS + h + 1, :]
            ig_row = g_t[h:h + 1, :]
            mp = mrow[:, h * HEAD_DIM:h * HEAD_DIM + 1]
            dmat = jnp.where(causal, (f_col - f_row) + ig_row, NEG)
            m_inter = mp + f_col
            m_t = jnp.maximum(m_inter, jnp.max(dmat, axis=1, keepdims=True))
            d_exp = jnp.exp(dmat - m_t)
            decay = jnp.exp(m_inter - m_t)
            qh = jnp.where(hm, q, 0.0).astype(BF16)
            s = _dot(qh, kb, NT) * d_exp
            num = jnp.where(hm, _dot(s.astype(BF16), vb), num)
            dens_b = jnp.where(hm, jnp.sum(s, axis=1, keepdims=True), dens_b)
            decay_b = jnp.where(hm, decay, decay_b)
            mt_b = jnp.where(hm, m_t, mt_b)
            f_last = f_col[L - 1:L, :]
            m_new = m_t[L - 1:L, :]
            w_b = jnp.where(hm, jnp.exp((f_last - f_col) + ig_col - m_new), w_b)
            cd_b = jnp.where(hm, jnp.exp(mp + f_last - m_new), cd_b)
            mnew_b = jnp.where(hm, m_new, mnew_b)
        inter = _dot(qb, cm.astype(BF16), NT)
        nq_b = _dot(q * nrow, ee, precision=HIGHEST)
        hnum = num + decay_b * inter
        den_b = dens_b + decay_b * nq_b
        hh = hnum / jnp.maximum(jnp.abs(den_b), jnp.exp(-mt_b))
        hn = _group_layernorm(hh, ee, HEAD_DIM) * ng_ref[...]
        y_ref[bi] = hn * _sigmoid(o)
        kw = k * w_b
        upd = _dot(vb, kw.astype(BF16), TN)
        c_ref[bi] = jnp.where(bdiag, cd_b * cm + upd, 0.0)
        n_ref[bi] = cd_b * nrow + jnp.sum(kw, axis=0, keepdims=True)
        m_ref[bi] = mnew_b


def _mlstm(za, zs, bias, ng, c0, n0, m0, L, t_valid, bb):
    b, t, _ = za.shape
    nc = t // L
    st = lambda i, c: (i, 0, 0)
    return pl.pallas_call(
        functools.partial(_mlstm_kernel, L=L, t_valid=t_valid, bb=bb),
        grid=(b // bb, nc),
        in_specs=[pl.BlockSpec((bb, L, 1024), lambda i, c: (i, c, 0)),
                  pl.BlockSpec((bb, L, LANES), lambda i, c: (i, c, 0)),
                  pl.BlockSpec((1, LANES), lambda i, c: (0, 0)),
                  pl.BlockSpec((1, MIX_W), lambda i, c: (0, 0)),
                  pl.BlockSpec((bb, MIX_W, MIX_W), st),
                  pl.BlockSpec((bb, 1, MIX_W), st),
                  pl.BlockSpec((bb, 1, MIX_W), st)],
        out_specs=[pl.BlockSpec((bb, L, MIX_W), lambda i, c: (i, c, 0)),
                   pl.BlockSpec((bb, MIX_W, MIX_W), st),
                   pl.BlockSpec((bb, 1, MIX_W), st),
                   pl.BlockSpec((bb, 1, MIX_W), st)],
        out_shape=[jax.ShapeDtypeStruct((b, t, MIX_W), F32),
                   jax.ShapeDtypeStruct((b, MIX_W, MIX_W), F32),
                   jax.ShapeDtypeStruct((b, 1, MIX_W), F32),
                   jax.ShapeDtypeStruct((b, 1, MIX_W), F32)],
        compiler_params=_cparams(("parallel", "arbitrary")), name="mlstm",
    )(za, zs, bias, ng, c0, n0, m0)


def _conv_kernel(zb_ref, pre_ref, w_ref, cb_ref, g_ref, beta_ref, y_ref, st_ref, ext_ref, *, tt, tv):
    @pl.when(pl.program_id(1) == 0)
    def _():
        ext_ref[0:CONV_PAD, :] = pre_ref[0]

    z = zb_ref[0]
    u = z[:, :MIX_W] * _sigmoid(z[:, MIX_W:])
    ext_ref[CONV_PAD:CONV_PAD + tt, :] = u
    off = CONV_PAD - (CONV_W - 1)
    acc = jnp.zeros((tt, MIX_W), F32)
    for j in range(CONV_W):
        acc = acc + w_ref[j:j + 1, :] * ext_ref[off + j:off + j + tt, :]
    ee = _block_ones(MIX_W, HEAD_DIM)
    y = _group_layernorm(acc + cb_ref[...], ee, HEAD_DIM) * g_ref[...] + beta_ref[...]
    y_ref[0] = y * _sigmoid(y)
    st_ref[0] = ext_ref[tv:tv + CONV_PAD, :]
    ext_ref[0:CONV_PAD, :] = ext_ref[tt:tt + CONV_PAD, :]


def _conv(zb, prefix, w, cb, g, beta, tt, tv):
    b, t, _ = zb.shape
    vec = lambda i, j: (0, 0)
    return pl.pallas_call(
        functools.partial(_conv_kernel, tt=tt, tv=tv),
        grid=(b, t // tt),
        in_specs=[pl.BlockSpec((1, tt, 2 * MIX_W), lambda i, j: (i, j, 0)),
                  pl.BlockSpec((1, CONV_PAD, MIX_W), lambda i, j: (i, 0, 0)),
                  pl.BlockSpec((CONV_W, MIX_W), vec),
                  pl.BlockSpec((1, MIX_W), vec),
                  pl.BlockSpec((1, MIX_W), vec),
                  pl.BlockSpec((1, MIX_W), vec)],
        out_specs=[pl.BlockSpec((1, tt, MIX_W), lambda i, j: (i, j, 0)),
                   pl.BlockSpec((1, CONV_PAD, MIX_W), lambda i, j: (i, 0, 0))],
        out_shape=[jax.ShapeDtypeStruct((b, t, MIX_W), F32),
                   jax.ShapeDtypeStruct((b, CONV_PAD, MIX_W), F32)],
        scratch_shapes=[pltpu.VMEM((CONV_PAD + tt, MIX_W), F32)],
        compiler_params=_cparams(("parallel", "arbitrary")), name="conv",
    )(zb, prefix, w, cb, g, beta)


def _gmlp_kernel(zd_ref, g_ref, beta_ref, ws_ref, bsb_ref, y_ref, v_ref, *, cpb):
    z = zd_ref[...]
    ee = _block_ones(MIX_W, HEAD_DIM)
    u = _gelu(z[:, :MIX_W])
    vv = _group_layernorm(_gelu(z[:, MIX_W:]), ee, HEAD_DIM) * g_ref[...] + beta_ref[...]
    v_ref[...] = vv
    tril = _iota((GMLP_CHUNK, GMLP_CHUNK), 0) >= _iota((GMLP_CHUNK, GMLP_CHUNK), 1)
    head_of_lane = _iota((1, MIX_W), 1) // HEAD_DIM
    wm = [jnp.where(tril, ws_ref[h], 0.0).astype(BF16) for h in range(N_HEADS)]
    for c in range(cpb):
        rows = slice(c * GMLP_CHUNK, (c + 1) * GMLP_CHUNK)
        vc = vv[rows].astype(BF16)
        mixed = jnp.zeros((GMLP_CHUNK, MIX_W), F32)
        for h in range(N_HEADS):
            mixed = jnp.where(head_of_lane == h, _dot(wm[h], vc), mixed)
        y_ref[rows, :] = u[rows] * (mixed + bsb_ref[...])


def _gmlp(zd, g, beta, ws, bsb, cpb):
    n = zd.shape[0]
    rows = cpb * GMLP_CHUNK
    vec = lambda i: (0, 0)
    return pl.pallas_call(
        functools.partial(_gmlp_kernel, cpb=cpb),
        grid=(n // rows,),
        in_specs=[pl.BlockSpec((rows, 2 * MIX_W), lambda i: (i, 0)),
                  pl.BlockSpec((1, MIX_W), vec),
                  pl.BlockSpec((1, MIX_W), vec),
                  pl.BlockSpec((N_HEADS, GMLP_CHUNK, GMLP_CHUNK), lambda i: (0, 0, 0)),
                  pl.BlockSpec((GMLP_CHUNK, MIX_W), vec)],
        out_specs=[pl.BlockSpec((rows, MIX_W), lambda i: (i, 0)),
                   pl.BlockSpec((rows, MIX_W), lambda i: (i, 0))],
        out_shape=[jax.ShapeDtypeStruct((n, MIX_W), F32),
                   jax.ShapeDtypeStruct((n, MIX_W), F32)],
        compiler_params=_cparams(("parallel",)), name="gmlp",
    )(zd, g, beta, ws, bsb)


def _combine_lo_hi(lo, hi):
    r = lo.shape[0]
    nxt = pltpu.roll(hi, r - 1, 0)
    return lo + jnp.where(_iota(lo.shape, 0) < r - 1, nxt, 0.0)


def _cmp_kernel(kv_ref, wlo_ref, whi_ref, o_ref):
    x = kv_ref[0]
    x3 = x.reshape(x.shape[0] // CMP_STRIDE, CMP_STRIDE, x.shape[1])
    lo = jnp.sum(x3 * wlo_ref[...][None], axis=1)
    hi = jnp.sum(x3 * whi_ref[...][None], axis=1)
    o_ref[0] = _combine_lo_hi(lo, hi)


def _cmp(okv3, wlo, whi):
    b, s, _ = okv3.shape
    return pl.pallas_call(
        _cmp_kernel,
        grid=(b,),
        in_specs=[pl.BlockSpec((1, s, 2 * LANES), lambda i: (i, 0, 0)),
                  pl.BlockSpec((CMP_STRIDE, 2 * LANES), lambda i: (0, 0)),
                  pl.BlockSpec((CMP_STRIDE, 2 * LANES), lambda i: (0, 0))],
        out_specs=pl.BlockSpec((1, s // CMP_STRIDE, 2 * LANES), lambda i: (i, 0, 0)),
        out_shape=jax.ShapeDtypeStruct((b, s // CMP_STRIDE, 2 * LANES), F32),
        compiler_params=_cparams(("parallel",)), name="nsa_cmp",
    )(okv3, wlo, whi)


def _nsa_prompt_kernel(q_ref, gs_ref, cb_ref, ka_ref, vs_ref, win_ref, y_ref, *, tq, s_len, n_top, tk):
    start = pl.program_id(1) * tq
    nc = s_len // CMP_STRIDE
    ns = s_len // SEL_BLOCK
    q = q_ref[0]
    gl = _sigmoid(gs_ref[0])
    lo_half = _iota((tq, LANES), 1) < HEAD_DIM
    q_rows = []
    for g in range(NSA_KV):
        qpair = q[:, g * LANES:(g + 1) * LANES]
        swapped = pltpu.roll(qpair, HEAD_DIM, 1)
        if g == 0:
            q_rows += [jnp.where(lo_half, qpair, 0.0), jnp.where(lo_half, swapped, 0.0)]
        else:
            q_rows += [jnp.where(lo_half, 0.0, swapped), jnp.where(lo_half, 0.0, qpair)]
    q128 = jnp.concatenate(q_rows, axis=0)
    q128b = q128.astype(BF16)
    trow = start + _iota((tq, 1), 0)
    t2 = jnp.concatenate([trow, trow], axis=0)
    t4 = jnp.concatenate([t2, t2], axis=0)
    cb = cb_ref[0]
    sc = _dot(q128b, cb[:, 0:LANES].astype(BF16), NT)
    validc = (_iota((1, nc), 1) * CMP_STRIDE + CMP_LEN) <= (t4 + 1)
    p = _masked_softmax(sc, validc)
    o_c = _dot(p.astype(BF16), cb[:, LANES:2 * LANES].astype(BF16))
    psum = jnp.concatenate([p[0:tq] + p[tq:2 * tq], p[2 * tq:3 * tq] + p[3 * tq:4 * tq]], axis=0)
    pool4 = (_iota((nc, ns), 0) // (SEL_BLOCK // CMP_STRIDE) == _iota((nc, ns), 1)).astype(F32)
    imp = _dot(psum, pool4, precision=HIGHEST)
    jidx = _iota((1, ns), 1)
    imp = jnp.where(jidx == t2 // SEL_BLOCK, SEL_FORCE, imp)
    imp = jnp.where(jidx * SEL_BLOCK <= t2, imp, -1.0)
    if n_top < ns:
        x = jnp.concatenate([imp[0:tq].T, imp[tq:2 * tq].T], axis=1)
        jrow = _iota((ns, 2 * tq), 0).astype(F32)
        sel_t = jnp.zeros((ns, 2 * tq), F32)
        for _ in range(n_top):
            m = jnp.max(x, axis=0, keepdims=True)
            first = jnp.min(jnp.where(x == m, jrow, float(ns)), axis=0, keepdims=True)
            hit = jrow == first
            sel_t = jnp.where(hit, 1.0, sel_t)
            x = jnp.where(hit, -3.0, x)
        sel = jnp.concatenate([sel_t[:, 0:tq].T, sel_t[:, tq:2 * tq].T], axis=0)
    else:
        sel = jnp.ones((2 * tq, ns), F32)
    selneg = (sel - 1.0) * (-NEG)
    if ns < LANES:
        selneg = jnp.concatenate([selneg, jnp.zeros((2 * tq, LANES - ns), F32)], axis=1)
    selneg4 = jnp.concatenate([selneg[0:tq], selneg[0:tq], selneg[tq:2 * tq], selneg[tq:2 * tq]], axis=0)
    qaug = jnp.concatenate([q128, selneg4], axis=1).astype(BF16)

    def tile(c, carry, diagonal):
        m, l, acc = carry
        k0 = pl.multiple_of(c * tk, tk)
        s = _dot(qaug, ka_ref[0, pl.ds(k0, tk), :], NT)
        if diagonal:
            s = jnp.where(k0 + _iota((1, tk), 1) <= t4, s, NEG)
        m_new = jnp.maximum(m, jnp.max(s, axis=1, keepdims=True))
        pt = jnp.exp(s - m_new)
        alpha = jnp.exp(m - m_new)
        l = alpha * l + jnp.sum(pt, axis=1, keepdims=True)
        acc = alpha * acc + _dot(pt.astype(BF16), vs_ref[0, pl.ds(k0, tk), :])
        return m_new, l, acc

    n_tiles = (start + tq + tk - 1) // tk
    carry = (jnp.full((4 * tq, 1), NEG, F32), jnp.zeros((4 * tq, 1), F32), jnp.zeros((4 * tq, LANES), F32))
    carry = lax.fori_loop(0, n_tiles - 1, lambda c, cr: tile(c, cr, False), carry)
    _, l, acc = tile(n_tiles - 1, carry, True)
    o_s = acc / jnp.maximum(l, 1e-30)
    wl = WINDOW + tq
    w0 = pl.multiple_of(jnp.maximum(start - WINDOW, 0), tq)
    sw = _dot(q128b, win_ref[0, pl.ds(w0, wl), 0:LANES], NT)
    kposw = w0 + _iota((1, wl), 1)
    validw = (kposw <= t4) & (kposw > t4 - WINDOW)
    o_w = _dot(_masked_softmax(sw, validw).astype(BF16), win_ref[0, pl.ds(w0, wl), LANES:2 * LANES])
    heads = []
    for hh in range(N_HEADS):
        col = 2 * N_HEADS + hh * 3
        rows = slice(hh * tq, (hh + 1) * tq)
        heads.append(o_c[rows] * gl[:, col:col + 1] + o_s[rows] * gl[:, col + 1:col + 2]
                     + o_w[rows] * gl[:, col + 2:col + 3])
    y_ref[0] = jnp.concatenate([jnp.where(lo_half, heads[0], pltpu.roll(heads[1], HEAD_DIM, 1)),
                                jnp.where(lo_half, pltpu.roll(heads[2], HEAD_DIM, 1), heads[3])], axis=1)


def _nsa_prompt(oq3, os3, cb, okaug3, ovsb3, owinb3, tq):
    b, s, _ = oq3.shape
    ns = s // SEL_BLOCK
    assert ns <= LANES
    n_top = min(N_SELECT, ns)
    tk = min(512, s)
    full = lambda i, j: (i, 0, 0)
    return pl.pallas_call(
        functools.partial(_nsa_prompt_kernel, tq=tq, s_len=s, n_top=n_top, tk=tk),
        grid=(b, s // tq),
        in_specs=[pl.BlockSpec((1, tq, MIX_W), lambda i, j: (i, j, 0)),
                  pl.BlockSpec((1, tq, LANES), lambda i, j: (i, j, 0)),
                  pl.BlockSpec((1, s // CMP_STRIDE, 2 * LANES), full),
                  pl.BlockSpec((1, s, 2 * LANES), full),
                  pl.BlockSpec((1, s, LANES), full),
                  pl.BlockSpec((1, s, 2 * LANES), full)],
        out_specs=pl.BlockSpec((1, tq, MIX_W), lambda i, j: (i, j, 0)),
        out_shape=jax.ShapeDtypeStruct((b, s, MIX_W), F32),
        compiler_params=_cparams(("parallel", "arbitrary")), name="nsa_prompt",
    )(oq3, os3, cb, okaug3, ovsb3, owinb3)


def _nsa_sample_kernel(pt_ref, q4_ref, gq_ref, newkv_ref, newwin_ref, newcol_ref, win_ref, wall_ref, *rest,
                       pp, npages, n_top):
    pool_refs = rest[:pp]
    o_ref, wo_ref, lo_s, hi_s, m_s, l_s, a_s = rest[pp:]
    i = pl.program_id(1)
    nb = 2 * npages
    nbp = m_s.shape[1]
    ncp = npages * (LANES // CMP_STRIDE)
    past = npages * LANES
    q4 = q4_ref[0]
    q4b = q4.astype(BF16)

    @pl.when(i == 0)
    def _():
        m_s[...] = jnp.zeros(m_s.shape, F32)
        l_s[...] = jnp.zeros(l_s.shape, F32)
        if nb < nbp:
            a_s[...] = jnp.zeros(a_s.shape, F32)

    lane8 = _iota((8, LANES), 1)
    lanej = _iota((8, nbp), 1)
    set_of_lane = _iota((8, 2 * LANES), 1) // HEAD_DIM
    for kk in range(pp):
        page = pool_refs[kk][0]
        pidx = i * pp + kk
        pooled = _dot(wall_ref[...].astype(BF16), page[0:2 * LANES, :].astype(BF16), NT)
        lo8 = jnp.zeros((8, 2 * LANES), F32)
        hi8 = jnp.zeros((8, 2 * LANES), F32)
        for st in range(4):
            lo8 = jnp.where(set_of_lane == st, pooled[st * 16:st * 16 + 8], lo8)
            hi8 = jnp.where(set_of_lane == st, pooled[st * 16 + 8:st * 16 + 16], hi8)
        r0 = pl.multiple_of(pidx * 8, 8)
        lo_s[pl.ds(r0, 8), :] = lo8
        hi_s[pl.ds(r0, 8), :] = hi8
        ks_t = page[2 * LANES:3 * LANES, :].astype(BF16)
        vs_t = page[3 * LANES:4 * LANES, :].astype(BF16)
        s = _dot(q4b, ks_t)
        for jb in range(LANES // SEL_BLOCK):
            inb = (lane8 >= jb * SEL_BLOCK) & (lane8 < (jb + 1) * SEL_BLOCK)
            sm = jnp.where(inb, s, NEG)
            m = jnp.max(sm, axis=1, keepdims=True)
            p = jnp.where(inb, jnp.exp(sm - m), 0.0)
            lsum = jnp.sum(p, axis=1, keepdims=True)
            a = _dot(p.astype(BF16), vs_t, NT)
            j = pidx * (LANES // SEL_BLOCK) + jb
            m_s[...] = jnp.where(lanej == j, m, m_s[...])
            l_s[...] = jnp.where(lanej == j, lsum, l_s[...])
            for hh in range(N_HEADS):
                a_s[hh, pl.ds(j, 1), :] = a[hh:hh + 1, :]

    @pl.when(i == pl.num_programs(1) - 1)
    def _():
        row8 = _iota((8, 1), 0)
        cbm = _combine_lo_hi(lo_s[...], hi_s[...])
        kcb = cbm[:, 0:LANES].astype(BF16)
        vcb = cbm[:, LANES:2 * LANES].astype(BF16)
        sc = _dot(q4b, kcb, NT)
        validc = (_iota((1, ncp), 1) * CMP_STRIDE + CMP_LEN) <= past + 1
        p = _masked_softmax(sc, validc)
        o_c = _dot(p.astype(BF16), vcb)
        pg = jnp.where(row8 == 0, p[0:1] + p[1:2], jnp.where(row8 == 1, p[2:3] + p[3:4], 0.0))
        pool4 = (_iota((ncp, nbp), 0) // (SEL_BLOCK // CMP_STRIDE) == _iota((ncp, nbp), 1)).astype(F32)
        imp2 = _dot(pg, pool4, precision=HIGHEST)
        ii = _iota((nbp, nbp), 0)
        jj = _iota((nbp, nbp), 1)
        sels = []
        for g in range(NSA_KV):
            mx = jnp.broadcast_to(imp2[g:g + 1, :], (nbp, nbp))
            mt = mx.T
            beats = ((mt > mx) | ((mt == mx) & (ii < jj))) & (ii < nb)
            rank = jnp.sum(jnp.where(beats, 1.0, 0.0), axis=0, keepdims=True)
            sels.append(jnp.where((rank < n_top - 1) & (_iota((1, nbp), 1) < nb), 1.0, 0.0))
        sel8 = jnp.where(row8 < 2, sels[0], sels[1]) > 0.5
        newkv = newkv_ref[0]
        ksn = newkv[:, 2 * LANES:3 * LANES]
        vsn = newkv[:, 3 * LANES:4 * LANES]
        s_new = jnp.sum(q4 * ksn, axis=1, keepdims=True)
        mrow = m_s[...]
        m_all = jnp.maximum(jnp.max(jnp.where(sel8, mrow, NEG), axis=1, keepdims=True), s_new)
        wj = jnp.where(sel8, jnp.exp(mrow - m_all), 0.0)
        w_new = jnp.exp(s_new - m_all)
        ltot = jnp.sum(wj * l_s[...], axis=1, keepdims=True) + w_new
        osum = w_new * vsn
        for hh in range(N_HEADS):
            osum = osum + jnp.where(row8 == hh, _dot(wj, a_s[hh], precision=HIGHEST), 0.0)
        o_s = osum / jnp.maximum(ltot, 1e-30)
        win_t = win_ref[0]
        nw = win_t.shape[1]
        lane_w = _iota((1, nw), 1)
        sw = _dot(q4b, win_t[0:LANES, :].astype(BF16))
        validw = (past - nw + lane_w) > past - WINDOW
        neww = newwin_ref[0]
        sw_new = jnp.sum(q4 * neww[:, 0:LANES], axis=1, keepdims=True)
        mw = jnp.maximum(jnp.max(jnp.where(validw, sw, NEG), axis=1, keepdims=True), sw_new)
        pw = jnp.where(validw, jnp.exp(sw - mw), 0.0)
        pn = jnp.exp(sw_new - mw)
        zw = jnp.sum(pw, axis=1, keepdims=True) + pn
        o_w = (_dot(pw.astype(BF16), win_t[LANES:2 * LANES, :].astype(BF16), NT) + pn * neww[:, LANES:2 * LANES]) / zw
        gg = _sigmoid(gq_ref[0])
        o_ref[0] = o_c * gg[:, 0:1] + o_s * gg[:, 1:2] + o_w * gg[:, 2:3]
        wo_ref[0] = jnp.where(lane_w == nw - 1, newcol_ref[0], pltpu.roll(win_t, nw - 1, 1))


def _nsa_sample(pt, q4, gq, newkv, newwin, newcol, win_t, win_off, wall, pool_t, pp):
    bd, npages = pt.shape
    nw = win_t.shape[2]
    nb = 2 * npages
    nbp = -(-nb // LANES) * LANES
    n_top = min(N_SELECT, nb + 1)
    per_b = lambda b, i, pt_ref: (b, 0, 0)
    const = lambda b, i, pt_ref: (0, 0)

    def page_map(kk):
        return lambda b, i, pt_ref: (pt_ref[b, i * pp + kk], 0, 0)

    grid_spec = pltpu.PrefetchScalarGridSpec(
        num_scalar_prefetch=1,
        grid=(bd, npages // pp),
        in_specs=[pl.BlockSpec((1, 8, LANES), per_b),
                  pl.BlockSpec((1, 8, LANES), per_b),
                  pl.BlockSpec((1, 1, 4 * LANES), per_b),
                  pl.BlockSpec((1, 1, 2 * LANES), per_b),
                  pl.BlockSpec((1, 2 * LANES, 1), per_b),
                  pl.BlockSpec((1, 2 * LANES, nw), lambda b, i, pt_ref: (win_off + b, 0, 0)),
                  pl.BlockSpec(wall.shape, const)]
                 + [pl.BlockSpec((1, 4 * LANES, LANES), page_map(kk)) for kk in range(pp)],
        out_specs=[pl.BlockSpec((1, 8, LANES), per_b),
                   pl.BlockSpec((1, 2 * LANES, nw), per_b)],
        scratch_shapes=[pltpu.VMEM((npages * 8, 2 * LANES), F32),
                        pltpu.VMEM((npages * 8, 2 * LANES), F32),
                        pltpu.VMEM((8, nbp), F32),
                        pltpu.VMEM((8, nbp), F32),
                        pltpu.VMEM((N_HEADS, nbp, LANES), F32)])
    return pl.pallas_call(
        functools.partial(_nsa_sample_kernel, pp=pp, npages=npages, n_top=n_top),
        grid_spec=grid_spec,
        out_shape=[jax.ShapeDtypeStruct((bd, 8, LANES), F32),
                   jax.ShapeDtypeStruct((bd, 2 * LANES, nw), F32)],
        compiler_params=_cparams(("parallel", "arbitrary")), name="nsa_sample",
    )(pt, q4, gq, newkv, newwin, newcol, win_t, wall, *([pool_t] * pp))


def _out_kernel(x_ref, ya_ref, yb_ref, yc_ref, yd_ref, wo_ref, g2_ref, wq_ref, k1_ref, k2_ref,
                xn_ref, ht_ref, s1_ref, s2_ref, *, nh, dq):
    acc = x_ref[...]
    for idx, y_ref in enumerate((ya_ref, yb_ref, yc_ref, yd_ref)):
        acc = acc + _dot(y_ref[...].astype(BF16), wo_ref[idx * MIX_W:(idx + 1) * MIX_W, :])
    xn_ref[...] = acc
    h2 = _rmsnorm(acc, g2_ref[...])
    ht_ref[...] = h2.T.astype(BF16)
    q = _dot(h2.astype(BF16), wq_ref[...])
    half = dq // 2
    for h in range(nh):
        s1_ref[h] = _dot(k1_ref[h], q[:, h * dq:h * dq + half].astype(BF16), NT)
        s2_ref[h] = _dot(k2_ref[h], q[:, h * dq + half:(h + 1) * dq].astype(BF16), NT)


def _out_proj(x, ya, yb, yc, yd, wo, g2, wq, k1, k2, tm):
    n, d = x.shape
    nh, nk, half = k1.shape
    row = lambda i: (i, 0)
    c2 = lambda i: (0, 0)
    return pl.pallas_call(
        functools.partial(_out_kernel, nh=nh, dq=2 * half),
        grid=(n // tm,),
        in_specs=[pl.BlockSpec((tm, d), row)] + [pl.BlockSpec((tm, MIX_W), row)] * 4
                 + [pl.BlockSpec(wo.shape, c2), pl.BlockSpec((1, d), c2), pl.BlockSpec(wq.shape, c2),
                    pl.BlockSpec(k1.shape, lambda i: (0, 0, 0)), pl.BlockSpec(k2.shape, lambda i: (0, 0, 0))],
        out_specs=[pl.BlockSpec((tm, d), row),
                   pl.BlockSpec((d, tm), lambda i: (0, i)),
                   pl.BlockSpec((nh, nk, tm), lambda i: (0, 0, i)),
                   pl.BlockSpec((nh, nk, tm), lambda i: (0, 0, i))],
        out_shape=[jax.ShapeDtypeStruct((n, d), F32),
                   jax.ShapeDtypeStruct((d, n), BF16),
                   jax.ShapeDtypeStruct((nh, nk, n), F32),
                   jax.ShapeDtypeStruct((nh, nk, n), F32)],
        compiler_params=_cparams(("parallel",)), name="out_proj",
    )(x, ya, yb, yc, yd, wo, g2, wq, k1, k2)


_PAIR_COUNTS = tuple(PEER_TOPK // (i + 1) for i in range(PEER_TOPK))
_PAIR_ROWS = -(-sum(_PAIR_COUNTS) // 8) * 8


def _gate_kernel(s1_ref, s2_ref, rk2_ref, e2_ref, rr_ref, c_ref, v1_s, v2_s, c_s):
    x1 = s1_ref[0]
    x2 = s2_ref[0]
    rank1 = jnp.full(x1.shape, float(PEER_TOPK), F32)
    rank2 = jnp.full(x2.shape, float(PEER_TOPK), F32)
    for i in range(PEER_TOPK):
        m1 = jnp.max(x1, axis=0, keepdims=True)
        m2 = jnp.max(x2, axis=0, keepdims=True)
        v1_s[i:i + 1, :] = m1
        v2_s[i:i + 1, :] = m2
        hit1 = x1 == m1
        hit2 = x2 == m2
        rank1 = jnp.where(hit1, float(i), rank1)
        rank2 = jnp.where(hit2, float(i), rank2)
        x1 = jnp.where(hit1, NEG, x1)
        x2 = jnp.where(hit2, NEG, x2)
    v1 = v1_s[...]
    v2 = v2_s[...]
    off = 0
    for i, cnt in enumerate(_PAIR_COUNTS):
        c_s[off:off + cnt, :] = v1[i:i + 1, :] + v2[0:cnt, :]
        off += cnt
    if off < _PAIR_ROWS:
        c_s[off:_PAIR_ROWS, :] = jnp.full((_PAIR_ROWS - off, v1.shape[1]), NEG, F32)
    c = c_s[...]
    mx = v1[0:1, :] + v2[0:1, :]
    z = jnp.zeros_like(mx)
    m = mx
    for i in range(PEER_TOPK):
        m = jnp.max(c, axis=0, keepdims=True)
        z = z + jnp.exp(m - mx)
        c = jnp.where(c == m, NEG, c)
    tau = m
    passing = jnp.zeros(v1.shape, F32)
    for jj in range(PEER_TOPK):
        passing = passing + jnp.where(v1 + v2[jj:jj + 1, :] >= tau, 1.0, 0.0)
    rr = jnp.zeros(rank1.shape, F32)
    for i in range(PEER_TOPK):
        rr = jnp.where(rank1 == float(i), passing[i:i + 1, :], rr)
    rk2_ref[0] = rank2.astype(BF16)
    rr_ref[0] = rr
    e2_ref[0] = jnp.exp(s2_ref[0] - v2[0:1, :]).astype(BF16)
    c_ref[0] = jnp.exp(s1_ref[0] - v1[0:1, :]) * (1.0 / z)


def _gate(s1t, s2t, tn):
    nh, nk, n = s1t.shape
    blk = lambda h, i: (h, 0, i)
    return pl.pallas_call(
        _gate_kernel,
        grid=(nh, n // tn),
        in_specs=[pl.BlockSpec((1, nk, tn), blk), pl.BlockSpec((1, nk, tn), blk)],
        out_specs=[pl.BlockSpec((1, nk, tn), blk)] * 4,
        out_shape=[jax.ShapeDtypeStruct((nh, nk, n), BF16), jax.ShapeDtypeStruct((nh, nk, n), BF16),
                   jax.ShapeDtypeStruct((nh, nk, n), F32), jax.ShapeDtypeStruct((nh, nk, n), F32)],
        scratch_shapes=[pltpu.VMEM((PEER_TOPK, tn), F32), pltpu.VMEM((PEER_TOPK, tn), F32),
                        pltpu.VMEM((_PAIR_ROWS, tn), F32)],
        compiler_params=_cparams(("parallel", "parallel")), name="peer_gate",
    )(s1t, s2t)


def _bf16_rows(row, n_rows):
    tile = jnp.broadcast_to(row, (16, row.shape[1])).astype(BF16)
    return jnp.concatenate([tile] * (n_rows // 16), axis=0)


def _peer_kernel(ht_ref, u_ref, vt_ref, rk2_ref, e2_ref, rr_ref, c_ref, xn_ref, fg_ref, o_ref,
                 acc_ref, f_ref, *, nh, nk, apc, final):
    j = pl.program_id(1)

    @pl.when(j == 0)
    def _():
        acc_ref[...] = jnp.zeros(acc_ref.shape, F32)

    ht = ht_ref[...]
    zero = jnp.zeros((), BF16)
    for aa in range(apc):
        a = j * apc + aa
        act = _gelu(_dot(u_ref[aa * nk:(aa + 1) * nk, :], ht)).astype(BF16)
        wgt = jnp.zeros(act.shape, BF16)
        for h in range(nh):
            limit = _bf16_rows(rr_ref[h, pl.ds(a, 1), :], nk)
            scale = _bf16_rows(c_ref[h, pl.ds(a, 1), :], nk)
            wgt = wgt + jnp.where(rk2_ref[h] < limit, e2_ref[h] * scale, zero)
        f_ref[aa * nk:(aa + 1) * nk, :] = wgt * act
    acc_ref[...] += _dot(vt_ref[...], f_ref[...])

    @pl.when(j == pl.num_programs(1) - 1)
    def _():
        out = xn_ref[...] + acc_ref[...].T
        if final:
            out = _rmsnorm(out, fg_ref[...])
        o_ref[...] = out


def _peer(ht, u, vt, rk2, e2, rr, c, xn, fg, tm, apc, final):
    d, n = ht.shape
    nh, nk, _ = rk2.shape
    assert nk % 16 == 0
    ne = u.shape[0]
    te = apc * nk
    tok3 = lambda i, j: (0, 0, i)
    return pl.pallas_call(
        functools.partial(_peer_kernel, nh=nh, nk=nk, apc=apc, final=final),
        grid=(n // tm, ne // te),
        in_specs=[pl.BlockSpec((d, tm), lambda i, j: (0, i)),
                  pl.BlockSpec((te, d), lambda i, j: (j, 0)),
                  pl.BlockSpec((d, te), lambda i, j: (0, j)),
                  pl.BlockSpec((nh, nk, tm), tok3),
                  pl.BlockSpec((nh, nk, tm), tok3),
                  pl.BlockSpec((nh, nk, tm), tok3),
                  pl.BlockSpec((nh, nk, tm), tok3),
                  pl.BlockSpec((tm, d), lambda i, j: (i, 0)),
                  pl.BlockSpec((1, d), lambda i, j: (0, 0))],
        out_specs=pl.BlockSpec((tm, d), lambda i, j: (i, 0)),
        out_shape=jax.ShapeDtypeStruct((n, d), F32),
        scratch_shapes=[pltpu.VMEM((d, tm), F32), pltpu.VMEM((te, tm), BF16)],
        compiler_params=_cparams(("parallel", "arbitrary")), name="peer",
    )(ht, u, vt, rk2, e2, rr, c, xn, fg)


def _rope_tables(pos):
    half = HEAD_DIM // 2
    inv_freq = ROPE_THETA ** (-jnp.arange(half, dtype=F32) / half)
    ang = pos.astype(F32)[:, None] * inv_freq[None, :]
    cos = jnp.cos(ang)
    sin = jnp.sin(ang)
    return jnp.tile(jnp.concatenate([cos, cos], axis=1), (1, 2)), jnp.tile(jnp.concatenate([-sin, sin], axis=1), (1, 2))


def _pad_rows(x, rows):
    return jnp.pad(x, ((0, rows - x.shape[0]),) + ((0, 0),) * (x.ndim - 1))


def _token_tile(n, pref):
    return pref if n % pref == 0 else n


def _page_pool_weights(wk, wv):
    wset = jnp.stack([wk[:, 0], wk[:, 1], wv[:, 0], wv[:, 1]]).reshape(4, 2, CMP_STRIDE)
    per_pos = jnp.tile(wset, (1, 1, LANES // CMP_STRIDE))
    chunk_of_pos = (jnp.arange(LANES) // CMP_STRIDE)[None, :] == jnp.arange(LANES // CMP_STRIDE)[:, None]
    return (per_pos[:, :, None, :] * chunk_of_pos[None, None].astype(F32)).reshape(64, LANES)


def kernel(x_prompt, x_sample, cache_nsa_kv, state_nsa_win, state_mlstm_C, state_mlstm_n, state_mlstm_m, state_conv, page_table, norm1_g, norm2_g, final_norm_g, w_in, w_out, mlstm_b_i, mlstm_b_f, mlstm_norm_g, conv_w, conv_b, conv_norm_g, conv_norm_b, nsa_cmp_wk, nsa_cmp_wv, gmlp_norm_g, gmlp_norm_b, gmlp_ws, gmlp_bs, peer_wq, peer_k1, peer_k2, peer_u, peer_v):
    depth = w_in.shape[0]
    bp, s_len, d = x_prompt.shape
    bd, t_dec, _ = x_sample.shape
    n_pool = cache_nsa_kv.shape[1]
    npages = page_table.shape[1]
    past = npages * cache_nsa_kv.shape[2]
    n_win = state_nsa_win.shape[2]
    assert t_dec == 1 and cache_nsa_kv.shape[2] == LANES and past % SEL_BLOCK == 0
    assert s_len >= WINDOW + 128 and n_win == WINDOW
    np_tok = bp * s_len
    ns_pad = LANES
    assert bd <= ns_pad

    xp = x_prompt.reshape(np_tok, d)
    xs = _pad_rows(x_sample.reshape(bd, d), ns_pad)
    cos_p, sin_p = _rope_tables(jnp.arange(s_len))
    cos_s, sin_s = _rope_tables(jnp.full((ns_pad,), past))
    pool_t = cache_nsa_kv.reshape(depth * n_pool, LANES, 4 * LANES).transpose(0, 2, 1)
    win_t = state_nsa_win.reshape(depth * bd, n_win, 2 * LANES).transpose(0, 2, 1)
    eye_h = jnp.eye(N_HEADS, dtype=F32)

    tm_in = _token_tile(s_len, 256)
    tm_out = _token_tile(np_tok, 256)
    tm_peer = _token_tile(np_tok, 512)
    nk = peer_k1.shape[2]
    apc = min(8, nk)

    new_p, new_s = [], []
    for l in range(depth):
        wi = w_in[l]
        w_perm = jnp.concatenate([wi[:, 0:1024], wi[:, 1032:2568], wi[:, 2580:3092]], axis=1).astype(BF16)
        w_small = jnp.concatenate([wi[:, 1024:1032], wi[:, 2568:2580], jnp.zeros((d, 108), F32)], axis=1).astype(BF16)
        g1 = norm1_g[l].reshape(1, d)
        g2 = norm2_g[l].reshape(1, d)
        gate_bias = jnp.concatenate([mlstm_b_i[l], mlstm_b_f[l], jnp.zeros((LANES - 2 * N_HEADS,), F32)]).reshape(1, LANES)
        mng = mlstm_norm_g[l].reshape(1, MIX_W)
        cw = conv_w[l]
        cbias = conv_b[l].reshape(1, MIX_W)
        cg = conv_norm_g[l].reshape(1, MIX_W)
        cbeta = conv_norm_b[l].reshape(1, MIX_W)
        w32 = jnp.concatenate([jnp.repeat(nsa_cmp_wk[l], HEAD_DIM, axis=1), jnp.repeat(nsa_cmp_wv[l], HEAD_DIM, axis=1)], axis=1)
        wlo, whi = w32[:CMP_STRIDE], w32[CMP_STRIDE:]
        wall = _page_pool_weights(nsa_cmp_wk[l], nsa_cmp_wv[l])
        gg = gmlp_norm_g[l].reshape(1, MIX_W)
        gbeta = gmlp_norm_b[l].reshape(1, MIX_W)
        gws = gmlp_ws[l]
        gbsb = jnp.repeat(gmlp_bs[l].T, HEAD_DIM, axis=1)
        wo = w_out[l].astype(BF16)
        wq = peer_wq[l].astype(BF16)
        k1 = peer_k1[l].astype(BF16)
        k2 = peer_k2[l].astype(BF16)
        ub = peer_u[l].astype(BF16)
        vtb = peer_v[l].astype(BF16).T
        fg = final_norm_g.reshape(1, d)
        final = l == depth - 1

        oa, ob, oq, okv, owin, okaug, ovsb, owinb, od, osm = _in_proj(xp, g1, w_perm, w_small, cos_p, sin_p, tm_in)
        ya, c_p, n_p, m_p = _mlstm(oa.reshape(bp, s_len, 1024), osm.reshape(bp, s_len, LANES), gate_bias, mng,
                                   jnp.zeros((bp, MIX_W, MIX_W), F32), jnp.zeros((bp, 1, MIX_W), F32),
                                   jnp.zeros((bp, 1, MIX_W), F32), MLSTM_CHUNK, MLSTM_CHUNK, bp)
        yb, conv_p = _conv(ob.reshape(bp, s_len, 2 * MIX_W), jnp.zeros((bp, CONV_PAD, MIX_W), F32),
                           cw, cbias, cg, cbeta, 512, 512)
        cb = _cmp(okv.reshape(bp, s_len, 4 * LANES), wlo, whi)
        yc = _nsa_prompt(oq.reshape(bp, s_len, MIX_W), osm.reshape(bp, s_len, LANES), cb,
                         okaug.reshape(bp, s_len, 2 * LANES), ovsb.reshape(bp, s_len, LANES),
                         owinb.reshape(bp, s_len, 2 * LANES), 128)
        yd, _ = _gmlp(od, gg, gbeta, gws, gbsb, 4)
        xn, ht, s1t, s2t = _out_proj(xp, ya.reshape(np_tok, MIX_W), yb.reshape(np_tok, MIX_W),
                                     yc.reshape(np_tok, MIX_W), yd, wo, g2, wq, k1, k2, tm_out)
        rk2, e2, rr, cw8 = _gate(s1t, s2t, _token_tile(np_tok, 512))
        xp = _peer(ht, ub, vtb, rk2, e2, rr, cw8, xn, fg, tm_peer, apc, final)
        new_p.append((okv.reshape(bp, s_len, 4, NSA_KV, HEAD_DIM),
                      owin.reshape(bp, s_len, 2, NSA_KV, HEAD_DIM)[:, s_len - n_win:],
                      jnp.stack([c_p[:, h * HEAD_DIM:(h + 1) * HEAD_DIM, h * HEAD_DIM:(h + 1) * HEAD_DIM]
                                 for h in range(N_HEADS)], axis=1),
                      n_p.reshape(bp, N_HEADS, HEAD_DIM),
                      m_p[:, 0, ::HEAD_DIM],
                      conv_p[:, CONV_PAD - (CONV_W - 1):]))

        sa, sb, sq, skv, swin, _, _, _, sd, ssm = _in_proj(xs, g1, w_perm, w_small, cos_s, sin_s, ns_pad)
        rows8 = lambda t: jnp.pad(t[:bd, None, :], ((0, 0), (0, 7), (0, 0)))
        c0 = jnp.einsum('bhvk,hg->bhvgk', state_mlstm_C[l], eye_h).reshape(bd, MIX_W, MIX_W)
        n0 = state_mlstm_n[l].reshape(bd, 1, MIX_W)
        m0 = jnp.repeat(state_mlstm_m[l], HEAD_DIM, axis=-1).reshape(bd, 1, MIX_W)
        bb_s = 2 if bd % 2 == 0 else 1
        ya_s, c_s, n_s, m_s = _mlstm(rows8(sa), rows8(ssm), gate_bias, mng, c0, n0, m0, 8, 1, bb_s)
        prefix = jnp.pad(state_conv[l], ((0, 0), (CONV_PAD - (CONV_W - 1), 0), (0, 0)))
        yb_s, conv_s = _conv(rows8(sb), prefix, cw, cbias, cg, cbeta, 8, 1)
        zd_s = jnp.pad(sd[:bd, None, :], ((0, 0), (0, GMLP_CHUNK - 1), (0, 0))).reshape(bd * GMLP_CHUNK, 2 * MIX_W)
        yd_s, v_s = _gmlp(zd_s, gg, gbeta, gws, gbsb, 1)
        q_heads = sq[:bd].reshape(bd, NSA_KV, 2, 1, HEAD_DIM)
        q4 = (q_heads * jnp.eye(NSA_KV, dtype=F32)[None, :, None, :, None]).reshape(bd, N_HEADS, LANES)
        q4 = jnp.pad(q4, ((0, 0), (0, 8 - N_HEADS), (0, 0)))
        gq = jnp.pad(ssm[:bd, 2 * N_HEADS:2 * N_HEADS + 3 * N_HEADS].reshape(bd, N_HEADS, 3),
                     ((0, 0), (0, 8 - N_HEADS), (0, LANES - 3)))
        pp = 16 if npages % 16 == 0 else (8 if npages % 8 == 0 else 1)
        o8, win_s = _nsa_sample(page_table + l * n_pool, q4, gq, skv[:bd, None, :], swin[:bd, None, :],
                                swin[:bd, :, None], win_t, l * bd, wall, pool_t, pp)
        yc_s = jnp.concatenate([o8[:, h, (h // 2) * HEAD_DIM:(h // 2 + 1) * HEAD_DIM] for h in range(N_HEADS)], axis=1)
        xn_s, ht_s, s1t_s, s2t_s = _out_proj(xs, _pad_rows(ya_s[:, 0], ns_pad), _pad_rows(yb_s[:, 0], ns_pad),
                                             _pad_rows(yc_s, ns_pad),
                                             _pad_rows(yd_s.reshape(bd, GMLP_CHUNK, MIX_W)[:, 0], ns_pad),
                                             wo, g2, wq, k1, k2, ns_pad)
        rk2_s, e2_s, rr_s, cw8_s = _gate(s1t_s, s2t_s, ns_pad)
        xs = _peer(ht_s, ub, vtb, rk2_s, e2_s, rr_s, cw8_s, xn_s, fg, ns_pad, apc, final)
        new_s.append((skv[:bd].reshape(bd, 1, 4, NSA_KV, HEAD_DIM),
                      win_s.transpose(0, 2, 1).reshape(bd, n_win, 2, NSA_KV, HEAD_DIM),
                      jnp.stack([c_s[:, h * HEAD_DIM:(h + 1) * HEAD_DIM, h * HEAD_DIM:(h + 1) * HEAD_DIM]
                                 for h in range(N_HEADS)], axis=1),
                      n_s.reshape(bd, N_HEADS, HEAD_DIM),
                      m_s[:, 0, ::HEAD_DIM],
                      conv_s[:, CONV_PAD - (CONV_W - 1):],
                      v_s.reshape(bd, GMLP_CHUNK, MIX_W)[:, 0:1]))

    def stack(states, i):
        return jnp.stack([st[i] for st in states])

    y_prompt = xp.reshape(bp, s_len, d)
    y_sample = xs[:bd].reshape(bd, 1, d)
    return (y_prompt, y_sample,
            stack(new_p, 0), stack(new_s, 0),
            stack(new_p, 1), stack(new_s, 1),
            stack(new_p, 2), stack(new_s, 2),
            stack(new_p, 3), stack(new_s, 3),
            stack(new_p, 4), stack(new_s, 4),
            stack(new_p, 5), stack(new_s, 5),
            stack(new_s, 6))
```

```python
import functools

import jax
import jax.numpy as jnp
from jax import lax
from jax.experimental import pallas as pl
from jax.experimental.pallas import tpu as pltpu

F32 = jnp.float32
BF16 = jnp.bfloat16
HIGHEST = lax.Precision.HIGHEST

HEAD_DIM = 64
N_HEADS = 4
MIX_W = 256
NSA_KV = 2
NORM_EPS = 1e-6
NEG = -1e30
MLSTM_CHUNK = 128
CONV_W = 31
CONV_PAD = 32
CMP_STRIDE = 16
CMP_LEN = 32
SEL_BLOCK = 64
N_SELECT = 16
SEL_FORCE = 1e9
WINDOW = 512
ROPE_THETA = 10000.0
ATTN_SCALE = HEAD_DIM ** -0.5
GMLP_CHUNK = 128
PEER_TOPK = 16
LANES = 128
VMEM_LIMIT = 48 * 1024 * 1024

NT = (((1,), (1,)), ((), ()))
TN = (((0,), (0,)), ((), ()))


def _cparams(sem):
    return pltpu.CompilerParams(dimension_semantics=sem, vmem_limit_bytes=VMEM_LIMIT)


def _iota(shape, dim):
    return lax.broadcasted_iota(jnp.int32, shape, dim)


def _sigmoid(x):
    return 1.0 / (1.0 + jnp.exp(-x))


def _gelu(x):
    c1 = -2.0 * 0.7978845608028654
    c2 = c1 * 0.044715
    return x / (1.0 + jnp.exp(x * (c1 + c2 * (x * x))))


def _dot(a, b, dims=None, precision=None):
    if dims is None:
        return jnp.dot(a, b, preferred_element_type=F32, precision=precision)
    return lax.dot_general(a, b, dims, preferred_element_type=F32, precision=precision)


def _block_ones(n, blk):
    return (_iota((n, n), 0) // blk == _iota((n, n), 1) // blk).astype(F32)


def _group_layernorm(x, ee, width):
    mu = _dot(x, ee, precision=HIGHEST) * (1.0 / width)
    d = x - mu
    var = _dot(d * d, ee, precision=HIGHEST) * (1.0 / width)
    return d * lax.rsqrt(var + NORM_EPS)


def _rmsnorm(x, g):
    return x * lax.rsqrt(jnp.mean(x * x, axis=-1, keepdims=True) + NORM_EPS) * g


def _masked_softmax(s, valid):
    sm = jnp.where(valid, s, NEG)
    p = jnp.where(valid, jnp.exp(sm - jnp.max(sm, axis=-1, keepdims=True)), 0.0)
    return p / jnp.maximum(jnp.sum(p, axis=-1, keepdims=True), 1e-30)


def _rope(x, cos, sin_signed):
    w = x.shape[1]
    fwd = pltpu.roll(x, w - HEAD_DIM // 2, 1)
    bwd = pltpu.roll(x, HEAD_DIM // 2, 1)
    first = (_iota(x.shape, 1) % HEAD_DIM) < HEAD_DIM // 2
    return x * cos + jnp.where(first, fwd, bwd) * sin_signed


def _in_kernel(x_ref, g_ref, w_ref, wsmall_ref, cos_ref, sin_ref,
               oa_ref, ob_ref, oq_ref, okv_ref, owin_ref, okaug_ref, ovsb_ref, owinb_ref, od_ref, os_ref, *, npb):
    tm = x_ref.shape[0]
    y = _rmsnorm(x_ref[...], g_ref[...])
    yb = y.astype(BF16)
    z = _dot(yb, w_ref[...], NT)
    os_ref[...] = _dot(yb, wsmall_ref[...], NT)
    oa_ref[...] = z[:, 0:1024]
    ob_ref[...] = z[:, 1024:1536]
    cos = cos_ref[...]
    sin = sin_ref[...]
    cos2 = jnp.concatenate([cos, cos], axis=1)
    sin2 = jnp.concatenate([sin, sin], axis=1)
    oq_ref[...] = _rope(z[:, 1536:1792], cos2, sin2) * ATTN_SCALE
    kc = _rope(z[:, 1792:1920], cos, sin)
    vc = z[:, 1920:2048]
    ks = _rope(z[:, 2048:2176], cos, sin)
    vs = z[:, 2176:2304]
    kw = _rope(z[:, 2304:2432], cos, sin)
    vw = z[:, 2432:2560]
    okv_ref[...] = jnp.concatenate([kc, vc, ks, vs], axis=1)
    owin = jnp.concatenate([kw, vw], axis=1)
    owin_ref[...] = owin
    pos = (pl.program_id(0) % npb) * tm + _iota((tm, LANES), 0)
    onehot = jnp.where(_iota((tm, LANES), 1) == pos // SEL_BLOCK, 1.0, 0.0)
    okaug_ref[...] = jnp.concatenate([ks, onehot], axis=1).astype(BF16)
    ovsb_ref[...] = vs.astype(BF16)
    owinb_ref[...] = owin.astype(BF16)
    od_ref[...] = z[:, 2560:3072]


def _in_proj(x, g, w, wsmall, cos_t, sin_t, tm):
    n, d = x.shape
    npb = cos_t.shape[0] // tm
    widths = (1024, 512, 256, 512, 256, 256, 128, 256, 512, 128)
    dtypes = (F32, F32, F32, F32, F32, BF16, BF16, BF16, F32, F32)
    row = lambda i: (i, 0)
    return pl.pallas_call(
        functools.partial(_in_kernel, npb=npb),
        grid=(n // tm,),
        in_specs=[pl.BlockSpec((tm, d), row),
                  pl.BlockSpec((1, d), lambda i: (0, 0)),
                  pl.BlockSpec(w.shape, lambda i: (0, 0)),
                  pl.BlockSpec(wsmall.shape, lambda i: (0, 0)),
                  pl.BlockSpec((tm, LANES), lambda i: (i % npb, 0)),
                  pl.BlockSpec((tm, LANES), lambda i: (i % npb, 0))],
        out_specs=[pl.BlockSpec((tm, wd), row) for wd in widths],
        out_shape=[jax.ShapeDtypeStruct((n, wd), dt) for wd, dt in zip(widths, dtypes)],
        compiler_params=_cparams(("parallel",)), name="in_proj",
    )(x, g, w, wsmall, cos_t, sin_t)


def _mlstm_kernel(za_ref, zs_ref, bias_ref, ng_ref, c0_ref, n0_ref, m0_ref,
                  y_ref, c_ref, n_ref, m_ref, *, L, t_valid, bb):
    @pl.when(pl.program_id(1) == 0)
    def _():
        c_ref[...] = c0_ref[...]
        n_ref[...] = n0_ref[...]
        m_ref[...] = m0_ref[...]

    head_of_lane = _iota((1, MIX_W), 1) // HEAD_DIM
    causal = _iota((L, L), 0) >= _iota((L, L), 1)
    tril = causal.astype(F32)
    ee = _block_ones(MIX_W, HEAD_DIM)
    bdiag = ee > 0.5
    row128 = _iota((L, LANES), 0)
    lane128 = _iota((L, LANES), 1)
    for bi in range(bb):
        za = za_ref[bi]
        q = za[:, 0:256]
        k = za[:, 256:512] * (HEAD_DIM ** -0.5)
        v = za[:, 512:768]
        o = za[:, 768:1024]
        gi = zs_ref[bi] + bias_ref[...]
        ls = jnp.minimum(gi, 0.0) - jnp.log(1.0 + jnp.exp(-jnp.abs(gi)))
        if t_valid < L:
            live = row128 < t_valid
            ig = jnp.where(live, gi, NEG)
            ls = jnp.where(live, ls, 0.0)
        else:
            ig = gi
        gmat = jnp.where(lane128 < N_HEADS, ig, 0.0)
        lfm = jnp.where((lane128 >= N_HEADS) & (lane128 < 2 * N_HEADS), ls, 0.0)
        fc = _dot(tril, lfm, precision=HIGHEST)
        g_t = gmat.T
        f_t = fc.T
        mrow = m_ref[bi]
        nrow = n_ref[bi]
        cm = c_ref[bi]
        qb = q.astype(BF16)
        kb = k.astype(BF16)
        vb = v.astype(BF16)
        num = jnp.zeros((L, MIX_W), F32)
        decay_b = jnp.zeros((L, MIX_W), F32)
        dens_b = jnp.zeros((L, MIX_W), F32)
        mt_b = jnp.zeros((L, MIX_W), F32)
        w_b = jnp.zeros((L, MIX_W), F32)
        cd_b = jnp.zeros((1, MIX_W), F32)
        mnew_b = jnp.zeros((1, MIX_W), F32)
        for h in range(N_HEADS):
            hm = head_of_lane == h
            f_col = fc[:, N_HEADS + h:N_HEADS + h + 1]
            ig_col = gmat[:, h:h + 1]
            f_row = f_t[N_HEADS + h:N_HEADS + h + 1, :]
            ig_row = g_t[h:h + 1, :]
            mp = mrow[:, h * HEAD_DIM:h * HEAD_DIM + 1]
            dmat = jnp.where(causal, (f_col - f_row) + ig_row, NEG)
            m_inter = mp + f_col
            m_t = jnp.maximum(m_inter, jnp.max(dmat, axis=1, keepdims=True))
            d_exp = jnp.exp(dmat - m_t)
            decay = jnp.exp(m_inter - m_t)
            qh = jnp.where(hm, q, 0.0).astype(BF16)
            s = _dot(qh, kb, NT) * d_exp
            num = jnp.where(hm, _dot(s.astype(BF16), vb), num)
            dens_b = jnp.where(hm, jnp.sum(s, axis=1, keepdims=True), dens_b)
            decay_b = jnp.where(hm, decay, decay_b)
            mt_b = jnp.where(hm, m_t, mt_b)
            f_last = f_col[L - 1:L, :]
            m_new = m_t[L - 1:L, :]
            w_b = jnp.where(hm, jnp.exp((f_last - f_col) + ig_col - m_new), w_b)
            cd_b = jnp.where(hm, jnp.exp(mp + f_last - m_new), cd_b)
            mnew_b = jnp.where(hm, m_new, mnew_b)
        inter = _dot(qb, cm.astype(BF16), NT)
        nq_b = _dot(q * nrow, ee, precision=HIGHEST)
        hnum = num + decay_b * inter
        den_b = dens_b + decay_b * nq_b
        hh = hnum / jnp.maximum(jnp.abs(den_b), jnp.exp(-mt_b))
        hn = _group_layernorm(hh, ee, HEAD_DIM) * ng_ref[...]
        y_ref[bi] = hn * _sigmoid(o)
        kw = k * w_b
        upd = _dot(vb, kw.astype(BF16), TN)
        c_ref[bi] = jnp.where(bdiag, cd_b * cm + upd, 0.0)
        n_ref[bi] = cd_b * nrow + jnp.sum(kw, axis=0, keepdims=True)
        m_ref[bi] = mnew_b


def _mlstm(za, zs, bias, ng, c0, n0, m0, L, t_valid, bb):
    b, t, _ = za.shape
    nc = t // L
    st = lambda i, c: (i, 0, 0)
    return pl.pallas_call(
        functools.partial(_mlstm_kernel, L=L, t_valid=t_valid, bb=bb),
        grid=(b // bb, nc),
        in_specs=[pl.BlockSpec((bb, L, 1024), lambda i, c: (i, c, 0)),
                  pl.BlockSpec((bb, L, LANES), lambda i, c: (i, c, 0)),
                  pl.BlockSpec((1, LANES), lambda i, c: (0, 0)),
                  pl.BlockSpec((1, MIX_W), lambda i, c: (0, 0)),
                  pl.BlockSpec((bb, MIX_W, MIX_W), st),
                  pl.BlockSpec((bb, 1, MIX_W), st),
                  pl.BlockSpec((bb, 1, MIX_W), st)],
        out_specs=[pl.BlockSpec((bb, L, MIX_W), lambda i, c: (i, c, 0)),
                   pl.BlockSpec((bb, MIX_W, MIX_W), st),
                   pl.BlockSpec((bb, 1, MIX_W), st),
                   pl.BlockSpec((bb, 1, MIX_W), st)],
        out_shape=[jax.ShapeDtypeStruct((b, t, MIX_W), F32),
                   jax.ShapeDtypeStruct((b, MIX_W, MIX_W), F32),
                   jax.ShapeDtypeStruct((b, 1, MIX_W), F32),
                   jax.ShapeDtypeStruct((b, 1, MIX_W), F32)],
        compiler_params=_cparams(("parallel", "arbitrary")), name="mlstm",
    )(za, zs, bias, ng, c0, n0, m0)


def _conv_kernel(zb_ref, pre_ref, w_ref, cb_ref, g_ref, beta_ref, y_ref, st_ref, ext_ref, *, tt, tv):
    @pl.when(pl.program_id(1) == 0)
    def _():
        ext_ref[0:CONV_PAD, :] = pre_ref[0]

    z = zb_ref[0]
    u = z[:, :MIX_W] * _sigmoid(z[:, MIX_W:])
    ext_ref[CONV_PAD:CONV_PAD + tt, :] = u
    off = CONV_PAD - (CONV_W - 1)
    acc = jnp.zeros((tt, MIX_W), F32)
    for j in range(CONV_W):
        acc = acc + w_ref[j:j + 1, :] * ext_ref[off + j:off + j + tt, :]
    ee = _block_ones(MIX_W, HEAD_DIM)
    y = _group_layernorm(acc + cb_ref[...], ee, HEAD_DIM) * g_ref[...] + beta_ref[...]
    y_ref[0] = y * _sigmoid(y)
    st_ref[0] = ext_ref[tv:tv + CONV_PAD, :]
    ext_ref[0:CONV_PAD, :] = ext_ref[tt:tt + CONV_PAD, :]


def _conv(zb, prefix, w, cb, g, beta, tt, tv):
    b, t, _ = zb.shape
    vec = lambda i, j: (0, 0)
    return pl.pallas_call(
        functools.partial(_conv_kernel, tt=tt, tv=tv),
        grid=(b, t // tt),
        in_specs=[pl.BlockSpec((1, tt, 2 * MIX_W), lambda i, j: (i, j, 0)),
                  pl.BlockSpec((1, CONV_PAD, MIX_W), lambda i, j: (i, 0, 0)),
                  pl.BlockSpec((CONV_W, MIX_W), vec),
                  pl.BlockSpec((1, MIX_W), vec),
                  pl.BlockSpec((1, MIX_W), vec),
                  pl.BlockSpec((1, MIX_W), vec)],
        out_specs=[pl.BlockSpec((1, tt, MIX_W), lambda i, j: (i, j, 0)),
                   pl.BlockSpec((1, CONV_PAD, MIX_W), lambda i, j: (i, 0, 0))],
        out_shape=[jax.ShapeDtypeStruct((b, t, MIX_W), F32),
                   jax.ShapeDtypeStruct((b, CONV_PAD, MIX_W), F32)],
        scratch_shapes=[pltpu.VMEM((CONV_PAD + tt, MIX_W), F32)],
        compiler_params=_cparams(("parallel", "arbitrary")), name="conv",
    )(zb, prefix, w, cb, g, beta)


def _gmlp_kernel(zd_ref, g_ref, beta_ref, ws_ref, bsb_ref, y_ref, v_ref, *, cpb):
    z = zd_ref[...]
    ee = _block_ones(MIX_W, HEAD_DIM)
    u = _gelu(z[:, :MIX_W])
    vv = _group_layernorm(_gelu(z[:, MIX_W:]), ee, HEAD_DIM) * g_ref[...] + beta_ref[...]
    v_ref[...] = vv
    tril = _iota((GMLP_CHUNK, GMLP_CHUNK), 0) >= _iota((GMLP_CHUNK, GMLP_CHUNK), 1)
    head_of_lane = _iota((1, MIX_W), 1) // HEAD_DIM
    wm = [jnp.where(tril, ws_ref[h], 0.0).astype(BF16) for h in range(N_HEADS)]
    for c in range(cpb):
        rows = slice(c * GMLP_CHUNK, (c + 1) * GMLP_CHUNK)
        vc = vv[rows].astype(BF16)
        mixed = jnp.zeros((GMLP_CHUNK, MIX_W), F32)
        for h in range(N_HEADS):
            mixed = jnp.where(head_of_lane == h, _dot(wm[h], vc), mixed)
        y_ref[rows, :] = u[rows] * (mixed + bsb_ref[...])


def _gmlp(zd, g, beta, ws, bsb, cpb):
    n = zd.shape[0]
    rows = cpb * GMLP_CHUNK
    vec = lambda i: (0, 0)
    return pl.pallas_call(
        functools.partial(_gmlp_kernel, cpb=cpb),
        grid=(n // rows,),
        in_specs=[pl.BlockSpec((rows, 2 * MIX_W), lambda i: (i, 0)),
                  pl.BlockSpec((1, MIX_W), vec),
                  pl.BlockSpec((1, MIX_W), vec),
                  pl.BlockSpec((N_HEADS, GMLP_CHUNK, GMLP_CHUNK), lambda i: (0, 0, 0)),
                  pl.BlockSpec((GMLP_CHUNK, MIX_W), vec)],
        out_specs=[pl.BlockSpec((rows, MIX_W), lambda i: (i, 0)),
                   pl.BlockSpec((rows, MIX_W), lambda i: (i, 0))],
        out_shape=[jax.ShapeDtypeStruct((n, MIX_W), F32),
                   jax.ShapeDtypeStruct((n, MIX_W), F32)],
        compiler_params=_cparams(("parallel",)), name="gmlp",
    )(zd, g, beta, ws, bsb)


def _combine_lo_hi(lo, hi):
    r = lo.shape[0]
    nxt = pltpu.roll(hi, r - 1, 0)
    return lo + jnp.where(_iota(lo.shape, 0) < r - 1, nxt, 0.0)


def _cmp_kernel(kv_ref, wlo_ref, whi_ref, o_ref):
    x = kv_ref[0]
    x3 = x.reshape(x.shape[0] // CMP_STRIDE, CMP_STRIDE, x.shape[1])
    lo = jnp.sum(x3 * wlo_ref[...][None], axis=1)
    hi = jnp.sum(x3 * whi_ref[...][None], axis=1)
    o_ref[0] = _combine_lo_hi(lo, hi)


def _cmp(okv3, wlo, whi):
    b, s, _ = okv3.shape
    return pl.pallas_call(
        _cmp_kernel,
        grid=(b,),
        in_specs=[pl.BlockSpec((1, s, 2 * LANES), lambda i: (i, 0, 0)),
                  pl.BlockSpec((CMP_STRIDE, 2 * LANES), lambda i: (0, 0)),
                  pl.BlockSpec((CMP_STRIDE, 2 * LANES), lambda i: (0, 0))],
        out_specs=pl.BlockSpec((1, s // CMP_STRIDE, 2 * LANES), lambda i: (i, 0, 0)),
        out_shape=jax.ShapeDtypeStruct((b, s // CMP_STRIDE, 2 * LANES), F32),
        compiler_params=_cparams(("parallel",)), name="nsa_cmp",
    )(okv3, wlo, whi)


def _nsa_prompt_kernel(q_ref, gs_ref, cb_ref, ka_ref, vs_ref, win_ref, y_ref, *, tq, s_len, n_top, tk):
    start = pl.program_id(1) * tq
    nc = s_len // CMP_STRIDE
    ns = s_len // SEL_BLOCK
    q = q_ref[0]
    gl = _sigmoid(gs_ref[0])
    lo_half = _iota((tq, LANES), 1) < HEAD_DIM
    q_rows = []
    for g in range(NSA_KV):
        qpair = q[:, g * LANES:(g + 1) * LANES]
        swapped = pltpu.roll(qpair, HEAD_DIM, 1)
        if g == 0:
            q_rows += [jnp.where(lo_half, qpair, 0.0), jnp.where(lo_half, swapped, 0.0)]
        else:
            q_rows += [jnp.where(lo_half, 0.0, swapped), jnp.where(lo_half, 0.0, qpair)]
    q128 = jnp.concatenate(q_rows, axis=0)
    q128b = q128.astype(BF16)
    trow = start + _iota((tq, 1), 0)
    t2 = jnp.concatenate([trow, trow], axis=0)
    t4 = jnp.concatenate([t2, t2], axis=0)
    cb = cb_ref[0]
    sc = _dot(q128b, cb[:, 0:LANES].astype(BF16), NT)
    validc = (_iota((1, nc), 1) * CMP_STRIDE + CMP_LEN) <= (t4 + 1)
    p = _masked_softmax(sc, validc)
    o_c = _dot(p.astype(BF16), cb[:, LANES:2 * LANES].astype(BF16))
    psum = jnp.concatenate([p[0:tq] + p[tq:2 * tq], p[2 * tq:3 * tq] + p[3 * tq:4 * tq]], axis=0)
    pool4 = (_iota((nc, ns), 0) // (SEL_BLOCK // CMP_STRIDE) == _iota((nc, ns), 1)).astype(F32)
    imp = _dot(psum, pool4, precision=HIGHEST)
    jidx = _iota((1, ns), 1)
    imp = jnp.where(jidx == t2 // SEL_BLOCK, SEL_FORCE, imp)
    imp = jnp.where(jidx * SEL_BLOCK <= t2, imp, -1.0)
    if n_top < ns:
        x = jnp.concatenate([imp[0:tq].T, imp[tq:2 * tq].T], axis=1)
        jrow = _iota((ns, 2 * tq), 0).astype(F32)
        sel_t = jnp.zeros((ns, 2 * tq), F32)
        for _ in range(n_top):
            m = jnp.max(x, axis=0, keepdims=True)
            first = jnp.min(jnp.where(x == m, jrow, float(ns)), axis=0, keepdims=True)
            hit = jrow == first
            sel_t = jnp.where(hit, 1.0, sel_t)
            x = jnp.where(hit, -3.0, x)
        sel = jnp.concatenate([sel_t[:, 0:tq].T, sel_t[:, tq:2 * tq].T], axis=0)
    else:
        sel = jnp.ones((2 * tq, ns), F32)
    selneg = (sel - 1.0) * (-NEG)
    if ns < LANES:
        selneg = jnp.concatenate([selneg, jnp.zeros((2 * tq, LANES - ns), F32)], axis=1)
    selneg4 = jnp.concatenate([selneg[0:tq], selneg[0:tq], selneg[tq:2 * tq], selneg[tq:2 * tq]], axis=0)
    qaug = jnp.concatenate([q128, selneg4], axis=1).astype(BF16)

    def tile(c, carry, diagonal):
        m, l, acc = carry
        k0 = pl.multiple_of(c * tk, tk)
        s = _dot(qaug, ka_ref[0, pl.ds(k0, tk), :], NT)
        if diagonal:
            s = jnp.where(k0 + _iota((1, tk), 1) <= t4, s, NEG)
        m_new = jnp.maximum(m, jnp.max(s, axis=1, keepdims=True))
        pt = jnp.exp(s - m_new)
        alpha = jnp.exp(m - m_new)
        l = alpha * l + jnp.sum(pt, axis=1, keepdims=True)
        acc = alpha * acc + _dot(pt.astype(BF16), vs_ref[0, pl.ds(k0, tk), :])
        return m_new, l, acc

    n_tiles = (start + tq + tk - 1) // tk
    carry = (jnp.full((4 * tq, 1), NEG, F32), jnp.zeros((4 * tq, 1), F32), jnp.zeros((4 * tq, LANES), F32))
    carry = lax.fori_loop(0, n_tiles - 1, lambda c, cr: tile(c, cr, False), carry)
    _, l, acc = tile(n_tiles - 1, carry, True)
    o_s = acc / jnp.maximum(l, 1e-30)
    wl = WINDOW + tq
    w0 = pl.multiple_of(jnp.maximum(start - WINDOW, 0), tq)
    sw = _dot(q128b, win_ref[0, pl.ds(w0, wl), 0:LANES], NT)
    kposw = w0 + _iota((1, wl), 1)
    validw = (kposw <= t4) & (kposw > t4 - WINDOW)
    o_w = _dot(_masked_softmax(sw, validw).astype(BF16), win_ref[0, pl.ds(w0, wl), LANES:2 * LANES])
    heads = []
    for hh in range(N_HEADS):
        col = 2 * N_HEADS + hh * 3
        rows = slice(hh * tq, (hh + 1) * tq)
        heads.append(o_c[rows] * gl[:, col:col + 1] + o_s[rows] * gl[:, col + 1:col + 2]
                     + o_w[rows] * gl[:, col + 2:col + 3])
    y_ref[0] = jnp.concatenate([jnp.where(lo_half, heads[0], pltpu.roll(heads[1], HEAD_DIM, 1)),
                                jnp.where(lo_half, pltpu.roll(heads[2], HEAD_DIM, 1), heads[3])], axis=1)


def _nsa_prompt(oq3, os3, cb, okaug3, ovsb3, owinb3, tq):
    b, s, _ = oq3.shape
    ns = s // SEL_BLOCK
    assert ns <= LANES
    n_top = min(N_SELECT, ns)
    tk = min(512, s)
    full = lambda i, j: (i, 0, 0)
    return pl.pallas_call(
        functools.partial(_nsa_prompt_kernel, tq=tq, s_len=s, n_top=n_top, tk=tk),
        grid=(b, s // tq),
        in_specs=[pl.BlockSpec((1, tq, MIX_W), lambda i, j: (i, j, 0)),
                  pl.BlockSpec((1, tq, LANES), lambda i, j: (i, j, 0)),
                  pl.BlockSpec((1, s // CMP_STRIDE, 2 * LANES), full),
                  pl.BlockSpec((1, s, 2 * LANES), full),
                  pl.BlockSpec((1, s, LANES), full),
                  pl.BlockSpec((1, s, 2 * LANES), full)],
        out_specs=pl.BlockSpec((1, tq, MIX_W), lambda i, j: (i, j, 0)),
        out_shape=jax.ShapeDtypeStruct((b, s, MIX_W), F32),
        compiler_params=_cparams(("parallel", "arbitrary")), name="nsa_prompt",
    )(oq3, os3, cb, okaug3, ovsb3, owinb3)


def _nsa_sample_kernel(pt_ref, q4_ref, gq_ref, newkv_ref, newwin_ref, newcol_ref, win_ref, wall_ref, *rest,
                       pp, npages, n_top):
    pool_refs = rest[:pp]
    o_ref, wo_ref, lo_s, hi_s, m_s, l_s, a_s = rest[pp:]
    i = pl.program_id(1)
    nb = 2 * npages
    nbp = m_s.shape[1]
    ncp = npages * (LANES // CMP_STRIDE)
    past = npages * LANES
    q4 = q4_ref[0]
    q4b = q4.astype(BF16)

    @pl.when(i == 0)
    def _():
        m_s[...] = jnp.zeros(m_s.shape, F32)
        l_s[...] = jnp.zeros(l_s.shape, F32)
        if nb < nbp:
            a_s[...] = jnp.zeros(a_s.shape, F32)

    lane8 = _iota((8, LANES), 1)
    lanej = _iota((8, nbp), 1)
    set_of_lane = _iota((8, 2 * LANES), 1) // HEAD_DIM
    for kk in range(pp):
        page = pool_refs[kk][0]
        pidx = i * pp + kk
        pooled = _dot(wall_ref[...].astype(BF16), page[0:2 * LANES, :].astype(BF16), NT)
        lo8 = jnp.zeros((8, 2 * LANES), F32)
        hi8 = jnp.zeros((8, 2 * LANES), F32)
        for st in range(4):
            lo8 = jnp.where(set_of_lane == st, pooled[st * 16:st * 16 + 8], lo8)
            hi8 = jnp.where(set_of_lane == st, pooled[st * 16 + 8:st * 16 + 16], hi8)
        r0 = pl.multiple_of(pidx * 8, 8)
        lo_s[pl.ds(r0, 8), :] = lo8
        hi_s[pl.ds(r0, 8), :] = hi8
        ks_t = page[2 * LANES:3 * LANES, :].astype(BF16)
        vs_t = page[3 * LANES:4 * LANES, :].astype(BF16)
        s = _dot(q4b, ks_t)
        for jb in range(LANES // SEL_BLOCK):
            inb = (lane8 >= jb * SEL_BLOCK) & (lane8 < (jb + 1) * SEL_BLOCK)
            sm = jnp.where(inb, s, NEG)
            m = jnp.max(sm, axis=1, keepdims=True)
            p = jnp.where(inb, jnp.exp(sm - m), 0.0)
            lsum = jnp.sum(p, axis=1, keepdims=True)
            a = _dot(p.astype(BF16), vs_t, NT)
            j = pidx * (LANES // SEL_BLOCK) + jb
            m_s[...] = jnp.where(lanej == j, m, m_s[...])
            l_s[...] = jnp.where(lanej == j, lsum, l_s[...])
            for hh in range(N_HEADS):
                a_s[hh, pl.ds(j, 1), :] = a[hh:hh + 1, :]

    @pl.when(i == pl.num_programs(1) - 1)
    def _():
        row8 = _iota((8, 1), 0)
        cbm = _combine_lo_hi(lo_s[...], hi_s[...])
        kcb = cbm[:, 0:LANES].astype(BF16)
        vcb = cbm[:, LANES:2 * LANES].astype(BF16)
        sc = _dot(q4b, kcb, NT)
        validc = (_iota((1, ncp), 1) * CMP_STRIDE + CMP_LEN) <= past + 1
        p = _masked_softmax(sc, validc)
        o_c = _dot(p.astype(BF16), vcb)
        pg = jnp.where(row8 == 0, p[0:1] + p[1:2], jnp.where(row8 == 1, p[2:3] + p[3:4], 0.0))
        pool4 = (_iota((ncp, nbp), 0) // (SEL_BLOCK // CMP_STRIDE) == _iota((ncp, nbp), 1)).astype(F32)
        imp2 = _dot(pg, pool4, precision=HIGHEST)
        ii = _iota((nbp, nbp), 0)
        jj = _iota((nbp, nbp), 1)
        sels = []
        for g in range(NSA_KV):
            mx = jnp.broadcast_to(imp2[g:g + 1, :], (nbp, nbp))
            mt = mx.T
            beats = ((mt > mx) | ((mt == mx) & (ii < jj))) & (ii < nb)
            rank = jnp.sum(jnp.where(beats, 1.0, 0.0), axis=0, keepdims=True)
            sels.append(jnp.where((rank < n_top - 1) & (_iota((1, nbp), 1) < nb), 1.0, 0.0))
        sel8 = jnp.where(row8 < 2, sels[0], sels[1]) > 0.5
        newkv = newkv_ref[0]
        ksn = newkv[:, 2 * LANES:3 * LANES]
        vsn = newkv[:, 3 * LANES:4 * LANES]
        s_new = jnp.sum(q4 * ksn, axis=1, keepdims=True)
        mrow = m_s[...]
        m_all = jnp.maximum(jnp.max(jnp.where(sel8, mrow, NEG), axis=1, keepdims=True), s_new)
        wj = jnp.where(sel8, jnp.exp(mrow - m_all), 0.0)
        w_new = jnp.exp(s_new - m_all)
        ltot = jnp.sum(wj * l_s[...], axis=1, keepdims=True) + w_new
        osum = w_new * vsn
        for hh in range(N_HEADS):
            osum = osum + jnp.where(row8 == hh, _dot(wj, a_s[hh], precision=HIGHEST), 0.0)
        o_s = osum / jnp.maximum(ltot, 1e-30)
        win_t = win_ref[0]
        nw = win_t.shape[1]
        lane_w = _iota((1, nw), 1)
        sw = _dot(q4b, win_t[0:LANES, :].astype(BF16))
        validw = (past - nw + lane_w) > past - WINDOW
        neww = newwin_ref[0]
        sw_new = jnp.sum(q4 * neww[:, 0:LANES], axis=1, keepdims=True)
        mw = jnp.maximum(jnp.max(jnp.where(validw, sw, NEG), axis=1, keepdims=True), sw_new)
        pw = jnp.where(validw, jnp.exp(sw - mw), 0.0)
        pn = jnp.exp(sw_new - mw)
        zw = jnp.sum(pw, axis=1, keepdims=True) + pn
        o_w = (_dot(pw.astype(BF16), win_t[LANES:2 * LANES, :].astype(BF16), NT) + pn * neww[:, LANES:2 * LANES]) / zw
        gg = _sigmoid(gq_ref[0])
        o_ref[0] = o_c * gg[:, 0:1] + o_s * gg[:, 1:2] + o_w * gg[:, 2:3]
        wo_ref[0] = jnp.where(lane_w == nw - 1, newcol_ref[0], pltpu.roll(win_t, nw - 1, 1))


def _nsa_sample(pt, q4, gq, newkv, newwin, newcol, win_t, win_off, wall, pool_t, pp):
    bd, npages = pt.shape
    nw = win_t.shape[2]
    nb = 2 * npages
    nbp = -(-nb // LANES) * LANES
    n_top = min(N_SELECT, nb + 1)
    per_b = lambda b, i, pt_ref: (b, 0, 0)
    const = lambda b, i, pt_ref: (0, 0)

    def page_map(kk):
        return lambda b, i, pt_ref: (pt_ref[b, i * pp + kk], 0, 0)

    grid_spec = pltpu.PrefetchScalarGridSpec(
        num_scalar_prefetch=1,
        grid=(bd, npages // pp),
        in_specs=[pl.BlockSpec((1, 8, LANES), per_b),
                  pl.BlockSpec((1, 8, LANES), per_b),
                  pl.BlockSpec((1, 1, 4 * LANES), per_b),
                  pl.BlockSpec((1, 1, 2 * LANES), per_b),
                  pl.BlockSpec((1, 2 * LANES, 1), per_b),
                  pl.BlockSpec((1, 2 * LANES, nw), lambda b, i, pt_ref: (win_off + b, 0, 0)),
                  pl.BlockSpec(wall.shape, const)]
                 + [pl.BlockSpec((1, 4 * LANES, LANES), page_map(kk)) for kk in range(pp)],
        out_specs=[pl.BlockSpec((1, 8, LANES), per_b),
                   pl.BlockSpec((1, 2 * LANES, nw), per_b)],
        scratch_shapes=[pltpu.VMEM((npages * 8, 2 * LANES), F32),
                        pltpu.VMEM((npages * 8, 2 * LANES), F32),
                        pltpu.VMEM((8, nbp), F32),
                        pltpu.VMEM((8, nbp), F32),
                        pltpu.VMEM((N_HEADS, nbp, LANES), F32)])
    return pl.pallas_call(
        functools.partial(_nsa_sample_kernel, pp=pp, npages=npages, n_top=n_top),
        grid_spec=grid_spec,
        out_shape=[jax.ShapeDtypeStruct((bd, 8, LANES), F32),
                   jax.ShapeDtypeStruct((bd, 2 * LANES, nw), F32)],
        compiler_params=_cparams(("parallel", "arbitrary")), name="nsa_sample",
    )(pt, q4, gq, newkv, newwin, newcol, win_t, wall, *([pool_t] * pp))


def _out_kernel(x_ref, ya_ref, yb_ref, yc_ref, yd_ref, wo_ref, g2_ref, wq_ref, k1_ref, k2_ref,
                xn_ref, ht_ref, s1_ref, s2_ref, *, nh, dq):
    acc = x_ref[...]
    for idx, y_ref in enumerate((ya_ref, yb_ref, yc_ref, yd_ref)):
        acc = acc + _dot(y_ref[...].astype(BF16), wo_ref[idx * MIX_W:(idx + 1) * MIX_W, :])
    xn_ref[...] = acc
    h2 = _rmsnorm(acc, g2_ref[...])
    ht_ref[...] = h2.T.astype(BF16)
    q = _dot(h2.astype(BF16), wq_ref[...])
    half = dq // 2
    for h in range(nh):
        s1_ref[h] = _dot(k1_ref[h], q[:, h * dq:h * dq + half].astype(BF16), NT)
        s2_ref[h] = _dot(k2_ref[h], q[:, h * dq + half:(h + 1) * dq].astype(BF16), NT)


def _out_proj(x, ya, yb, yc, yd, wo, g2, wq, k1, k2, tm):
    n, d = x.shape
    nh, nk, half = k1.shape
    row = lambda i: (i, 0)
    c2 = lambda i: (0, 0)
    return pl.pallas_call(
        functools.partial(_out_kernel, nh=nh, dq=2 * half),
        grid=(n // tm,),
        in_specs=[pl.BlockSpec((tm, d), row)] + [pl.BlockSpec((tm, MIX_W), row)] * 4
                 + [pl.BlockSpec(wo.shape, c2), pl.BlockSpec((1, d), c2), pl.BlockSpec(wq.shape, c2),
                    pl.BlockSpec(k1.shape, lambda i: (0, 0, 0)), pl.BlockSpec(k2.shape, lambda i: (0, 0, 0))],
        out_specs=[pl.BlockSpec((tm, d), row),
                   pl.BlockSpec((d, tm), lambda i: (0, i)),
                   pl.BlockSpec((nh, nk, tm), lambda i: (0, 0, i)),
                   pl.BlockSpec((nh, nk, tm), lambda i: (0, 0, i))],
        out_shape=[jax.ShapeDtypeStruct((n, d), F32),
                   jax.ShapeDtypeStruct((d, n), BF16),
                   jax.ShapeDtypeStruct((nh, nk, n), F32),
                   jax.ShapeDtypeStruct((nh, nk, n), F32)],
        compiler_params=_cparams(("parallel",)), name="out_proj",
    )(x, ya, yb, yc, yd, wo, g2, wq, k1, k2)


_PAIR_COUNTS = tuple(PEER_TOPK // (i + 1) for i in range(PEER_TOPK))
_PAIR_ROWS = -(-sum(_PAIR_COUNTS) // 8) * 8


def _gate_kernel(s1_ref, s2_ref, rk2_ref, e2_ref, rr_ref, c_ref, v1_s, v2_s, c_s):
    x1 = s1_ref[0]
    x2 = s2_ref[0]
    rank1 = jnp.full(x1.shape, float(PEER_TOPK), F32)
    rank2 = jnp.full(x2.shape, float(PEER_TOPK), F32)
    for i in range(PEER_TOPK):
        m1 = jnp.max(x1, axis=0, keepdims=True)
        m2 = jnp.max(x2, axis=0, keepdims=True)
        v1_s[i:i + 1, :] = m1
        v2_s[i:i + 1, :] = m2
        hit1 = x1 == m1
        hit2 = x2 == m2
        rank1 = jnp.where(hit1, float(i), rank1)
        rank2 = jnp.where(hit2, float(i), rank2)
        x1 = jnp.where(hit1, NEG, x1)
        x2 = jnp.where(hit2, NEG, x2)
    v1 = v1_s[...]
    v2 = v2_s[...]
    off = 0
    for i, cnt in enumerate(_PAIR_COUNTS):
        c_s[off:off + cnt, :] = v1[i:i + 1, :] + v2[0:cnt, :]
        off += cnt
    if off < _PAIR_ROWS:
        c_s[off:_PAIR_ROWS, :] = jnp.full((_PAIR_ROWS - off, v1.shape[1]), NEG, F32)
    c = c_s[...]
    mx = v1[0:1, :] + v2[0:1, :]
    z = jnp.zeros_like(mx)
    m = mx
    for i in range(PEER_TOPK):
        m = jnp.max(c, axis=0, keepdims=True)
        z = z + jnp.exp(m - mx)
        c = jnp.where(c == m, NEG, c)
    tau = m
    passing = jnp.zeros(v1.shape, F32)
    for jj in range(PEER_TOPK):
        passing = passing + jnp.where(v1 + v2[jj:jj + 1, :] >= tau, 1.0, 0.0)
    rr = jnp.zeros(rank1.shape, F32)
    for i in range(PEER_TOPK):
        rr = jnp.where(rank1 == float(i), passing[i:i + 1, :], rr)
    rk2_ref[0] = rank2.astype(BF16)
    rr_ref[0] = rr
    e2_ref[0] = jnp.exp(s2_ref[0] - v2[0:1, :]).astype(BF16)
    c_ref[0] = jnp.exp(s1_ref[0] - v1[0:1, :]) * (1.0 / z)


def _gate(s1t, s2t, tn):
    nh, nk, n = s1t.shape
    blk = lambda h, i: (h, 0, i)
    return pl.pallas_call(
        _gate_kernel,
        grid=(nh, n // tn),
        in_specs=[pl.BlockSpec((1, nk, tn), blk), pl.BlockSpec((1, nk, tn), blk)],
        out_specs=[pl.BlockSpec((1, nk, tn), blk)] * 4,
        out_shape=[jax.ShapeDtypeStruct((nh, nk, n), BF16), jax.ShapeDtypeStruct((nh, nk, n), BF16),
                   jax.ShapeDtypeStruct((nh, nk, n), F32), jax.ShapeDtypeStruct((nh, nk, n), F32)],
        scratch_shapes=[pltpu.VMEM((PEER_TOPK, tn), F32), pltpu.VMEM((PEER_TOPK, tn), F32),
                        pltpu.VMEM((_PAIR_ROWS, tn), F32)],
        compiler_params=_cparams(("parallel", "parallel")), name="peer_gate",
    )(s1t, s2t)


def _bf16_rows(row, n_rows):
    tile = jnp.broadcast_to(row, (16, row.shape[1])).astype(BF16)
    return jnp.concatenate([tile] * (n_rows // 16), axis=0)


def _peer_kernel(ht_ref, u_ref, vt_ref, rk2_ref, e2_ref, rr_ref, c_ref, xn_ref, fg_ref, o_ref,
                 acc_ref, f_ref, *, nh, nk, apc, final):
    j = pl.program_id(1)

    @pl.when(j == 0)
    def _():
        acc_ref[...] = jnp.zeros(acc_ref.shape, F32)

    ht = ht_ref[...]
    zero = jnp.zeros((), BF16)
    for aa in range(apc):
        a = j * apc + aa
        act = _gelu(_dot(u_ref[aa * nk:(aa + 1) * nk, :], ht)).astype(BF16)
        wgt = jnp.zeros(act.shape, BF16)
        for h in range(nh):
            limit = _bf16_rows(rr_ref[h, pl.ds(a, 1), :], nk)
            scale = _bf16_rows(c_ref[h, pl.ds(a, 1), :], nk)
            wgt = wgt + jnp.where(rk2_ref[h] < limit, e2_ref[h] * scale, zero)
        f_ref[aa * nk:(aa + 1) * nk, :] = wgt * act
    acc_ref[...] += _dot(vt_ref[...], f_ref[...])

    @pl.when(j == pl.num_programs(1) - 1)
    def _():
        out = xn_ref[...] + acc_ref[...].T
        if final:
            out = _rmsnorm(out, fg_ref[...])
        o_ref[...] = out


def _peer(ht, u, vt, rk2, e2, rr, c, xn, fg, tm, apc, final):
    d, n = ht.shape
    nh, nk, _ = rk2.shape
    assert nk % 16 == 0
    ne = u.shape[0]
    te = apc * nk
    tok3 = lambda i, j: (0, 0, i)
    return pl.pallas_call(
        functools.partial(_peer_kernel, nh=nh, nk=nk, apc=apc, final=final),
        grid=(n // tm, ne // te),
        in_specs=[pl.BlockSpec((d, tm), lambda i, j: (0, i)),
                  pl.BlockSpec((te, d), lambda i, j: (j, 0)),
                  pl.BlockSpec((d, te), lambda i, j: (0, j)),
                  pl.BlockSpec((nh, nk, tm), tok3),
                  pl.BlockSpec((nh, nk, tm), tok3),
                  pl.BlockSpec((nh, nk, tm), tok3),
                  pl.BlockSpec((nh, nk, tm), tok3),
                  pl.BlockSpec((tm, d), lambda i, j: (i, 0)),
                  pl.BlockSpec((1, d), lambda i, j: (0, 0))],
        out_specs=pl.BlockSpec((tm, d), lambda i, j: (i, 0)),
        out_shape=jax.ShapeDtypeStruct((n, d), F32),
        scratch_shapes=[pltpu.VMEM((d, tm), F32), pltpu.VMEM((te, tm), BF16)],
        compiler_params=_cparams(("parallel", "arbitrary")), name="peer",
    )(ht, u, vt, rk2, e2, rr, c, xn, fg)


def _rope_tables(pos):
    half = HEAD_DIM // 2
    inv_freq = ROPE_THETA ** (-jnp.arange(half, dtype=F32) / half)
    ang = pos.astype(F32)[:, None] * inv_freq[None, :]
    cos = jnp.cos(ang)
    sin = jnp.sin(ang)
    return jnp.tile(jnp.concatenate([cos, cos], axis=1), (1, 2)), jnp.tile(jnp.concatenate([-sin, sin], axis=1), (1, 2))


def _pad_rows(x, rows):
    return jnp.pad(x, ((0, rows - x.shape[0]),) + ((0, 0),) * (x.ndim - 1))


def _token_tile(n, pref):
    return pref if n % pref == 0 else n


def _page_pool_weights(wk, wv):
    wset = jnp.stack([wk[:, 0], wk[:, 1], wv[:, 0], wv[:, 1]]).reshape(4, 2, CMP_STRIDE)
    per_pos = jnp.tile(wset, (1, 1, LANES // CMP_STRIDE))
    chunk_of_pos = (jnp.arange(LANES) // CMP_STRIDE)[None, :] == jnp.arange(LANES // CMP_STRIDE)[:, None]
    return (per_pos[:, :, None, :] * chunk_of_pos[None, None].astype(F32)).reshape(64, LANES)


def kernel(x_prompt, x_sample, cache_nsa_kv, state_nsa_win, state_mlstm_C, state_mlstm_n, state_mlstm_m, state_conv, page_table, norm1_g, norm2_g, final_norm_g, w_in, w_out, mlstm_b_i, mlstm_b_f, mlstm_norm_g, conv_w, conv_b, conv_norm_g, conv_norm_b, nsa_cmp_wk, nsa_cmp_wv, gmlp_norm_g, gmlp_norm_b, gmlp_ws, gmlp_bs, peer_wq, peer_k1, peer_k2, peer_u, peer_v):
    depth = w_in.shape[0]
    bp, s_len, d = x_prompt.shape
    bd, t_dec, _ = x_sample.shape
    n_pool = cache_nsa_kv.shape[1]
    npages = page_table.shape[1]
    past = npages * cache_nsa_kv.shape[2]
    n_win = state_nsa_win.shape[2]
    assert t_dec == 1 and cache_nsa_kv.shape[2] == LANES and past % SEL_BLOCK == 0
    assert s_len >= WINDOW + 128 and n_win == WINDOW
    np_tok = bp * s_len
    ns_pad = LANES
    assert bd <= ns_pad

    xp = x_prompt.reshape(np_tok, d)
    xs = _pad_rows(x_sample.reshape(bd, d), ns_pad)
    cos_p, sin_p = _rope_tables(jnp.arange(s_len))
    cos_s, sin_s = _rope_tables(jnp.full((ns_pad,), past))
    pool_t = cache_nsa_kv.reshape(depth * n_pool, LANES, 4 * LANES).transpose(0, 2, 1)
    win_t = state_nsa_win.reshape(depth * bd, n_win, 2 * LANES).transpose(0, 2, 1)
    eye_h = jnp.eye(N_HEADS, dtype=F32)

    tm_in = _token_tile(s_len, 256)
    tm_out = _token_tile(np_tok, 256)
    tm_peer = _token_tile(np_tok, 512)
    nk = peer_k1.shape[2]
    apc = min(16, nk)
    w_in_t = w_in.transpose(2, 0, 1)

    new_p, new_s = [], []
    for l in range(depth):
        wi = w_in_t[:, l, :]
        w_perm = jnp.concatenate([wi[0:1024], wi[1032:2568], wi[2580:3092]], axis=0).astype(BF16)
        w_small = jnp.concatenate([wi[1024:1032], wi[2568:2580], jnp.zeros((108, d), F32)], axis=0).astype(BF16)
        g1 = norm1_g[l].reshape(1, d)
        g2 = norm2_g[l].reshape(1, d)
        gate_bias = jnp.concatenate([mlstm_b_i[l], mlstm_b_f[l], jnp.zeros((LANES - 2 * N_HEADS,), F32)]).reshape(1, LANES)
        mng = mlstm_norm_g[l].reshape(1, MIX_W)
        cw = conv_w[l]
        cbias = conv_b[l].reshape(1, MIX_W)
        cg = conv_norm_g[l].reshape(1, MIX_W)
        cbeta = conv_norm_b[l].reshape(1, MIX_W)
        w32 = jnp.concatenate([jnp.repeat(nsa_cmp_wk[l], HEAD_DIM, axis=1), jnp.repeat(nsa_cmp_wv[l], HEAD_DIM, axis=1)], axis=1)
        wlo, whi = w32[:CMP_STRIDE], w32[CMP_STRIDE:]
        wall = _page_pool_weights(nsa_cmp_wk[l], nsa_cmp_wv[l])
        gg = gmlp_norm_g[l].reshape(1, MIX_W)
        gbeta = gmlp_norm_b[l].reshape(1, MIX_W)
        gws = gmlp_ws[l]
        gbsb = jnp.repeat(gmlp_bs[l].T, HEAD_DIM, axis=1)
        wo = w_out[l].astype(BF16)
        wq = peer_wq[l].astype(BF16)
        k1 = peer_k1[l].astype(BF16)
        k2 = peer_k2[l].astype(BF16)
        ub = peer_u[l].astype(BF16)
        vtb = peer_v[l].astype(BF16).T
        fg = final_norm_g.reshape(1, d)
        final = l == depth - 1

        oa, ob, oq, okv, owin, okaug, ovsb, owinb, od, osm = _in_proj(xp, g1, w_perm, w_small, cos_p, sin_p, tm_in)
        ya, c_p, n_p, m_p = _mlstm(oa.reshape(bp, s_len, 1024), osm.reshape(bp, s_len, LANES), gate_bias, mng,
                                   jnp.zeros((bp, MIX_W, MIX_W), F32), jnp.zeros((bp, 1, MIX_W), F32),
                                   jnp.zeros((bp, 1, MIX_W), F32), MLSTM_CHUNK, MLSTM_CHUNK, bp)
        yb, conv_p = _conv(ob.reshape(bp, s_len, 2 * MIX_W), jnp.zeros((bp, CONV_PAD, MIX_W), F32),
                           cw, cbias, cg, cbeta, 512, 512)
        cb = _cmp(okv.reshape(bp, s_len, 4 * LANES), wlo, whi)
        yc = _nsa_prompt(oq.reshape(bp, s_len, MIX_W), osm.reshape(bp, s_len, LANES), cb,
                         okaug.reshape(bp, s_len, 2 * LANES), ovsb.reshape(bp, s_len, LANES),
                         owinb.reshape(bp, s_len, 2 * LANES), 128)
        yd, _ = _gmlp(od, gg, gbeta, gws, gbsb, 4)
        xn, ht, s1t, s2t = _out_proj(xp, ya.reshape(np_tok, MIX_W), yb.reshape(np_tok, MIX_W),
                                     yc.reshape(np_tok, MIX_W), yd, wo, g2, wq, k1, k2, tm_out)
        rk2, e2, rr, cw8 = _gate(s1t, s2t, _token_tile(np_tok, 512))
        xp = _peer(ht, ub, vtb, rk2, e2, rr, cw8, xn, fg, tm_peer, apc, final)
        new_p.append((okv.reshape(bp, s_len, 4, NSA_KV, HEAD_DIM),
                      owin.reshape(bp, s_len, 2, NSA_KV, HEAD_DIM)[:, s_len - n_win:],
                      jnp.stack([c_p[:, h * HEAD_DIM:(h + 1) * HEAD_DIM, h * HEAD_DIM:(h + 1) * HEAD_DIM]
                                 for h in range(N_HEADS)], axis=1),
                      n_p.reshape(bp, N_HEADS, HEAD_DIM),
                      m_p[:, 0, ::HEAD_DIM],
                      conv_p[:, CONV_PAD - (CONV_W - 1):]))

        sa, sb, sq, skv, swin, _, _, _, sd, ssm = _in_proj(xs, g1, w_perm, w_small, cos_s, sin_s, ns_pad)
        rows8 = lambda t: jnp.pad(t[:bd, None, :], ((0, 0), (0, 7), (0, 0)))
        c0 = jnp.einsum('bhvk,hg->bhvgk', state_mlstm_C[l], eye_h).reshape(bd, MIX_W, MIX_W)
        n0 = state_mlstm_n[l].reshape(bd, 1, MIX_W)
        m0 = jnp.repeat(state_mlstm_m[l], HEAD_DIM, axis=-1).reshape(bd, 1, MIX_W)
        bb_s = 2 if bd % 2 == 0 else 1
        ya_s, c_s, n_s, m_s = _mlstm(rows8(sa), rows8(ssm), gate_bias, mng, c0, n0, m0, 8, 1, bb_s)
        prefix = jnp.pad(state_conv[l], ((0, 0), (CONV_PAD - (CONV_W - 1), 0), (0, 0)))
        yb_s, conv_s = _conv(rows8(sb), prefix, cw, cbias, cg, cbeta, 8, 1)
        zd_s = jnp.pad(sd[:bd, None, :], ((0, 0), (0, GMLP_CHUNK - 1), (0, 0))).reshape(bd * GMLP_CHUNK, 2 * MIX_W)
        yd_s, v_s = _gmlp(zd_s, gg, gbeta, gws, gbsb, 1)
        q_heads = sq[:bd].reshape(bd, NSA_KV, 2, 1, HEAD_DIM)
        q4 = (q_heads * jnp.eye(NSA_KV, dtype=F32)[None, :, None, :, None]).reshape(bd, N_HEADS, LANES)
        q4 = jnp.pad(q4, ((0, 0), (0, 8 - N_HEADS), (0, 0)))
        gq = jnp.pad(ssm[:bd, 2 * N_HEADS:2 * N_HEADS + 3 * N_HEADS].reshape(bd, N_HEADS, 3),
                     ((0, 0), (0, 8 - N_HEADS), (0, LANES - 3)))
        pp = 16 if npages % 16 == 0 else (8 if npages % 8 == 0 else 1)
        o8, win_s = _nsa_sample(page_table + l * n_pool, q4, gq, skv[:bd, None, :], swin[:bd, None, :],
                                swin[:bd, :, None], win_t, l * bd, wall, pool_t, pp)
        yc_s = jnp.concatenate([o8[:, h, (h // 2) * HEAD_DIM:(h // 2 + 1) * HEAD_DIM] for h in range(N_HEADS)], axis=1)
        xn_s, ht_s, s1t_s, s2t_s = _out_proj(xs, _pad_rows(ya_s[:, 0], ns_pad), _pad_rows(yb_s[:, 0], ns_pad),
                                             _pad_rows(yc_s, ns_pad),
                                             _pad_rows(yd_s.reshape(bd, GMLP_CHUNK, MIX_W)[:, 0], ns_pad),
                                             wo, g2, wq, k1, k2, ns_pad)
        rk2_s, e2_s, rr_s, cw8_s = _gate(s1t_s, s2t_s, ns_pad)
        xs = _peer(ht_s, ub, vtb, rk2_s, e2_s, rr_s, cw8_s, xn_s, fg, ns_pad, apc, final)
        new_s.append((skv[:bd].reshape(bd, 1, 4, NSA_KV, HEAD_DIM),
                      win_s.transpose(0, 2, 1).reshape(bd, n_win, 2, NSA_KV, HEAD_DIM),
                      jnp.stack([c_s[:, h * HEAD_DIM:(h + 1) * HEAD_DIM, h * HEAD_DIM:(h + 1) * HEAD_DIM]
                                 for h in range(N_HEADS)], axis=1),
                      n_s.reshape(bd, N_HEADS, HEAD_DIM),
                      m_s[:, 0, ::HEAD_DIM],
                      conv_s[:, CONV_PAD - (CONV_W - 1):],
                      v_s.reshape(bd, GMLP_CHUNK, MIX_W)[:, 0:1]))

    def stack(states, i):
        return jnp.stack([st[i] for st in states])

    y_prompt = xp.reshape(bp, s_len, d)
    y_sample = xs[:bd].reshape(bd, 1, d)
    return (y_prompt, y_sample,
            stack(new_p, 0), stack(new_s, 0),
            stack(new_p, 1), stack(new_s, 1),
            stack(new_p, 2), stack(new_s, 2),
            stack(new_p, 3), stack(new_s, 3),
            stack(new_p, 4), stack(new_s, 4),
            stack(new_p, 5), stack(new_s, 5),
            stack(new_s, 6))
```

```python
import functools

import jax
import jax.numpy as jnp
from jax import lax
from jax.experimental import pallas as pl
from jax.experimental.pallas import tpu as pltpu

F32 = jnp.float32
BF16 = jnp.bfloat16
HIGHEST = lax.Precision.HIGHEST

HEAD_DIM = 64
N_HEADS = 4
MIX_W = 256
NSA_KV = 2
NORM_EPS = 1e-6
NEG = -1e30
MLSTM_CHUNK = 128
CONV_W = 31
CONV_PAD = 32
CMP_STRIDE = 16
CMP_LEN = 32
SEL_BLOCK = 64
N_SELECT = 16
SEL_FORCE = 1e9
WINDOW = 512
ROPE_THETA = 10000.0
ATTN_SCALE = HEAD_DIM ** -0.5
GMLP_CHUNK = 128
PEER_TOPK = 16
LANES = 128
VMEM_LIMIT = 48 * 1024 * 1024

NT = (((1,), (1,)), ((), ()))
TN = (((0,), (0,)), ((), ()))


def _cparams(sem):
    return pltpu.CompilerParams(dimension_semantics=sem, vmem_limit_bytes=VMEM_LIMIT)


def _iota(shape, dim):
    return lax.broadcasted_iota(jnp.int32, shape, dim)


def _sigmoid(x):
    return 1.0 / (1.0 + jnp.exp(-x))


def _gelu(x):
    c1 = -2.0 * 0.7978845608028654
    c2 = c1 * 0.044715
    return x / (1.0 + jnp.exp(x * (c1 + c2 * (x * x))))


def _dot(a, b, dims=None, precision=None):
    if dims is None:
        return jnp.dot(a, b, preferred_element_type=F32, precision=precision)
    return lax.dot_general(a, b, dims, preferred_element_type=F32, precision=precision)


def _block_ones(n, blk):
    return (_iota((n, n), 0) // blk == _iota((n, n), 1) // blk).astype(F32)


def _group_layernorm(x, ee, width):
    mu = _dot(x, ee, precision=HIGHEST) * (1.0 / width)
    d = x - mu
    var = _dot(d * d, ee, precision=HIGHEST) * (1.0 / width)
    return d * lax.rsqrt(var + NORM_EPS)


def _rmsnorm(x, g):
    return x * lax.rsqrt(jnp.mean(x * x, axis=-1, keepdims=True) + NORM_EPS) * g


def _masked_softmax(s, valid):
    sm = jnp.where(valid, s, NEG)
    p = jnp.where(valid, jnp.exp(sm - jnp.max(sm, axis=-1, keepdims=True)), 0.0)
    return p / jnp.maximum(jnp.sum(p, axis=-1, keepdims=True), 1e-30)


def _rope(x, cos, sin_signed):
    w = x.shape[1]
    fwd = pltpu.roll(x, w - HEAD_DIM // 2, 1)
    bwd = pltpu.roll(x, HEAD_DIM // 2, 1)
    first = (_iota(x.shape, 1) % HEAD_DIM) < HEAD_DIM // 2
    return x * cos + jnp.where(first, fwd, bwd) * sin_signed


def _in_kernel(x_ref, g_ref, w_ref, wsmall_ref, cos_ref, sin_ref,
               oa_ref, ob_ref, oq_ref, okv_ref, owin_ref, okaug_ref, ovsb_ref, owinb_ref, od_ref, os_ref, *, npb):
    tm = x_ref.shape[0]
    y = _rmsnorm(x_ref[...], g_ref[...])
    yb = y.astype(BF16)
    z = _dot(yb, w_ref[...], NT)
    os_ref[...] = _dot(yb, wsmall_ref[...], NT)
    oa_ref[...] = z[:, 0:1024]
    ob_ref[...] = z[:, 1024:1536]
    cos = cos_ref[...]
    sin = sin_ref[...]
    cos2 = jnp.concatenate([cos, cos], axis=1)
    sin2 = jnp.concatenate([sin, sin], axis=1)
    oq_ref[...] = _rope(z[:, 1536:1792], cos2, sin2) * ATTN_SCALE
    kc = _rope(z[:, 1792:1920], cos, sin)
    vc = z[:, 1920:2048]
    ks = _rope(z[:, 2048:2176], cos, sin)
    vs = z[:, 2176:2304]
    kw = _rope(z[:, 2304:2432], cos, sin)
    vw = z[:, 2432:2560]
    okv_ref[...] = jnp.concatenate([kc, vc, ks, vs], axis=1)
    owin = jnp.concatenate([kw, vw], axis=1)
    owin_ref[...] = owin
    pos = (pl.program_id(0) % npb) * tm + _iota((tm, LANES), 0)
    onehot = jnp.where(_iota((tm, LANES), 1) == pos // SEL_BLOCK, 1.0, 0.0)
    okaug_ref[...] = jnp.concatenate([ks, onehot], axis=1).astype(BF16)
    ovsb_ref[...] = vs.astype(BF16)
    owinb_ref[...] = owin.astype(BF16)
    od_ref[...] = z[:, 2560:3072]


def _in_proj(x, g, w, wsmall, cos_t, sin_t, tm):
    n, d = x.shape
    npb = cos_t.shape[0] // tm
    widths = (1024, 512, 256, 512, 256, 256, 128, 256, 512, 128)
    dtypes = (F32, F32, F32, F32, F32, BF16, BF16, BF16, F32, F32)
    row = lambda i: (i, 0)
    return pl.pallas_call(
        functools.partial(_in_kernel, npb=npb),
        grid=(n // tm,),
        in_specs=[pl.BlockSpec((tm, d), row),
                  pl.BlockSpec((1, d), lambda i: (0, 0)),
                  pl.BlockSpec(w.shape, lambda i: (0, 0)),
                  pl.BlockSpec(wsmall.shape, lambda i: (0, 0)),
                  pl.BlockSpec((tm, LANES), lambda i: (i % npb, 0)),
                  pl.BlockSpec((tm, LANES), lambda i: (i % npb, 0))],
        out_specs=[pl.BlockSpec((tm, wd), row) for wd in widths],
        out_shape=[jax.ShapeDtypeStruct((n, wd), dt) for wd, dt in zip(widths, dtypes)],
        compiler_params=_cparams(("parallel",)), name="in_proj",
    )(x, g, w, wsmall, cos_t, sin_t)


def _mlstm_kernel(za_ref, zs_ref, bias_ref, ng_ref, c0_ref, n0_ref, m0_ref,
                  y_ref, c_ref, n_ref, m_ref, *, L, t_valid, bb):
    @pl.when(pl.program_id(1) == 0)
    def _():
        c_ref[...] = c0_ref[...]
        n_ref[...] = n0_ref[...]
        m_ref[...] = m0_ref[...]

    head_of_lane = _iota((1, MIX_W), 1) // HEAD_DIM
    causal = _iota((L, L), 0) >= _iota((L, L), 1)
    tril = causal.astype(F32)
    ee = _block_ones(MIX_W, HEAD_DIM)
    bdiag = ee > 0.5
    row128 = _iota((L, LANES), 0)
    lane128 = _iota((L, LANES), 1)
    for bi in range(bb):
        za = za_ref[bi]
        q = za[:, 0:256]
        k = za[:, 256:512] * (HEAD_DIM ** -0.5)
        v = za[:, 512:768]
        o = za[:, 768:1024]
        gi = zs_ref[bi] + bias_ref[...]
        ls = jnp.minimum(gi, 0.0) - jnp.log(1.0 + jnp.exp(-jnp.abs(gi)))
        if t_valid < L:
            live = row128 < t_valid
            ig = jnp.where(live, gi, NEG)
            ls = jnp.where(live, ls, 0.0)
        else:
            ig = gi
        gmat = jnp.where(lane128 < N_HEADS, ig, 0.0)
        lfm = jnp.where((lane128 >= N_HEADS) & (lane128 < 2 * N_HEADS), ls, 0.0)
        fc = _dot(tril, lfm, precision=HIGHEST)
        g_t = gmat.T
        f_t = fc.T
        mrow = m_ref[bi]
        nrow = n_ref[bi]
        cm = c_ref[bi]
        qb = q.astype(BF16)
        kb = k.astype(BF16)
        vb = v.astype(BF16)
        num = jnp.zeros((L, MIX_W), F32)
        decay_b = jnp.zeros((L, MIX_W), F32)
        dens_b = jnp.zeros((L, MIX_W), F32)
        mt_b = jnp.zeros((L, MIX_W), F32)
        w_b = jnp.zeros((L, MIX_W), F32)
        cd_b = jnp.zeros((1, MIX_W), F32)
        mnew_b = jnp.zeros((1, MIX_W), F32)
        for h in range(N_HEADS):
            hm = head_of_lane == h
            f_col = fc[:, N_HEADS + h:N_HEADS + h + 1]
            ig_col = gmat[:, h:h + 1]
            f_row = f_t[N_HEADS + h:N_HEADS + h + 1, :]
            ig_row = g_t[h:h + 1, :]
            mp = mrow[:, h * HEAD_DIM:h * HEAD_DIM + 1]
            dmat = jnp.where(causal, (f_col - f_row) + ig_row, NEG)
            m_inter = mp + f_col
            m_t = jnp.maximum(m_inter, jnp.max(dmat, axis=1, keepdims=True))
            d_exp = jnp.exp(dmat - m_t)
            decay = jnp.exp(m_inter - m_t)
            qh = jnp.where(hm, q, 0.0).astype(BF16)
            s = _dot(qh, kb, NT) * d_exp
            num = jnp.where(hm, _dot(s.astype(BF16), vb), num)
            dens_b = jnp.where(hm, jnp.sum(s, axis=1, keepdims=True), dens_b)
            decay_b = jnp.where(hm, decay, decay_b)
            mt_b = jnp.where(hm, m_t, mt_b)
            f_last = f_col[L - 1:L, :]
            m_new = m_t[L - 1:L, :]
            w_b = jnp.where(hm, jnp.exp((f_last - f_col) + ig_col - m_new), w_b)
            cd_b = jnp.where(hm, jnp.exp(mp + f_last - m_new), cd_b)
            mnew_b = jnp.where(hm, m_new, mnew_b)
        inter = _dot(qb, cm.astype(BF16), NT)
        nq_b = _dot(q * nrow, ee, precision=HIGHEST)
        hnum = num + decay_b * inter
        den_b = dens_b + decay_b * nq_b
        hh = hnum / jnp.maximum(jnp.abs(den_b), jnp.exp(-mt_b))
        hn = _group_layernorm(hh, ee, HEAD_DIM) * ng_ref[...]
        y_ref[bi] = hn * _sigmoid(o)
        kw = k * w_b
        upd = _dot(vb, kw.astype(BF16), TN)
        c_ref[bi] = jnp.where(bdiag, cd_b * cm + upd, 0.0)
        n_ref[bi] = cd_b * nrow + jnp.sum(kw, axis=0, keepdims=True)
        m_ref[bi] = mnew_b


def _mlstm(za, zs, bias, ng, c0, n0, m0, L, t_valid, bb):
    b, t, _ = za.shape
    nc = t // L
    st = lambda i, c: (i, 0, 0)
    return pl.pallas_call(
        functools.partial(_mlstm_kernel, L=L, t_valid=t_valid, bb=bb),
        grid=(b // bb, nc),
        in_specs=[pl.BlockSpec((bb, L, 1024), lambda i, c: (i, c, 0)),
                  pl.BlockSpec((bb, L, LANES), lambda i, c: (i, c, 0)),
                  pl.BlockSpec((1, LANES), lambda i, c: (0, 0)),
                  pl.BlockSpec((1, MIX_W), lambda i, c: (0, 0)),
                  pl.BlockSpec((bb, MIX_W, MIX_W), st),
                  pl.BlockSpec((bb, 1, MIX_W), st),
                  pl.BlockSpec((bb, 1, MIX_W), st)],
        out_specs=[pl.BlockSpec((bb, L, MIX_W), lambda i, c: (i, c, 0)),
                   pl.BlockSpec((bb, MIX_W, MIX_W), st),
                   pl.BlockSpec((bb, 1, MIX_W), st),
                   pl.BlockSpec((bb, 1, MIX_W), st)],
        out_shape=[jax.ShapeDtypeStruct((b, t, MIX_W), F32),
                   jax.ShapeDtypeStruct((b, MIX_W, MIX_W), F32),
                   jax.ShapeDtypeStruct((b, 1, MIX_W), F32),
                   jax.ShapeDtypeStruct((b, 1, MIX_W), F32)],
        compiler_params=_cparams(("parallel", "arbitrary")), name="mlstm",
    )(za, zs, bias, ng, c0, n0, m0)


def _conv_kernel(zb_ref, pre_ref, w_ref, cb_ref, g_ref, beta_ref, y_ref, st_ref, ext_ref, *, tt, tv):
    @pl.when(pl.program_id(1) == 0)
    def _():
        ext_ref[0:CONV_PAD, :] = pre_ref[0]

    z = zb_ref[0]
    u = z[:, :MIX_W] * _sigmoid(z[:, MIX_W:])
    ext_ref[CONV_PAD:CONV_PAD + tt, :] = u
    off = CONV_PAD - (CONV_W - 1)
    acc = jnp.zeros((tt, MIX_W), F32)
    for j in range(CONV_W):
        acc = acc + w_ref[j:j + 1, :] * ext_ref[off + j:off + j + tt, :]
    ee = _block_ones(MIX_W, HEAD_DIM)
    y = _group_layernorm(acc + cb_ref[...], ee, HEAD_DIM) * g_ref[...] + beta_ref[...]
    y_ref[0] = y * _sigmoid(y)
    st_ref[0] = ext_ref[tv:tv + CONV_PAD, :]
    ext_ref[0:CONV_PAD, :] = ext_ref[tt:tt + CONV_PAD, :]


def _conv(zb, prefix, w, cb, g, beta, tt, tv):
    b, t, _ = zb.shape
    vec = lambda i, j: (0, 0)
    return pl.pallas_call(
        functools.partial(_conv_kernel, tt=tt, tv=tv),
        grid=(b, t // tt),
        in_specs=[pl.BlockSpec((1, tt, 2 * MIX_W), lambda i, j: (i, j, 0)),
                  pl.BlockSpec((1, CONV_PAD, MIX_W), lambda i, j: (i, 0, 0)),
                  pl.BlockSpec((CONV_W, MIX_W), vec),
                  pl.BlockSpec((1, MIX_W), vec),
                  pl.BlockSpec((1, MIX_W), vec),
                  pl.BlockSpec((1, MIX_W), vec)],
        out_specs=[pl.BlockSpec((1, tt, MIX_W), lambda i, j: (i, j, 0)),
                   pl.BlockSpec((1, CONV_PAD, MIX_W), lambda i, j: (i, 0, 0))],
        out_shape=[jax.ShapeDtypeStruct((b, t, MIX_W), F32),
                   jax.ShapeDtypeStruct((b, CONV_PAD, MIX_W), F32)],
        scratch_shapes=[pltpu.VMEM((CONV_PAD + tt, MIX_W), F32)],
        compiler_params=_cparams(("parallel", "arbitrary")), name="conv",
    )(zb, prefix, w, cb, g, beta)


def _gmlp_kernel(zd_ref, g_ref, beta_ref, ws_ref, bsb_ref, y_ref, v_ref, *, cpb):
    z = zd_ref[...]
    ee = _block_ones(MIX_W, HEAD_DIM)
    u = _gelu(z[:, :MIX_W])
    vv = _group_layernorm(_gelu(z[:, MIX_W:]), ee, HEAD_DIM) * g_ref[...] + beta_ref[...]
    v_ref[...] = vv
    tril = _iota((GMLP_CHUNK, GMLP_CHUNK), 0) >= _iota((GMLP_CHUNK, GMLP_CHUNK), 1)
    head_of_lane = _iota((1, MIX_W), 1) // HEAD_DIM
    wm = [jnp.where(tril, ws_ref[h], 0.0).astype(BF16) for h in range(N_HEADS)]
    for c in range(cpb):
        rows = slice(c * GMLP_CHUNK, (c + 1) * GMLP_CHUNK)
        vc = vv[rows].astype(BF16)
        mixed = jnp.zeros((GMLP_CHUNK, MIX_W), F32)
        for h in range(N_HEADS):
            mixed = jnp.where(head_of_lane == h, _dot(wm[h], vc), mixed)
        y_ref[rows, :] = u[rows] * (mixed + bsb_ref[...])


def _gmlp(zd, g, beta, ws, bsb, cpb):
    n = zd.shape[0]
    rows = cpb * GMLP_CHUNK
    vec = lambda i: (0, 0)
    return pl.pallas_call(
        functools.partial(_gmlp_kernel, cpb=cpb),
        grid=(n // rows,),
        in_specs=[pl.BlockSpec((rows, 2 * MIX_W), lambda i: (i, 0)),
                  pl.BlockSpec((1, MIX_W), vec),
                  pl.BlockSpec((1, MIX_W), vec),
                  pl.BlockSpec((N_HEADS, GMLP_CHUNK, GMLP_CHUNK), lambda i: (0, 0, 0)),
                  pl.BlockSpec((GMLP_CHUNK, MIX_W), vec)],
        out_specs=[pl.BlockSpec((rows, MIX_W), lambda i: (i, 0)),
                   pl.BlockSpec((rows, MIX_W), lambda i: (i, 0))],
        out_shape=[jax.ShapeDtypeStruct((n, MIX_W), F32),
                   jax.ShapeDtypeStruct((n, MIX_W), F32)],
        compiler_params=_cparams(("parallel",)), name="gmlp",
    )(zd, g, beta, ws, bsb)


def _combine_lo_hi(lo, hi):
    r = lo.shape[0]
    nxt = pltpu.roll(hi, r - 1, 0)
    return lo + jnp.where(_iota(lo.shape, 0) < r - 1, nxt, 0.0)


def _cmp_kernel(kv_ref, wlo_ref, whi_ref, o_ref):
    x = kv_ref[0]
    x3 = x.reshape(x.shape[0] // CMP_STRIDE, CMP_STRIDE, x.shape[1])
    lo = jnp.sum(x3 * wlo_ref[...][None], axis=1)
    hi = jnp.sum(x3 * whi_ref[...][None], axis=1)
    o_ref[0] = _combine_lo_hi(lo, hi)


def _cmp(okv3, wlo, whi):
    b, s, _ = okv3.shape
    return pl.pallas_call(
        _cmp_kernel,
        grid=(b,),
        in_specs=[pl.BlockSpec((1, s, 2 * LANES), lambda i: (i, 0, 0)),
                  pl.BlockSpec((CMP_STRIDE, 2 * LANES), lambda i: (0, 0)),
                  pl.BlockSpec((CMP_STRIDE, 2 * LANES), lambda i: (0, 0))],
        out_specs=pl.BlockSpec((1, s // CMP_STRIDE, 2 * LANES), lambda i: (i, 0, 0)),
        out_shape=jax.ShapeDtypeStruct((b, s // CMP_STRIDE, 2 * LANES), F32),
        compiler_params=_cparams(("parallel",)), name="nsa_cmp",
    )(okv3, wlo, whi)


def _nsa_prompt_kernel(q_ref, gs_ref, cb_ref, ka_ref, vs_ref, win_ref, y_ref, s_buf, *, tq, s_len, n_top, tk):
    start = pl.program_id(1) * tq
    nc = s_len // CMP_STRIDE
    ns = s_len // SEL_BLOCK
    q = q_ref[0]
    gl = _sigmoid(gs_ref[0])
    lo_half = _iota((tq, LANES), 1) < HEAD_DIM
    q_rows = []
    for g in range(NSA_KV):
        qpair = q[:, g * LANES:(g + 1) * LANES]
        swapped = pltpu.roll(qpair, HEAD_DIM, 1)
        if g == 0:
            q_rows += [jnp.where(lo_half, qpair, 0.0), jnp.where(lo_half, swapped, 0.0)]
        else:
            q_rows += [jnp.where(lo_half, 0.0, swapped), jnp.where(lo_half, 0.0, qpair)]
    q128 = jnp.concatenate(q_rows, axis=0)
    q128b = q128.astype(BF16)
    trow = start + _iota((tq, 1), 0)
    t2 = jnp.concatenate([trow, trow], axis=0)
    t4 = jnp.concatenate([t2, t2], axis=0)
    cb = cb_ref[0]
    sc = _dot(q128b, cb[:, 0:LANES].astype(BF16), NT)
    validc = (_iota((1, nc), 1) * CMP_STRIDE + CMP_LEN) <= (t4 + 1)
    p = _masked_softmax(sc, validc)
    o_c = _dot(p.astype(BF16), cb[:, LANES:2 * LANES].astype(BF16))
    psum = jnp.concatenate([p[0:tq] + p[tq:2 * tq], p[2 * tq:3 * tq] + p[3 * tq:4 * tq]], axis=0)
    pool4 = (_iota((nc, ns), 0) // (SEL_BLOCK // CMP_STRIDE) == _iota((nc, ns), 1)).astype(F32)
    imp = _dot(psum, pool4, precision=HIGHEST)
    jidx = _iota((1, ns), 1)
    imp = jnp.where(jidx == t2 // SEL_BLOCK, SEL_FORCE, imp)
    imp = jnp.where(jidx * SEL_BLOCK <= t2, imp, -1.0)
    if n_top < ns:
        x = jnp.concatenate([imp[0:tq].T, imp[tq:2 * tq].T], axis=1)
        jrow = _iota((ns, 2 * tq), 0).astype(F32)
        sel_t = jnp.zeros((ns, 2 * tq), F32)
        for _ in range(n_top):
            m = jnp.max(x, axis=0, keepdims=True)
            first = jnp.min(jnp.where(x == m, jrow, float(ns)), axis=0, keepdims=True)
            hit = jrow == first
            sel_t = jnp.where(hit, 1.0, sel_t)
            x = jnp.where(hit, -3.0, x)
        sel = jnp.concatenate([sel_t[:, 0:tq].T, sel_t[:, tq:2 * tq].T], axis=0)
    else:
        sel = jnp.ones((2 * tq, ns), F32)
    selneg = (sel - 1.0) * (-NEG)
    if ns < LANES:
        selneg = jnp.concatenate([selneg, jnp.zeros((2 * tq, LANES - ns), F32)], axis=1)
    selneg4 = jnp.concatenate([selneg[0:tq], selneg[0:tq], selneg[tq:2 * tq], selneg[tq:2 * tq]], axis=0)
    qaug = jnp.concatenate([q128, selneg4], axis=1).astype(BF16)

    def scores(c):
        return _dot(qaug, ka_ref[0, pl.ds(pl.multiple_of(c * tk, tk), tk), :], NT)

    def absorb(c, s, carry):
        m, l, acc = carry
        m_new = jnp.maximum(m, jnp.max(s, axis=1, keepdims=True))
        pt = jnp.exp(s - m_new)
        alpha = jnp.exp(m - m_new)
        l = alpha * l + jnp.sum(pt, axis=1, keepdims=True)
        acc = alpha * acc + _dot(pt.astype(BF16), vs_ref[0, pl.ds(pl.multiple_of(c * tk, tk), tk), :])
        return m_new, l, acc

    def tile(c, carry):
        s = s_buf[...]
        s_buf[...] = scores(c + 1)
        return absorb(c, s, carry)

    n_tiles = (start + tq + tk - 1) // tk
    carry = (jnp.full((4 * tq, 1), NEG, F32), jnp.zeros((4 * tq, 1), F32), jnp.zeros((4 * tq, LANES), F32))
    s_buf[...] = scores(0)
    carry = lax.fori_loop(0, n_tiles - 1, tile, carry)
    last = n_tiles - 1
    s_diag = jnp.where(last * tk + _iota((1, tk), 1) <= t4, s_buf[...], NEG)
    _, l, acc = absorb(last, s_diag, carry)
    o_s = acc / jnp.maximum(l, 1e-30)
    wl = WINDOW + tq
    w0 = pl.multiple_of(jnp.maximum(start - WINDOW, 0), tq)
    sw = _dot(q128b, win_ref[0, pl.ds(w0, wl), 0:LANES], NT)
    kposw = w0 + _iota((1, wl), 1)
    validw = (kposw <= t4) & (kposw > t4 - WINDOW)
    o_w = _dot(_masked_softmax(sw, validw).astype(BF16), win_ref[0, pl.ds(w0, wl), LANES:2 * LANES])
    heads = []
    for hh in range(N_HEADS):
        col = 2 * N_HEADS + hh * 3
        rows = slice(hh * tq, (hh + 1) * tq)
        heads.append(o_c[rows] * gl[:, col:col + 1] + o_s[rows] * gl[:, col + 1:col + 2]
                     + o_w[rows] * gl[:, col + 2:col + 3])
    y_ref[0] = jnp.concatenate([jnp.where(lo_half, heads[0], pltpu.roll(heads[1], HEAD_DIM, 1)),
                                jnp.where(lo_half, pltpu.roll(heads[2], HEAD_DIM, 1), heads[3])], axis=1)


def _nsa_prompt(oq3, os3, cb, okaug3, ovsb3, owinb3, tq):
    b, s, _ = oq3.shape
    ns = s // SEL_BLOCK
    assert ns <= LANES
    n_top = min(N_SELECT, ns)
    tk = min(512, s)
    full = lambda i, j: (i, 0, 0)
    return pl.pallas_call(
        functools.partial(_nsa_prompt_kernel, tq=tq, s_len=s, n_top=n_top, tk=tk),
        grid=(b, s // tq),
        in_specs=[pl.BlockSpec((1, tq, MIX_W), lambda i, j: (i, j, 0)),
                  pl.BlockSpec((1, tq, LANES), lambda i, j: (i, j, 0)),
                  pl.BlockSpec((1, s // CMP_STRIDE, 2 * LANES), full),
                  pl.BlockSpec((1, s, 2 * LANES), full),
                  pl.BlockSpec((1, s, LANES), full),
                  pl.BlockSpec((1, s, 2 * LANES), full)],
        out_specs=pl.BlockSpec((1, tq, MIX_W), lambda i, j: (i, j, 0)),
        out_shape=jax.ShapeDtypeStruct((b, s, MIX_W), F32),
        scratch_shapes=[pltpu.VMEM((N_HEADS * tq, tk), F32)],
        compiler_params=_cparams(("parallel", "arbitrary")), name="nsa_prompt",
    )(oq3, os3, cb, okaug3, ovsb3, owinb3)


def _nsa_sample_kernel(pt_ref, q4_ref, gq_ref, newkv_ref, newwin_ref, newcol_ref, win_ref, wall_ref, *rest,
                       pp, npages, n_top):
    pool_refs = rest[:pp]
    o_ref, wo_ref, lo_s, hi_s, m_s, l_s, a_s, a_stage = rest[pp:]
    i = pl.program_id(1)
    nb = 2 * npages
    nbp = m_s.shape[1]
    ncp = npages * (LANES // CMP_STRIDE)
    past = npages * LANES
    q4 = q4_ref[0]
    q4b = q4.astype(BF16)

    @pl.when(i == 0)
    def _():
        m_s[...] = jnp.zeros(m_s.shape, F32)
        l_s[...] = jnp.zeros(l_s.shape, F32)
        if nb < nbp:
            a_s[...] = jnp.zeros(a_s.shape, F32)

    lane8 = _iota((8, LANES), 1)
    lanej = _iota((8, nbp), 1)
    set_of_lane = _iota((8, 2 * LANES), 1) // HEAD_DIM
    bpp = LANES // SEL_BLOCK
    wall = wall_ref[...].astype(BF16)
    m_acc = m_s[...]
    l_acc = l_s[...]
    los, his = [], []
    for kk in range(pp):
        page = pool_refs[kk][0]
        pooled = _dot(wall, page[0:2 * LANES, :].astype(BF16), NT)
        lo8 = jnp.zeros((8, 2 * LANES), F32)
        hi8 = jnp.zeros((8, 2 * LANES), F32)
        for st in range(4):
            lo8 = jnp.where(set_of_lane == st, pooled[st * 16:st * 16 + 8], lo8)
            hi8 = jnp.where(set_of_lane == st, pooled[st * 16 + 8:st * 16 + 16], hi8)
        los.append(lo8)
        his.append(hi8)
        ks_t = page[2 * LANES:3 * LANES, :].astype(BF16)
        vs_t = page[3 * LANES:4 * LANES, :].astype(BF16)
        s = _dot(q4b, ks_t)
        probs = []
        for jb in range(bpp):
            inb = (lane8 >= jb * SEL_BLOCK) & (lane8 < (jb + 1) * SEL_BLOCK)
            sm = jnp.where(inb, s, NEG)
            m = jnp.max(sm, axis=1, keepdims=True)
            p = jnp.where(inb, jnp.exp(sm - m), 0.0)
            probs.append(p)
            j = i * (pp * bpp) + kk * bpp + jb
            m_acc = jnp.where(lanej == j, m, m_acc)
            l_acc = jnp.where(lanej == j, jnp.sum(p, axis=1, keepdims=True), l_acc)
        a = _dot(jnp.concatenate(probs, axis=0).astype(BF16), vs_t, NT)
        for jb in range(bpp):
            for hh in range(N_HEADS):
                a_stage[hh, kk * bpp + jb:kk * bpp + jb + 1, :] = a[jb * 8 + hh:jb * 8 + hh + 1, :]
    m_s[...] = m_acc
    l_s[...] = l_acc
    r0 = pl.multiple_of(i * (pp * 8), pp * 8)
    lo_s[pl.ds(r0, pp * 8), :] = jnp.concatenate(los, axis=0)
    hi_s[pl.ds(r0, pp * 8), :] = jnp.concatenate(his, axis=0)
    b0 = pl.multiple_of(i * (pp * bpp), pp * bpp)
    for hh in range(N_HEADS):
        a_s[hh, pl.ds(b0, pp * bpp), :] = a_stage[hh]

    @pl.when(i == pl.num_programs(1) - 1)
    def _():
        row8 = _iota((8, 1), 0)
        cbm = _combine_lo_hi(lo_s[...], hi_s[...])
        kcb = cbm[:, 0:LANES].astype(BF16)
        vcb = cbm[:, LANES:2 * LANES].astype(BF16)
        sc = _dot(q4b, kcb, NT)
        validc = (_iota((1, ncp), 1) * CMP_STRIDE + CMP_LEN) <= past + 1
        p = _masked_softmax(sc, validc)
        o_c = _dot(p.astype(BF16), vcb)
        pg = jnp.where(row8 == 0, p[0:1] + p[1:2], jnp.where(row8 == 1, p[2:3] + p[3:4], 0.0))
        pool4 = (_iota((ncp, nbp), 0) // (SEL_BLOCK // CMP_STRIDE) == _iota((ncp, nbp), 1)).astype(F32)
        imp2 = _dot(pg, pool4, precision=HIGHEST)
        ii = _iota((nbp, nbp), 0)
        jj = _iota((nbp, nbp), 1)
        sels = []
        for g in range(NSA_KV):
            mx = jnp.broadcast_to(imp2[g:g + 1, :], (nbp, nbp))
            mt = mx.T
            beats = ((mt > mx) | ((mt == mx) & (ii < jj))) & (ii < nb)
            rank = jnp.sum(jnp.where(beats, 1.0, 0.0), axis=0, keepdims=True)
            sels.append(jnp.where((rank < n_top - 1) & (_iota((1, nbp), 1) < nb), 1.0, 0.0))
        sel8 = jnp.where(row8 < 2, sels[0], sels[1]) > 0.5
        newkv = newkv_ref[0]
        ksn = newkv[:, 2 * LANES:3 * LANES]
        vsn = newkv[:, 3 * LANES:4 * LANES]
        s_new = jnp.sum(q4 * ksn, axis=1, keepdims=True)
        mrow = m_s[...]
        m_all = jnp.maximum(jnp.max(jnp.where(sel8, mrow, NEG), axis=1, keepdims=True), s_new)
        wj = jnp.where(sel8, jnp.exp(mrow - m_all), 0.0)
        w_new = jnp.exp(s_new - m_all)
        ltot = jnp.sum(wj * l_s[...], axis=1, keepdims=True) + w_new
        osum = w_new * vsn
        for hh in range(N_HEADS):
            osum = osum + jnp.where(row8 == hh, _dot(wj, a_s[hh], precision=HIGHEST), 0.0)
        o_s = osum / jnp.maximum(ltot, 1e-30)
        win_t = win_ref[0]
        nw = win_t.shape[1]
        lane_w = _iota((1, nw), 1)
        sw = _dot(q4b, win_t[0:LANES, :].astype(BF16))
        validw = (past - nw + lane_w) > past - WINDOW
        neww = newwin_ref[0]
        sw_new = jnp.sum(q4 * neww[:, 0:LANES], axis=1, keepdims=True)
        mw = jnp.maximum(jnp.max(jnp.where(validw, sw, NEG), axis=1, keepdims=True), sw_new)
        pw = jnp.where(validw, jnp.exp(sw - mw), 0.0)
        pn = jnp.exp(sw_new - mw)
        zw = jnp.sum(pw, axis=1, keepdims=True) + pn
        o_w = (_dot(pw.astype(BF16), win_t[LANES:2 * LANES, :].astype(BF16), NT) + pn * neww[:, LANES:2 * LANES]) / zw
        gg = _sigmoid(gq_ref[0])
        o_ref[0] = o_c * gg[:, 0:1] + o_s * gg[:, 1:2] + o_w * gg[:, 2:3]
        wo_ref[0] = jnp.where(lane_w == nw - 1, newcol_ref[0], pltpu.roll(win_t, nw - 1, 1))


def _nsa_sample(pt, q4, gq, newkv, newwin, newcol, win_t, win_off, wall, pool_t, pp):
    bd, npages = pt.shape
    nw = win_t.shape[2]
    nb = 2 * npages
    nbp = -(-nb // LANES) * LANES
    n_top = min(N_SELECT, nb + 1)
    per_b = lambda b, i, pt_ref: (b, 0, 0)
    const = lambda b, i, pt_ref: (0, 0)

    def page_map(kk):
        return lambda b, i, pt_ref: (pt_ref[b, i * pp + kk], 0, 0)

    grid_spec = pltpu.PrefetchScalarGridSpec(
        num_scalar_prefetch=1,
        grid=(bd, npages // pp),
        in_specs=[pl.BlockSpec((1, 8, LANES), per_b),
                  pl.BlockSpec((1, 8, LANES), per_b),
                  pl.BlockSpec((1, 1, 4 * LANES), per_b),
                  pl.BlockSpec((1, 1, 2 * LANES), per_b),
                  pl.BlockSpec((1, 2 * LANES, 1), per_b),
                  pl.BlockSpec((1, 2 * LANES, nw), lambda b, i, pt_ref: (win_off + b, 0, 0)),
                  pl.BlockSpec(wall.shape, const)]
                 + [pl.BlockSpec((1, 4 * LANES, LANES), page_map(kk)) for kk in range(pp)],
        out_specs=[pl.BlockSpec((1, 8, LANES), per_b),
                   pl.BlockSpec((1, 2 * LANES, nw), per_b)],
        scratch_shapes=[pltpu.VMEM((npages * 8, 2 * LANES), F32),
                        pltpu.VMEM((npages * 8, 2 * LANES), F32),
                        pltpu.VMEM((8, nbp), F32),
                        pltpu.VMEM((8, nbp), F32),
                        pltpu.VMEM((N_HEADS, nbp, LANES), F32),
                        pltpu.VMEM((N_HEADS, pp * (LANES // SEL_BLOCK), LANES), F32)])
    return pl.pallas_call(
        functools.partial(_nsa_sample_kernel, pp=pp, npages=npages, n_top=n_top),
        grid_spec=grid_spec,
        out_shape=[jax.ShapeDtypeStruct((bd, 8, LANES), F32),
                   jax.ShapeDtypeStruct((bd, 2 * LANES, nw), F32)],
        compiler_params=_cparams(("parallel", "arbitrary")), name="nsa_sample",
    )(pt, q4, gq, newkv, newwin, newcol, win_t, wall, *([pool_t] * pp))


def _split_bf16(x):
    hi = x.astype(BF16)
    return hi, (x - hi.astype(F32)).astype(BF16)


def _out_kernel(x_ref, ya_ref, yb_ref, yc_ref, yd_ref, wo_ref, g2_ref, wq_ref, k1_ref, k2_ref,
                xn_ref, ht_ref, s1_ref, s2_ref, *, nh, dq):
    acc = x_ref[...]
    for idx, y_ref in enumerate((ya_ref, yb_ref, yc_ref, yd_ref)):
        acc = acc + _dot(y_ref[...].astype(BF16), wo_ref[idx * MIX_W:(idx + 1) * MIX_W, :])
    xn_ref[...] = acc
    h2 = _rmsnorm(acc, g2_ref[...])
    ht_ref[...] = h2.T.astype(BF16)
    h_hi, h_lo = _split_bf16(h2)
    q = _dot(h_hi, wq_ref[0]) + _dot(h_lo, wq_ref[0]) + _dot(h_hi, wq_ref[1])
    q_hi, q_lo = _split_bf16(q)
    half = dq // 2
    for h in range(nh):
        for s_ref, k_ref, cols in ((s1_ref, k1_ref, slice(h * dq, h * dq + half)),
                                   (s2_ref, k2_ref, slice(h * dq + half, (h + 1) * dq))):
            s_ref[h] = (_dot(k_ref[0, h], q_hi[:, cols], NT) + _dot(k_ref[1, h], q_hi[:, cols], NT)
                        + _dot(k_ref[0, h], q_lo[:, cols], NT))


def _out_proj(x, ya, yb, yc, yd, wo, g2, wq, k1, k2, tm):
    n, d = x.shape
    _, nh, nk, half = k1.shape
    row = lambda i: (i, 0)
    c2 = lambda i: (0, 0)
    return pl.pallas_call(
        functools.partial(_out_kernel, nh=nh, dq=2 * half),
        grid=(n // tm,),
        in_specs=[pl.BlockSpec((tm, d), row)] + [pl.BlockSpec((tm, MIX_W), row)] * 4
                 + [pl.BlockSpec(wo.shape, c2), pl.BlockSpec((1, d), c2), pl.BlockSpec(wq.shape, lambda i: (0, 0, 0)),
                    pl.BlockSpec(k1.shape, lambda i: (0, 0, 0, 0)), pl.BlockSpec(k2.shape, lambda i: (0, 0, 0, 0))],
        out_specs=[pl.BlockSpec((tm, d), row),
                   pl.BlockSpec((d, tm), lambda i: (0, i)),
                   pl.BlockSpec((nh, nk, tm), lambda i: (0, 0, i)),
                   pl.BlockSpec((nh, nk, tm), lambda i: (0, 0, i))],
        out_shape=[jax.ShapeDtypeStruct((n, d), F32),
                   jax.ShapeDtypeStruct((d, n), BF16),
                   jax.ShapeDtypeStruct((nh, nk, n), F32),
                   jax.ShapeDtypeStruct((nh, nk, n), F32)],
        compiler_params=_cparams(("parallel",)), name="out_proj",
    )(x, ya, yb, yc, yd, wo, g2, wq, k1, k2)


_PAIR_COUNTS = tuple(PEER_TOPK // (i + 1) for i in range(PEER_TOPK))
_PAIR_ROWS = -(-sum(_PAIR_COUNTS) // 8) * 8


def _gate_kernel(s1_ref, s2_ref, rk2_ref, e2_ref, rr_ref, c_ref, v1_s, v2_s, c_s):
    x1 = s1_ref[0]
    x2 = s2_ref[0]
    rank1 = jnp.full(x1.shape, float(PEER_TOPK), F32)
    rank2 = jnp.full(x2.shape, float(PEER_TOPK), F32)
    for i in range(PEER_TOPK):
        m1 = jnp.max(x1, axis=0, keepdims=True)
        m2 = jnp.max(x2, axis=0, keepdims=True)
        v1_s[i:i + 1, :] = m1
        v2_s[i:i + 1, :] = m2
        hit1 = x1 == m1
        hit2 = x2 == m2
        rank1 = jnp.where(hit1, float(i), rank1)
        rank2 = jnp.where(hit2, float(i), rank2)
        x1 = jnp.where(hit1, NEG, x1)
        x2 = jnp.where(hit2, NEG, x2)
    v1 = v1_s[...]
    v2 = v2_s[...]
    off = 0
    for i, cnt in enumerate(_PAIR_COUNTS):
        c_s[off:off + cnt, :] = v1[i:i + 1, :] + v2[0:cnt, :]
        off += cnt
    if off < _PAIR_ROWS:
        c_s[off:_PAIR_ROWS, :] = jnp.full((_PAIR_ROWS - off, v1.shape[1]), NEG, F32)
    c = c_s[...]
    mx = v1[0:1, :] + v2[0:1, :]
    z = jnp.zeros_like(mx)
    m = mx
    for i in range(PEER_TOPK):
        m = jnp.max(c, axis=0, keepdims=True)
        z = z + jnp.exp(m - mx)
        c = jnp.where(c == m, NEG, c)
    tau = m
    passing = jnp.zeros(v1.shape, F32)
    for jj in range(PEER_TOPK):
        passing = passing + jnp.where(v1 + v2[jj:jj + 1, :] >= tau, 1.0, 0.0)
    rr = jnp.zeros(rank1.shape, F32)
    for i in range(PEER_TOPK):
        rr = jnp.where(rank1 == float(i), passing[i:i + 1, :], rr)
    rk2_ref[0] = rank2.astype(BF16)
    rr_ref[0] = rr
    e2_ref[0] = jnp.exp(s2_ref[0] - v2[0:1, :]).astype(BF16)
    c_ref[0] = jnp.exp(s1_ref[0] - v1[0:1, :]) * (1.0 / z)


def _gate(s1t, s2t, tn):
    nh, nk, n = s1t.shape
    blk = lambda h, i: (h, 0, i)
    return pl.pallas_call(
        _gate_kernel,
        grid=(nh, n // tn),
        in_specs=[pl.BlockSpec((1, nk, tn), blk), pl.BlockSpec((1, nk, tn), blk)],
        out_specs=[pl.BlockSpec((1, nk, tn), blk)] * 4,
        out_shape=[jax.ShapeDtypeStruct((nh, nk, n), BF16), jax.ShapeDtypeStruct((nh, nk, n), BF16),
                   jax.ShapeDtypeStruct((nh, nk, n), F32), jax.ShapeDtypeStruct((nh, nk, n), F32)],
        scratch_shapes=[pltpu.VMEM((PEER_TOPK, tn), F32), pltpu.VMEM((PEER_TOPK, tn), F32),
                        pltpu.VMEM((_PAIR_ROWS, tn), F32)],
        compiler_params=_cparams(("parallel", "parallel")), name="peer_gate",
    )(s1t, s2t)


def _bf16_rows(row, n_rows):
    tile = jnp.broadcast_to(row, (16, row.shape[1])).astype(BF16)
    return jnp.concatenate([tile] * (n_rows // 16), axis=0)


def _peer_kernel(ht_ref, u_ref, vt_ref, rk2_ref, e2_ref, rr_ref, c_ref, xn_ref, fg_ref, o_ref,
                 acc_ref, f_ref, *, nh, nk, apc, final):
    j = pl.program_id(1)

    @pl.when(j == 0)
    def _():
        acc_ref[...] = jnp.zeros(acc_ref.shape, F32)

    ht = ht_ref[...]
    zero = jnp.zeros((), BF16)
    for aa in range(apc):
        a = j * apc + aa
        act = _gelu(_dot(u_ref[aa * nk:(aa + 1) * nk, :], ht)).astype(BF16)
        wgt = jnp.zeros(act.shape, BF16)
        for h in range(nh):
            limit = _bf16_rows(rr_ref[h, pl.ds(a, 1), :], nk)
            scale = _bf16_rows(c_ref[h, pl.ds(a, 1), :], nk)
            wgt = wgt + jnp.where(rk2_ref[h] < limit, e2_ref[h] * scale, zero)
        f_ref[aa * nk:(aa + 1) * nk, :] = wgt * act
    acc_ref[...] += _dot(vt_ref[...], f_ref[...])

    @pl.when(j == pl.num_programs(1) - 1)
    def _():
        out = xn_ref[...] + acc_ref[...].T
        if final:
            out = _rmsnorm(out, fg_ref[...])
        o_ref[...] = out


def _peer(ht, u, vt, rk2, e2, rr, c, xn, fg, tm, apc, final):
    d, n = ht.shape
    nh, nk, _ = rk2.shape
    assert nk % 16 == 0
    ne = u.shape[0]
    te = apc * nk
    tok3 = lambda i, j: (0, 0, i)
    return pl.pallas_call(
        functools.partial(_peer_kernel, nh=nh, nk=nk, apc=apc, final=final),
        grid=(n // tm, ne // te),
        in_specs=[pl.BlockSpec((d, tm), lambda i, j: (0, i)),
                  pl.BlockSpec((te, d), lambda i, j: (j, 0)),
                  pl.BlockSpec((d, te), lambda i, j: (0, j)),
                  pl.BlockSpec((nh, nk, tm), tok3),
                  pl.BlockSpec((nh, nk, tm), tok3),
                  pl.BlockSpec((nh, nk, tm), tok3),
                  pl.BlockSpec((nh, nk, tm), tok3),
                  pl.BlockSpec((tm, d), lambda i, j: (i, 0)),
                  pl.BlockSpec((1, d), lambda i, j: (0, 0))],
        out_specs=pl.BlockSpec((tm, d), lambda i, j: (i, 0)),
        out_shape=jax.ShapeDtypeStruct((n, d), F32),
        scratch_shapes=[pltpu.VMEM((d, tm), F32), pltpu.VMEM((te, tm), BF16)],
        compiler_params=_cparams(("parallel", "arbitrary")), name="peer",
    )(ht, u, vt, rk2, e2, rr, c, xn, fg)


def _rope_tables(pos):
    half = HEAD_DIM // 2
    inv_freq = ROPE_THETA ** (-jnp.arange(half, dtype=F32) / half)
    ang = pos.astype(F32)[:, None] * inv_freq[None, :]
    cos = jnp.cos(ang)
    sin = jnp.sin(ang)
    return jnp.tile(jnp.concatenate([cos, cos], axis=1), (1, 2)), jnp.tile(jnp.concatenate([-sin, sin], axis=1), (1, 2))


def _pad_rows(x, rows):
    return jnp.pad(x, ((0, rows - x.shape[0]),) + ((0, 0),) * (x.ndim - 1))


def _token_tile(n, pref):
    return pref if n % pref == 0 else n


def _page_pool_weights(wk, wv):
    wset = jnp.stack([wk[:, 0], wk[:, 1], wv[:, 0], wv[:, 1]]).reshape(4, 2, CMP_STRIDE)
    per_pos = jnp.tile(wset, (1, 1, LANES // CMP_STRIDE))
    chunk_of_pos = (jnp.arange(LANES) // CMP_STRIDE)[None, :] == jnp.arange(LANES // CMP_STRIDE)[:, None]
    return (per_pos[:, :, None, :] * chunk_of_pos[None, None].astype(F32)).reshape(64, LANES)


def kernel(x_prompt, x_sample, cache_nsa_kv, state_nsa_win, state_mlstm_C, state_mlstm_n, state_mlstm_m, state_conv, page_table, norm1_g, norm2_g, final_norm_g, w_in, w_out, mlstm_b_i, mlstm_b_f, mlstm_norm_g, conv_w, conv_b, conv_norm_g, conv_norm_b, nsa_cmp_wk, nsa_cmp_wv, gmlp_norm_g, gmlp_norm_b, gmlp_ws, gmlp_bs, peer_wq, peer_k1, peer_k2, peer_u, peer_v):
    depth = w_in.shape[0]
    bp, s_len, d = x_prompt.shape
    bd, t_dec, _ = x_sample.shape
    n_pool = cache_nsa_kv.shape[1]
    npages = page_table.shape[1]
    past = npages * cache_nsa_kv.shape[2]
    n_win = state_nsa_win.shape[2]
    assert t_dec == 1 and cache_nsa_kv.shape[2] == LANES and past % SEL_BLOCK == 0
    assert s_len >= WINDOW + 128 and n_win == WINDOW
    np_tok = bp * s_len
    ns_pad = LANES
    assert bd <= ns_pad

    xp = x_prompt.reshape(np_tok, d)
    xs = _pad_rows(x_sample.reshape(bd, d), ns_pad)
    cos_p, sin_p = _rope_tables(jnp.arange(s_len))
    cos_s, sin_s = _rope_tables(jnp.full((ns_pad,), past))
    pool_t = cache_nsa_kv.reshape(depth * n_pool, LANES, 4 * LANES).transpose(0, 2, 1)
    win_t = state_nsa_win.reshape(depth * bd, n_win, 2 * LANES).transpose(0, 2, 1)
    eye_h = jnp.eye(N_HEADS, dtype=F32)

    tm_in = _token_tile(s_len, 256)
    tm_out = _token_tile(np_tok, 256)
    tm_peer = _token_tile(np_tok, 512)
    nk = peer_k1.shape[2]
    apc = min(16, nk)
    w_in_t = w_in.transpose(2, 0, 1)

    new_p, new_s = [], []
    for l in range(depth):
        wi = w_in_t[:, l, :]
        w_perm = jnp.concatenate([wi[0:1024], wi[1032:2568], wi[2580:3092]], axis=0).astype(BF16)
        w_small = jnp.concatenate([wi[1024:1032], wi[2568:2580], jnp.zeros((108, d), F32)], axis=0).astype(BF16)
        g1 = norm1_g[l].reshape(1, d)
        g2 = norm2_g[l].reshape(1, d)
        gate_bias = jnp.concatenate([mlstm_b_i[l], mlstm_b_f[l], jnp.zeros((LANES - 2 * N_HEADS,), F32)]).reshape(1, LANES)
        mng = mlstm_norm_g[l].reshape(1, MIX_W)
        cw = conv_w[l]
        cbias = conv_b[l].reshape(1, MIX_W)
        cg = conv_norm_g[l].reshape(1, MIX_W)
        cbeta = conv_norm_b[l].reshape(1, MIX_W)
        w32 = jnp.concatenate([jnp.repeat(nsa_cmp_wk[l], HEAD_DIM, axis=1), jnp.repeat(nsa_cmp_wv[l], HEAD_DIM, axis=1)], axis=1)
        wlo, whi = w32[:CMP_STRIDE], w32[CMP_STRIDE:]
        wall = _page_pool_weights(nsa_cmp_wk[l], nsa_cmp_wv[l])
        gg = gmlp_norm_g[l].reshape(1, MIX_W)
        gbeta = gmlp_norm_b[l].reshape(1, MIX_W)
        gws = gmlp_ws[l]
        gbsb = jnp.repeat(gmlp_bs[l].T, HEAD_DIM, axis=1)
        wo = w_out[l].astype(BF16)
        wq = jnp.stack(_split_bf16(peer_wq[l]))
        k1 = jnp.stack(_split_bf16(peer_k1[l]))
        k2 = jnp.stack(_split_bf16(peer_k2[l]))
        ub = peer_u[l].astype(BF16)
        vtb = peer_v[l].astype(BF16).T
        fg = final_norm_g.reshape(1, d)
        final = l == depth - 1

        oa, ob, oq, okv, owin, okaug, ovsb, owinb, od, osm = _in_proj(xp, g1, w_perm, w_small, cos_p, sin_p, tm_in)
        ya, c_p, n_p, m_p = _mlstm(oa.reshape(bp, s_len, 1024), osm.reshape(bp, s_len, LANES), gate_bias, mng,
                                   jnp.zeros((bp, MIX_W, MIX_W), F32), jnp.zeros((bp, 1, MIX_W), F32),
                                   jnp.zeros((bp, 1, MIX_W), F32), MLSTM_CHUNK, MLSTM_CHUNK, bp)
        yb, conv_p = _conv(ob.reshape(bp, s_len, 2 * MIX_W), jnp.zeros((bp, CONV_PAD, MIX_W), F32),
                           cw, cbias, cg, cbeta, 512, 512)
        cb = _cmp(okv.reshape(bp, s_len, 4 * LANES), wlo, whi)
        yc = _nsa_prompt(oq.reshape(bp, s_len, MIX_W), osm.reshape(bp, s_len, LANES), cb,
                         okaug.reshape(bp, s_len, 2 * LANES), ovsb.reshape(bp, s_len, LANES),
                         owinb.reshape(bp, s_len, 2 * LANES), 128)
        yd, _ = _gmlp(od, gg, gbeta, gws, gbsb, 4)
        xn, ht, s1t, s2t = _out_proj(xp, ya.reshape(np_tok, MIX_W), yb.reshape(np_tok, MIX_W),
                                     yc.reshape(np_tok, MIX_W), yd, wo, g2, wq, k1, k2, tm_out)
        rk2, e2, rr, cw8 = _gate(s1t, s2t, _token_tile(np_tok, 512))
        xp = _peer(ht, ub, vtb, rk2, e2, rr, cw8, xn, fg, tm_peer, apc, final)
        new_p.append((okv.reshape(bp, s_len, 4, NSA_KV, HEAD_DIM),
                      owin.reshape(bp, s_len, 2, NSA_KV, HEAD_DIM)[:, s_len - n_win:],
                      jnp.stack([c_p[:, h * HEAD_DIM:(h + 1) * HEAD_DIM, h * HEAD_DIM:(h + 1) * HEAD_DIM]
                                 for h in range(N_HEADS)], axis=1),
                      n_p.reshape(bp, N_HEADS, HEAD_DIM),
                      m_p[:, 0, ::HEAD_DIM],
                      conv_p[:, CONV_PAD - (CONV_W - 1):]))

        sa, sb, sq, skv, swin, _, _, _, sd, ssm = _in_proj(xs, g1, w_perm, w_small, cos_s, sin_s, ns_pad)
        rows8 = lambda t: jnp.pad(t[:bd, None, :], ((0, 0), (0, 7), (0, 0)))
        c0 = jnp.einsum('bhvk,hg->bhvgk', state_mlstm_C[l], eye_h).reshape(bd, MIX_W, MIX_W)
        n0 = state_mlstm_n[l].reshape(bd, 1, MIX_W)
        m0 = jnp.repeat(state_mlstm_m[l], HEAD_DIM, axis=-1).reshape(bd, 1, MIX_W)
        bb_s = 2 if bd % 2 == 0 else 1
        ya_s, c_s, n_s, m_s = _mlstm(rows8(sa), rows8(ssm), gate_bias, mng, c0, n0, m0, 8, 1, bb_s)
        prefix = jnp.pad(state_conv[l], ((0, 0), (CONV_PAD - (CONV_W - 1), 0), (0, 0)))
        yb_s, conv_s = _conv(rows8(sb), prefix, cw, cbias, cg, cbeta, 8, 1)
        zd_s = jnp.pad(sd[:bd, None, :], ((0, 0), (0, GMLP_CHUNK - 1), (0, 0))).reshape(bd * GMLP_CHUNK, 2 * MIX_W)
        yd_s, v_s = _gmlp(zd_s, gg, gbeta, gws, gbsb, 1)
        q_heads = sq[:bd].reshape(bd, NSA_KV, 2, 1, HEAD_DIM)
        q4 = (q_heads * jnp.eye(NSA_KV, dtype=F32)[None, :, None, :, None]).reshape(bd, N_HEADS, LANES)
        q4 = jnp.pad(q4, ((0, 0), (0, 8 - N_HEADS), (0, 0)))
        gq = jnp.pad(ssm[:bd, 2 * N_HEADS:2 * N_HEADS + 3 * N_HEADS].reshape(bd, N_HEADS, 3),
                     ((0, 0), (0, 8 - N_HEADS), (0, LANES - 3)))
        assert npages % 4 == 0
        pp = 16 if npages % 16 == 0 else (8 if npages % 8 == 0 else 4)
        o8, win_s = _nsa_sample(page_table + l * n_pool, q4, gq, skv[:bd, None, :], swin[:bd, None, :],
                                swin[:bd, :, None], win_t, l * bd, wall, pool_t, pp)
        yc_s = jnp.concatenate([o8[:, h, (h // 2) * HEAD_DIM:(h // 2 + 1) * HEAD_DIM] for h in range(N_HEADS)], axis=1)
        xn_s, ht_s, s1t_s, s2t_s = _out_proj(xs, _pad_rows(ya_s[:, 0], ns_pad), _pad_rows(yb_s[:, 0], ns_pad),
                                             _pad_rows(yc_s, ns_pad),
                                             _pad_rows(yd_s.reshape(bd, GMLP_CHUNK, MIX_W)[:, 0], ns_pad),
                                             wo, g2, wq, k1, k2, ns_pad)
        rk2_s, e2_s, rr_s, cw8_s = _gate(s1t_s, s2t_s, ns_pad)
        xs = _peer(ht_s, ub, vtb, rk2_s, e2_s, rr_s, cw8_s, xn_s, fg, ns_pad, apc, final)
        new_s.append((skv[:bd].reshape(bd, 1, 4, NSA_KV, HEAD_DIM),
                      win_s.transpose(0, 2, 1).reshape(bd, n_win, 2, NSA_KV, HEAD_DIM),
                      jnp.stack([c_s[:, h * HEAD_DIM:(h + 1) * HEAD_DIM, h * HEAD_DIM:(h + 1) * HEAD_DIM]
                                 for h in range(N_HEADS)], axis=1),
                      n_s.reshape(bd, N_HEADS, HEAD_DIM),
                      m_s[:, 0, ::HEAD_DIM],
                      conv_s[:, CONV_PAD - (CONV_W - 1):],
                      v_s.reshape(bd, GMLP_CHUNK, MIX_W)[:, 0:1]))

    def stack(states, i):
        return jnp.stack([st[i] for st in states])

    y_prompt = xp.reshape(bp, s_len, d)
    y_sample = xs[:bd].reshape(bd, 1, d)
    return (y_prompt, y_sample,
            stack(new_p, 0), stack(new_s, 0),
            stack(new_p, 1), stack(new_s, 1),
            stack(new_p, 2), stack(new_s, 2),
            stack(new_p, 3), stack(new_s, 3),
            stack(new_p, 4), stack(new_s, 4),
            stack(new_p, 5), stack(new_s, 5),
            stack(new_s, 6))
```

```python
import functools

import jax
import jax.numpy as jnp
from jax import lax
from jax.experimental import pallas as pl
from jax.experimental.pallas import tpu as pltpu

F32 = jnp.float32
BF16 = jnp.bfloat16
HIGHEST = lax.Precision.HIGHEST

HEAD_DIM = 64
N_HEADS = 4
MIX_W = 256
NSA_KV = 2
NORM_EPS = 1e-6
NEG = -1e30
MLSTM_CHUNK = 128
CONV_W = 31
CONV_PAD = 32
CMP_STRIDE = 16
CMP_LEN = 32
SEL_BLOCK = 64
N_SELECT = 16
SEL_FORCE = 1e9
WINDOW = 512
ROPE_THETA = 10000.0
ATTN_SCALE = HEAD_DIM ** -0.5
GMLP_CHUNK = 128
PEER_TOPK = 16
LANES = 128
VMEM_LIMIT = 48 * 1024 * 1024

NT = (((1,), (1,)), ((), ()))
TN = (((0,), (0,)), ((), ()))


def _cparams(sem):
    return pltpu.CompilerParams(dimension_semantics=sem, vmem_limit_bytes=VMEM_LIMIT)


def _iota(shape, dim):
    return lax.broadcasted_iota(jnp.int32, shape, dim)


def _sigmoid(x):
    return 1.0 / (1.0 + jnp.exp(-x))


def _gelu(x):
    c1 = -2.0 * 0.7978845608028654 * 1.4426950408889634
    c2 = c1 * 0.044715
    return x / (1.0 + jnp.exp2(x * (c1 + c2 * (x * x))))


def _dot(a, b, dims=None, precision=None):
    if dims is None:
        return jnp.dot(a, b, preferred_element_type=F32, precision=precision)
    return lax.dot_general(a, b, dims, preferred_element_type=F32, precision=precision)


def _block_ones(n, blk):
    return (_iota((n, n), 0) // blk == _iota((n, n), 1) // blk).astype(F32)


def _group_layernorm(x, ee, width):
    mu = _dot(x, ee, precision=HIGHEST) * (1.0 / width)
    d = x - mu
    var = _dot(d * d, ee, precision=HIGHEST) * (1.0 / width)
    return d * lax.rsqrt(var + NORM_EPS)


def _rmsnorm(x, g):
    return x * lax.rsqrt(jnp.mean(x * x, axis=-1, keepdims=True) + NORM_EPS) * g


def _masked_softmax(s, valid):
    sm = jnp.where(valid, s, NEG)
    p = jnp.where(valid, jnp.exp(sm - jnp.max(sm, axis=-1, keepdims=True)), 0.0)
    return p / jnp.maximum(jnp.sum(p, axis=-1, keepdims=True), 1e-30)


def _rope(x, cos, sin_signed):
    w = x.shape[1]
    fwd = pltpu.roll(x, w - HEAD_DIM // 2, 1)
    bwd = pltpu.roll(x, HEAD_DIM // 2, 1)
    first = (_iota(x.shape, 1) % HEAD_DIM) < HEAD_DIM // 2
    return x * cos + jnp.where(first, fwd, bwd) * sin_signed


def _in_kernel(x_ref, g_ref, w_ref, wsmall_ref, cos_ref, sin_ref,
               oa_ref, ob_ref, oq_ref, okv_ref, owin_ref, okaug_ref, ovsb_ref, owinb_ref, od_ref, os_ref, *, npb):
    tm = x_ref.shape[0]
    y = _rmsnorm(x_ref[...], g_ref[...])
    yb = y.astype(BF16)
    z = _dot(yb, w_ref[...], NT)
    os_ref[...] = _dot(yb, wsmall_ref[...], NT)
    oa_ref[...] = z[:, 0:1024]
    ob_ref[...] = z[:, 1024:1536]
    cos = cos_ref[...]
    sin = sin_ref[...]
    cos2 = jnp.concatenate([cos, cos], axis=1)
    sin2 = jnp.concatenate([sin, sin], axis=1)
    oq_ref[...] = _rope(z[:, 1536:1792], cos2, sin2) * ATTN_SCALE
    kc = _rope(z[:, 1792:1920], cos, sin)
    vc = z[:, 1920:2048]
    ks = _rope(z[:, 2048:2176], cos, sin)
    vs = z[:, 2176:2304]
    kw = _rope(z[:, 2304:2432], cos, sin)
    vw = z[:, 2432:2560]
    okv_ref[...] = jnp.concatenate([kc, vc, ks, vs], axis=1)
    owin = jnp.concatenate([kw, vw], axis=1)
    owin_ref[...] = owin
    pos = (pl.program_id(0) % npb) * tm + _iota((tm, LANES), 0)
    onehot = jnp.where(_iota((tm, LANES), 1) == pos // SEL_BLOCK, 1.0, 0.0)
    okaug_ref[...] = jnp.concatenate([ks, onehot], axis=1).astype(BF16)
    ovsb_ref[...] = vs.astype(BF16)
    owinb_ref[...] = owin.astype(BF16)
    od_ref[...] = z[:, 2560:3072]


def _in_proj(x, g, w, wsmall, cos_t, sin_t, tm):
    n, d = x.shape
    npb = cos_t.shape[0] // tm
    widths = (1024, 512, 256, 512, 256, 256, 128, 256, 512, 128)
    dtypes = (F32, F32, F32, F32, F32, BF16, BF16, BF16, F32, F32)
    row = lambda i: (i, 0)
    return pl.pallas_call(
        functools.partial(_in_kernel, npb=npb),
        grid=(n // tm,),
        in_specs=[pl.BlockSpec((tm, d), row),
                  pl.BlockSpec((1, d), lambda i: (0, 0)),
                  pl.BlockSpec(w.shape, lambda i: (0, 0)),
                  pl.BlockSpec(wsmall.shape, lambda i: (0, 0)),
                  pl.BlockSpec((tm, LANES), lambda i: (i % npb, 0)),
                  pl.BlockSpec((tm, LANES), lambda i: (i % npb, 0))],
        out_specs=[pl.BlockSpec((tm, wd), row) for wd in widths],
        out_shape=[jax.ShapeDtypeStruct((n, wd), dt) for wd, dt in zip(widths, dtypes)],
        compiler_params=_cparams(("parallel",)), name="in_proj",
    )(x, g, w, wsmall, cos_t, sin_t)


def _mlstm_kernel(za_ref, zs_ref, bias_ref, ng_ref, c0_ref, n0_ref, m0_ref,
                  y_ref, c_ref, n_ref, m_ref, *, L, t_valid, bb):
    @pl.when(pl.program_id(1) == 0)
    def _():
        c_ref[...] = c0_ref[...]
        n_ref[...] = n0_ref[...]
        m_ref[...] = m0_ref[...]

    head_of_lane = _iota((1, MIX_W), 1) // HEAD_DIM
    causal = _iota((L, L), 0) >= _iota((L, L), 1)
    tril = causal.astype(F32)
    ee = _block_ones(MIX_W, HEAD_DIM)
    bdiag = ee > 0.5
    row128 = _iota((L, LANES), 0)
    lane128 = _iota((L, LANES), 1)
    for bi in range(bb):
        za = za_ref[bi]
        q = za[:, 0:256]
        k = za[:, 256:512] * (HEAD_DIM ** -0.5)
        v = za[:, 512:768]
        o = za[:, 768:1024]
        gi = zs_ref[bi] + bias_ref[...]
        ls = jnp.minimum(gi, 0.0) - jnp.log(1.0 + jnp.exp(-jnp.abs(gi)))
        if t_valid < L:
            live = row128 < t_valid
            ig = jnp.where(live, gi, NEG)
            ls = jnp.where(live, ls, 0.0)
        else:
            ig = gi
        gmat = jnp.where(lane128 < N_HEADS, ig, 0.0)
        lfm = jnp.where((lane128 >= N_HEADS) & (lane128 < 2 * N_HEADS), ls, 0.0)
        fc = _dot(tril, lfm, precision=HIGHEST)
        g_t = gmat.T
        f_t = fc.T
        mrow = m_ref[bi]
        nrow = n_ref[bi]
        cm = c_ref[bi]
        qb = q.astype(BF16)
        kb = k.astype(BF16)
        vb = v.astype(BF16)
        num = jnp.zeros((L, MIX_W), F32)
        decay_b = jnp.zeros((L, MIX_W), F32)
        dens_b = jnp.zeros((L, MIX_W), F32)
        mt_b = jnp.zeros((L, MIX_W), F32)
        w_b = jnp.zeros((L, MIX_W), F32)
        cd_b = jnp.zeros((1, MIX_W), F32)
        mnew_b = jnp.zeros((1, MIX_W), F32)
        for h in range(N_HEADS):
            hm = head_of_lane == h
            f_col = fc[:, N_HEADS + h:N_HEADS + h + 1]
            ig_col = gmat[:, h:h + 1]
            f_row = f_t[N_HEADS + h:N_HEADS + h + 1, :]
            ig_row = g_t[h:h + 1, :]
            mp = mrow[:, h * HEAD_DIM:h * HEAD_DIM + 1]
            dmat = jnp.where(causal, (f_col - f_row) + ig_row, NEG)
            m_inter = mp + f_col
            m_t = jnp.maximum(m_inter, jnp.max(dmat, axis=1, keepdims=True))
            d_exp = jnp.exp(dmat - m_t)
            decay = jnp.exp(m_inter - m_t)
            qh = jnp.where(hm, q, 0.0).astype(BF16)
            s = _dot(qh, kb, NT) * d_exp
            num = jnp.where(hm, _dot(s.astype(BF16), vb), num)
            dens_b = jnp.where(hm, jnp.sum(s, axis=1, keepdims=True), dens_b)
            decay_b = jnp.where(hm, decay, decay_b)
            mt_b = jnp.where(hm, m_t, mt_b)
            f_last = f_col[L - 1:L, :]
            m_new = m_t[L - 1:L, :]
            w_b = jnp.where(hm, jnp.exp((f_last - f_col) + ig_col - m_new), w_b)
            cd_b = jnp.where(hm, jnp.exp(mp + f_last - m_new), cd_b)
            mnew_b = jnp.where(hm, m_new, mnew_b)
        inter = _dot(qb, cm.astype(BF16), NT)
        nq_b = _dot(q * nrow, ee, precision=HIGHEST)
        hnum = num + decay_b * inter
        den_b = dens_b + decay_b * nq_b
        hh = hnum / jnp.maximum(jnp.abs(den_b), jnp.exp(-mt_b))
        hn = _group_layernorm(hh, ee, HEAD_DIM) * ng_ref[...]
        y_ref[bi] = hn * _sigmoid(o)
        kw = k * w_b
        upd = _dot(vb, kw.astype(BF16), TN)
        c_ref[bi] = jnp.where(bdiag, cd_b * cm + upd, 0.0)
        n_ref[bi] = cd_b * nrow + jnp.sum(kw, axis=0, keepdims=True)
        m_ref[bi] = mnew_b


def _mlstm(za, zs, bias, ng, c0, n0, m0, L, t_valid, bb):
    b, t, _ = za.shape
    nc = t // L
    st = lambda i, c: (i, 0, 0)
    return pl.pallas_call(
        functools.partial(_mlstm_kernel, L=L, t_valid=t_valid, bb=bb),
        grid=(b // bb, nc),
        in_specs=[pl.BlockSpec((bb, L, 1024), lambda i, c: (i, c, 0)),
                  pl.BlockSpec((bb, L, LANES), lambda i, c: (i, c, 0)),
                  pl.BlockSpec((1, LANES), lambda i, c: (0, 0)),
                  pl.BlockSpec((1, MIX_W), lambda i, c: (0, 0)),
                  pl.BlockSpec((bb, MIX_W, MIX_W), st),
                  pl.BlockSpec((bb, 1, MIX_W), st),
                  pl.BlockSpec((bb, 1, MIX_W), st)],
        out_specs=[pl.BlockSpec((bb, L, MIX_W), lambda i, c: (i, c, 0)),
                   pl.BlockSpec((bb, MIX_W, MIX_W), st),
                   pl.BlockSpec((bb, 1, MIX_W), st),
                   pl.BlockSpec((bb, 1, MIX_W), st)],
        out_shape=[jax.ShapeDtypeStruct((b, t, MIX_W), F32),
                   jax.ShapeDtypeStruct((b, MIX_W, MIX_W), F32),
                   jax.ShapeDtypeStruct((b, 1, MIX_W), F32),
                   jax.ShapeDtypeStruct((b, 1, MIX_W), F32)],
        compiler_params=_cparams(("parallel", "arbitrary")), name="mlstm",
    )(za, zs, bias, ng, c0, n0, m0)


def _conv_kernel(zb_ref, pre_ref, w_ref, cb_ref, g_ref, beta_ref, y_ref, st_ref, ext_ref, *, tt, tv):
    @pl.when(pl.program_id(1) == 0)
    def _():
        ext_ref[0:CONV_PAD, :] = pre_ref[0]

    z = zb_ref[0]
    u = z[:, :MIX_W] * _sigmoid(z[:, MIX_W:])
    ext_ref[CONV_PAD:CONV_PAD + tt, :] = u
    off = CONV_PAD - (CONV_W - 1)
    acc = jnp.zeros((tt, MIX_W), F32)
    for j in range(CONV_W):
        acc = acc + w_ref[j:j + 1, :] * ext_ref[off + j:off + j + tt, :]
    ee = _block_ones(MIX_W, HEAD_DIM)
    y = _group_layernorm(acc + cb_ref[...], ee, HEAD_DIM) * g_ref[...] + beta_ref[...]
    y_ref[0] = y * _sigmoid(y)
    st_ref[0] = ext_ref[tv:tv + CONV_PAD, :]
    ext_ref[0:CONV_PAD, :] = ext_ref[tt:tt + CONV_PAD, :]


def _conv(zb, prefix, w, cb, g, beta, tt, tv):
    b, t, _ = zb.shape
    vec = lambda i, j: (0, 0)
    return pl.pallas_call(
        functools.partial(_conv_kernel, tt=tt, tv=tv),
        grid=(b, t // tt),
        in_specs=[pl.BlockSpec((1, tt, 2 * MIX_W), lambda i, j: (i, j, 0)),
                  pl.BlockSpec((1, CONV_PAD, MIX_W), lambda i, j: (i, 0, 0)),
                  pl.BlockSpec((CONV_W, MIX_W), vec),
                  pl.BlockSpec((1, MIX_W), vec),
                  pl.BlockSpec((1, MIX_W), vec),
                  pl.BlockSpec((1, MIX_W), vec)],
        out_specs=[pl.BlockSpec((1, tt, MIX_W), lambda i, j: (i, j, 0)),
                   pl.BlockSpec((1, CONV_PAD, MIX_W), lambda i, j: (i, 0, 0))],
        out_shape=[jax.ShapeDtypeStruct((b, t, MIX_W), F32),
                   jax.ShapeDtypeStruct((b, CONV_PAD, MIX_W), F32)],
        scratch_shapes=[pltpu.VMEM((CONV_PAD + tt, MIX_W), F32)],
        compiler_params=_cparams(("parallel", "arbitrary")), name="conv",
    )(zb, prefix, w, cb, g, beta)


def _gmlp_kernel(zd_ref, g_ref, beta_ref, ws_ref, bsb_ref, y_ref, v_ref, *, cpb):
    z = zd_ref[...]
    ee = _block_ones(MIX_W, HEAD_DIM)
    u = _gelu(z[:, :MIX_W])
    vv = _group_layernorm(_gelu(z[:, MIX_W:]), ee, HEAD_DIM) * g_ref[...] + beta_ref[...]
    v_ref[...] = vv
    tril = _iota((GMLP_CHUNK, GMLP_CHUNK), 0) >= _iota((GMLP_CHUNK, GMLP_CHUNK), 1)
    head_of_lane = _iota((1, MIX_W), 1) // HEAD_DIM
    wm = [jnp.where(tril, ws_ref[h], 0.0).astype(BF16) for h in range(N_HEADS)]
    for c in range(cpb):
        rows = slice(c * GMLP_CHUNK, (c + 1) * GMLP_CHUNK)
        vc = vv[rows].astype(BF16)
        mixed = jnp.zeros((GMLP_CHUNK, MIX_W), F32)
        for h in range(N_HEADS):
            mixed = jnp.where(head_of_lane == h, _dot(wm[h], vc), mixed)
        y_ref[rows, :] = u[rows] * (mixed + bsb_ref[...])


def _gmlp(zd, g, beta, ws, bsb, cpb):
    n = zd.shape[0]
    rows = cpb * GMLP_CHUNK
    vec = lambda i: (0, 0)
    return pl.pallas_call(
        functools.partial(_gmlp_kernel, cpb=cpb),
        grid=(n // rows,),
        in_specs=[pl.BlockSpec((rows, 2 * MIX_W), lambda i: (i, 0)),
                  pl.BlockSpec((1, MIX_W), vec),
                  pl.BlockSpec((1, MIX_W), vec),
                  pl.BlockSpec((N_HEADS, GMLP_CHUNK, GMLP_CHUNK), lambda i: (0, 0, 0)),
                  pl.BlockSpec((GMLP_CHUNK, MIX_W), vec)],
        out_specs=[pl.BlockSpec((rows, MIX_W), lambda i: (i, 0)),
                   pl.BlockSpec((rows, MIX_W), lambda i: (i, 0))],
        out_shape=[jax.ShapeDtypeStruct((n, MIX_W), F32),
                   jax.ShapeDtypeStruct((n, MIX_W), F32)],
        compiler_params=_cparams(("parallel",)), name="gmlp",
    )(zd, g, beta, ws, bsb)


def _combine_lo_hi(lo, hi):
    r = lo.shape[0]
    nxt = pltpu.roll(hi, r - 1, 0)
    return lo + jnp.where(_iota(lo.shape, 0) < r - 1, nxt, 0.0)


def _cmp_kernel(kv_ref, wlo_ref, whi_ref, o_ref):
    x = kv_ref[0]
    x3 = x.reshape(x.shape[0] // CMP_STRIDE, CMP_STRIDE, x.shape[1])
    lo = jnp.sum(x3 * wlo_ref[...][None], axis=1)
    hi = jnp.sum(x3 * whi_ref[...][None], axis=1)
    o_ref[0] = _combine_lo_hi(lo, hi)


def _cmp(okv3, wlo, whi):
    b, s, _ = okv3.shape
    return pl.pallas_call(
        _cmp_kernel,
        grid=(b,),
        in_specs=[pl.BlockSpec((1, s, 2 * LANES), lambda i: (i, 0, 0)),
                  pl.BlockSpec((CMP_STRIDE, 2 * LANES), lambda i: (0, 0)),
                  pl.BlockSpec((CMP_STRIDE, 2 * LANES), lambda i: (0, 0))],
        out_specs=pl.BlockSpec((1, s // CMP_STRIDE, 2 * LANES), lambda i: (i, 0, 0)),
        out_shape=jax.ShapeDtypeStruct((b, s // CMP_STRIDE, 2 * LANES), F32),
        compiler_params=_cparams(("parallel",)), name="nsa_cmp",
    )(okv3, wlo, whi)


def _nsa_prompt_kernel(q_ref, gs_ref, cb_ref, ka_ref, vs_ref, win_ref, y_ref, s_buf, *, tq, s_len, n_top, tk):
    start = pl.program_id(1) * tq
    nc = s_len // CMP_STRIDE
    ns = s_len // SEL_BLOCK
    q = q_ref[0]
    gl = _sigmoid(gs_ref[0])
    lo_half = _iota((tq, LANES), 1) < HEAD_DIM
    q_rows = []
    for g in range(NSA_KV):
        qpair = q[:, g * LANES:(g + 1) * LANES]
        swapped = pltpu.roll(qpair, HEAD_DIM, 1)
        if g == 0:
            q_rows += [jnp.where(lo_half, qpair, 0.0), jnp.where(lo_half, swapped, 0.0)]
        else:
            q_rows += [jnp.where(lo_half, 0.0, swapped), jnp.where(lo_half, 0.0, qpair)]
    q128 = jnp.concatenate(q_rows, axis=0)
    q128b = q128.astype(BF16)
    trow = start + _iota((tq, 1), 0)
    t2 = jnp.concatenate([trow, trow], axis=0)
    t4 = jnp.concatenate([t2, t2], axis=0)
    cb = cb_ref[0]
    sc = _dot(q128b, cb[:, 0:LANES].astype(BF16), NT)
    validc = (_iota((1, nc), 1) * CMP_STRIDE + CMP_LEN) <= (t4 + 1)
    p = _masked_softmax(sc, validc)
    o_c = _dot(p.astype(BF16), cb[:, LANES:2 * LANES].astype(BF16))
    psum = jnp.concatenate([p[0:tq] + p[tq:2 * tq], p[2 * tq:3 * tq] + p[3 * tq:4 * tq]], axis=0)
    pool4 = (_iota((nc, ns), 0) // (SEL_BLOCK // CMP_STRIDE) == _iota((nc, ns), 1)).astype(F32)
    imp = _dot(psum, pool4, precision=HIGHEST)
    jidx = _iota((1, ns), 1)
    imp = jnp.where(jidx == t2 // SEL_BLOCK, SEL_FORCE, imp)
    imp = jnp.where(jidx * SEL_BLOCK <= t2, imp, -1.0)
    if n_top < ns:
        x = jnp.concatenate([imp[0:tq].T, imp[tq:2 * tq].T], axis=1)
        jrow = _iota((ns, 2 * tq), 0).astype(F32)
        sel_t = jnp.zeros((ns, 2 * tq), F32)
        for _ in range(n_top):
            m = jnp.max(x, axis=0, keepdims=True)
            first = jnp.min(jnp.where(x == m, jrow, float(ns)), axis=0, keepdims=True)
            hit = jrow == first
            sel_t = jnp.where(hit, 1.0, sel_t)
            x = jnp.where(hit, -3.0, x)
        sel = jnp.concatenate([sel_t[:, 0:tq].T, sel_t[:, tq:2 * tq].T], axis=0)
    else:
        sel = jnp.ones((2 * tq, ns), F32)
    selneg = (sel - 1.0) * (-NEG)
    if ns < LANES:
        selneg = jnp.concatenate([selneg, jnp.zeros((2 * tq, LANES - ns), F32)], axis=1)
    selneg4 = jnp.concatenate([selneg[0:tq], selneg[0:tq], selneg[tq:2 * tq], selneg[tq:2 * tq]], axis=0)
    qaug = jnp.concatenate([q128, selneg4], axis=1).astype(BF16)

    def scores(c):
        return _dot(qaug, ka_ref[0, pl.ds(pl.multiple_of(c * tk, tk), tk), :], NT)

    def absorb(c, s, carry):
        m, l, acc = carry
        m_new = jnp.maximum(m, jnp.max(s, axis=1, keepdims=True))
        pt = jnp.exp(s - m_new)
        alpha = jnp.exp(m - m_new)
        l = alpha * l + jnp.sum(pt, axis=1, keepdims=True)
        acc = alpha * acc + _dot(pt.astype(BF16), vs_ref[0, pl.ds(pl.multiple_of(c * tk, tk), tk), :])
        return m_new, l, acc

    def tile(c, carry):
        s = s_buf[...]
        s_buf[...] = scores(c + 1)
        return absorb(c, s, carry)

    n_tiles = (start + tq + tk - 1) // tk
    carry = (jnp.full((4 * tq, 1), NEG, F32), jnp.zeros((4 * tq, 1), F32), jnp.zeros((4 * tq, LANES), F32))
    s_buf[...] = scores(0)
    carry = lax.fori_loop(0, n_tiles - 1, tile, carry)
    last = n_tiles - 1
    s_diag = jnp.where(last * tk + _iota((1, tk), 1) <= t4, s_buf[...], NEG)
    _, l, acc = absorb(last, s_diag, carry)
    o_s = acc / jnp.maximum(l, 1e-30)
    wl = WINDOW + tq
    w0 = pl.multiple_of(jnp.maximum(start - WINDOW, 0), tq)
    sw = _dot(q128b, win_ref[0, pl.ds(w0, wl), 0:LANES], NT)
    kposw = w0 + _iota((1, wl), 1)
    validw = (kposw <= t4) & (kposw > t4 - WINDOW)
    o_w = _dot(_masked_softmax(sw, validw).astype(BF16), win_ref[0, pl.ds(w0, wl), LANES:2 * LANES])
    heads = []
    for hh in range(N_HEADS):
        col = 2 * N_HEADS + hh * 3
        rows = slice(hh * tq, (hh + 1) * tq)
        heads.append(o_c[rows] * gl[:, col:col + 1] + o_s[rows] * gl[:, col + 1:col + 2]
                     + o_w[rows] * gl[:, col + 2:col + 3])
    y_ref[0] = jnp.concatenate([jnp.where(lo_half, heads[0], pltpu.roll(heads[1], HEAD_DIM, 1)),
                                jnp.where(lo_half, pltpu.roll(heads[2], HEAD_DIM, 1), heads[3])], axis=1)


def _nsa_prompt(oq3, os3, cb, okaug3, ovsb3, owinb3, tq):
    b, s, _ = oq3.shape
    ns = s // SEL_BLOCK
    assert ns <= LANES
    n_top = min(N_SELECT, ns)
    tk = min(512, s)
    full = lambda i, j: (i, 0, 0)
    return pl.pallas_call(
        functools.partial(_nsa_prompt_kernel, tq=tq, s_len=s, n_top=n_top, tk=tk),
        grid=(b, s // tq),
        in_specs=[pl.BlockSpec((1, tq, MIX_W), lambda i, j: (i, j, 0)),
                  pl.BlockSpec((1, tq, LANES), lambda i, j: (i, j, 0)),
                  pl.BlockSpec((1, s // CMP_STRIDE, 2 * LANES), full),
                  pl.BlockSpec((1, s, 2 * LANES), full),
                  pl.BlockSpec((1, s, LANES), full),
                  pl.BlockSpec((1, s, 2 * LANES), full)],
        out_specs=pl.BlockSpec((1, tq, MIX_W), lambda i, j: (i, j, 0)),
        out_shape=jax.ShapeDtypeStruct((b, s, MIX_W), F32),
        scratch_shapes=[pltpu.VMEM((N_HEADS * tq, tk), F32)],
        compiler_params=_cparams(("parallel", "arbitrary")), name="nsa_prompt",
    )(oq3, os3, cb, okaug3, ovsb3, owinb3)


def _nsa_sample_kernel(pt_ref, q4_ref, gq_ref, newkv_ref, newwin_ref, newcol_ref, win_ref, wall_ref, *rest,
                       pp, npages, n_top):
    pool_refs = rest[:pp]
    o_ref, wo_ref, lo_s, hi_s, m_s, l_s, a_s, a_stage = rest[pp:]
    i = pl.program_id(1)
    nb = 2 * npages
    nbp = m_s.shape[1]
    ncp = npages * (LANES // CMP_STRIDE)
    past = npages * LANES
    q4 = q4_ref[0]
    q4b = q4.astype(BF16)

    @pl.when(i == 0)
    def _():
        m_s[...] = jnp.zeros(m_s.shape, F32)
        l_s[...] = jnp.zeros(l_s.shape, F32)
        if nb < nbp:
            a_s[...] = jnp.zeros(a_s.shape, F32)

    lane8 = _iota((8, LANES), 1)
    lanej = _iota((8, nbp), 1)
    set_of_lane = _iota((8, 2 * LANES), 1) // HEAD_DIM
    bpp = LANES // SEL_BLOCK
    wall = wall_ref[...].astype(BF16)
    m_acc = m_s[...]
    l_acc = l_s[...]
    los, his = [], []
    for kk in range(pp):
        page = pool_refs[kk][0]
        pooled = _dot(wall, page[0:2 * LANES, :].astype(BF16), NT)
        lo8 = jnp.zeros((8, 2 * LANES), F32)
        hi8 = jnp.zeros((8, 2 * LANES), F32)
        for st in range(4):
            lo8 = jnp.where(set_of_lane == st, pooled[st * 16:st * 16 + 8], lo8)
            hi8 = jnp.where(set_of_lane == st, pooled[st * 16 + 8:st * 16 + 16], hi8)
        los.append(lo8)
        his.append(hi8)
        ks_t = page[2 * LANES:3 * LANES, :].astype(BF16)
        vs_t = page[3 * LANES:4 * LANES, :].astype(BF16)
        s = _dot(q4b, ks_t)
        probs = []
        for jb in range(bpp):
            inb = (lane8 >= jb * SEL_BLOCK) & (lane8 < (jb + 1) * SEL_BLOCK)
            sm = jnp.where(inb, s, NEG)
            m = jnp.max(sm, axis=1, keepdims=True)
            p = jnp.where(inb, jnp.exp(sm - m), 0.0)
            probs.append(p)
            j = i * (pp * bpp) + kk * bpp + jb
            m_acc = jnp.where(lanej == j, m, m_acc)
            l_acc = jnp.where(lanej == j, jnp.sum(p, axis=1, keepdims=True), l_acc)
        a = _dot(jnp.concatenate(probs, axis=0).astype(BF16), vs_t, NT)
        for jb in range(bpp):
            for hh in range(N_HEADS):
                a_stage[hh, kk * bpp + jb:kk * bpp + jb + 1, :] = a[jb * 8 + hh:jb * 8 + hh + 1, :]
    m_s[...] = m_acc
    l_s[...] = l_acc
    r0 = pl.multiple_of(i * (pp * 8), pp * 8)
    lo_s[pl.ds(r0, pp * 8), :] = jnp.concatenate(los, axis=0)
    hi_s[pl.ds(r0, pp * 8), :] = jnp.concatenate(his, axis=0)
    b0 = pl.multiple_of(i * (pp * bpp), pp * bpp)
    for hh in range(N_HEADS):
        a_s[hh, pl.ds(b0, pp * bpp), :] = a_stage[hh]

    @pl.when(i == pl.num_programs(1) - 1)
    def _():
        row8 = _iota((8, 1), 0)
        cbm = _combine_lo_hi(lo_s[...], hi_s[...])
        kcb = cbm[:, 0:LANES].astype(BF16)
        vcb = cbm[:, LANES:2 * LANES].astype(BF16)
        sc = _dot(q4b, kcb, NT)
        validc = (_iota((1, ncp), 1) * CMP_STRIDE + CMP_LEN) <= past + 1
        p = _masked_softmax(sc, validc)
        o_c = _dot(p.astype(BF16), vcb)
        pg = jnp.where(row8 == 0, p[0:1] + p[1:2], jnp.where(row8 == 1, p[2:3] + p[3:4], 0.0))
        pool4 = (_iota((ncp, nbp), 0) // (SEL_BLOCK // CMP_STRIDE) == _iota((ncp, nbp), 1)).astype(F32)
        imp2 = _dot(pg, pool4, precision=HIGHEST)
        ii = _iota((nbp, nbp), 0)
        jj = _iota((nbp, nbp), 1)
        sels = []
        for g in range(NSA_KV):
            mx = jnp.broadcast_to(imp2[g:g + 1, :], (nbp, nbp))
            mt = mx.T
            beats = ((mt > mx) | ((mt == mx) & (ii < jj))) & (ii < nb)
            rank = jnp.sum(jnp.where(beats, 1.0, 0.0), axis=0, keepdims=True)
            sels.append(jnp.where((rank < n_top - 1) & (_iota((1, nbp), 1) < nb), 1.0, 0.0))
        sel8 = jnp.where(row8 < 2, sels[0], sels[1]) > 0.5
        newkv = newkv_ref[0]
        ksn = newkv[:, 2 * LANES:3 * LANES]
        vsn = newkv[:, 3 * LANES:4 * LANES]
        s_new = jnp.sum(q4 * ksn, axis=1, keepdims=True)
        mrow = m_s[...]
        m_all = jnp.maximum(jnp.max(jnp.where(sel8, mrow, NEG), axis=1, keepdims=True), s_new)
        wj = jnp.where(sel8, jnp.exp(mrow - m_all), 0.0)
        w_new = jnp.exp(s_new - m_all)
        ltot = jnp.sum(wj * l_s[...], axis=1, keepdims=True) + w_new
        osum = w_new * vsn
        for hh in range(N_HEADS):
            osum = osum + jnp.where(row8 == hh, _dot(wj, a_s[hh], precision=HIGHEST), 0.0)
        o_s = osum / jnp.maximum(ltot, 1e-30)
        win_t = win_ref[0]
        nw = win_t.shape[1]
        lane_w = _iota((1, nw), 1)
        sw = _dot(q4b, win_t[0:LANES, :].astype(BF16))
        validw = (past - nw + lane_w) > past - WINDOW
        neww = newwin_ref[0]
        sw_new = jnp.sum(q4 * neww[:, 0:LANES], axis=1, keepdims=True)
        mw = jnp.maximum(jnp.max(jnp.where(validw, sw, NEG), axis=1, keepdims=True), sw_new)
        pw = jnp.where(validw, jnp.exp(sw - mw), 0.0)
        pn = jnp.exp(sw_new - mw)
        zw = jnp.sum(pw, axis=1, keepdims=True) + pn
        o_w = (_dot(pw.astype(BF16), win_t[LANES:2 * LANES, :].astype(BF16), NT) + pn * neww[:, LANES:2 * LANES]) / zw
        gg = _sigmoid(gq_ref[0])
        o_ref[0] = o_c * gg[:, 0:1] + o_s * gg[:, 1:2] + o_w * gg[:, 2:3]
        wo_ref[0] = jnp.where(lane_w == nw - 1, newcol_ref[0], pltpu.roll(win_t, nw - 1, 1))


def _nsa_sample(pt, q4, gq, newkv, newwin, newcol, win_t, win_off, wall, pool_t, pp):
    bd, npages = pt.shape
    nw = win_t.shape[2]
    nb = 2 * npages
    nbp = -(-nb // LANES) * LANES
    n_top = min(N_SELECT, nb + 1)
    per_b = lambda b, i, pt_ref: (b, 0, 0)
    const = lambda b, i, pt_ref: (0, 0)

    def page_map(kk):
        return lambda b, i, pt_ref: (pt_ref[b, i * pp + kk], 0, 0)

    grid_spec = pltpu.PrefetchScalarGridSpec(
        num_scalar_prefetch=1,
        grid=(bd, npages // pp),
        in_specs=[pl.BlockSpec((1, 8, LANES), per_b),
                  pl.BlockSpec((1, 8, LANES), per_b),
                  pl.BlockSpec((1, 1, 4 * LANES), per_b),
                  pl.BlockSpec((1, 1, 2 * LANES), per_b),
                  pl.BlockSpec((1, 2 * LANES, 1), per_b),
                  pl.BlockSpec((1, 2 * LANES, nw), lambda b, i, pt_ref: (win_off + b, 0, 0)),
                  pl.BlockSpec(wall.shape, const)]
                 + [pl.BlockSpec((1, 4 * LANES, LANES), page_map(kk)) for kk in range(pp)],
        out_specs=[pl.BlockSpec((1, 8, LANES), per_b),
                   pl.BlockSpec((1, 2 * LANES, nw), per_b)],
        scratch_shapes=[pltpu.VMEM((npages * 8, 2 * LANES), F32),
                        pltpu.VMEM((npages * 8, 2 * LANES), F32),
                        pltpu.VMEM((8, nbp), F32),
                        pltpu.VMEM((8, nbp), F32),
                        pltpu.VMEM((N_HEADS, nbp, LANES), F32),
                        pltpu.VMEM((N_HEADS, pp * (LANES // SEL_BLOCK), LANES), F32)])
    return pl.pallas_call(
        functools.partial(_nsa_sample_kernel, pp=pp, npages=npages, n_top=n_top),
        grid_spec=grid_spec,
        out_shape=[jax.ShapeDtypeStruct((bd, 8, LANES), F32),
                   jax.ShapeDtypeStruct((bd, 2 * LANES, nw), F32)],
        compiler_params=_cparams(("parallel", "arbitrary")), name="nsa_sample",
    )(pt, q4, gq, newkv, newwin, newcol, win_t, wall, *([pool_t] * pp))


def _split_bf16(x):
    hi = x.astype(BF16)
    return hi, (x - hi.astype(F32)).astype(BF16)


def _out_kernel(x_ref, ya_ref, yb_ref, yc_ref, yd_ref, wo_ref, g2_ref, wq_ref, k1_ref, k2_ref,
                xn_ref, ht_ref, s1_ref, s2_ref, *, nh, dq):
    acc = x_ref[...]
    for idx, y_ref in enumerate((ya_ref, yb_ref, yc_ref, yd_ref)):
        acc = acc + _dot(y_ref[...].astype(BF16), wo_ref[idx * MIX_W:(idx + 1) * MIX_W, :])
    xn_ref[...] = acc
    h2 = _rmsnorm(acc, g2_ref[...])
    ht_ref[...] = h2.T.astype(BF16)
    h_hi, h_lo = _split_bf16(h2)
    q = _dot(h_hi, wq_ref[0]) + _dot(h_lo, wq_ref[0]) + _dot(h_hi, wq_ref[1])
    q_hi, q_lo = _split_bf16(q)
    half = dq // 2
    for h in range(nh):
        for s_ref, k_ref, cols in ((s1_ref, k1_ref, slice(h * dq, h * dq + half)),
                                   (s2_ref, k2_ref, slice(h * dq + half, (h + 1) * dq))):
            s_ref[h] = (_dot(k_ref[0, h], q_hi[:, cols], NT) + _dot(k_ref[1, h], q_hi[:, cols], NT)
                        + _dot(k_ref[0, h], q_lo[:, cols], NT))


def _out_proj(x, ya, yb, yc, yd, wo, g2, wq, k1, k2, tm):
    n, d = x.shape
    _, nh, nk, half = k1.shape
    row = lambda i: (i, 0)
    c2 = lambda i: (0, 0)
    return pl.pallas_call(
        functools.partial(_out_kernel, nh=nh, dq=2 * half),
        grid=(n // tm,),
        in_specs=[pl.BlockSpec((tm, d), row)] + [pl.BlockSpec((tm, MIX_W), row)] * 4
                 + [pl.BlockSpec(wo.shape, c2), pl.BlockSpec((1, d), c2), pl.BlockSpec(wq.shape, lambda i: (0, 0, 0)),
                    pl.BlockSpec(k1.shape, lambda i: (0, 0, 0, 0)), pl.BlockSpec(k2.shape, lambda i: (0, 0, 0, 0))],
        out_specs=[pl.BlockSpec((tm, d), row),
                   pl.BlockSpec((d, tm), lambda i: (0, i)),
                   pl.BlockSpec((nh, nk, tm), lambda i: (0, 0, i)),
                   pl.BlockSpec((nh, nk, tm), lambda i: (0, 0, i))],
        out_shape=[jax.ShapeDtypeStruct((n, d), F32),
                   jax.ShapeDtypeStruct((d, n), BF16),
                   jax.ShapeDtypeStruct((nh, nk, n), F32),
                   jax.ShapeDtypeStruct((nh, nk, n), F32)],
        compiler_params=_cparams(("parallel",)), name="out_proj",
    )(x, ya, yb, yc, yd, wo, g2, wq, k1, k2)


_PAIR_COUNTS = tuple(PEER_TOPK // (i + 1) for i in range(PEER_TOPK))
_PAIR_ROWS = -(-sum(_PAIR_COUNTS) // 8) * 8


def _gate_kernel(s1_ref, s2_ref, rk2_ref, e2_ref, rr_ref, c_ref, v1_s, v2_s, c_s):
    x1 = s1_ref[0]
    x2 = s2_ref[0]
    rank1 = jnp.full(x1.shape, float(PEER_TOPK), F32)
    rank2 = jnp.full(x2.shape, float(PEER_TOPK), F32)
    for i in range(PEER_TOPK):
        m1 = jnp.max(x1, axis=0, keepdims=True)
        m2 = jnp.max(x2, axis=0, keepdims=True)
        v1_s[i:i + 1, :] = m1
        v2_s[i:i + 1, :] = m2
        hit1 = x1 == m1
        hit2 = x2 == m2
        rank1 = jnp.where(hit1, float(i), rank1)
        rank2 = jnp.where(hit2, float(i), rank2)
        x1 = jnp.where(hit1, NEG, x1)
        x2 = jnp.where(hit2, NEG, x2)
    v1 = v1_s[...]
    v2 = v2_s[...]
    off = 0
    for i, cnt in enumerate(_PAIR_COUNTS):
        c_s[off:off + cnt, :] = v1[i:i + 1, :] + v2[0:cnt, :]
        off += cnt
    if off < _PAIR_ROWS:
        c_s[off:_PAIR_ROWS, :] = jnp.full((_PAIR_ROWS - off, v1.shape[1]), NEG, F32)
    c = c_s[...]
    mx = v1[0:1, :] + v2[0:1, :]
    z = jnp.zeros_like(mx)
    m = mx
    for i in range(PEER_TOPK):
        m = jnp.max(c, axis=0, keepdims=True)
        z = z + jnp.exp(m - mx)
        c = jnp.where(c == m, NEG, c)
    tau = m
    passing = jnp.zeros(v1.shape, F32)
    for jj in range(PEER_TOPK):
        passing = passing + jnp.where(v1 + v2[jj:jj + 1, :] >= tau, 1.0, 0.0)
    rr = jnp.zeros(rank1.shape, F32)
    for i in range(PEER_TOPK):
        rr = jnp.where(rank1 == float(i), passing[i:i + 1, :], rr)
    rk2_ref[0] = rank2.astype(BF16)
    rr_ref[0] = rr
    e2_ref[0] = jnp.exp(s2_ref[0] - v2[0:1, :]).astype(BF16)
    c_ref[0] = jnp.exp(s1_ref[0] - v1[0:1, :]) * (1.0 / z)


def _gate(s1t, s2t, tn):
    nh, nk, n = s1t.shape
    blk = lambda h, i: (h, 0, i)
    return pl.pallas_call(
        _gate_kernel,
        grid=(nh, n // tn),
        in_specs=[pl.BlockSpec((1, nk, tn), blk), pl.BlockSpec((1, nk, tn), blk)],
        out_specs=[pl.BlockSpec((1, nk, tn), blk)] * 4,
        out_shape=[jax.ShapeDtypeStruct((nh, nk, n), BF16), jax.ShapeDtypeStruct((nh, nk, n), BF16),
                   jax.ShapeDtypeStruct((nh, nk, n), F32), jax.ShapeDtypeStruct((nh, nk, n), F32)],
        scratch_shapes=[pltpu.VMEM((PEER_TOPK, tn), F32), pltpu.VMEM((PEER_TOPK, tn), F32),
                        pltpu.VMEM((_PAIR_ROWS, tn), F32)],
        compiler_params=_cparams(("parallel", "parallel")), name="peer_gate",
    )(s1t, s2t)


def _bf16_rows(row, n_rows):
    tile = jnp.broadcast_to(row, (16, row.shape[1])).astype(BF16)
    return jnp.concatenate([tile] * (n_rows // 16), axis=0)


def _peer_kernel(ht_ref, u_ref, vt_ref, rk2_ref, e2_ref, rr_ref, c_ref, xn_ref, fg_ref, o_ref,
                 acc_ref, f_ref, *, nh, nk, apc, final):
    j = pl.program_id(1)

    @pl.when(j == 0)
    def _():
        acc_ref[...] = jnp.zeros(acc_ref.shape, F32)

    ht = ht_ref[...]
    zero = jnp.zeros((), BF16)
    for aa in range(apc):
        a = j * apc + aa
        act = _gelu(_dot(u_ref[aa * nk:(aa + 1) * nk, :], ht)).astype(BF16)
        wgt = jnp.zeros(act.shape, BF16)
        for h in range(nh):
            limit = _bf16_rows(rr_ref[h, pl.ds(a, 1), :], nk)
            scale = _bf16_rows(c_ref[h, pl.ds(a, 1), :], nk)
            wgt = wgt + jnp.where(rk2_ref[h] < limit, e2_ref[h] * scale, zero)
        f_ref[aa * nk:(aa + 1) * nk, :] = wgt * act
    acc_ref[...] += _dot(vt_ref[...], f_ref[...])

    @pl.when(j == pl.num_programs(1) - 1)
    def _():
        out = xn_ref[...] + acc_ref[...].T
        if final:
            out = _rmsnorm(out, fg_ref[...])
        o_ref[...] = out


def _peer(ht, u, vt, rk2, e2, rr, c, xn, fg, tm, apc, final):
    d, n = ht.shape
    nh, nk, _ = rk2.shape
    assert nk % 16 == 0
    ne = u.shape[0]
    te = apc * nk
    tok3 = lambda i, j: (0, 0, i)
    return pl.pallas_call(
        functools.partial(_peer_kernel, nh=nh, nk=nk, apc=apc, final=final),
        grid=(n // tm, ne // te),
        in_specs=[pl.BlockSpec((d, tm), lambda i, j: (0, i)),
                  pl.BlockSpec((te, d), lambda i, j: (j, 0)),
                  pl.BlockSpec((d, te), lambda i, j: (0, j)),
                  pl.BlockSpec((nh, nk, tm), tok3),
                  pl.BlockSpec((nh, nk, tm), tok3),
                  pl.BlockSpec((nh, nk, tm), tok3),
                  pl.BlockSpec((nh, nk, tm), tok3),
                  pl.BlockSpec((tm, d), lambda i, j: (i, 0)),
                  pl.BlockSpec((1, d), lambda i, j: (0, 0))],
        out_specs=pl.BlockSpec((tm, d), lambda i, j: (i, 0)),
        out_shape=jax.ShapeDtypeStruct((n, d), F32),
        scratch_shapes=[pltpu.VMEM((d, tm), F32), pltpu.VMEM((te, tm), BF16)],
        compiler_params=_cparams(("parallel", "arbitrary")), name="peer",
    )(ht, u, vt, rk2, e2, rr, c, xn, fg)


def _rope_tables(pos):
    half = HEAD_DIM // 2
    inv_freq = ROPE_THETA ** (-jnp.arange(half, dtype=F32) / half)
    ang = pos.astype(F32)[:, None] * inv_freq[None, :]
    cos = jnp.cos(ang)
    sin = jnp.sin(ang)
    return jnp.tile(jnp.concatenate([cos, cos], axis=1), (1, 2)), jnp.tile(jnp.concatenate([-sin, sin], axis=1), (1, 2))


def _pad_rows(x, rows):
    return jnp.pad(x, ((0, rows - x.shape[0]),) + ((0, 0),) * (x.ndim - 1))


def _token_tile(n, pref):
    return pref if n % pref == 0 else n


def _page_pool_weights(wk, wv):
    wset = jnp.stack([wk[:, 0], wk[:, 1], wv[:, 0], wv[:, 1]]).reshape(4, 2, CMP_STRIDE)
    per_pos = jnp.tile(wset, (1, 1, LANES // CMP_STRIDE))
    chunk_of_pos = (jnp.arange(LANES) // CMP_STRIDE)[None, :] == jnp.arange(LANES // CMP_STRIDE)[:, None]
    return (per_pos[:, :, None, :] * chunk_of_pos[None, None].astype(F32)).reshape(64, LANES)


def kernel(x_prompt, x_sample, cache_nsa_kv, state_nsa_win, state_mlstm_C, state_mlstm_n, state_mlstm_m, state_conv, page_table, norm1_g, norm2_g, final_norm_g, w_in, w_out, mlstm_b_i, mlstm_b_f, mlstm_norm_g, conv_w, conv_b, conv_norm_g, conv_norm_b, nsa_cmp_wk, nsa_cmp_wv, gmlp_norm_g, gmlp_norm_b, gmlp_ws, gmlp_bs, peer_wq, peer_k1, peer_k2, peer_u, peer_v):
    depth = w_in.shape[0]
    bp, s_len, d = x_prompt.shape
    bd, t_dec, _ = x_sample.shape
    n_pool = cache_nsa_kv.shape[1]
    npages = page_table.shape[1]
    past = npages * cache_nsa_kv.shape[2]
    n_win = state_nsa_win.shape[2]
    assert t_dec == 1 and cache_nsa_kv.shape[2] == LANES and past % SEL_BLOCK == 0
    assert s_len >= WINDOW + 128 and n_win == WINDOW
    np_tok = bp * s_len
    ns_pad = LANES
    assert bd <= ns_pad

    xp = x_prompt.reshape(np_tok, d)
    xs = _pad_rows(x_sample.reshape(bd, d), ns_pad)
    cos_p, sin_p = _rope_tables(jnp.arange(s_len))
    cos_s, sin_s = _rope_tables(jnp.full((ns_pad,), past))
    pool_t = cache_nsa_kv.reshape(depth * n_pool, LANES, 4 * LANES).transpose(0, 2, 1)
    win_t = state_nsa_win.reshape(depth * bd, n_win, 2 * LANES).transpose(0, 2, 1)
    eye_h = jnp.eye(N_HEADS, dtype=F32)

    tm_in = _token_tile(s_len, 256)
    tm_out = _token_tile(np_tok, 256)
    tm_peer = _token_tile(np_tok, 512)
    nk = peer_k1.shape[2]
    apc = min(16, nk)
    w_in_t = w_in.transpose(2, 0, 1)

    new_p, new_s = [], []
    for l in range(depth):
        wi = w_in_t[:, l, :]
        w_perm = jnp.concatenate([wi[0:1024], wi[1032:2568], wi[2580:3092]], axis=0).astype(BF16)
        w_small = jnp.concatenate([wi[1024:1032], wi[2568:2580], jnp.zeros((108, d), F32)], axis=0).astype(BF16)
        g1 = norm1_g[l].reshape(1, d)
        g2 = norm2_g[l].reshape(1, d)
        gate_bias = jnp.concatenate([mlstm_b_i[l], mlstm_b_f[l], jnp.zeros((LANES - 2 * N_HEADS,), F32)]).reshape(1, LANES)
        mng = mlstm_norm_g[l].reshape(1, MIX_W)
        cw = conv_w[l]
        cbias = conv_b[l].reshape(1, MIX_W)
        cg = conv_norm_g[l].reshape(1, MIX_W)
        cbeta = conv_norm_b[l].reshape(1, MIX_W)
        w32 = jnp.concatenate([jnp.repeat(nsa_cmp_wk[l], HEAD_DIM, axis=1), jnp.repeat(nsa_cmp_wv[l], HEAD_DIM, axis=1)], axis=1)
        wlo, whi = w32[:CMP_STRIDE], w32[CMP_STRIDE:]
        wall = _page_pool_weights(nsa_cmp_wk[l], nsa_cmp_wv[l])
        gg = gmlp_norm_g[l].reshape(1, MIX_W)
        gbeta = gmlp_norm_b[l].reshape(1, MIX_W)
        gws = gmlp_ws[l]
        gbsb = jnp.repeat(gmlp_bs[l].T, HEAD_DIM, axis=1)
        wo = w_out[l].astype(BF16)
        wq = jnp.stack(_split_bf16(peer_wq[l]))
        k1 = jnp.stack(_split_bf16(peer_k1[l]))
        k2 = jnp.stack(_split_bf16(peer_k2[l]))
        ub = peer_u[l].astype(BF16)
        vtb = peer_v[l].astype(BF16).T
        fg = final_norm_g.reshape(1, d)
        final = l == depth - 1

        oa, ob, oq, okv, owin, okaug, ovsb, owinb, od, osm = _in_proj(xp, g1, w_perm, w_small, cos_p, sin_p, tm_in)
        ya, c_p, n_p, m_p = _mlstm(oa.reshape(bp, s_len, 1024), osm.reshape(bp, s_len, LANES), gate_bias, mng,
                                   jnp.zeros((bp, MIX_W, MIX_W), F32), jnp.zeros((bp, 1, MIX_W), F32),
                                   jnp.zeros((bp, 1, MIX_W), F32), MLSTM_CHUNK, MLSTM_CHUNK, bp)
        yb, conv_p = _conv(ob.reshape(bp, s_len, 2 * MIX_W), jnp.zeros((bp, CONV_PAD, MIX_W), F32),
                           cw, cbias, cg, cbeta, 512, 512)
        cb = _cmp(okv.reshape(bp, s_len, 4 * LANES), wlo, whi)
        yc = _nsa_prompt(oq.reshape(bp, s_len, MIX_W), osm.reshape(bp, s_len, LANES), cb,
                         okaug.reshape(bp, s_len, 2 * LANES), ovsb.reshape(bp, s_len, LANES),
                         owinb.reshape(bp, s_len, 2 * LANES), 128)
        yd, _ = _gmlp(od, gg, gbeta, gws, gbsb, 4)
        xn, ht, s1t, s2t = _out_proj(xp, ya.reshape(np_tok, MIX_W), yb.reshape(np_tok, MIX_W),
                                     yc.reshape(np_tok, MIX_W), yd, wo, g2, wq, k1, k2, tm_out)
        rk2, e2, rr, cw8 = _gate(s1t, s2t, _token_tile(np_tok, 512))
        xp = _peer(ht, ub, vtb, rk2, e2, rr, cw8, xn, fg, tm_peer, apc, final)
        new_p.append((okv.reshape(bp, s_len, 4, NSA_KV, HEAD_DIM),
                      owin.reshape(bp, s_len, 2, NSA_KV, HEAD_DIM)[:, s_len - n_win:],
                      jnp.stack([c_p[:, h * HEAD_DIM:(h + 1) * HEAD_DIM, h * HEAD_DIM:(h + 1) * HEAD_DIM]
                                 for h in range(N_HEADS)], axis=1),
                      n_p.reshape(bp, N_HEADS, HEAD_DIM),
                      m_p[:, 0, ::HEAD_DIM],
                      conv_p[:, CONV_PAD - (CONV_W - 1):]))

        sa, sb, sq, skv, swin, _, _, _, sd, ssm = _in_proj(xs, g1, w_perm, w_small, cos_s, sin_s, ns_pad)
        rows8 = lambda t: jnp.pad(t[:bd, None, :], ((0, 0), (0, 7), (0, 0)))
        c0 = jnp.einsum('bhvk,hg->bhvgk', state_mlstm_C[l], eye_h).reshape(bd, MIX_W, MIX_W)
        n0 = state_mlstm_n[l].reshape(bd, 1, MIX_W)
        m0 = jnp.repeat(state_mlstm_m[l], HEAD_DIM, axis=-1).reshape(bd, 1, MIX_W)
        bb_s = 2 if bd % 2 == 0 else 1
        ya_s, c_s, n_s, m_s = _mlstm(rows8(sa), rows8(ssm), gate_bias, mng, c0, n0, m0, 8, 1, bb_s)
        prefix = jnp.pad(state_conv[l], ((0, 0), (CONV_PAD - (CONV_W - 1), 0), (0, 0)))
        yb_s, conv_s = _conv(rows8(sb), prefix, cw, cbias, cg, cbeta, 8, 1)
        zd_s = jnp.pad(sd[:bd, None, :], ((0, 0), (0, GMLP_CHUNK - 1), (0, 0))).reshape(bd * GMLP_CHUNK, 2 * MIX_W)
        yd_s, v_s = _gmlp(zd_s, gg, gbeta, gws, gbsb, 1)
        q_heads = sq[:bd].reshape(bd, NSA_KV, 2, 1, HEAD_DIM)
        q4 = (q_heads * jnp.eye(NSA_KV, dtype=F32)[None, :, None, :, None]).reshape(bd, N_HEADS, LANES)
        q4 = jnp.pad(q4, ((0, 0), (0, 8 - N_HEADS), (0, 0)))
        gq = jnp.pad(ssm[:bd, 2 * N_HEADS:2 * N_HEADS + 3 * N_HEADS].reshape(bd, N_HEADS, 3),
                     ((0, 0), (0, 8 - N_HEADS), (0, LANES - 3)))
        assert npages % 4 == 0
        pp = next(c for c in (32, 16, 8, 4) if npages % c == 0)
        o8, win_s = _nsa_sample(page_table + l * n_pool, q4, gq, skv[:bd, None, :], swin[:bd, None, :],
                                swin[:bd, :, None], win_t, l * bd, wall, pool_t, pp)
        yc_s = jnp.concatenate([o8[:, h, (h // 2) * HEAD_DIM:(h // 2 + 1) * HEAD_DIM] for h in range(N_HEADS)], axis=1)
        xn_s, ht_s, s1t_s, s2t_s = _out_proj(xs, _pad_rows(ya_s[:, 0], ns_pad), _pad_rows(yb_s[:, 0], ns_pad),
                                             _pad_rows(yc_s, ns_pad),
                                             _pad_rows(yd_s.reshape(bd, GMLP_CHUNK, MIX_W)[:, 0], ns_pad),
                                             wo, g2, wq, k1, k2, ns_pad)
        rk2_s, e2_s, rr_s, cw8_s = _gate(s1t_s, s2t_s, ns_pad)
        xs = _peer(ht_s, ub, vtb, rk2_s, e2_s, rr_s, cw8_s, xn_s, fg, ns_pad, apc, final)
        new_s.append((skv[:bd].reshape(bd, 1, 4, NSA_KV, HEAD_DIM),
                      win_s.transpose(0, 2, 1).reshape(bd, n_win, 2, NSA_KV, HEAD_DIM),
                      jnp.stack([c_s[:, h * HEAD_DIM:(h + 1) * HEAD_DIM, h * HEAD_DIM:(h + 1) * HEAD_DIM]
                                 for h in range(N_HEADS)], axis=1),
                      n_s.reshape(bd, N_HEADS, HEAD_DIM),
                      m_s[:, 0, ::HEAD_DIM],
                      conv_s[:, CONV_PAD - (CONV_W - 1):],
                      v_s.reshape(bd, GMLP_CHUNK, MIX_W)[:, 0:1]))

    def stack(states, i):
        return jnp.stack([st[i] for st in states])

    y_prompt = xp.reshape(bp, s_len, d)
    y_sample = xs[:bd].reshape(bd, 1, d)
    return (y_prompt, y_sample,
            stack(new_p, 0), stack(new_s, 0),
            stack(new_p, 1), stack(new_s, 1),
            stack(new_p, 2), stack(new_s, 2),
            stack(new_p, 3), stack(new_s, 3),
            stack(new_p, 4), stack(new_s, 4),
            stack(new_p, 5), stack(new_s, 5),
            stack(new_s, 6))
```

```python
import functools

import jax
import jax.numpy as jnp
from jax import lax
from jax.experimental import pallas as pl
from jax.experimental.pallas import tpu as pltpu

F32 = jnp.float32
BF16 = jnp.bfloat16
HIGHEST = lax.Precision.HIGHEST

HEAD_DIM = 64
N_HEADS = 4
MIX_W = 256
NSA_KV = 2
NORM_EPS = 1e-6
NEG = -1e30
MLSTM_CHUNK = 128
CONV_W = 31
CONV_PAD = 32
CMP_STRIDE = 16
CMP_LEN = 32
SEL_BLOCK = 64
N_SELECT = 16
SEL_FORCE = 1e9
WINDOW = 512
ROPE_THETA = 10000.0
ATTN_SCALE = HEAD_DIM ** -0.5
GMLP_CHUNK = 128
PEER_TOPK = 16
RANK_MARK = 2.0 ** 100
PEER_UP_ROWS = 4
LANES = 128
VMEM_LIMIT = 48 * 1024 * 1024

NT = (((1,), (1,)), ((), ()))
TN = (((0,), (0,)), ((), ()))


def _cparams(sem):
    return pltpu.CompilerParams(dimension_semantics=sem, vmem_limit_bytes=VMEM_LIMIT)


def _iota(shape, dim):
    return lax.broadcasted_iota(jnp.int32, shape, dim)


def _sigmoid(x):
    return 1.0 / (1.0 + jnp.exp(-x))


def _gelu(x):
    c1 = -2.0 * 0.7978845608028654 * 1.4426950408889634
    c2 = c1 * 0.044715
    return x / (1.0 + jnp.exp2(x * (c1 + c2 * (x * x))))


def _dot(a, b, dims=None, precision=None):
    if dims is None:
        return jnp.dot(a, b, preferred_element_type=F32, precision=precision)
    return lax.dot_general(a, b, dims, preferred_element_type=F32, precision=precision)


def _block_ones(n, blk):
    return (_iota((n, n), 0) // blk == _iota((n, n), 1) // blk).astype(F32)


def _group_layernorm(x, ee, width):
    mu = _dot(x, ee, precision=HIGHEST) * (1.0 / width)
    d = x - mu
    var = _dot(d * d, ee, precision=HIGHEST) * (1.0 / width)
    return d * lax.rsqrt(var + NORM_EPS)


def _rmsnorm(x, g):
    return x * lax.rsqrt(jnp.mean(x * x, axis=-1, keepdims=True) + NORM_EPS) * g


def _masked_softmax(s, valid):
    sm = jnp.where(valid, s, NEG)
    p = jnp.where(valid, jnp.exp(sm - jnp.max(sm, axis=-1, keepdims=True)), 0.0)
    return p / jnp.maximum(jnp.sum(p, axis=-1, keepdims=True), 1e-30)


def _rope(x, cos, sin_signed):
    w = x.shape[1]
    fwd = pltpu.roll(x, w - HEAD_DIM // 2, 1)
    bwd = pltpu.roll(x, HEAD_DIM // 2, 1)
    first = (_iota(x.shape, 1) % HEAD_DIM) < HEAD_DIM // 2
    return x * cos + jnp.where(first, fwd, bwd) * sin_signed


def _in_kernel(x_ref, g_ref, w_ref, wsmall_ref, cos_ref, sin_ref,
               oa_ref, ob_ref, oq_ref, okv_ref, owin_ref, okaug_ref, ovsb_ref, owinb_ref, od_ref, os_ref, *, npb):
    tm = x_ref.shape[0]
    y = _rmsnorm(x_ref[...], g_ref[...])
    yb = y.astype(BF16)
    z = _dot(yb, w_ref[...], NT)
    os_ref[...] = _dot(yb, wsmall_ref[...], NT)
    oa_ref[...] = z[:, 0:1024]
    ob_ref[...] = z[:, 1024:1536]
    cos = cos_ref[...]
    sin = sin_ref[...]
    cos2 = jnp.concatenate([cos, cos], axis=1)
    sin2 = jnp.concatenate([sin, sin], axis=1)
    oq_ref[...] = _rope(z[:, 1536:1792], cos2, sin2) * ATTN_SCALE
    kc = _rope(z[:, 1792:1920], cos, sin)
    vc = z[:, 1920:2048]
    ks = _rope(z[:, 2048:2176], cos, sin)
    vs = z[:, 2176:2304]
    kw = _rope(z[:, 2304:2432], cos, sin)
    vw = z[:, 2432:2560]
    okv_ref[...] = jnp.concatenate([kc, vc, ks, vs], axis=1)
    owin = jnp.concatenate([kw, vw], axis=1)
    owin_ref[...] = owin
    pos = (pl.program_id(0) % npb) * tm + _iota((tm, LANES), 0)
    onehot = jnp.where(_iota((tm, LANES), 1) == pos // SEL_BLOCK, 1.0, 0.0)
    okaug_ref[...] = jnp.concatenate([ks, onehot], axis=1).astype(BF16)
    ovsb_ref[...] = vs.astype(BF16)
    owinb_ref[...] = owin.astype(BF16)
    od_ref[...] = z[:, 2560:3072]


def _in_proj(x, g, w, wsmall, cos_t, sin_t, tm):
    n, d = x.shape
    npb = cos_t.shape[0] // tm
    widths = (1024, 512, 256, 512, 256, 256, 128, 256, 512, 128)
    dtypes = (F32, F32, F32, F32, F32, BF16, BF16, BF16, F32, F32)
    row = lambda i: (i, 0)
    return pl.pallas_call(
        functools.partial(_in_kernel, npb=npb),
        grid=(n // tm,),
        in_specs=[pl.BlockSpec((tm, d), row),
                  pl.BlockSpec((1, d), lambda i: (0, 0)),
                  pl.BlockSpec(w.shape, lambda i: (0, 0)),
                  pl.BlockSpec(wsmall.shape, lambda i: (0, 0)),
                  pl.BlockSpec((tm, LANES), lambda i: (i % npb, 0)),
                  pl.BlockSpec((tm, LANES), lambda i: (i % npb, 0))],
        out_specs=[pl.BlockSpec((tm, wd), row) for wd in widths],
        out_shape=[jax.ShapeDtypeStruct((n, wd), dt) for wd, dt in zip(widths, dtypes)],
        compiler_params=_cparams(("parallel",)), name="in_proj",
    )(x, g, w, wsmall, cos_t, sin_t)


def _mlstm_kernel(za_ref, zs_ref, bias_ref, ng_ref, c0_ref, n0_ref, m0_ref,
                  y_ref, c_ref, n_ref, m_ref, *, L, t_valid, bb):
    @pl.when(pl.program_id(1) == 0)
    def _():
        c_ref[...] = c0_ref[...]
        n_ref[...] = n0_ref[...]
        m_ref[...] = m0_ref[...]

    head_of_lane = _iota((1, MIX_W), 1) // HEAD_DIM
    causal = _iota((L, L), 0) >= _iota((L, L), 1)
    tril = causal.astype(F32)
    ee = _block_ones(MIX_W, HEAD_DIM)
    bdiag = ee > 0.5
    row128 = _iota((L, LANES), 0)
    lane128 = _iota((L, LANES), 1)
    for bi in range(bb):
        za = za_ref[bi]
        q = za[:, 0:256]
        k = za[:, 256:512] * (HEAD_DIM ** -0.5)
        v = za[:, 512:768]
        o = za[:, 768:1024]
        gi = zs_ref[bi] + bias_ref[...]
        ls = jnp.minimum(gi, 0.0) - jnp.log(1.0 + jnp.exp(-jnp.abs(gi)))
        if t_valid < L:
            live = row128 < t_valid
            ig = jnp.where(live, gi, NEG)
            ls = jnp.where(live, ls, 0.0)
        else:
            ig = gi
        gmat = jnp.where(lane128 < N_HEADS, ig, 0.0)
        lfm = jnp.where((lane128 >= N_HEADS) & (lane128 < 2 * N_HEADS), ls, 0.0)
        fc = _dot(tril, lfm, precision=HIGHEST)
        g_t = gmat.T
        f_t = fc.T
        mrow = m_ref[bi]
        nrow = n_ref[bi]
        cm = c_ref[bi]
        qb = q.astype(BF16)
        kb = k.astype(BF16)
        vb = v.astype(BF16)
        num = jnp.zeros((L, MIX_W), F32)
        decay_b = jnp.zeros((L, MIX_W), F32)
        dens_b = jnp.zeros((L, MIX_W), F32)
        mt_b = jnp.zeros((L, MIX_W), F32)
        w_b = jnp.zeros((L, MIX_W), F32)
        cd_b = jnp.zeros((1, MIX_W), F32)
        mnew_b = jnp.zeros((1, MIX_W), F32)
        for h in range(N_HEADS):
            hm = head_of_lane == h
            f_col = fc[:, N_HEADS + h:N_HEADS + h + 1]
            ig_col = gmat[:, h:h + 1]
            f_row = f_t[N_HEADS + h:N_HEADS + h + 1, :]
            ig_row = g_t[h:h + 1, :]
            mp = mrow[:, h * HEAD_DIM:h * HEAD_DIM + 1]
            dmat = jnp.where(causal, (f_col - f_row) + ig_row, NEG)
            m_inter = mp + f_col
            m_t = jnp.maximum(m_inter, jnp.max(dmat, axis=1, keepdims=True))
            d_exp = jnp.exp(dmat - m_t)
            decay = jnp.exp(m_inter - m_t)
            qh = jnp.where(hm, q, 0.0).astype(BF16)
            s = _dot(qh, kb, NT) * d_exp
            num = jnp.where(hm, _dot(s.astype(BF16), vb), num)
            dens_b = jnp.where(hm, jnp.sum(s, axis=1, keepdims=True), dens_b)
            decay_b = jnp.where(hm, decay, decay_b)
            mt_b = jnp.where(hm, m_t, mt_b)
            f_last = f_col[L - 1:L, :]
            m_new = m_t[L - 1:L, :]
            w_b = jnp.where(hm, jnp.exp((f_last - f_col) + ig_col - m_new), w_b)
            cd_b = jnp.where(hm, jnp.exp(mp + f_last - m_new), cd_b)
            mnew_b = jnp.where(hm, m_new, mnew_b)
        inter = _dot(qb, cm.astype(BF16), NT)
        nq_b = _dot(q * nrow, ee, precision=HIGHEST)
        hnum = num + decay_b * inter
        den_b = dens_b + decay_b * nq_b
        hh = hnum / jnp.maximum(jnp.abs(den_b), jnp.exp(-mt_b))
        hn = _group_layernorm(hh, ee, HEAD_DIM) * ng_ref[...]
        y_ref[bi] = hn * _sigmoid(o)
        kw = k * w_b
        upd = _dot(vb, kw.astype(BF16), TN)
        c_ref[bi] = jnp.where(bdiag, cd_b * cm + upd, 0.0)
        n_ref[bi] = cd_b * nrow + jnp.sum(kw, axis=0, keepdims=True)
        m_ref[bi] = mnew_b


def _mlstm(za, zs, bias, ng, c0, n0, m0, L, t_valid, bb):
    b, t, _ = za.shape
    nc = t // L
    st = lambda i, c: (i, 0, 0)
    return pl.pallas_call(
        functools.partial(_mlstm_kernel, L=L, t_valid=t_valid, bb=bb),
        grid=(b // bb, nc),
        in_specs=[pl.BlockSpec((bb, L, 1024), lambda i, c: (i, c, 0)),
                  pl.BlockSpec((bb, L, LANES), lambda i, c: (i, c, 0)),
                  pl.BlockSpec((1, LANES), lambda i, c: (0, 0)),
                  pl.BlockSpec((1, MIX_W), lambda i, c: (0, 0)),
                  pl.BlockSpec((bb, MIX_W, MIX_W), st),
                  pl.BlockSpec((bb, 1, MIX_W), st),
                  pl.BlockSpec((bb, 1, MIX_W), st)],
        out_specs=[pl.BlockSpec((bb, L, MIX_W), lambda i, c: (i, c, 0)),
                   pl.BlockSpec((bb, MIX_W, MIX_W), st),
                   pl.BlockSpec((bb, 1, MIX_W), st),
                   pl.BlockSpec((bb, 1, MIX_W), st)],
        out_shape=[jax.ShapeDtypeStruct((b, t, MIX_W), F32),
                   jax.ShapeDtypeStruct((b, MIX_W, MIX_W), F32),
                   jax.ShapeDtypeStruct((b, 1, MIX_W), F32),
                   jax.ShapeDtypeStruct((b, 1, MIX_W), F32)],
        compiler_params=_cparams(("parallel", "arbitrary")), name="mlstm",
    )(za, zs, bias, ng, c0, n0, m0)


def _conv_kernel(zb_ref, pre_ref, w_ref, cb_ref, g_ref, beta_ref, y_ref, st_ref, ext_ref, *, tt, tv):
    @pl.when(pl.program_id(1) == 0)
    def _():
        ext_ref[0:CONV_PAD, :] = pre_ref[0]

    z = zb_ref[0]
    u = z[:, :MIX_W] * _sigmoid(z[:, MIX_W:])
    ext_ref[CONV_PAD:CONV_PAD + tt, :] = u
    off = CONV_PAD - (CONV_W - 1)
    acc = jnp.zeros((tt, MIX_W), F32)
    for j in range(CONV_W):
        acc = acc + w_ref[j:j + 1, :] * ext_ref[off + j:off + j + tt, :]
    ee = _block_ones(MIX_W, HEAD_DIM)
    y = _group_layernorm(acc + cb_ref[...], ee, HEAD_DIM) * g_ref[...] + beta_ref[...]
    y_ref[0] = y * _sigmoid(y)
    st_ref[0] = ext_ref[tv:tv + CONV_PAD, :]
    ext_ref[0:CONV_PAD, :] = ext_ref[tt:tt + CONV_PAD, :]


def _conv(zb, prefix, w, cb, g, beta, tt, tv):
    b, t, _ = zb.shape
    vec = lambda i, j: (0, 0)
    return pl.pallas_call(
        functools.partial(_conv_kernel, tt=tt, tv=tv),
        grid=(b, t // tt),
        in_specs=[pl.BlockSpec((1, tt, 2 * MIX_W), lambda i, j: (i, j, 0)),
                  pl.BlockSpec((1, CONV_PAD, MIX_W), lambda i, j: (i, 0, 0)),
                  pl.BlockSpec((CONV_W, MIX_W), vec),
                  pl.BlockSpec((1, MIX_W), vec),
                  pl.BlockSpec((1, MIX_W), vec),
                  pl.BlockSpec((1, MIX_W), vec)],
        out_specs=[pl.BlockSpec((1, tt, MIX_W), lambda i, j: (i, j, 0)),
                   pl.BlockSpec((1, CONV_PAD, MIX_W), lambda i, j: (i, 0, 0))],
        out_shape=[jax.ShapeDtypeStruct((b, t, MIX_W), F32),
                   jax.ShapeDtypeStruct((b, CONV_PAD, MIX_W), F32)],
        scratch_shapes=[pltpu.VMEM((CONV_PAD + tt, MIX_W), F32)],
        compiler_params=_cparams(("parallel", "arbitrary")), name="conv",
    )(zb, prefix, w, cb, g, beta)


def _gmlp_kernel(zd_ref, g_ref, beta_ref, ws_ref, bsb_ref, y_ref, v_ref, *, cpb):
    z = zd_ref[...]
    ee = _block_ones(MIX_W, HEAD_DIM)
    u = _gelu(z[:, :MIX_W])
    vv = _group_layernorm(_gelu(z[:, MIX_W:]), ee, HEAD_DIM) * g_ref[...] + beta_ref[...]
    v_ref[...] = vv
    tril = _iota((GMLP_CHUNK, GMLP_CHUNK), 0) >= _iota((GMLP_CHUNK, GMLP_CHUNK), 1)
    head_of_lane = _iota((1, MIX_W), 1) // HEAD_DIM
    wm = [jnp.where(tril, ws_ref[h], 0.0).astype(BF16) for h in range(N_HEADS)]
    for c in range(cpb):
        rows = slice(c * GMLP_CHUNK, (c + 1) * GMLP_CHUNK)
        vc = vv[rows].astype(BF16)
        mixed = jnp.zeros((GMLP_CHUNK, MIX_W), F32)
        for h in range(N_HEADS):
            mixed = jnp.where(head_of_lane == h, _dot(wm[h], vc), mixed)
        y_ref[rows, :] = u[rows] * (mixed + bsb_ref[...])


def _gmlp(zd, g, beta, ws, bsb, cpb):
    n = zd.shape[0]
    rows = cpb * GMLP_CHUNK
    vec = lambda i: (0, 0)
    return pl.pallas_call(
        functools.partial(_gmlp_kernel, cpb=cpb),
        grid=(n // rows,),
        in_specs=[pl.BlockSpec((rows, 2 * MIX_W), lambda i: (i, 0)),
                  pl.BlockSpec((1, MIX_W), vec),
                  pl.BlockSpec((1, MIX_W), vec),
                  pl.BlockSpec((N_HEADS, GMLP_CHUNK, GMLP_CHUNK), lambda i: (0, 0, 0)),
                  pl.BlockSpec((GMLP_CHUNK, MIX_W), vec)],
        out_specs=[pl.BlockSpec((rows, MIX_W), lambda i: (i, 0)),
                   pl.BlockSpec((rows, MIX_W), lambda i: (i, 0))],
        out_shape=[jax.ShapeDtypeStruct((n, MIX_W), F32),
                   jax.ShapeDtypeStruct((n, MIX_W), F32)],
        compiler_params=_cparams(("parallel",)), name="gmlp",
    )(zd, g, beta, ws, bsb)


def _combine_lo_hi(lo, hi):
    r = lo.shape[0]
    nxt = pltpu.roll(hi, r - 1, 0)
    return lo + jnp.where(_iota(lo.shape, 0) < r - 1, nxt, 0.0)


def _cmp_kernel(kv_ref, wlo_ref, whi_ref, o_ref):
    x = kv_ref[0]
    x3 = x.reshape(x.shape[0] // CMP_STRIDE, CMP_STRIDE, x.shape[1])
    lo = jnp.sum(x3 * wlo_ref[...][None], axis=1)
    hi = jnp.sum(x3 * whi_ref[...][None], axis=1)
    o_ref[0] = _combine_lo_hi(lo, hi)


def _cmp(okv3, wlo, whi):
    b, s, _ = okv3.shape
    return pl.pallas_call(
        _cmp_kernel,
        grid=(b,),
        in_specs=[pl.BlockSpec((1, s, 2 * LANES), lambda i: (i, 0, 0)),
                  pl.BlockSpec((CMP_STRIDE, 2 * LANES), lambda i: (0, 0)),
                  pl.BlockSpec((CMP_STRIDE, 2 * LANES), lambda i: (0, 0))],
        out_specs=pl.BlockSpec((1, s // CMP_STRIDE, 2 * LANES), lambda i: (i, 0, 0)),
        out_shape=jax.ShapeDtypeStruct((b, s // CMP_STRIDE, 2 * LANES), F32),
        compiler_params=_cparams(("parallel",)), name="nsa_cmp",
    )(okv3, wlo, whi)


def _nsa_prompt_kernel(q_ref, gs_ref, cb_ref, ka_ref, vs_ref, win_ref, y_ref, s_buf, *, tq, s_len, n_top, tk):
    start = pl.program_id(1) * tq
    nc = s_len // CMP_STRIDE
    ns = s_len // SEL_BLOCK
    q = q_ref[0]
    gl = _sigmoid(gs_ref[0])
    lo_half = _iota((tq, LANES), 1) < HEAD_DIM
    q_rows = []
    for g in range(NSA_KV):
        qpair = q[:, g * LANES:(g + 1) * LANES]
        swapped = pltpu.roll(qpair, HEAD_DIM, 1)
        if g == 0:
            q_rows += [jnp.where(lo_half, qpair, 0.0), jnp.where(lo_half, swapped, 0.0)]
        else:
            q_rows += [jnp.where(lo_half, 0.0, swapped), jnp.where(lo_half, 0.0, qpair)]
    q128 = jnp.concatenate(q_rows, axis=0)
    q128b = q128.astype(BF16)
    trow = start + _iota((tq, 1), 0)
    t2 = jnp.concatenate([trow, trow], axis=0)
    t4 = jnp.concatenate([t2, t2], axis=0)
    cb = cb_ref[0]
    sc = _dot(q128b, cb[:, 0:LANES].astype(BF16), NT)
    validc = (_iota((1, nc), 1) * CMP_STRIDE + CMP_LEN) <= (t4 + 1)
    p = _masked_softmax(sc, validc)
    o_c = _dot(p.astype(BF16), cb[:, LANES:2 * LANES].astype(BF16))
    psum = jnp.concatenate([p[0:tq] + p[tq:2 * tq], p[2 * tq:3 * tq] + p[3 * tq:4 * tq]], axis=0)
    pool4 = (_iota((nc, ns), 0) // (SEL_BLOCK // CMP_STRIDE) == _iota((nc, ns), 1)).astype(F32)
    imp = _dot(psum, pool4, precision=HIGHEST)
    jidx = _iota((1, ns), 1)
    imp = jnp.where(jidx == t2 // SEL_BLOCK, SEL_FORCE, imp)
    imp = jnp.where(jidx * SEL_BLOCK <= t2, imp, -1.0)
    if n_top < ns:
        x = jnp.concatenate([imp[0:tq].T, imp[tq:2 * tq].T], axis=1)
        jrow = _iota((ns, 2 * tq), 0).astype(F32)
        for _ in range(n_top):
            m = jnp.max(x, axis=0, keepdims=True)
            first = jnp.min(jnp.where(x == m, jrow, float(ns)), axis=0, keepdims=True)
            x = jnp.where(jrow == first, -3.0, x)
        sel_t = jnp.where(x == -3.0, 1.0, 0.0)
        sel = jnp.concatenate([sel_t[:, 0:tq].T, sel_t[:, tq:2 * tq].T], axis=0)
    else:
        sel = jnp.ones((2 * tq, ns), F32)
    selneg = (sel - 1.0) * (-NEG)
    if ns < LANES:
        selneg = jnp.concatenate([selneg, jnp.zeros((2 * tq, LANES - ns), F32)], axis=1)
    selneg4 = jnp.concatenate([selneg[0:tq], selneg[0:tq], selneg[tq:2 * tq], selneg[tq:2 * tq]], axis=0)
    qaug = jnp.concatenate([q128, selneg4], axis=1).astype(BF16)

    def scores(c):
        return _dot(qaug, ka_ref[0, pl.ds(pl.multiple_of(c * tk, tk), tk), :], NT)

    def absorb(c, s, carry):
        m, l, acc = carry
        m_new = jnp.maximum(m, jnp.max(s, axis=1, keepdims=True))
        pt = jnp.exp(s - m_new)
        alpha = jnp.exp(m - m_new)
        l = alpha * l + jnp.sum(pt, axis=1, keepdims=True)
        acc = alpha * acc + _dot(pt.astype(BF16), vs_ref[0, pl.ds(pl.multiple_of(c * tk, tk), tk), :])
        return m_new, l, acc

    def tile(c, carry):
        s = s_buf[...]
        s_buf[...] = scores(c + 1)
        return absorb(c, s, carry)

    n_tiles = (start + tq + tk - 1) // tk
    carry = (jnp.full((4 * tq, 1), NEG, F32), jnp.zeros((4 * tq, 1), F32), jnp.zeros((4 * tq, LANES), F32))
    s_buf[...] = scores(0)
    carry = lax.fori_loop(0, n_tiles - 1, tile, carry)
    last = n_tiles - 1
    s_diag = jnp.where(last * tk + _iota((1, tk), 1) <= t4, s_buf[...], NEG)
    _, l, acc = absorb(last, s_diag, carry)
    o_s = acc / jnp.maximum(l, 1e-30)
    wl = WINDOW + tq
    w0 = pl.multiple_of(jnp.maximum(start - WINDOW, 0), tq)
    sw = _dot(q128b, win_ref[0, pl.ds(w0, wl), 0:LANES], NT)
    kposw = w0 + _iota((1, wl), 1)
    validw = (kposw <= t4) & (kposw > t4 - WINDOW)
    o_w = _dot(_masked_softmax(sw, validw).astype(BF16), win_ref[0, pl.ds(w0, wl), LANES:2 * LANES])
    heads = []
    for hh in range(N_HEADS):
        col = 2 * N_HEADS + hh * 3
        rows = slice(hh * tq, (hh + 1) * tq)
        heads.append(o_c[rows] * gl[:, col:col + 1] + o_s[rows] * gl[:, col + 1:col + 2]
                     + o_w[rows] * gl[:, col + 2:col + 3])
    y_ref[0] = jnp.concatenate([jnp.where(lo_half, heads[0], pltpu.roll(heads[1], HEAD_DIM, 1)),
                                jnp.where(lo_half, pltpu.roll(heads[2], HEAD_DIM, 1), heads[3])], axis=1)


def _nsa_prompt(oq3, os3, cb, okaug3, ovsb3, owinb3, tq):
    b, s, _ = oq3.shape
    ns = s // SEL_BLOCK
    assert ns <= LANES
    n_top = min(N_SELECT, ns)
    tk = min(512, s)
    full = lambda i, j: (i, 0, 0)
    return pl.pallas_call(
        functools.partial(_nsa_prompt_kernel, tq=tq, s_len=s, n_top=n_top, tk=tk),
        grid=(b, s // tq),
        in_specs=[pl.BlockSpec((1, tq, MIX_W), lambda i, j: (i, j, 0)),
                  pl.BlockSpec((1, tq, LANES), lambda i, j: (i, j, 0)),
                  pl.BlockSpec((1, s // CMP_STRIDE, 2 * LANES), full),
                  pl.BlockSpec((1, s, 2 * LANES), full),
                  pl.BlockSpec((1, s, LANES), full),
                  pl.BlockSpec((1, s, 2 * LANES), full)],
        out_specs=pl.BlockSpec((1, tq, MIX_W), lambda i, j: (i, j, 0)),
        out_shape=jax.ShapeDtypeStruct((b, s, MIX_W), F32),
        scratch_shapes=[pltpu.VMEM((N_HEADS * tq, tk), F32)],
        compiler_params=_cparams(("parallel", "arbitrary")), name="nsa_prompt",
    )(oq3, os3, cb, okaug3, ovsb3, owinb3)


def _nsa_sample_kernel(pt_ref, q4_ref, gq_ref, newkv_ref, newwin_ref, newcol_ref, win_ref, wall_ref, *rest,
                       pp, npages, n_top):
    pool_refs = rest[:pp]
    o_ref, wo_ref, lo_s, hi_s, m_s, l_s, a_s, a_stage = rest[pp:]
    i = pl.program_id(1)
    nb = 2 * npages
    nbp = m_s.shape[1]
    ncp = npages * (LANES // CMP_STRIDE)
    past = npages * LANES
    q4 = q4_ref[0]
    q4b = q4.astype(BF16)

    @pl.when(i == 0)
    def _():
        m_s[...] = jnp.zeros(m_s.shape, F32)
        l_s[...] = jnp.zeros(l_s.shape, F32)
        if nb < nbp:
            a_s[...] = jnp.zeros(a_s.shape, F32)

    lane8 = _iota((8, LANES), 1)
    lanej = _iota((8, nbp), 1)
    set_of_lane = _iota((8, 2 * LANES), 1) // HEAD_DIM
    bpp = LANES // SEL_BLOCK
    wall = wall_ref[...].astype(BF16)
    m_acc = m_s[...]
    l_acc = l_s[...]
    los, his = [], []
    for kk in range(pp):
        page = pool_refs[kk][0]
        pooled = _dot(wall, page[0:2 * LANES, :].astype(BF16), NT)
        lo8 = jnp.zeros((8, 2 * LANES), F32)
        hi8 = jnp.zeros((8, 2 * LANES), F32)
        for st in range(4):
            lo8 = jnp.where(set_of_lane == st, pooled[st * 16:st * 16 + 8], lo8)
            hi8 = jnp.where(set_of_lane == st, pooled[st * 16 + 8:st * 16 + 16], hi8)
        los.append(lo8)
        his.append(hi8)
        ks_t = page[2 * LANES:3 * LANES, :].astype(BF16)
        vs_t = page[3 * LANES:4 * LANES, :].astype(BF16)
        s = _dot(q4b, ks_t)
        probs = []
        for jb in range(bpp):
            inb = (lane8 >= jb * SEL_BLOCK) & (lane8 < (jb + 1) * SEL_BLOCK)
            sm = jnp.where(inb, s, NEG)
            m = jnp.max(sm, axis=1, keepdims=True)
            p = jnp.where(inb, jnp.exp(sm - m), 0.0)
            probs.append(p)
            j = i * (pp * bpp) + kk * bpp + jb
            m_acc = jnp.where(lanej == j, m, m_acc)
            l_acc = jnp.where(lanej == j, jnp.sum(p, axis=1, keepdims=True), l_acc)
        a = _dot(jnp.concatenate(probs, axis=0).astype(BF16), vs_t, NT)
        for jb in range(bpp):
            for hh in range(N_HEADS):
                a_stage[hh, kk * bpp + jb:kk * bpp + jb + 1, :] = a[jb * 8 + hh:jb * 8 + hh + 1, :]
    m_s[...] = m_acc
    l_s[...] = l_acc
    r0 = pl.multiple_of(i * (pp * 8), pp * 8)
    lo_s[pl.ds(r0, pp * 8), :] = jnp.concatenate(los, axis=0)
    hi_s[pl.ds(r0, pp * 8), :] = jnp.concatenate(his, axis=0)
    b0 = pl.multiple_of(i * (pp * bpp), pp * bpp)
    for hh in range(N_HEADS):
        a_s[hh, pl.ds(b0, pp * bpp), :] = a_stage[hh]

    @pl.when(i == pl.num_programs(1) - 1)
    def _():
        row8 = _iota((8, 1), 0)
        cbm = _combine_lo_hi(lo_s[...], hi_s[...])
        kcb = cbm[:, 0:LANES].astype(BF16)
        vcb = cbm[:, LANES:2 * LANES].astype(BF16)
        sc = _dot(q4b, kcb, NT)
        validc = (_iota((1, ncp), 1) * CMP_STRIDE + CMP_LEN) <= past + 1
        p = _masked_softmax(sc, validc)
        o_c = _dot(p.astype(BF16), vcb)
        pg = jnp.where(row8 == 0, p[0:1] + p[1:2], jnp.where(row8 == 1, p[2:3] + p[3:4], 0.0))
        pool4 = (_iota((ncp, nbp), 0) // (SEL_BLOCK // CMP_STRIDE) == _iota((ncp, nbp), 1)).astype(F32)
        imp2 = _dot(pg, pool4, precision=HIGHEST)
        ii = _iota((nbp, nbp), 0)
        jj = _iota((nbp, nbp), 1)
        sels = []
        for g in range(NSA_KV):
            mx = jnp.broadcast_to(imp2[g:g + 1, :], (nbp, nbp))
            mt = mx.T
            beats = ((mt > mx) | ((mt == mx) & (ii < jj))) & (ii < nb)
            rank = jnp.sum(jnp.where(beats, 1.0, 0.0), axis=0, keepdims=True)
            sels.append(jnp.where((rank < n_top - 1) & (_iota((1, nbp), 1) < nb), 1.0, 0.0))
        sel8 = jnp.where(row8 < 2, sels[0], sels[1]) > 0.5
        newkv = newkv_ref[0]
        ksn = newkv[:, 2 * LANES:3 * LANES]
        vsn = newkv[:, 3 * LANES:4 * LANES]
        s_new = jnp.sum(q4 * ksn, axis=1, keepdims=True)
        mrow = m_s[...]
        m_all = jnp.maximum(jnp.max(jnp.where(sel8, mrow, NEG), axis=1, keepdims=True), s_new)
        wj = jnp.where(sel8, jnp.exp(mrow - m_all), 0.0)
        w_new = jnp.exp(s_new - m_all)
        ltot = jnp.sum(wj * l_s[...], axis=1, keepdims=True) + w_new
        osum = w_new * vsn
        for hh in range(N_HEADS):
            osum = osum + jnp.where(row8 == hh, _dot(wj, a_s[hh], precision=HIGHEST), 0.0)
        o_s = osum / jnp.maximum(ltot, 1e-30)
        win_t = win_ref[0]
        nw = win_t.shape[1]
        lane_w = _iota((1, nw), 1)
        sw = _dot(q4b, win_t[0:LANES, :].astype(BF16))
        validw = (past - nw + lane_w) > past - WINDOW
        neww = newwin_ref[0]
        sw_new = jnp.sum(q4 * neww[:, 0:LANES], axis=1, keepdims=True)
        mw = jnp.maximum(jnp.max(jnp.where(validw, sw, NEG), axis=1, keepdims=True), sw_new)
        pw = jnp.where(validw, jnp.exp(sw - mw), 0.0)
        pn = jnp.exp(sw_new - mw)
        zw = jnp.sum(pw, axis=1, keepdims=True) + pn
        o_w = (_dot(pw.astype(BF16), win_t[LANES:2 * LANES, :].astype(BF16), NT) + pn * neww[:, LANES:2 * LANES]) / zw
        gg = _sigmoid(gq_ref[0])
        o_ref[0] = o_c * gg[:, 0:1] + o_s * gg[:, 1:2] + o_w * gg[:, 2:3]
        wo_ref[0] = jnp.where(lane_w == nw - 1, newcol_ref[0], pltpu.roll(win_t, nw - 1, 1))


def _nsa_sample(pt, q4, gq, newkv, newwin, newcol, win_t, win_off, wall, pool_t, pp):
    bd, npages = pt.shape
    nw = win_t.shape[2]
    nb = 2 * npages
    nbp = -(-nb // LANES) * LANES
    n_top = min(N_SELECT, nb + 1)
    per_b = lambda b, i, pt_ref: (b, 0, 0)
    const = lambda b, i, pt_ref: (0, 0)

    def page_map(kk):
        return lambda b, i, pt_ref: (pt_ref[b, i * pp + kk], 0, 0)

    grid_spec = pltpu.PrefetchScalarGridSpec(
        num_scalar_prefetch=1,
        grid=(bd, npages // pp),
        in_specs=[pl.BlockSpec((1, 8, LANES), per_b),
                  pl.BlockSpec((1, 8, LANES), per_b),
                  pl.BlockSpec((1, 1, 4 * LANES), per_b),
                  pl.BlockSpec((1, 1, 2 * LANES), per_b),
                  pl.BlockSpec((1, 2 * LANES, 1), per_b),
                  pl.BlockSpec((1, 2 * LANES, nw), lambda b, i, pt_ref: (win_off + b, 0, 0)),
                  pl.BlockSpec(wall.shape, const)]
                 + [pl.BlockSpec((1, 4 * LANES, LANES), page_map(kk)) for kk in range(pp)],
        out_specs=[pl.BlockSpec((1, 8, LANES), per_b),
                   pl.BlockSpec((1, 2 * LANES, nw), per_b)],
        scratch_shapes=[pltpu.VMEM((npages * 8, 2 * LANES), F32),
                        pltpu.VMEM((npages * 8, 2 * LANES), F32),
                        pltpu.VMEM((8, nbp), F32),
                        pltpu.VMEM((8, nbp), F32),
                        pltpu.VMEM((N_HEADS, nbp, LANES), F32),
                        pltpu.VMEM((N_HEADS, pp * (LANES // SEL_BLOCK), LANES), F32)])
    return pl.pallas_call(
        functools.partial(_nsa_sample_kernel, pp=pp, npages=npages, n_top=n_top),
        grid_spec=grid_spec,
        out_shape=[jax.ShapeDtypeStruct((bd, 8, LANES), F32),
                   jax.ShapeDtypeStruct((bd, 2 * LANES, nw), F32)],
        compiler_params=_cparams(("parallel", "arbitrary")), name="nsa_sample",
    )(pt, q4, gq, newkv, newwin, newcol, win_t, wall, *([pool_t] * pp))


def _split_bf16(x):
    hi = x.astype(BF16)
    return hi, (x - hi.astype(F32)).astype(BF16)


def _out_kernel(x_ref, ya_ref, yb_ref, yc_ref, yd_ref, wo_ref, g2_ref, wq_ref, k1_ref, k2_ref,
                xn_ref, ht_ref, s1_ref, s2_ref, *, nh, dq):
    acc = x_ref[...]
    for idx, y_ref in enumerate((ya_ref, yb_ref, yc_ref, yd_ref)):
        acc = acc + _dot(y_ref[...].astype(BF16), wo_ref[idx * MIX_W:(idx + 1) * MIX_W, :])
    xn_ref[...] = acc
    h2 = _rmsnorm(acc, g2_ref[...])
    ht_ref[...] = h2.T.astype(BF16)
    h_hi, h_lo = _split_bf16(h2)
    q = _dot(h_hi, wq_ref[0]) + _dot(h_lo, wq_ref[0]) + _dot(h_hi, wq_ref[1])
    q_hi, q_lo = _split_bf16(q)
    half = dq // 2
    for h in range(nh):
        for s_ref, k_ref, cols in ((s1_ref, k1_ref, slice(h * dq, h * dq + half)),
                                   (s2_ref, k2_ref, slice(h * dq + half, (h + 1) * dq))):
            s_ref[h] = (_dot(k_ref[0, h], q_hi[:, cols], NT) + _dot(k_ref[1, h], q_hi[:, cols], NT)
                        + _dot(k_ref[0, h], q_lo[:, cols], NT))


def _out_proj(x, ya, yb, yc, yd, wo, g2, wq, k1, k2, tm):
    n, d = x.shape
    _, nh, nk, half = k1.shape
    row = lambda i: (i, 0)
    c2 = lambda i: (0, 0)
    return pl.pallas_call(
        functools.partial(_out_kernel, nh=nh, dq=2 * half),
        grid=(n // tm,),
        in_specs=[pl.BlockSpec((tm, d), row)] + [pl.BlockSpec((tm, MIX_W), row)] * 4
                 + [pl.BlockSpec(wo.shape, c2), pl.BlockSpec((1, d), c2), pl.BlockSpec(wq.shape, lambda i: (0, 0, 0)),
                    pl.BlockSpec(k1.shape, lambda i: (0, 0, 0, 0)), pl.BlockSpec(k2.shape, lambda i: (0, 0, 0, 0))],
        out_specs=[pl.BlockSpec((tm, d), row),
                   pl.BlockSpec((d, tm), lambda i: (0, i)),
                   pl.BlockSpec((nh, nk, tm), lambda i: (0, 0, i)),
                   pl.BlockSpec((nh, nk, tm), lambda i: (0, 0, i))],
        out_shape=[jax.ShapeDtypeStruct((n, d), F32),
                   jax.ShapeDtypeStruct((d, n), BF16),
                   jax.ShapeDtypeStruct((nh, nk, n), F32),
                   jax.ShapeDtypeStruct((nh, nk, n), F32)],
        compiler_params=_cparams(("parallel",)), name="out_proj",
    )(x, ya, yb, yc, yd, wo, g2, wq, k1, k2)


_PAIR_COUNTS = tuple(PEER_TOPK // (i + 1) for i in range(PEER_TOPK))
_PAIR_ROWS = -(-sum(_PAIR_COUNTS) // 8) * 8


def _gate_kernel(s1_ref, s2_ref, rk2_ref, e2_ref, rr_ref, c_ref, v1_s, v2_s, c_s):
    x1 = s1_ref[0]
    x2 = s2_ref[0]
    for i in range(PEER_TOPK):
        m1 = jnp.max(x1, axis=0, keepdims=True)
        m2 = jnp.max(x2, axis=0, keepdims=True)
        v1_s[i:i + 1, :] = m1
        v2_s[i:i + 1, :] = m2
        marker = -RANK_MARK * (1.0 + i / PEER_TOPK)
        x1 = jnp.where(x1 == m1, marker, x1)
        x2 = jnp.where(x2 == m2, marker, x2)
    rank1 = jnp.where(x1 <= -RANK_MARK, (x1 * (-1.0 / RANK_MARK) - 1.0) * PEER_TOPK, float(PEER_TOPK))
    rank2 = jnp.where(x2 <= -RANK_MARK, (x2 * (-1.0 / RANK_MARK) - 1.0) * PEER_TOPK, float(PEER_TOPK))
    v1 = v1_s[...]
    v2 = v2_s[...]
    off = 0
    for i, cnt in enumerate(_PAIR_COUNTS):
        c_s[off:off + cnt, :] = v1[i:i + 1, :] + v2[0:cnt, :]
        off += cnt
    if off < _PAIR_ROWS:
        c_s[off:_PAIR_ROWS, :] = jnp.full((_PAIR_ROWS - off, v1.shape[1]), NEG, F32)
    c = c_s[...]
    mx = v1[0:1, :] + v2[0:1, :]
    z = jnp.zeros_like(mx)
    m = mx
    for i in range(PEER_TOPK):
        m = jnp.max(c, axis=0, keepdims=True)
        z = z + jnp.exp(m - mx)
        c = jnp.where(c == m, NEG, c)
    tau = m
    passing = jnp.zeros(v1.shape, F32)
    for jj in range(PEER_TOPK):
        passing = passing + jnp.where(v1 + v2[jj:jj + 1, :] >= tau, 1.0, 0.0)
    rr = jnp.zeros(rank1.shape, F32)
    for i in range(PEER_TOPK):
        rr = jnp.where(rank1 == float(i), passing[i:i + 1, :], rr)
    rk2_ref[0] = rank2.astype(BF16)
    rr_ref[0] = rr
    e2_ref[0] = jnp.exp(s2_ref[0] - v2[0:1, :]).astype(BF16)
    c_ref[0] = jnp.exp(s1_ref[0] - v1[0:1, :]) * (1.0 / z)


def _gate(s1t, s2t, tn):
    nh, nk, n = s1t.shape
    blk = lambda h, i: (h, 0, i)
    return pl.pallas_call(
        _gate_kernel,
        grid=(nh, n // tn),
        in_specs=[pl.BlockSpec((1, nk, tn), blk), pl.BlockSpec((1, nk, tn), blk)],
        out_specs=[pl.BlockSpec((1, nk, tn), blk)] * 4,
        out_shape=[jax.ShapeDtypeStruct((nh, nk, n), BF16), jax.ShapeDtypeStruct((nh, nk, n), BF16),
                   jax.ShapeDtypeStruct((nh, nk, n), F32), jax.ShapeDtypeStruct((nh, nk, n), F32)],
        scratch_shapes=[pltpu.VMEM((PEER_TOPK, tn), F32), pltpu.VMEM((PEER_TOPK, tn), F32),
                        pltpu.VMEM((_PAIR_ROWS, tn), F32)],
        compiler_params=_cparams(("parallel", "parallel")), name="peer_gate",
    )(s1t, s2t)


def _bf16_rows(row, n_rows):
    tile = jnp.broadcast_to(row, (16, row.shape[1])).astype(BF16)
    return jnp.concatenate([tile] * (n_rows // 16), axis=0)


def _peer_kernel(ht_ref, u_ref, vt_ref, rk2_ref, e2_ref, rr_ref, c_ref, xn_ref, fg_ref, o_ref,
                 acc_ref, f_ref, *, nh, nk, apc, final):
    j = pl.program_id(1)

    @pl.when(j == 0)
    def _():
        acc_ref[...] = jnp.zeros(acc_ref.shape, F32)

    ht = ht_ref[...]
    zero = jnp.zeros((), BF16)
    group = min(PEER_UP_ROWS, apc)
    for aa in range(apc):
        a = j * apc + aa
        if aa % group == 0:
            pre = _dot(u_ref[aa * nk:(aa + group) * nk, :], ht)
        k = aa % group
        act = _gelu(pre[k * nk:(k + 1) * nk, :]).astype(BF16)
        wgt = jnp.zeros(act.shape, BF16)
        for h in range(nh):
            limit = _bf16_rows(rr_ref[h, pl.ds(a, 1), :], nk)
            scale = _bf16_rows(c_ref[h, pl.ds(a, 1), :], nk)
            wgt = wgt + jnp.where(rk2_ref[h] < limit, e2_ref[h] * scale, zero)
        f_ref[aa * nk:(aa + 1) * nk, :] = wgt * act
    acc_ref[...] += _dot(vt_ref[...], f_ref[...])

    @pl.when(j == pl.num_programs(1) - 1)
    def _():
        out = xn_ref[...] + acc_ref[...].T
        if final:
            out = _rmsnorm(out, fg_ref[...])
        o_ref[...] = out


def _peer(ht, u, vt, rk2, e2, rr, c, xn, fg, tm, apc, final):
    d, n = ht.shape
    nh, nk, _ = rk2.shape
    assert nk % 16 == 0
    ne = u.shape[0]
    te = apc * nk
    tok3 = lambda i, j: (0, 0, i)
    return pl.pallas_call(
        functools.partial(_peer_kernel, nh=nh, nk=nk, apc=apc, final=final),
        grid=(n // tm, ne // te),
        in_specs=[pl.BlockSpec((d, tm), lambda i, j: (0, i)),
                  pl.BlockSpec((te, d), lambda i, j: (j, 0)),
                  pl.BlockSpec((d, te), lambda i, j: (0, j)),
                  pl.BlockSpec((nh, nk, tm), tok3),
                  pl.BlockSpec((nh, nk, tm), tok3),
                  pl.BlockSpec((nh, nk, tm), tok3),
                  pl.BlockSpec((nh, nk, tm), tok3),
                  pl.BlockSpec((tm, d), lambda i, j: (i, 0)),
                  pl.BlockSpec((1, d), lambda i, j: (0, 0))],
        out_specs=pl.BlockSpec((tm, d), lambda i, j: (i, 0)),
        out_shape=jax.ShapeDtypeStruct((n, d), F32),
        scratch_shapes=[pltpu.VMEM((d, tm), F32), pltpu.VMEM((te, tm), BF16)],
        compiler_params=_cparams(("parallel", "arbitrary")), name="peer",
    )(ht, u, vt, rk2, e2, rr, c, xn, fg)


def _rope_tables(pos):
    half = HEAD_DIM // 2
    inv_freq = ROPE_THETA ** (-jnp.arange(half, dtype=F32) / half)
    ang = pos.astype(F32)[:, None] * inv_freq[None, :]
    cos = jnp.cos(ang)
    sin = jnp.sin(ang)
    return jnp.tile(jnp.concatenate([cos, cos], axis=1), (1, 2)), jnp.tile(jnp.concatenate([-sin, sin], axis=1), (1, 2))


def _pad_rows(x, rows):
    return jnp.pad(x, ((0, rows - x.shape[0]),) + ((0, 0),) * (x.ndim - 1))


def _token_tile(n, pref):
    return pref if n % pref == 0 else n


def _page_pool_weights(wk, wv):
    wset = jnp.stack([wk[:, 0], wk[:, 1], wv[:, 0], wv[:, 1]]).reshape(4, 2, CMP_STRIDE)
    per_pos = jnp.tile(wset, (1, 1, LANES // CMP_STRIDE))
    chunk_of_pos = (jnp.arange(LANES) // CMP_STRIDE)[None, :] == jnp.arange(LANES // CMP_STRIDE)[:, None]
    return (per_pos[:, :, None, :] * chunk_of_pos[None, None].astype(F32)).reshape(64, LANES)


def kernel(x_prompt, x_sample, cache_nsa_kv, state_nsa_win, state_mlstm_C, state_mlstm_n, state_mlstm_m, state_conv, page_table, norm1_g, norm2_g, final_norm_g, w_in, w_out, mlstm_b_i, mlstm_b_f, mlstm_norm_g, conv_w, conv_b, conv_norm_g, conv_norm_b, nsa_cmp_wk, nsa_cmp_wv, gmlp_norm_g, gmlp_norm_b, gmlp_ws, gmlp_bs, peer_wq, peer_k1, peer_k2, peer_u, peer_v):
    depth = w_in.shape[0]
    bp, s_len, d = x_prompt.shape
    bd, t_dec, _ = x_sample.shape
    n_pool = cache_nsa_kv.shape[1]
    npages = page_table.shape[1]
    past = npages * cache_nsa_kv.shape[2]
    n_win = state_nsa_win.shape[2]
    assert t_dec == 1 and cache_nsa_kv.shape[2] == LANES and past % SEL_BLOCK == 0
    assert s_len >= WINDOW + 128 and n_win == WINDOW
    np_tok = bp * s_len
    ns_pad = LANES
    assert bd <= ns_pad

    xp = x_prompt.reshape(np_tok, d)
    xs = _pad_rows(x_sample.reshape(bd, d), ns_pad)
    cos_p, sin_p = _rope_tables(jnp.arange(s_len))
    cos_s, sin_s = _rope_tables(jnp.full((ns_pad,), past))
    pool_t = cache_nsa_kv.reshape(depth * n_pool, LANES, 4 * LANES).transpose(0, 2, 1)
    win_t = state_nsa_win.reshape(depth * bd, n_win, 2 * LANES).transpose(0, 2, 1)
    eye_h = jnp.eye(N_HEADS, dtype=F32)

    tm_in = _token_tile(s_len, 256)
    tm_out = _token_tile(np_tok, 256)
    tm_peer = _token_tile(np_tok, 512)
    nk = peer_k1.shape[2]
    apc = min(16, nk)
    w_in_t = w_in.transpose(2, 0, 1)

    new_p, new_s = [], []
    for l in range(depth):
        wi = w_in_t[:, l, :]
        w_perm = jnp.concatenate([wi[0:1024], wi[1032:2568], wi[2580:3092]], axis=0).astype(BF16)
        w_small = jnp.concatenate([wi[1024:1032], wi[2568:2580], jnp.zeros((108, d), F32)], axis=0).astype(BF16)
        g1 = norm1_g[l].reshape(1, d)
        g2 = norm2_g[l].reshape(1, d)
        gate_bias = jnp.concatenate([mlstm_b_i[l], mlstm_b_f[l], jnp.zeros((LANES - 2 * N_HEADS,), F32)]).reshape(1, LANES)
        mng = mlstm_norm_g[l].reshape(1, MIX_W)
        cw = conv_w[l]
        cbias = conv_b[l].reshape(1, MIX_W)
        cg = conv_norm_g[l].reshape(1, MIX_W)
        cbeta = conv_norm_b[l].reshape(1, MIX_W)
        w32 = jnp.concatenate([jnp.repeat(nsa_cmp_wk[l], HEAD_DIM, axis=1), jnp.repeat(nsa_cmp_wv[l], HEAD_DIM, axis=1)], axis=1)
        wlo, whi = w32[:CMP_STRIDE], w32[CMP_STRIDE:]
        wall = _page_pool_weights(nsa_cmp_wk[l], nsa_cmp_wv[l])
        gg = gmlp_norm_g[l].reshape(1, MIX_W)
        gbeta = gmlp_norm_b[l].reshape(1, MIX_W)
        gws = gmlp_ws[l]
        gbsb = jnp.repeat(gmlp_bs[l].T, HEAD_DIM, axis=1)
        wo = w_out[l].astype(BF16)
        wq = jnp.stack(_split_bf16(peer_wq[l]))
        k1 = jnp.stack(_split_bf16(peer_k1[l]))
        k2 = jnp.stack(_split_bf16(peer_k2[l]))
        ub = peer_u[l].astype(BF16)
        vtb = peer_v[l].astype(BF16).T
        fg = final_norm_g.reshape(1, d)
        final = l == depth - 1

        oa, ob, oq, okv, owin, okaug, ovsb, owinb, od, osm = _in_proj(xp, g1, w_perm, w_small, cos_p, sin_p, tm_in)
        ya, c_p, n_p, m_p = _mlstm(oa.reshape(bp, s_len, 1024), osm.reshape(bp, s_len, LANES), gate_bias, mng,
                                   jnp.zeros((bp, MIX_W, MIX_W), F32), jnp.zeros((bp, 1, MIX_W), F32),
                                   jnp.zeros((bp, 1, MIX_W), F32), MLSTM_CHUNK, MLSTM_CHUNK, bp)
        yb, conv_p = _conv(ob.reshape(bp, s_len, 2 * MIX_W), jnp.zeros((bp, CONV_PAD, MIX_W), F32),
                           cw, cbias, cg, cbeta, 512, 512)
        cb = _cmp(okv.reshape(bp, s_len, 4 * LANES), wlo, whi)
        yc = _nsa_prompt(oq.reshape(bp, s_len, MIX_W), osm.reshape(bp, s_len, LANES), cb,
                         okaug.reshape(bp, s_len, 2 * LANES), ovsb.reshape(bp, s_len, LANES),
                         owinb.reshape(bp, s_len, 2 * LANES), 128)
        yd, _ = _gmlp(od, gg, gbeta, gws, gbsb, 4)
        xn, ht, s1t, s2t = _out_proj(xp, ya.reshape(np_tok, MIX_W), yb.reshape(np_tok, MIX_W),
                                     yc.reshape(np_tok, MIX_W), yd, wo, g2, wq, k1, k2, tm_out)
        rk2, e2, rr, cw8 = _gate(s1t, s2t, _token_tile(np_tok, 512))
        xp = _peer(ht, ub, vtb, rk2, e2, rr, cw8, xn, fg, tm_peer, apc, final)
        new_p.append((okv.reshape(bp, s_len, 4, NSA_KV, HEAD_DIM),
                      owin.reshape(bp, s_len, 2, NSA_KV, HEAD_DIM)[:, s_len - n_win:],
                      jnp.stack([c_p[:, h * HEAD_DIM:(h + 1) * HEAD_DIM, h * HEAD_DIM:(h + 1) * HEAD_DIM]
                                 for h in range(N_HEADS)], axis=1),
                      n_p.reshape(bp, N_HEADS, HEAD_DIM),
                      m_p[:, 0, ::HEAD_DIM],
                      conv_p[:, CONV_PAD - (CONV_W - 1):]))

        sa, sb, sq, skv, swin, _, _, _, sd, ssm = _in_proj(xs, g1, w_perm, w_small, cos_s, sin_s, ns_pad)
        rows8 = lambda t: jnp.pad(t[:bd, None, :], ((0, 0), (0, 7), (0, 0)))
        c0 = jnp.einsum('bhvk,hg->bhvgk', state_mlstm_C[l], eye_h).reshape(bd, MIX_W, MIX_W)
        n0 = state_mlstm_n[l].reshape(bd, 1, MIX_W)
        m0 = jnp.repeat(state_mlstm_m[l], HEAD_DIM, axis=-1).reshape(bd, 1, MIX_W)
        bb_s = 2 if bd % 2 == 0 else 1
        ya_s, c_s, n_s, m_s = _mlstm(rows8(sa), rows8(ssm), gate_bias, mng, c0, n0, m0, 8, 1, bb_s)
        prefix = jnp.pad(state_conv[l], ((0, 0), (CONV_PAD - (CONV_W - 1), 0), (0, 0)))
        yb_s, conv_s = _conv(rows8(sb), prefix, cw, cbias, cg, cbeta, 8, 1)
        zd_s = jnp.pad(sd[:bd, None, :], ((0, 0), (0, GMLP_CHUNK - 1), (0, 0))).reshape(bd * GMLP_CHUNK, 2 * MIX_W)
        yd_s, v_s = _gmlp(zd_s, gg, gbeta, gws, gbsb, 1)
        q_heads = sq[:bd].reshape(bd, NSA_KV, 2, 1, HEAD_DIM)
        q4 = (q_heads * jnp.eye(NSA_KV, dtype=F32)[None, :, None, :, None]).reshape(bd, N_HEADS, LANES)
        q4 = jnp.pad(q4, ((0, 0), (0, 8 - N_HEADS), (0, 0)))
        gq = jnp.pad(ssm[:bd, 2 * N_HEADS:2 * N_HEADS + 3 * N_HEADS].reshape(bd, N_HEADS, 3),
                     ((0, 0), (0, 8 - N_HEADS), (0, LANES - 3)))
        assert npages % 4 == 0
        pp = next(c for c in (32, 16, 8, 4) if npages % c == 0)
        o8, win_s = _nsa_sample(page_table + l * n_pool, q4, gq, skv[:bd, None, :], swin[:bd, None, :],
                                swin[:bd, :, None], win_t, l * bd, wall, pool_t, pp)
        yc_s = jnp.concatenate([o8[:, h, (h // 2) * HEAD_DIM:(h // 2 + 1) * HEAD_DIM] for h in range(N_HEADS)], axis=1)
        xn_s, ht_s, s1t_s, s2t_s = _out_proj(xs, _pad_rows(ya_s[:, 0], ns_pad), _pad_rows(yb_s[:, 0], ns_pad),
                                             _pad_rows(yc_s, ns_pad),
                                             _pad_rows(yd_s.reshape(bd, GMLP_CHUNK, MIX_W)[:, 0], ns_pad),
                                             wo, g2, wq, k1, k2, ns_pad)
        rk2_s, e2_s, rr_s, cw8_s = _gate(s1t_s, s2t_s, ns_pad)
        xs = _peer(ht_s, ub, vtb, rk2_s, e2_s, rr_s, cw8_s, xn_s, fg, ns_pad, apc, final)
        new_s.append((skv[:bd].reshape(bd, 1, 4, NSA_KV, HEAD_DIM),
                      win_s.transpose(0, 2, 1).reshape(bd, n_win, 2, NSA_KV, HEAD_DIM),
                      jnp.stack([c_s[:, h * HEAD_DIM:(h + 1) * HEAD_DIM, h * HEAD_DIM:(h + 1) * HEAD_DIM]
                                 for h in range(N_HEADS)], axis=1),
                      n_s.reshape(bd, N_HEADS, HEAD_DIM),
                      m_s[:, 0, ::HEAD_DIM],
                      conv_s[:, CONV_PAD - (CONV_W - 1):],
                      v_s.reshape(bd, GMLP_CHUNK, MIX_W)[:, 0:1]))

    def stack(states, i):
        return jnp.stack([st[i] for st in states])

    y_prompt = xp.reshape(bp, s_len, d)
    y_sample = xs[:bd].reshape(bd, 1, d)
    return (y_prompt, y_sample,
            stack(new_p, 0), stack(new_s, 0),
            stack(new_p, 1), stack(new_s, 1),
            stack(new_p, 2), stack(new_s, 2),
            stack(new_p, 3), stack(new_s, 3),
            stack(new_p, 4), stack(new_s, 4),
            stack(new_p, 5), stack(new_s, 5),
            stack(new_s, 6))
```

```python
import functools

import jax
import jax.numpy as jnp
from jax import lax
from jax.experimental import pallas as pl
from jax.experimental.pallas import tpu as pltpu

F32 = jnp.float32
BF16 = jnp.bfloat16
HIGHEST = lax.Precision.HIGHEST

HEAD_DIM = 64
N_HEADS = 4
MIX_W = 256
NSA_KV = 2
NORM_EPS = 1e-6
NEG = -1e30
MLSTM_CHUNK = 128
CONV_W = 31
CONV_PAD = 32
CMP_STRIDE = 16
CMP_LEN = 32
SEL_BLOCK = 64
N_SELECT = 16
SEL_FORCE = 1e9
WINDOW = 512
ROPE_THETA = 10000.0
ATTN_SCALE = HEAD_DIM ** -0.5
GMLP_CHUNK = 128
PEER_TOPK = 16
RANK_MARK = 2.0 ** 100
PEER_UP_ROWS = 1
LANES = 128
VMEM_LIMIT = 48 * 1024 * 1024

NT = (((1,), (1,)), ((), ()))
TN = (((0,), (0,)), ((), ()))


def _cparams(sem):
    return pltpu.CompilerParams(dimension_semantics=sem, vmem_limit_bytes=VMEM_LIMIT)


def _iota(shape, dim):
    return lax.broadcasted_iota(jnp.int32, shape, dim)


def _sigmoid(x):
    return 1.0 / (1.0 + jnp.exp(-x))


def _gelu(x):
    c1 = -2.0 * 0.7978845608028654 * 1.4426950408889634
    c2 = c1 * 0.044715
    return x / (1.0 + jnp.exp2(x * (c1 + c2 * (x * x))))


def _dot(a, b, dims=None, precision=None):
    if dims is None:
        return jnp.dot(a, b, preferred_element_type=F32, precision=precision)
    return lax.dot_general(a, b, dims, preferred_element_type=F32, precision=precision)


def _block_ones(n, blk):
    return (_iota((n, n), 0) // blk == _iota((n, n), 1) // blk).astype(F32)


def _group_layernorm(x, ee, width):
    mu = _dot(x, ee, precision=HIGHEST) * (1.0 / width)
    d = x - mu
    var = _dot(d * d, ee, precision=HIGHEST) * (1.0 / width)
    return d * lax.rsqrt(var + NORM_EPS)


def _rmsnorm(x, g):
    return x * lax.rsqrt(jnp.mean(x * x, axis=-1, keepdims=True) + NORM_EPS) * g


def _masked_softmax(s, valid):
    sm = jnp.where(valid, s, NEG)
    p = jnp.where(valid, jnp.exp(sm - jnp.max(sm, axis=-1, keepdims=True)), 0.0)
    return p / jnp.maximum(jnp.sum(p, axis=-1, keepdims=True), 1e-30)


def _rope(x, cos, sin_signed):
    w = x.shape[1]
    fwd = pltpu.roll(x, w - HEAD_DIM // 2, 1)
    bwd = pltpu.roll(x, HEAD_DIM // 2, 1)
    first = (_iota(x.shape, 1) % HEAD_DIM) < HEAD_DIM // 2
    return x * cos + jnp.where(first, fwd, bwd) * sin_signed


def _in_kernel(x_ref, g_ref, w_ref, wsmall_ref, cos_ref, sin_ref,
               oa_ref, ob_ref, oq_ref, okv_ref, owin_ref, okaug_ref, ovsb_ref, owinb_ref, od_ref, os_ref, *, npb):
    tm = x_ref.shape[0]
    y = _rmsnorm(x_ref[...], g_ref[...])
    yb = y.astype(BF16)
    z = _dot(yb, w_ref[...], NT)
    os_ref[...] = _dot(yb, wsmall_ref[...], NT)
    oa_ref[...] = z[:, 0:1024]
    ob_ref[...] = z[:, 1024:1536]
    cos = cos_ref[...]
    sin = sin_ref[...]
    cos2 = jnp.concatenate([cos, cos], axis=1)
    sin2 = jnp.concatenate([sin, sin], axis=1)
    oq_ref[...] = _rope(z[:, 1536:1792], cos2, sin2) * ATTN_SCALE
    kc = _rope(z[:, 1792:1920], cos, sin)
    vc = z[:, 1920:2048]
    ks = _rope(z[:, 2048:2176], cos, sin)
    vs = z[:, 2176:2304]
    kw = _rope(z[:, 2304:2432], cos, sin)
    vw = z[:, 2432:2560]
    okv_ref[...] = jnp.concatenate([kc, vc, ks, vs], axis=1)
    owin = jnp.concatenate([kw, vw], axis=1)
    owin_ref[...] = owin
    pos = (pl.program_id(0) % npb) * tm + _iota((tm, LANES), 0)
    onehot = jnp.where(_iota((tm, LANES), 1) == pos // SEL_BLOCK, 1.0, 0.0)
    okaug_ref[...] = jnp.concatenate([ks, onehot], axis=1).astype(BF16)
    ovsb_ref[...] = vs.astype(BF16)
    owinb_ref[...] = owin.astype(BF16)
    od_ref[...] = z[:, 2560:3072]


def _in_proj(x, g, w, wsmall, cos_t, sin_t, tm):
    n, d = x.shape
    npb = cos_t.shape[0] // tm
    widths = (1024, 512, 256, 512, 256, 256, 128, 256, 512, 128)
    dtypes = (F32, F32, F32, F32, F32, BF16, BF16, BF16, F32, F32)
    row = lambda i: (i, 0)
    return pl.pallas_call(
        functools.partial(_in_kernel, npb=npb),
        grid=(n // tm,),
        in_specs=[pl.BlockSpec((tm, d), row),
                  pl.BlockSpec((1, d), lambda i: (0, 0)),
                  pl.BlockSpec(w.shape, lambda i: (0, 0)),
                  pl.BlockSpec(wsmall.shape, lambda i: (0, 0)),
                  pl.BlockSpec((tm, LANES), lambda i: (i % npb, 0)),
                  pl.BlockSpec((tm, LANES), lambda i: (i % npb, 0))],
        out_specs=[pl.BlockSpec((tm, wd), row) for wd in widths],
        out_shape=[jax.ShapeDtypeStruct((n, wd), dt) for wd, dt in zip(widths, dtypes)],
        compiler_params=_cparams(("parallel",)), name="in_proj",
    )(x, g, w, wsmall, cos_t, sin_t)


def _mlstm_kernel(za_ref, zs_ref, bias_ref, ng_ref, c0_ref, n0_ref, m0_ref,
                  y_ref, c_ref, n_ref, m_ref, *, L, t_valid, bb):
    @pl.when(pl.program_id(1) == 0)
    def _():
        c_ref[...] = c0_ref[...]
        n_ref[...] = n0_ref[...]
        m_ref[...] = m0_ref[...]

    head_of_lane = _iota((1, MIX_W), 1) // HEAD_DIM
    causal = _iota((L, L), 0) >= _iota((L, L), 1)
    tril = causal.astype(F32)
    ee = _block_ones(MIX_W, HEAD_DIM)
    bdiag = ee > 0.5
    row128 = _iota((L, LANES), 0)
    lane128 = _iota((L, LANES), 1)
    for bi in range(bb):
        za = za_ref[bi]
        q = za[:, 0:256]
        k = za[:, 256:512] * (HEAD_DIM ** -0.5)
        v = za[:, 512:768]
        o = za[:, 768:1024]
        gi = zs_ref[bi] + bias_ref[...]
        ls = jnp.minimum(gi, 0.0) - jnp.log(1.0 + jnp.exp(-jnp.abs(gi)))
        if t_valid < L:
            live = row128 < t_valid
            ig = jnp.where(live, gi, NEG)
            ls = jnp.where(live, ls, 0.0)
        else:
            ig = gi
        gmat = jnp.where(lane128 < N_HEADS, ig, 0.0)
        lfm = jnp.where((lane128 >= N_HEADS) & (lane128 < 2 * N_HEADS), ls, 0.0)
        fc = _dot(tril, lfm, precision=HIGHEST)
        g_t = gmat.T
        f_t = fc.T
        mrow = m_ref[bi]
        nrow = n_ref[bi]
        cm = c_ref[bi]
        qb = q.astype(BF16)
        kb = k.astype(BF16)
        vb = v.astype(BF16)
        num = jnp.zeros((L, MIX_W), F32)
        decay_b = jnp.zeros((L, MIX_W), F32)
        dens_b = jnp.zeros((L, MIX_W), F32)
        mt_b = jnp.zeros((L, MIX_W), F32)
        w_b = jnp.zeros((L, MIX_W), F32)
        cd_b = jnp.zeros((1, MIX_W), F32)
        mnew_b = jnp.zeros((1, MIX_W), F32)
        for h in range(N_HEADS):
            hm = head_of_lane == h
            f_col = fc[:, N_HEADS + h:N_HEADS + h + 1]
            ig_col = gmat[:, h:h + 1]
            f_row = f_t[N_HEADS + h:N_HEADS + h + 1, :]
            ig_row = g_t[h:h + 1, :]
            mp = mrow[:, h * HEAD_DIM:h * HEAD_DIM + 1]
            dmat = jnp.where(causal, (f_col - f_row) + ig_row, NEG)
            m_inter = mp + f_col
            m_t = jnp.maximum(m_inter, jnp.max(dmat, axis=1, keepdims=True))
            d_exp = jnp.exp(dmat - m_t)
            decay = jnp.exp(m_inter - m_t)
            qh = jnp.where(hm, q, 0.0).astype(BF16)
            s = _dot(qh, kb, NT) * d_exp
            num = jnp.where(hm, _dot(s.astype(BF16), vb), num)
            dens_b = jnp.where(hm, jnp.sum(s, axis=1, keepdims=True), dens_b)
            decay_b = jnp.where(hm, decay, decay_b)
            mt_b = jnp.where(hm, m_t, mt_b)
            f_last = f_col[L - 1:L, :]
            m_new = m_t[L - 1:L, :]
            w_b = jnp.where(hm, jnp.exp((f_last - f_col) + ig_col - m_new), w_b)
            cd_b = jnp.where(hm, jnp.exp(mp + f_last - m_new), cd_b)
            mnew_b = jnp.where(hm, m_new, mnew_b)
        inter = _dot(qb, cm.astype(BF16), NT)
        nq_b = _dot(q * nrow, ee, precision=HIGHEST)
        hnum = num + decay_b * inter
        den_b = dens_b + decay_b * nq_b
        hh = hnum / jnp.maximum(jnp.abs(den_b), jnp.exp(-mt_b))
        hn = _group_layernorm(hh, ee, HEAD_DIM) * ng_ref[...]
        y_ref[bi] = hn * _sigmoid(o)
        kw = k * w_b
        upd = _dot(vb, kw.astype(BF16), TN)
        c_ref[bi] = jnp.where(bdiag, cd_b * cm + upd, 0.0)
        n_ref[bi] = cd_b * nrow + jnp.sum(kw, axis=0, keepdims=True)
        m_ref[bi] = mnew_b


def _mlstm(za, zs, bias, ng, c0, n0, m0, L, t_valid, bb):
    b, t, _ = za.shape
    nc = t // L
    st = lambda i, c: (i, 0, 0)
    return pl.pallas_call(
        functools.partial(_mlstm_kernel, L=L, t_valid=t_valid, bb=bb),
        grid=(b // bb, nc),
        in_specs=[pl.BlockSpec((bb, L, 1024), lambda i, c: (i, c, 0)),
                  pl.BlockSpec((bb, L, LANES), lambda i, c: (i, c, 0)),
                  pl.BlockSpec((1, LANES), lambda i, c: (0, 0)),
                  pl.BlockSpec((1, MIX_W), lambda i, c: (0, 0)),
                  pl.BlockSpec((bb, MIX_W, MIX_W), st),
                  pl.BlockSpec((bb, 1, MIX_W), st),
                  pl.BlockSpec((bb, 1, MIX_W), st)],
        out_specs=[pl.BlockSpec((bb, L, MIX_W), lambda i, c: (i, c, 0)),
                   pl.BlockSpec((bb, MIX_W, MIX_W), st),
                   pl.BlockSpec((bb, 1, MIX_W), st),
                   pl.BlockSpec((bb, 1, MIX_W), st)],
        out_shape=[jax.ShapeDtypeStruct((b, t, MIX_W), F32),
                   jax.ShapeDtypeStruct((b, MIX_W, MIX_W), F32),
                   jax.ShapeDtypeStruct((b, 1, MIX_W), F32),
                   jax.ShapeDtypeStruct((b, 1, MIX_W), F32)],
        compiler_params=_cparams(("parallel", "arbitrary")), name="mlstm",
    )(za, zs, bias, ng, c0, n0, m0)


def _conv_kernel(zb_ref, pre_ref, w_ref, cb_ref, g_ref, beta_ref, y_ref, st_ref, ext_ref, *, tt, tv):
    @pl.when(pl.program_id(1) == 0)
    def _():
        ext_ref[0:CONV_PAD, :] = pre_ref[0]

    z = zb_ref[0]
    u = z[:, :MIX_W] * _sigmoid(z[:, MIX_W:])
    ext_ref[CONV_PAD:CONV_PAD + tt, :] = u
    off = CONV_PAD - (CONV_W - 1)
    acc = jnp.zeros((tt, MIX_W), F32)
    for j in range(CONV_W):
        acc = acc + w_ref[j:j + 1, :] * ext_ref[off + j:off + j + tt, :]
    ee = _block_ones(MIX_W, HEAD_DIM)
    y = _group_layernorm(acc + cb_ref[...], ee, HEAD_DIM) * g_ref[...] + beta_ref[...]
    y_ref[0] = y * _sigmoid(y)
    st_ref[0] = ext_ref[tv:tv + CONV_PAD, :]
    ext_ref[0:CONV_PAD, :] = ext_ref[tt:tt + CONV_PAD, :]


def _conv(zb, prefix, w, cb, g, beta, tt, tv):
    b, t, _ = zb.shape
    vec = lambda i, j: (0, 0)
    return pl.pallas_call(
        functools.partial(_conv_kernel, tt=tt, tv=tv),
        grid=(b, t // tt),
        in_specs=[pl.BlockSpec((1, tt, 2 * MIX_W), lambda i, j: (i, j, 0)),
                  pl.BlockSpec((1, CONV_PAD, MIX_W), lambda i, j: (i, 0, 0)),
                  pl.BlockSpec((CONV_W, MIX_W), vec),
                  pl.BlockSpec((1, MIX_W), vec),
                  pl.BlockSpec((1, MIX_W), vec),
                  pl.BlockSpec((1, MIX_W), vec)],
        out_specs=[pl.BlockSpec((1, tt, MIX_W), lambda i, j: (i, j, 0)),
                   pl.BlockSpec((1, CONV_PAD, MIX_W), lambda i, j: (i, 0, 0))],
        out_shape=[jax.ShapeDtypeStruct((b, t, MIX_W), F32),
                   jax.ShapeDtypeStruct((b, CONV_PAD, MIX_W), F32)],
        scratch_shapes=[pltpu.VMEM((CONV_PAD + tt, MIX_W), F32)],
        compiler_params=_cparams(("parallel", "arbitrary")), name="conv",
    )(zb, prefix, w, cb, g, beta)


def _gmlp_kernel(zd_ref, g_ref, beta_ref, ws_ref, bsb_ref, y_ref, v_ref, *, cpb):
    z = zd_ref[...]
    ee = _block_ones(MIX_W, HEAD_DIM)
    u = _gelu(z[:, :MIX_W])
    vv = _group_layernorm(_gelu(z[:, MIX_W:]), ee, HEAD_DIM) * g_ref[...] + beta_ref[...]
    v_ref[...] = vv
    tril = _iota((GMLP_CHUNK, GMLP_CHUNK), 0) >= _iota((GMLP_CHUNK, GMLP_CHUNK), 1)
    head_of_lane = _iota((1, MIX_W), 1) // HEAD_DIM
    wm = [jnp.where(tril, ws_ref[h], 0.0).astype(BF16) for h in range(N_HEADS)]
    for c in range(cpb):
        rows = slice(c * GMLP_CHUNK, (c + 1) * GMLP_CHUNK)
        vc = vv[rows].astype(BF16)
        mixed = jnp.zeros((GMLP_CHUNK, MIX_W), F32)
        for h in range(N_HEADS):
            mixed = jnp.where(head_of_lane == h, _dot(wm[h], vc), mixed)
        y_ref[rows, :] = u[rows] * (mixed + bsb_ref[...])


def _gmlp(zd, g, beta, ws, bsb, cpb):
    n = zd.shape[0]
    rows = cpb * GMLP_CHUNK
    vec = lambda i: (0, 0)
    return pl.pallas_call(
        functools.partial(_gmlp_kernel, cpb=cpb),
        grid=(n // rows,),
        in_specs=[pl.BlockSpec((rows, 2 * MIX_W), lambda i: (i, 0)),
                  pl.BlockSpec((1, MIX_W), vec),
                  pl.BlockSpec((1, MIX_W), vec),
                  pl.BlockSpec((N_HEADS, GMLP_CHUNK, GMLP_CHUNK), lambda i: (0, 0, 0)),
                  pl.BlockSpec((GMLP_CHUNK, MIX_W), vec)],
        out_specs=[pl.BlockSpec((rows, MIX_W), lambda i: (i, 0)),
                   pl.BlockSpec((rows, MIX_W), lambda i: (i, 0))],
        out_shape=[jax.ShapeDtypeStruct((n, MIX_W), F32),
                   jax.ShapeDtypeStruct((n, MIX_W), F32)],
        compiler_params=_cparams(("parallel",)), name="gmlp",
    )(zd, g, beta, ws, bsb)


def _combine_lo_hi(lo, hi):
    r = lo.shape[0]
    nxt = pltpu.roll(hi, r - 1, 0)
    return lo + jnp.where(_iota(lo.shape, 0) < r - 1, nxt, 0.0)


def _cmp_kernel(kv_ref, wlo_ref, whi_ref, o_ref):
    x = kv_ref[0]
    x3 = x.reshape(x.shape[0] // CMP_STRIDE, CMP_STRIDE, x.shape[1])
    lo = jnp.sum(x3 * wlo_ref[...][None], axis=1)
    hi = jnp.sum(x3 * whi_ref[...][None], axis=1)
    o_ref[0] = _combine_lo_hi(lo, hi)


def _cmp(okv3, wlo, whi):
    b, s, _ = okv3.shape
    return pl.pallas_call(
        _cmp_kernel,
        grid=(b,),
        in_specs=[pl.BlockSpec((1, s, 2 * LANES), lambda i: (i, 0, 0)),
                  pl.BlockSpec((CMP_STRIDE, 2 * LANES), lambda i: (0, 0)),
                  pl.BlockSpec((CMP_STRIDE, 2 * LANES), lambda i: (0, 0))],
        out_specs=pl.BlockSpec((1, s // CMP_STRIDE, 2 * LANES), lambda i: (i, 0, 0)),
        out_shape=jax.ShapeDtypeStruct((b, s // CMP_STRIDE, 2 * LANES), F32),
        compiler_params=_cparams(("parallel",)), name="nsa_cmp",
    )(okv3, wlo, whi)


def _nsa_prompt_kernel(q_ref, gs_ref, cb_ref, ka_ref, vs_ref, win_ref, y_ref, s_buf, *, tq, s_len, n_top, tk):
    start = pl.program_id(1) * tq
    nc = s_len // CMP_STRIDE
    ns = s_len // SEL_BLOCK
    q = q_ref[0]
    gl = _sigmoid(gs_ref[0])
    lo_half = _iota((tq, LANES), 1) < HEAD_DIM
    q_rows = []
    for g in range(NSA_KV):
        qpair = q[:, g * LANES:(g + 1) * LANES]
        swapped = pltpu.roll(qpair, HEAD_DIM, 1)
        if g == 0:
            q_rows += [jnp.where(lo_half, qpair, 0.0), jnp.where(lo_half, swapped, 0.0)]
        else:
            q_rows += [jnp.where(lo_half, 0.0, swapped), jnp.where(lo_half, 0.0, qpair)]
    q128 = jnp.concatenate(q_rows, axis=0)
    q128b = q128.astype(BF16)
    trow = start + _iota((tq, 1), 0)
    t2 = jnp.concatenate([trow, trow], axis=0)
    t4 = jnp.concatenate([t2, t2], axis=0)
    cb = cb_ref[0]
    sc = _dot(q128b, cb[:, 0:LANES].astype(BF16), NT)
    validc = (_iota((1, nc), 1) * CMP_STRIDE + CMP_LEN) <= (t4 + 1)
    p = _masked_softmax(sc, validc)
    o_c = _dot(p.astype(BF16), cb[:, LANES:2 * LANES].astype(BF16))
    psum = jnp.concatenate([p[0:tq] + p[tq:2 * tq], p[2 * tq:3 * tq] + p[3 * tq:4 * tq]], axis=0)
    pool4 = (_iota((nc, ns), 0) // (SEL_BLOCK // CMP_STRIDE) == _iota((nc, ns), 1)).astype(F32)
    imp = _dot(psum, pool4, precision=HIGHEST)
    jidx = _iota((1, ns), 1)
    imp = jnp.where(jidx == t2 // SEL_BLOCK, SEL_FORCE, imp)
    imp = jnp.where(jidx * SEL_BLOCK <= t2, imp, -1.0)
    if n_top < ns:
        x = jnp.concatenate([imp[0:tq].T, imp[tq:2 * tq].T], axis=1)
        jrow = _iota((ns, 2 * tq), 0).astype(F32)
        for _ in range(n_top):
            m = jnp.max(x, axis=0, keepdims=True)
            first = jnp.min(jnp.where(x == m, jrow, float(ns)), axis=0, keepdims=True)
            x = jnp.where(jrow == first, -3.0, x)
        sel_t = jnp.where(x == -3.0, 1.0, 0.0)
        sel = jnp.concatenate([sel_t[:, 0:tq].T, sel_t[:, tq:2 * tq].T], axis=0)
    else:
        sel = jnp.ones((2 * tq, ns), F32)
    selneg = (sel - 1.0) * (-NEG)
    if ns < LANES:
        selneg = jnp.concatenate([selneg, jnp.zeros((2 * tq, LANES - ns), F32)], axis=1)
    selneg4 = jnp.concatenate([selneg[0:tq], selneg[0:tq], selneg[tq:2 * tq], selneg[tq:2 * tq]], axis=0)
    qaug = jnp.concatenate([q128, selneg4], axis=1).astype(BF16)

    def scores(c):
        return _dot(qaug, ka_ref[0, pl.ds(pl.multiple_of(c * tk, tk), tk), :], NT)

    def absorb(c, s, carry):
        m, l, acc = carry
        m_new = jnp.maximum(m, jnp.max(s, axis=1, keepdims=True))
        pt = jnp.exp(s - m_new)
        alpha = jnp.exp(m - m_new)
        l = alpha * l + jnp.sum(pt, axis=1, keepdims=True)
        acc = alpha * acc + _dot(pt.astype(BF16), vs_ref[0, pl.ds(pl.multiple_of(c * tk, tk), tk), :])
        return m_new, l, acc

    def tile(c, carry):
        s = s_buf[...]
        s_buf[...] = scores(c + 1)
        return absorb(c, s, carry)

    n_tiles = (start + tq + tk - 1) // tk
    carry = (jnp.full((4 * tq, 1), NEG, F32), jnp.zeros((4 * tq, 1), F32), jnp.zeros((4 * tq, LANES), F32))
    s_buf[...] = scores(0)
    carry = lax.fori_loop(0, n_tiles - 1, tile, carry)
    last = n_tiles - 1
    s_diag = jnp.where(last * tk + _iota((1, tk), 1) <= t4, s_buf[...], NEG)
    _, l, acc = absorb(last, s_diag, carry)
    o_s = acc / jnp.maximum(l, 1e-30)
    wl = WINDOW + tq
    w0 = pl.multiple_of(jnp.maximum(start - WINDOW, 0), tq)
    sw = _dot(q128b, win_ref[0, pl.ds(w0, wl), 0:LANES], NT)
    kposw = w0 + _iota((1, wl), 1)
    validw = (kposw <= t4) & (kposw > t4 - WINDOW)
    o_w = _dot(_masked_softmax(sw, validw).astype(BF16), win_ref[0, pl.ds(w0, wl), LANES:2 * LANES])
    heads = []
    for hh in range(N_HEADS):
        col = 2 * N_HEADS + hh * 3
        rows = slice(hh * tq, (hh + 1) * tq)
        heads.append(o_c[rows] * gl[:, col:col + 1] + o_s[rows] * gl[:, col + 1:col + 2]
                     + o_w[rows] * gl[:, col + 2:col + 3])
    y_ref[0] = jnp.concatenate([jnp.where(lo_half, heads[0], pltpu.roll(heads[1], HEAD_DIM, 1)),
                                jnp.where(lo_half, pltpu.roll(heads[2], HEAD_DIM, 1), heads[3])], axis=1)


def _nsa_prompt(oq3, os3, cb, okaug3, ovsb3, owinb3, tq):
    b, s, _ = oq3.shape
    ns = s // SEL_BLOCK
    assert ns <= LANES
    n_top = min(N_SELECT, ns)
    tk = min(512, s)
    full = lambda i, j: (i, 0, 0)
    return pl.pallas_call(
        functools.partial(_nsa_prompt_kernel, tq=tq, s_len=s, n_top=n_top, tk=tk),
        grid=(b, s // tq),
        in_specs=[pl.BlockSpec((1, tq, MIX_W), lambda i, j: (i, j, 0)),
                  pl.BlockSpec((1, tq, LANES), lambda i, j: (i, j, 0)),
                  pl.BlockSpec((1, s // CMP_STRIDE, 2 * LANES), full),
                  pl.BlockSpec((1, s, 2 * LANES), full),
                  pl.BlockSpec((1, s, LANES), full),
                  pl.BlockSpec((1, s, 2 * LANES), full)],
        out_specs=pl.BlockSpec((1, tq, MIX_W), lambda i, j: (i, j, 0)),
        out_shape=jax.ShapeDtypeStruct((b, s, MIX_W), F32),
        scratch_shapes=[pltpu.VMEM((N_HEADS * tq, tk), F32)],
        compiler_params=_cparams(("parallel", "arbitrary")), name="nsa_prompt",
    )(oq3, os3, cb, okaug3, ovsb3, owinb3)


def _nsa_sample_kernel(pt_ref, q4_ref, gq_ref, newkv_ref, newwin_ref, newcol_ref, win_ref, wall_ref, *rest,
                       pp, npages, n_top):
    pool_refs = rest[:pp]
    o_ref, wo_ref, lo_s, hi_s, m_s, l_s, a_s, a_stage = rest[pp:]
    i = pl.program_id(1)
    nb = 2 * npages
    nbp = m_s.shape[1]
    ncp = npages * (LANES // CMP_STRIDE)
    past = npages * LANES
    q4 = q4_ref[0]
    q4b = q4.astype(BF16)

    @pl.when(i == 0)
    def _():
        m_s[...] = jnp.zeros(m_s.shape, F32)
        l_s[...] = jnp.zeros(l_s.shape, F32)
        if nb < nbp:
            a_s[...] = jnp.zeros(a_s.shape, F32)

    lane8 = _iota((8, LANES), 1)
    lanej = _iota((8, nbp), 1)
    set_of_lane = _iota((8, 2 * LANES), 1) // HEAD_DIM
    bpp = LANES // SEL_BLOCK
    wall = wall_ref[...].astype(BF16)
    m_acc = m_s[...]
    l_acc = l_s[...]
    los, his = [], []
    for kk in range(pp):
        page = pool_refs[kk][0]
        pooled = _dot(wall, page[0:2 * LANES, :].astype(BF16), NT)
        lo8 = jnp.zeros((8, 2 * LANES), F32)
        hi8 = jnp.zeros((8, 2 * LANES), F32)
        for st in range(4):
            lo8 = jnp.where(set_of_lane == st, pooled[st * 16:st * 16 + 8], lo8)
            hi8 = jnp.where(set_of_lane == st, pooled[st * 16 + 8:st * 16 + 16], hi8)
        los.append(lo8)
        his.append(hi8)
        ks_t = page[2 * LANES:3 * LANES, :].astype(BF16)
        vs_t = page[3 * LANES:4 * LANES, :].astype(BF16)
        s = _dot(q4b, ks_t)
        probs = []
        for jb in range(bpp):
            inb = (lane8 >= jb * SEL_BLOCK) & (lane8 < (jb + 1) * SEL_BLOCK)
            sm = jnp.where(inb, s, NEG)
            m = jnp.max(sm, axis=1, keepdims=True)
            p = jnp.where(inb, jnp.exp(sm - m), 0.0)
            probs.append(p)
            j = i * (pp * bpp) + kk * bpp + jb
            m_acc = jnp.where(lanej == j, m, m_acc)
            l_acc = jnp.where(lanej == j, jnp.sum(p, axis=1, keepdims=True), l_acc)
        a = _dot(jnp.concatenate(probs, axis=0).astype(BF16), vs_t, NT)
        for jb in range(bpp):
            for hh in range(N_HEADS):
                a_stage[hh, kk * bpp + jb:kk * bpp + jb + 1, :] = a[jb * 8 + hh:jb * 8 + hh + 1, :]
    m_s[...] = m_acc
    l_s[...] = l_acc
    r0 = pl.multiple_of(i * (pp * 8), pp * 8)
    lo_s[pl.ds(r0, pp * 8), :] = jnp.concatenate(los, axis=0)
    hi_s[pl.ds(r0, pp * 8), :] = jnp.concatenate(his, axis=0)
    b0 = pl.multiple_of(i * (pp * bpp), pp * bpp)
    for hh in range(N_HEADS):
        a_s[hh, pl.ds(b0, pp * bpp), :] = a_stage[hh]

    @pl.when(i == pl.num_programs(1) - 1)
    def _():
        row8 = _iota((8, 1), 0)
        cbm = _combine_lo_hi(lo_s[...], hi_s[...])
        kcb = cbm[:, 0:LANES].astype(BF16)
        vcb = cbm[:, LANES:2 * LANES].astype(BF16)
        sc = _dot(q4b, kcb, NT)
        validc = (_iota((1, ncp), 1) * CMP_STRIDE + CMP_LEN) <= past + 1
        p = _masked_softmax(sc, validc)
        o_c = _dot(p.astype(BF16), vcb)
        pg = jnp.where(row8 == 0, p[0:1] + p[1:2], jnp.where(row8 == 1, p[2:3] + p[3:4], 0.0))
        pool4 = (_iota((ncp, nbp), 0) // (SEL_BLOCK // CMP_STRIDE) == _iota((ncp, nbp), 1)).astype(F32)
        imp2 = _dot(pg, pool4, precision=HIGHEST)
        ii = _iota((nbp, nbp), 0)
        jj = _iota((nbp, nbp), 1)
        sels = []
        for g in range(NSA_KV):
            mx = jnp.broadcast_to(imp2[g:g + 1, :], (nbp, nbp))
            mt = mx.T
            beats = ((mt > mx) | ((mt == mx) & (ii < jj))) & (ii < nb)
            rank = jnp.sum(jnp.where(beats, 1.0, 0.0), axis=0, keepdims=True)
            sels.append(jnp.where((rank < n_top - 1) & (_iota((1, nbp), 1) < nb), 1.0, 0.0))
        sel8 = jnp.where(row8 < 2, sels[0], sels[1]) > 0.5
        newkv = newkv_ref[0]
        ksn = newkv[:, 2 * LANES:3 * LANES]
        vsn = newkv[:, 3 * LANES:4 * LANES]
        s_new = jnp.sum(q4 * ksn, axis=1, keepdims=True)
        mrow = m_s[...]
        m_all = jnp.maximum(jnp.max(jnp.where(sel8, mrow, NEG), axis=1, keepdims=True), s_new)
        wj = jnp.where(sel8, jnp.exp(mrow - m_all), 0.0)
        w_new = jnp.exp(s_new - m_all)
        ltot = jnp.sum(wj * l_s[...], axis=1, keepdims=True) + w_new
        osum = w_new * vsn
        for hh in range(N_HEADS):
            osum = osum + jnp.where(row8 == hh, _dot(wj, a_s[hh], precision=HIGHEST), 0.0)
        o_s = osum / jnp.maximum(ltot, 1e-30)
        win_t = win_ref[0]
        nw = win_t.shape[1]
        lane_w = _iota((1, nw), 1)
        sw = _dot(q4b, win_t[0:LANES, :].astype(BF16))
        validw = (past - nw + lane_w) > past - WINDOW
        neww = newwin_ref[0]
        sw_new = jnp.sum(q4 * neww[:, 0:LANES], axis=1, keepdims=True)
        mw = jnp.maximum(jnp.max(jnp.where(validw, sw, NEG), axis=1, keepdims=True), sw_new)
        pw = jnp.where(validw, jnp.exp(sw - mw), 0.0)
        pn = jnp.exp(sw_new - mw)
        zw = jnp.sum(pw, axis=1, keepdims=True) + pn
        o_w = (_dot(pw.astype(BF16), win_t[LANES:2 * LANES, :].astype(BF16), NT) + pn * neww[:, LANES:2 * LANES]) / zw
        gg = _sigmoid(gq_ref[0])
        o_ref[0] = o_c * gg[:, 0:1] + o_s * gg[:, 1:2] + o_w * gg[:, 2:3]
        wo_ref[0] = jnp.where(lane_w == nw - 1, newcol_ref[0], pltpu.roll(win_t, nw - 1, 1))


def _nsa_sample(pt, q4, gq, newkv, newwin, newcol, win_t, win_off, wall, pool_t, pp):
    bd, npages = pt.shape
    nw = win_t.shape[2]
    nb = 2 * npages
    nbp = -(-nb // LANES) * LANES
    n_top = min(N_SELECT, nb + 1)
    per_b = lambda b, i, pt_ref: (b, 0, 0)
    const = lambda b, i, pt_ref: (0, 0)

    def page_map(kk):
        return lambda b, i, pt_ref: (pt_ref[b, i * pp + kk], 0, 0)

    grid_spec = pltpu.PrefetchScalarGridSpec(
        num_scalar_prefetch=1,
        grid=(bd, npages // pp),
        in_specs=[pl.BlockSpec((1, 8, LANES), per_b),
                  pl.BlockSpec((1, 8, LANES), per_b),
                  pl.BlockSpec((1, 1, 4 * LANES), per_b),
                  pl.BlockSpec((1, 1, 2 * LANES), per_b),
                  pl.BlockSpec((1, 2 * LANES, 1), per_b),
                  pl.BlockSpec((1, 2 * LANES, nw), lambda b, i, pt_ref: (win_off + b, 0, 0)),
                  pl.BlockSpec(wall.shape, const)]
                 + [pl.BlockSpec((1, 4 * LANES, LANES), page_map(kk)) for kk in range(pp)],
        out_specs=[pl.BlockSpec((1, 8, LANES), per_b),
                   pl.BlockSpec((1, 2 * LANES, nw), per_b)],
        scratch_shapes=[pltpu.VMEM((npages * 8, 2 * LANES), F32),
                        pltpu.VMEM((npages * 8, 2 * LANES), F32),
                        pltpu.VMEM((8, nbp), F32),
                        pltpu.VMEM((8, nbp), F32),
                        pltpu.VMEM((N_HEADS, nbp, LANES), F32),
                        pltpu.VMEM((N_HEADS, pp * (LANES // SEL_BLOCK), LANES), F32)])
    return pl.pallas_call(
        functools.partial(_nsa_sample_kernel, pp=pp, npages=npages, n_top=n_top),
        grid_spec=grid_spec,
        out_shape=[jax.ShapeDtypeStruct((bd, 8, LANES), F32),
                   jax.ShapeDtypeStruct((bd, 2 * LANES, nw), F32)],
        compiler_params=_cparams(("parallel", "arbitrary")), name="nsa_sample",
    )(pt, q4, gq, newkv, newwin, newcol, win_t, wall, *([pool_t] * pp))


def _split_bf16(x):
    hi = x.astype(BF16)
    return hi, (x - hi.astype(F32)).astype(BF16)


def _out_kernel(x_ref, ya_ref, yb_ref, yc_ref, yd_ref, wo_ref, g2_ref, wq_ref, k1_ref, k2_ref,
                xn_ref, ht_ref, s1_ref, s2_ref, *, nh, dq):
    acc = x_ref[...]
    for idx, y_ref in enumerate((ya_ref, yb_ref, yc_ref, yd_ref)):
        acc = acc + _dot(y_ref[...].astype(BF16), wo_ref[idx * MIX_W:(idx + 1) * MIX_W, :])
    xn_ref[...] = acc
    h2 = _rmsnorm(acc, g2_ref[...])
    ht_ref[...] = h2.T.astype(BF16)
    h_hi, h_lo = _split_bf16(h2)
    q = _dot(h_hi, wq_ref[0]) + _dot(h_lo, wq_ref[0]) + _dot(h_hi, wq_ref[1])
    q_hi, q_lo = _split_bf16(q)
    half = dq // 2
    for h in range(nh):
        for s_ref, k_ref, cols in ((s1_ref, k1_ref, slice(h * dq, h * dq + half)),
                                   (s2_ref, k2_ref, slice(h * dq + half, (h + 1) * dq))):
            s_ref[h] = (_dot(k_ref[0, h], q_hi[:, cols], NT) + _dot(k_ref[1, h], q_hi[:, cols], NT)
                        + _dot(k_ref[0, h], q_lo[:, cols], NT))


def _out_proj(x, ya, yb, yc, yd, wo, g2, wq, k1, k2, tm):
    n, d = x.shape
    _, nh, nk, half = k1.shape
    row = lambda i: (i, 0)
    c2 = lambda i: (0, 0)
    return pl.pallas_call(
        functools.partial(_out_kernel, nh=nh, dq=2 * half),
        grid=(n // tm,),
        in_specs=[pl.BlockSpec((tm, d), row)] + [pl.BlockSpec((tm, MIX_W), row)] * 4
                 + [pl.BlockSpec(wo.shape, c2), pl.BlockSpec((1, d), c2), pl.BlockSpec(wq.shape, lambda i: (0, 0, 0)),
                    pl.BlockSpec(k1.shape, lambda i: (0, 0, 0, 0)), pl.BlockSpec(k2.shape, lambda i: (0, 0, 0, 0))],
        out_specs=[pl.BlockSpec((tm, d), row),
                   pl.BlockSpec((d, tm), lambda i: (0, i)),
                   pl.BlockSpec((nh, nk, tm), lambda i: (0, 0, i)),
                   pl.BlockSpec((nh, nk, tm), lambda i: (0, 0, i))],
        out_shape=[jax.ShapeDtypeStruct((n, d), F32),
                   jax.ShapeDtypeStruct((d, n), BF16),
                   jax.ShapeDtypeStruct((nh, nk, n), F32),
                   jax.ShapeDtypeStruct((nh, nk, n), F32)],
        compiler_params=_cparams(("parallel",)), name="out_proj",
    )(x, ya, yb, yc, yd, wo, g2, wq, k1, k2)


_PAIR_COUNTS = tuple(PEER_TOPK // (i + 1) for i in range(PEER_TOPK))
_PAIR_ROWS = -(-sum(_PAIR_COUNTS) // 8) * 8


def _gate_kernel(s1_ref, s2_ref, rk2_ref, e2_ref, rr_ref, c_ref, v1_s, v2_s, c_s):
    x1 = s1_ref[0]
    x2 = s2_ref[0]
    for i in range(PEER_TOPK):
        m1 = jnp.max(x1, axis=0, keepdims=True)
        m2 = jnp.max(x2, axis=0, keepdims=True)
        v1_s[i:i + 1, :] = m1
        v2_s[i:i + 1, :] = m2
        marker = -RANK_MARK * (1.0 + i / PEER_TOPK)
        x1 = jnp.where(x1 == m1, marker, x1)
        x2 = jnp.where(x2 == m2, marker, x2)
    rank1 = jnp.where(x1 <= -RANK_MARK, (x1 * (-1.0 / RANK_MARK) - 1.0) * PEER_TOPK, float(PEER_TOPK))
    rank2 = jnp.where(x2 <= -RANK_MARK, (x2 * (-1.0 / RANK_MARK) - 1.0) * PEER_TOPK, float(PEER_TOPK))
    v1 = v1_s[...]
    v2 = v2_s[...]
    off = 0
    for i, cnt in enumerate(_PAIR_COUNTS):
        c_s[off:off + cnt, :] = v1[i:i + 1, :] + v2[0:cnt, :]
        off += cnt
    if off < _PAIR_ROWS:
        c_s[off:_PAIR_ROWS, :] = jnp.full((_PAIR_ROWS - off, v1.shape[1]), NEG, F32)
    c = c_s[...]
    mx = v1[0:1, :] + v2[0:1, :]
    z = jnp.zeros_like(mx)
    m = mx
    for i in range(PEER_TOPK):
        m = jnp.max(c, axis=0, keepdims=True)
        z = z + jnp.exp(m - mx)
        c = jnp.where(c == m, NEG, c)
    tau = m
    passing = jnp.zeros(v1.shape, F32)
    for jj in range(PEER_TOPK):
        passing = passing + jnp.where(v1 + v2[jj:jj + 1, :] >= tau, 1.0, 0.0)
    rr = jnp.zeros(rank1.shape, F32)
    for i in range(PEER_TOPK):
        rr = jnp.where(rank1 == float(i), passing[i:i + 1, :], rr)
    rk2_ref[0] = rank2.astype(BF16)
    rr_ref[0] = rr
    e2_ref[0] = jnp.exp(s2_ref[0] - v2[0:1, :]).astype(BF16)
    c_ref[0] = jnp.exp(s1_ref[0] - v1[0:1, :]) * (1.0 / z)


def _gate(s1t, s2t, tn):
    nh, nk, n = s1t.shape
    blk = lambda h, i: (h, 0, i)
    return pl.pallas_call(
        _gate_kernel,
        grid=(nh, n // tn),
        in_specs=[pl.BlockSpec((1, nk, tn), blk), pl.BlockSpec((1, nk, tn), blk)],
        out_specs=[pl.BlockSpec((1, nk, tn), blk)] * 4,
        out_shape=[jax.ShapeDtypeStruct((nh, nk, n), BF16), jax.ShapeDtypeStruct((nh, nk, n), BF16),
                   jax.ShapeDtypeStruct((nh, nk, n), F32), jax.ShapeDtypeStruct((nh, nk, n), F32)],
        scratch_shapes=[pltpu.VMEM((PEER_TOPK, tn), F32), pltpu.VMEM((PEER_TOPK, tn), F32),
                        pltpu.VMEM((_PAIR_ROWS, tn), F32)],
        compiler_params=_cparams(("parallel", "parallel")), name="peer_gate",
    )(s1t, s2t)


def _bf16_rows(row, n_rows):
    tile = jnp.broadcast_to(row, (16, row.shape[1])).astype(BF16)
    return jnp.concatenate([tile] * (n_rows // 16), axis=0)


def _peer_kernel(ht_ref, u_ref, vt_ref, rk2_ref, e2_ref, rr_ref, c_ref, xn_ref, fg_ref, o_ref,
                 acc_ref, f_ref, *, nh, nk, apc, final):
    j = pl.program_id(1)

    @pl.when(j == 0)
    def _():
        acc_ref[...] = jnp.zeros(acc_ref.shape, F32)

    ht = ht_ref[...]
    zero = jnp.zeros((), BF16)
    group = min(PEER_UP_ROWS, apc)
    for aa in range(apc):
        a = j * apc + aa
        if aa % group == 0:
            pre = _dot(u_ref[aa * nk:(aa + group) * nk, :], ht)
        k = aa % group
        act = _gelu(pre[k * nk:(k + 1) * nk, :]).astype(BF16)
        wgt = jnp.zeros(act.shape, BF16)
        for h in range(nh):
            limit = _bf16_rows(rr_ref[h, pl.ds(a, 1), :], nk)
            scale = _bf16_rows(c_ref[h, pl.ds(a, 1), :], nk)
            wgt = wgt + jnp.where(rk2_ref[h] < limit, e2_ref[h] * scale, zero)
        f_ref[aa * nk:(aa + 1) * nk, :] = wgt * act
    acc_ref[...] += _dot(vt_ref[...], f_ref[...])

    @pl.when(j == pl.num_programs(1) - 1)
    def _():
        out = xn_ref[...] + acc_ref[...].T
        if final:
            out = _rmsnorm(out, fg_ref[...])
        o_ref[...] = out


def _peer(ht, u, vt, rk2, e2, rr, c, xn, fg, tm, apc, final):
    d, n = ht.shape
    nh, nk, _ = rk2.shape
    assert nk % 16 == 0
    ne = u.shape[0]
    te = apc * nk
    tok3 = lambda i, j: (0, 0, i)
    return pl.pallas_call(
        functools.partial(_peer_kernel, nh=nh, nk=nk, apc=apc, final=final),
        grid=(n // tm, ne // te),
        in_specs=[pl.BlockSpec((d, tm), lambda i, j: (0, i)),
                  pl.BlockSpec((te, d), lambda i, j: (j, 0)),
                  pl.BlockSpec((d, te), lambda i, j: (0, j)),
                  pl.BlockSpec((nh, nk, tm), tok3),
                  pl.BlockSpec((nh, nk, tm), tok3),
                  pl.BlockSpec((nh, nk, tm), tok3),
                  pl.BlockSpec((nh, nk, tm), tok3),
                  pl.BlockSpec((tm, d), lambda i, j: (i, 0)),
                  pl.BlockSpec((1, d), lambda i, j: (0, 0))],
        out_specs=pl.BlockSpec((tm, d), lambda i, j: (i, 0)),
        out_shape=jax.ShapeDtypeStruct((n, d), F32),
        scratch_shapes=[pltpu.VMEM((d, tm), F32), pltpu.VMEM((te, tm), BF16)],
        compiler_params=_cparams(("parallel", "arbitrary")), name="peer",
    )(ht, u, vt, rk2, e2, rr, c, xn, fg)


def _rope_tables(pos):
    half = HEAD_DIM // 2
    inv_freq = ROPE_THETA ** (-jnp.arange(half, dtype=F32) / half)
    ang = pos.astype(F32)[:, None] * inv_freq[None, :]
    cos = jnp.cos(ang)
    sin = jnp.sin(ang)
    return jnp.tile(jnp.concatenate([cos, cos], axis=1), (1, 2)), jnp.tile(jnp.concatenate([-sin, sin], axis=1), (1, 2))


def _pad_rows(x, rows):
    return jnp.pad(x, ((0, rows - x.shape[0]),) + ((0, 0),) * (x.ndim - 1))


def _token_tile(n, pref):
    return pref if n % pref == 0 else n


def _page_pool_weights(wk, wv):
    wset = jnp.stack([wk[:, 0], wk[:, 1], wv[:, 0], wv[:, 1]]).reshape(4, 2, CMP_STRIDE)
    per_pos = jnp.tile(wset, (1, 1, LANES // CMP_STRIDE))
    chunk_of_pos = (jnp.arange(LANES) // CMP_STRIDE)[None, :] == jnp.arange(LANES // CMP_STRIDE)[:, None]
    return (per_pos[:, :, None, :] * chunk_of_pos[None, None].astype(F32)).reshape(64, LANES)


def kernel(x_prompt, x_sample, cache_nsa_kv, state_nsa_win, state_mlstm_C, state_mlstm_n, state_mlstm_m, state_conv, page_table, norm1_g, norm2_g, final_norm_g, w_in, w_out, mlstm_b_i, mlstm_b_f, mlstm_norm_g, conv_w, conv_b, conv_norm_g, conv_norm_b, nsa_cmp_wk, nsa_cmp_wv, gmlp_norm_g, gmlp_norm_b, gmlp_ws, gmlp_bs, peer_wq, peer_k1, peer_k2, peer_u, peer_v):
    depth = w_in.shape[0]
    bp, s_len, d = x_prompt.shape
    bd, t_dec, _ = x_sample.shape
    n_pool = cache_nsa_kv.shape[1]
    npages = page_table.shape[1]
    past = npages * cache_nsa_kv.shape[2]
    n_win = state_nsa_win.shape[2]
    assert t_dec == 1 and cache_nsa_kv.shape[2] == LANES and past % SEL_BLOCK == 0
    assert s_len >= WINDOW + 128 and n_win == WINDOW
    np_tok = bp * s_len
    ns_pad = LANES
    assert bd <= ns_pad

    xp = x_prompt.reshape(np_tok, d)
    xs = _pad_rows(x_sample.reshape(bd, d), ns_pad)
    cos_p, sin_p = _rope_tables(jnp.arange(s_len))
    cos_s, sin_s = _rope_tables(jnp.full((ns_pad,), past))
    pool_t = cache_nsa_kv.reshape(depth * n_pool, LANES, 4 * LANES).transpose(0, 2, 1)
    win_t = state_nsa_win.reshape(depth * bd, n_win, 2 * LANES).transpose(0, 2, 1)
    eye_h = jnp.eye(N_HEADS, dtype=F32)

    tm_in = _token_tile(s_len, 256)
    tm_out = _token_tile(np_tok, 256)
    tm_peer = _token_tile(np_tok, 512)
    nk = peer_k1.shape[2]
    apc = min(16, nk)
    w_in_t = w_in.transpose(2, 0, 1)

    new_p, new_s = [], []
    for l in range(depth):
        wi = w_in_t[:, l, :]
        w_perm = jnp.concatenate([wi[0:1024], wi[1032:2568], wi[2580:3092]], axis=0).astype(BF16)
        w_small = jnp.concatenate([wi[1024:1032], wi[2568:2580], jnp.zeros((108, d), F32)], axis=0).astype(BF16)
        g1 = norm1_g[l].reshape(1, d)
        g2 = norm2_g[l].reshape(1, d)
        gate_bias = jnp.concatenate([mlstm_b_i[l], mlstm_b_f[l], jnp.zeros((LANES - 2 * N_HEADS,), F32)]).reshape(1, LANES)
        mng = mlstm_norm_g[l].reshape(1, MIX_W)
        cw = conv_w[l]
        cbias = conv_b[l].reshape(1, MIX_W)
        cg = conv_norm_g[l].reshape(1, MIX_W)
        cbeta = conv_norm_b[l].reshape(1, MIX_W)
        w32 = jnp.concatenate([jnp.repeat(nsa_cmp_wk[l], HEAD_DIM, axis=1), jnp.repeat(nsa_cmp_wv[l], HEAD_DIM, axis=1)], axis=1)
        wlo, whi = w32[:CMP_STRIDE], w32[CMP_STRIDE:]
        wall = _page_pool_weights(nsa_cmp_wk[l], nsa_cmp_wv[l])
        gg = gmlp_norm_g[l].reshape(1, MIX_W)
        gbeta = gmlp_norm_b[l].reshape(1, MIX_W)
        gws = gmlp_ws[l]
        gbsb = jnp.repeat(gmlp_bs[l].T, HEAD_DIM, axis=1)
        wo = w_out[l].astype(BF16)
        wq = jnp.stack(_split_bf16(peer_wq[l]))
        k1 = jnp.stack(_split_bf16(peer_k1[l]))
        k2 = jnp.stack(_split_bf16(peer_k2[l]))
        ub = peer_u[l].astype(BF16)
        vtb = peer_v[l].astype(BF16).T
        fg = final_norm_g.reshape(1, d)
        final = l == depth - 1

        oa, ob, oq, okv, owin, okaug, ovsb, owinb, od, osm = _in_proj(xp, g1, w_perm, w_small, cos_p, sin_p, tm_in)
        ya, c_p, n_p, m_p = _mlstm(oa.reshape(bp, s_len, 1024), osm.reshape(bp, s_len, LANES), gate_bias, mng,
                                   jnp.zeros((bp, MIX_W, MIX_W), F32), jnp.zeros((bp, 1, MIX_W), F32),
                                   jnp.zeros((bp, 1, MIX_W), F32), MLSTM_CHUNK, MLSTM_CHUNK, bp)
        yb, conv_p = _conv(ob.reshape(bp, s_len, 2 * MIX_W), jnp.zeros((bp, CONV_PAD, MIX_W), F32),
                           cw, cbias, cg, cbeta, 512, 512)
        cb = _cmp(okv.reshape(bp, s_len, 4 * LANES), wlo, whi)
        yc = _nsa_prompt(oq.reshape(bp, s_len, MIX_W), osm.reshape(bp, s_len, LANES), cb,
                         okaug.reshape(bp, s_len, 2 * LANES), ovsb.reshape(bp, s_len, LANES),
                         owinb.reshape(bp, s_len, 2 * LANES), 128)
        yd, _ = _gmlp(od, gg, gbeta, gws, gbsb, 4)
        xn, ht, s1t, s2t = _out_proj(xp, ya.reshape(np_tok, MIX_W), yb.reshape(np_tok, MIX_W),
                                     yc.reshape(np_tok, MIX_W), yd, wo, g2, wq, k1, k2, tm_out)
        rk2, e2, rr, cw8 = _gate(s1t, s2t, _token_tile(np_tok, 512))
        xp = _peer(ht, ub, vtb, rk2, e2, rr, cw8, xn, fg, tm_peer, apc, final)
        new_p.append((okv.reshape(bp, s_len, 4, NSA_KV, HEAD_DIM),
                      owin.reshape(bp, s_len, 2, NSA_KV, HEAD_DIM)[:, s_len - n_win:],
                      jnp.stack([c_p[:, h * HEAD_DIM:(h + 1) * HEAD_DIM, h * HEAD_DIM:(h + 1) * HEAD_DIM]
                                 for h in range(N_HEADS)], axis=1),
                      n_p.reshape(bp, N_HEADS, HEAD_DIM),
                      m_p[:, 0, ::HEAD_DIM],
                      conv_p[:, CONV_PAD - (CONV_W - 1):]))

        sa, sb, sq, skv, swin, _, _, _, sd, ssm = _in_proj(xs, g1, w_perm, w_small, cos_s, sin_s, ns_pad)
        rows8 = lambda t: jnp.pad(t[:bd, None, :], ((0, 0), (0, 7), (0, 0)))
        c0 = jnp.einsum('bhvk,hg->bhvgk', state_mlstm_C[l], eye_h).reshape(bd, MIX_W, MIX_W)
        n0 = state_mlstm_n[l].reshape(bd, 1, MIX_W)
        m0 = jnp.repeat(state_mlstm_m[l], HEAD_DIM, axis=-1).reshape(bd, 1, MIX_W)
        bb_s = 2 if bd % 2 == 0 else 1
        ya_s, c_s, n_s, m_s = _mlstm(rows8(sa), rows8(ssm), gate_bias, mng, c0, n0, m0, 8, 1, bb_s)
        prefix = jnp.pad(state_conv[l], ((0, 0), (CONV_PAD - (CONV_W - 1), 0), (0, 0)))
        yb_s, conv_s = _conv(rows8(sb), prefix, cw, cbias, cg, cbeta, 8, 1)
        zd_s = jnp.pad(sd[:bd, None, :], ((0, 0), (0, GMLP_CHUNK - 1), (0, 0))).reshape(bd * GMLP_CHUNK, 2 * MIX_W)
        yd_s, v_s = _gmlp(zd_s, gg, gbeta, gws, gbsb, 1)
        q_heads = sq[:bd].reshape(bd, NSA_KV, 2, 1, HEAD_DIM)
        q4 = (q_heads * jnp.eye(NSA_KV, dtype=F32)[None, :, None, :, None]).reshape(bd, N_HEADS, LANES)
        q4 = jnp.pad(q4, ((0, 0), (0, 8 - N_HEADS), (0, 0)))
        gq = jnp.pad(ssm[:bd, 2 * N_HEADS:2 * N_HEADS + 3 * N_HEADS].reshape(bd, N_HEADS, 3),
                     ((0, 0), (0, 8 - N_HEADS), (0, LANES - 3)))
        assert npages % 4 == 0
        pp = next(c for c in (32, 16, 8, 4) if npages % c == 0)
        o8, win_s = _nsa_sample(page_table + l * n_pool, q4, gq, skv[:bd, None, :], swin[:bd, None, :],
                                swin[:bd, :, None], win_t, l * bd, wall, pool_t, pp)
        yc_s = jnp.concatenate([o8[:, h, (h // 2) * HEAD_DIM:(h // 2 + 1) * HEAD_DIM] for h in range(N_HEADS)], axis=1)
        xn_s, ht_s, s1t_s, s2t_s = _out_proj(xs, _pad_rows(ya_s[:, 0], ns_pad), _pad_rows(yb_s[:, 0], ns_pad),
                                             _pad_rows(yc_s, ns_pad),
                                             _pad_rows(yd_s.reshape(bd, GMLP_CHUNK, MIX_W)[:, 0], ns_pad),
                                             wo, g2, wq, k1, k2, ns_pad)
        rk2_s, e2_s, rr_s, cw8_s = _gate(s1t_s, s2t_s, ns_pad)
        xs = _peer(ht_s, ub, vtb, rk2_s, e2_s, rr_s, cw8_s, xn_s, fg, ns_pad, apc, final)
        new_s.append((skv[:bd].reshape(bd, 1, 4, NSA_KV, HEAD_DIM),
                      win_s.transpose(0, 2, 1).reshape(bd, n_win, 2, NSA_KV, HEAD_DIM),
                      jnp.stack([c_s[:, h * HEAD_DIM:(h + 1) * HEAD_DIM, h * HEAD_DIM:(h + 1) * HEAD_DIM]
                                 for h in range(N_HEADS)], axis=1),
                      n_s.reshape(bd, N_HEADS, HEAD_DIM),
                      m_s[:, 0, ::HEAD_DIM],
                      conv_s[:, CONV_PAD - (CONV_W - 1):],
                      v_s.reshape(bd, GMLP_CHUNK, MIX_W)[:, 0:1]))

    def stack(states, i):
        return jnp.stack([st[i] for st in states])

    y_prompt = xp.reshape(bp, s_len, d)
    y_sample = xs[:bd].reshape(bd, 1, d)
    return (y_prompt, y_sample,
            stack(new_p, 0), stack(new_s, 0),
            stack(new_p, 1), stack(new_s, 1),
            stack(new_p, 2), stack(new_s, 2),
            stack(new_p, 3), stack(new_s, 3),
            stack(new_p, 4), stack(new_s, 4),
            stack(new_p, 5), stack(new_s, 5),
            stack(new_s, 6))
```

```python
import functools

import jax
import jax.numpy as jnp
from jax import lax
from jax.experimental import pallas as pl
from jax.experimental.pallas import tpu as pltpu

F32 = jnp.float32
BF16 = jnp.bfloat16
HIGHEST = lax.Precision.HIGHEST

HEAD_DIM = 64
N_HEADS = 4
MIX_W = 256
NSA_KV = 2
NORM_EPS = 1e-6
NEG = -1e30
MLSTM_CHUNK = 128
CONV_W = 31
CONV_PAD = 32
CMP_STRIDE = 16
CMP_LEN = 32
SEL_BLOCK = 64
N_SELECT = 16
SEL_FORCE = 1e9
WINDOW = 512
ROPE_THETA = 10000.0
ATTN_SCALE = HEAD_DIM ** -0.5
GMLP_CHUNK = 128
PEER_TOPK = 16
RANK_MARK = 2.0 ** 100
PEER_UP_ROWS = 1
LANES = 128
VMEM_LIMIT = 48 * 1024 * 1024

NT = (((1,), (1,)), ((), ()))
TN = (((0,), (0,)), ((), ()))


def _cparams(sem):
    return pltpu.CompilerParams(dimension_semantics=sem, vmem_limit_bytes=VMEM_LIMIT)


def _iota(shape, dim):
    return lax.broadcasted_iota(jnp.int32, shape, dim)


def _sigmoid(x):
    return 1.0 / (1.0 + jnp.exp(-x))


def _gelu(x):
    c1 = -2.0 * 0.7978845608028654 * 1.4426950408889634
    c2 = c1 * 0.044715
    return x / (1.0 + jnp.exp2(x * (c1 + c2 * (x * x))))


def _dot(a, b, dims=None, precision=None):
    if dims is None:
        return jnp.dot(a, b, preferred_element_type=F32, precision=precision)
    return lax.dot_general(a, b, dims, preferred_element_type=F32, precision=precision)


def _block_ones(n, blk):
    return (_iota((n, n), 0) // blk == _iota((n, n), 1) // blk).astype(F32)


def _group_layernorm(x, ee, width):
    mu = _dot(x, ee, precision=HIGHEST) * (1.0 / width)
    d = x - mu
    var = _dot(d * d, ee, precision=HIGHEST) * (1.0 / width)
    return d * lax.rsqrt(var + NORM_EPS)


def _rmsnorm(x, g):
    return x * lax.rsqrt(jnp.mean(x * x, axis=-1, keepdims=True) + NORM_EPS) * g


def _masked_softmax(s, valid):
    sm = jnp.where(valid, s, NEG)
    p = jnp.where(valid, jnp.exp(sm - jnp.max(sm, axis=-1, keepdims=True)), 0.0)
    return p / jnp.maximum(jnp.sum(p, axis=-1, keepdims=True), 1e-30)


def _rope(x, cos, sin_signed):
    w = x.shape[1]
    fwd = pltpu.roll(x, w - HEAD_DIM // 2, 1)
    bwd = pltpu.roll(x, HEAD_DIM // 2, 1)
    first = (_iota(x.shape, 1) % HEAD_DIM) < HEAD_DIM // 2
    return x * cos + jnp.where(first, fwd, bwd) * sin_signed


def _in_kernel(x_ref, g_ref, w_ref, wsmall_ref, cos_ref, sin_ref,
               oa_ref, ob_ref, oq_ref, okv_ref, owin_ref, okaug_ref, ovsb_ref, owinb_ref, od_ref, os_ref, *, npb):
    tm = x_ref.shape[0]
    y = _rmsnorm(x_ref[...], g_ref[...])
    yb = y.astype(BF16)
    z = _dot(yb, w_ref[...], NT)
    os_ref[...] = _dot(yb, wsmall_ref[...], NT)
    oa_ref[...] = z[:, 0:1024]
    ob_ref[...] = z[:, 1024:1536]
    cos = cos_ref[...]
    sin = sin_ref[...]
    cos2 = jnp.concatenate([cos, cos], axis=1)
    sin2 = jnp.concatenate([sin, sin], axis=1)
    oq_ref[...] = _rope(z[:, 1536:1792], cos2, sin2) * ATTN_SCALE
    kc = _rope(z[:, 1792:1920], cos, sin)
    vc = z[:, 1920:2048]
    ks = _rope(z[:, 2048:2176], cos, sin)
    vs = z[:, 2176:2304]
    kw = _rope(z[:, 2304:2432], cos, sin)
    vw = z[:, 2432:2560]
    okv_ref[...] = jnp.concatenate([kc, vc, ks, vs], axis=1)
    owin = jnp.concatenate([kw, vw], axis=1)
    owin_ref[...] = owin
    pos = (pl.program_id(0) % npb) * tm + _iota((tm, LANES), 0)
    onehot = jnp.where(_iota((tm, LANES), 1) == pos // SEL_BLOCK, 1.0, 0.0)
    okaug_ref[...] = jnp.concatenate([ks, onehot], axis=1).astype(BF16)
    ovsb_ref[...] = vs.astype(BF16)
    owinb_ref[...] = owin.astype(BF16)
    od_ref[...] = z[:, 2560:3072]


def _in_proj(x, g, w, wsmall, cos_t, sin_t, tm):
    n, d = x.shape
    npb = cos_t.shape[0] // tm
    widths = (1024, 512, 256, 512, 256, 256, 128, 256, 512, 128)
    dtypes = (F32, F32, F32, F32, F32, BF16, BF16, BF16, F32, F32)
    row = lambda i: (i, 0)
    return pl.pallas_call(
        functools.partial(_in_kernel, npb=npb),
        grid=(n // tm,),
        in_specs=[pl.BlockSpec((tm, d), row),
                  pl.BlockSpec((1, d), lambda i: (0, 0)),
                  pl.BlockSpec(w.shape, lambda i: (0, 0)),
                  pl.BlockSpec(wsmall.shape, lambda i: (0, 0)),
                  pl.BlockSpec((tm, LANES), lambda i: (i % npb, 0)),
                  pl.BlockSpec((tm, LANES), lambda i: (i % npb, 0))],
        out_specs=[pl.BlockSpec((tm, wd), row) for wd in widths],
        out_shape=[jax.ShapeDtypeStruct((n, wd), dt) for wd, dt in zip(widths, dtypes)],
        compiler_params=_cparams(("parallel",)), name="in_proj",
    )(x, g, w, wsmall, cos_t, sin_t)


def _mlstm_kernel(za_ref, zs_ref, bias_ref, ng_ref, c0_ref, n0_ref, m0_ref,
                  y_ref, c_ref, n_ref, m_ref, *, L, t_valid, bb):
    @pl.when(pl.program_id(1) == 0)
    def _():
        c_ref[...] = c0_ref[...]
        n_ref[...] = n0_ref[...]
        m_ref[...] = m0_ref[...]

    head_of_lane = _iota((1, MIX_W), 1) // HEAD_DIM
    causal = _iota((L, L), 0) >= _iota((L, L), 1)
    tril = causal.astype(F32)
    ee = _block_ones(MIX_W, HEAD_DIM)
    bdiag = ee > 0.5
    row128 = _iota((L, LANES), 0)
    lane128 = _iota((L, LANES), 1)
    for bi in range(bb):
        za = za_ref[bi]
        q = za[:, 0:256]
        k = za[:, 256:512] * (HEAD_DIM ** -0.5)
        v = za[:, 512:768]
        o = za[:, 768:1024]
        gi = zs_ref[bi] + bias_ref[...]
        ls = jnp.minimum(gi, 0.0) - jnp.log(1.0 + jnp.exp(-jnp.abs(gi)))
        if t_valid < L:
            live = row128 < t_valid
            ig = jnp.where(live, gi, NEG)
            ls = jnp.where(live, ls, 0.0)
        else:
            ig = gi
        gmat = jnp.where(lane128 < N_HEADS, ig, 0.0)
        lfm = jnp.where((lane128 >= N_HEADS) & (lane128 < 2 * N_HEADS), ls, 0.0)
        fc = _dot(tril, lfm, precision=HIGHEST)
        g_t = gmat.T
        f_t = fc.T
        mrow = m_ref[bi]
        nrow = n_ref[bi]
        cm = c_ref[bi]
        qb = q.astype(BF16)
        kb = k.astype(BF16)
        vb = v.astype(BF16)
        num = jnp.zeros((L, MIX_W), F32)
        decay_b = jnp.zeros((L, MIX_W), F32)
        dens_b = jnp.zeros((L, MIX_W), F32)
        mt_b = jnp.zeros((L, MIX_W), F32)
        w_b = jnp.zeros((L, MIX_W), F32)
        cd_b = jnp.zeros((1, MIX_W), F32)
        mnew_b = jnp.zeros((1, MIX_W), F32)
        for h in range(N_HEADS):
            hm = head_of_lane == h
            f_col = fc[:, N_HEADS + h:N_HEADS + h + 1]
            ig_col = gmat[:, h:h + 1]
            f_row = f_t[N_HEADS + h:N_HEADS + h + 1, :]
            ig_row = g_t[h:h + 1, :]
            mp = mrow[:, h * HEAD_DIM:h * HEAD_DIM + 1]
            dmat = jnp.where(causal, (f_col - f_row) + ig_row, NEG)
            m_inter = mp + f_col
            m_t = jnp.maximum(m_inter, jnp.max(dmat, axis=1, keepdims=True))
            d_exp = jnp.exp(dmat - m_t)
            decay = jnp.exp(m_inter - m_t)
            qh = jnp.where(hm, q, 0.0).astype(BF16)
            s = _dot(qh, kb, NT) * d_exp
            num = jnp.where(hm, _dot(s.astype(BF16), vb), num)
            dens_b = jnp.where(hm, jnp.sum(s, axis=1, keepdims=True), dens_b)
            decay_b = jnp.where(hm, decay, decay_b)
            mt_b = jnp.where(hm, m_t, mt_b)
            f_last = f_col[L - 1:L, :]
            m_new = m_t[L - 1:L, :]
            w_b = jnp.where(hm, jnp.exp((f_last - f_col) + ig_col - m_new), w_b)
            cd_b = jnp.where(hm, jnp.exp(mp + f_last - m_new), cd_b)
            mnew_b = jnp.where(hm, m_new, mnew_b)
        inter = _dot(qb, cm.astype(BF16), NT)
        nq_b = _dot(q * nrow, ee, precision=HIGHEST)
        hnum = num + decay_b * inter
        den_b = dens_b + decay_b * nq_b
        hh = hnum / jnp.maximum(jnp.abs(den_b), jnp.exp(-mt_b))
        hn = _group_layernorm(hh, ee, HEAD_DIM) * ng_ref[...]
        y_ref[bi] = hn * _sigmoid(o)
        kw = k * w_b
        upd = _dot(vb, kw.astype(BF16), TN)
        c_ref[bi] = jnp.where(bdiag, cd_b * cm + upd, 0.0)
        n_ref[bi] = cd_b * nrow + jnp.sum(kw, axis=0, keepdims=True)
        m_ref[bi] = mnew_b


def _mlstm(za, zs, bias, ng, c0, n0, m0, L, t_valid, bb):
    b, t, _ = za.shape
    nc = t // L
    st = lambda i, c: (i, 0, 0)
    return pl.pallas_call(
        functools.partial(_mlstm_kernel, L=L, t_valid=t_valid, bb=bb),
        grid=(b // bb, nc),
        in_specs=[pl.BlockSpec((bb, L, 1024), lambda i, c: (i, c, 0)),
                  pl.BlockSpec((bb, L, LANES), lambda i, c: (i, c, 0)),
                  pl.BlockSpec((1, LANES), lambda i, c: (0, 0)),
                  pl.BlockSpec((1, MIX_W), lambda i, c: (0, 0)),
                  pl.BlockSpec((bb, MIX_W, MIX_W), st),
                  pl.BlockSpec((bb, 1, MIX_W), st),
                  pl.BlockSpec((bb, 1, MIX_W), st)],
        out_specs=[pl.BlockSpec((bb, L, MIX_W), lambda i, c: (i, c, 0)),
                   pl.BlockSpec((bb, MIX_W, MIX_W), st),
                   pl.BlockSpec((bb, 1, MIX_W), st),
                   pl.BlockSpec((bb, 1, MIX_W), st)],
        out_shape=[jax.ShapeDtypeStruct((b, t, MIX_W), F32),
                   jax.ShapeDtypeStruct((b, MIX_W, MIX_W), F32),
                   jax.ShapeDtypeStruct((b, 1, MIX_W), F32),
                   jax.ShapeDtypeStruct((b, 1, MIX_W), F32)],
        compiler_params=_cparams(("parallel", "arbitrary")), name="mlstm",
    )(za, zs, bias, ng, c0, n0, m0)


def _conv_kernel(zb_ref, pre_ref, w_ref, cb_ref, g_ref, beta_ref, y_ref, st_ref, ext_ref, *, tt, tv):
    @pl.when(pl.program_id(1) == 0)
    def _():
        ext_ref[0:CONV_PAD, :] = pre_ref[0]

    z = zb_ref[0]
    u = z[:, :MIX_W] * _sigmoid(z[:, MIX_W:])
    ext_ref[CONV_PAD:CONV_PAD + tt, :] = u
    off = CONV_PAD - (CONV_W - 1)
    acc = jnp.zeros((tt, MIX_W), F32)
    for j in range(CONV_W):
        acc = acc + w_ref[j:j + 1, :] * ext_ref[off + j:off + j + tt, :]
    ee = _block_ones(MIX_W, HEAD_DIM)
    y = _group_layernorm(acc + cb_ref[...], ee, HEAD_DIM) * g_ref[...] + beta_ref[...]
    y_ref[0] = y * _sigmoid(y)
    st_ref[0] = ext_ref[tv:tv + CONV_PAD, :]
    ext_ref[0:CONV_PAD, :] = ext_ref[tt:tt + CONV_PAD, :]


def _conv(zb, prefix, w, cb, g, beta, tt, tv):
    b, t, _ = zb.shape
    vec = lambda i, j: (0, 0)
    return pl.pallas_call(
        functools.partial(_conv_kernel, tt=tt, tv=tv),
        grid=(b, t // tt),
        in_specs=[pl.BlockSpec((1, tt, 2 * MIX_W), lambda i, j: (i, j, 0)),
                  pl.BlockSpec((1, CONV_PAD, MIX_W), lambda i, j: (i, 0, 0)),
                  pl.BlockSpec((CONV_W, MIX_W), vec),
                  pl.BlockSpec((1, MIX_W), vec),
                  pl.BlockSpec((1, MIX_W), vec),
                  pl.BlockSpec((1, MIX_W), vec)],
        out_specs=[pl.BlockSpec((1, tt, MIX_W), lambda i, j: (i, j, 0)),
                   pl.BlockSpec((1, CONV_PAD, MIX_W), lambda i, j: (i, 0, 0))],
        out_shape=[jax.ShapeDtypeStruct((b, t, MIX_W), F32),
                   jax.ShapeDtypeStruct((b, CONV_PAD, MIX_W), F32)],
        scratch_shapes=[pltpu.VMEM((CONV_PAD + tt, MIX_W), F32)],
        compiler_params=_cparams(("parallel", "arbitrary")), name="conv",
    )(zb, prefix, w, cb, g, beta)


def _gmlp_kernel(zd_ref, g_ref, beta_ref, ws_ref, bsb_ref, y_ref, v_ref, *, cpb):
    z = zd_ref[...]
    ee = _block_ones(MIX_W, HEAD_DIM)
    u = _gelu(z[:, :MIX_W])
    vv = _group_layernorm(_gelu(z[:, MIX_W:]), ee, HEAD_DIM) * g_ref[...] + beta_ref[...]
    v_ref[...] = vv
    tril = _iota((GMLP_CHUNK, GMLP_CHUNK), 0) >= _iota((GMLP_CHUNK, GMLP_CHUNK), 1)
    head_of_lane = _iota((1, MIX_W), 1) // HEAD_DIM
    wm = [jnp.where(tril, ws_ref[h], 0.0).astype(BF16) for h in range(N_HEADS)]
    for c in range(cpb):
        rows = slice(c * GMLP_CHUNK, (c + 1) * GMLP_CHUNK)
        vc = vv[rows].astype(BF16)
        mixed = jnp.zeros((GMLP_CHUNK, MIX_W), F32)
        for h in range(N_HEADS):
            mixed = jnp.where(head_of_lane == h, _dot(wm[h], vc), mixed)
        y_ref[rows, :] = u[rows] * (mixed + bsb_ref[...])


def _gmlp(zd, g, beta, ws, bsb, cpb):
    n = zd.shape[0]
    rows = cpb * GMLP_CHUNK
    vec = lambda i: (0, 0)
    return pl.pallas_call(
        functools.partial(_gmlp_kernel, cpb=cpb),
        grid=(n // rows,),
        in_specs=[pl.BlockSpec((rows, 2 * MIX_W), lambda i: (i, 0)),
                  pl.BlockSpec((1, MIX_W), vec),
                  pl.BlockSpec((1, MIX_W), vec),
                  pl.BlockSpec((N_HEADS, GMLP_CHUNK, GMLP_CHUNK), lambda i: (0, 0, 0)),
                  pl.BlockSpec((GMLP_CHUNK, MIX_W), vec)],
        out_specs=[pl.BlockSpec((rows, MIX_W), lambda i: (i, 0)),
                   pl.BlockSpec((rows, MIX_W), lambda i: (i, 0))],
        out_shape=[jax.ShapeDtypeStruct((n, MIX_W), F32),
                   jax.ShapeDtypeStruct((n, MIX_W), F32)],
        compiler_params=_cparams(("parallel",)), name="gmlp",
    )(zd, g, beta, ws, bsb)


def _combine_lo_hi(lo, hi):
    r = lo.shape[0]
    nxt = pltpu.roll(hi, r - 1, 0)
    return lo + jnp.where(_iota(lo.shape, 0) < r - 1, nxt, 0.0)


def _cmp_kernel(kv_ref, wlo_ref, whi_ref, o_ref):
    x = kv_ref[0]
    x3 = x.reshape(x.shape[0] // CMP_STRIDE, CMP_STRIDE, x.shape[1])
    lo = jnp.sum(x3 * wlo_ref[...][None], axis=1)
    hi = jnp.sum(x3 * whi_ref[...][None], axis=1)
    o_ref[0] = _combine_lo_hi(lo, hi)


def _cmp(okv3, wlo, whi):
    b, s, _ = okv3.shape
    return pl.pallas_call(
        _cmp_kernel,
        grid=(b,),
        in_specs=[pl.BlockSpec((1, s, 2 * LANES), lambda i: (i, 0, 0)),
                  pl.BlockSpec((CMP_STRIDE, 2 * LANES), lambda i: (0, 0)),
                  pl.BlockSpec((CMP_STRIDE, 2 * LANES), lambda i: (0, 0))],
        out_specs=pl.BlockSpec((1, s // CMP_STRIDE, 2 * LANES), lambda i: (i, 0, 0)),
        out_shape=jax.ShapeDtypeStruct((b, s // CMP_STRIDE, 2 * LANES), F32),
        compiler_params=_cparams(("parallel",)), name="nsa_cmp",
    )(okv3, wlo, whi)


def _nsa_prompt_kernel(q_ref, gs_ref, cb_ref, ka_ref, vs_ref, win_ref, y_ref, s_buf, *, tq, s_len, n_top, tk):
    start = pl.program_id(1) * tq
    nc = s_len // CMP_STRIDE
    ns = s_len // SEL_BLOCK
    q = q_ref[0]
    gl = _sigmoid(gs_ref[0])
    lo_half = _iota((tq, LANES), 1) < HEAD_DIM
    q_rows = []
    for g in range(NSA_KV):
        qpair = q[:, g * LANES:(g + 1) * LANES]
        swapped = pltpu.roll(qpair, HEAD_DIM, 1)
        if g == 0:
            q_rows += [jnp.where(lo_half, qpair, 0.0), jnp.where(lo_half, swapped, 0.0)]
        else:
            q_rows += [jnp.where(lo_half, 0.0, swapped), jnp.where(lo_half, 0.0, qpair)]
    q128 = jnp.concatenate(q_rows, axis=0)
    q128b = q128.astype(BF16)
    trow = start + _iota((tq, 1), 0)
    t2 = jnp.concatenate([trow, trow], axis=0)
    t4 = jnp.concatenate([t2, t2], axis=0)
    cb = cb_ref[0]
    sc = _dot(q128b, cb[:, 0:LANES].astype(BF16), NT)
    validc = (_iota((1, nc), 1) * CMP_STRIDE + CMP_LEN) <= (t4 + 1)
    p = _masked_softmax(sc, validc)
    o_c = _dot(p.astype(BF16), cb[:, LANES:2 * LANES].astype(BF16))
    psum = jnp.concatenate([p[0:tq] + p[tq:2 * tq], p[2 * tq:3 * tq] + p[3 * tq:4 * tq]], axis=0)
    pool4 = (_iota((nc, ns), 0) // (SEL_BLOCK // CMP_STRIDE) == _iota((nc, ns), 1)).astype(F32)
    imp = _dot(psum, pool4, precision=HIGHEST)
    jidx = _iota((1, ns), 1)
    imp = jnp.where(jidx == t2 // SEL_BLOCK, SEL_FORCE, imp)
    imp = jnp.where(jidx * SEL_BLOCK <= t2, imp, -1.0)
    if n_top < ns:
        x = jnp.concatenate([imp[0:tq].T, imp[tq:2 * tq].T], axis=1)
        jrow = _iota((ns, 2 * tq), 0).astype(F32)
        for _ in range(n_top):
            m = jnp.max(x, axis=0, keepdims=True)
            first = jnp.min(jnp.where(x == m, jrow, float(ns)), axis=0, keepdims=True)
            x = jnp.where(jrow == first, -3.0, x)
        sel_t = jnp.where(x == -3.0, 1.0, 0.0)
        sel = jnp.concatenate([sel_t[:, 0:tq].T, sel_t[:, tq:2 * tq].T], axis=0)
    else:
        sel = jnp.ones((2 * tq, ns), F32)
    selneg = (sel - 1.0) * (-NEG)
    if ns < LANES:
        selneg = jnp.concatenate([selneg, jnp.zeros((2 * tq, LANES - ns), F32)], axis=1)
    selneg4 = jnp.concatenate([selneg[0:tq], selneg[0:tq], selneg[tq:2 * tq], selneg[tq:2 * tq]], axis=0)
    qaug = jnp.concatenate([q128, selneg4], axis=1).astype(BF16)

    def scores(c):
        return _dot(qaug, ka_ref[0, pl.ds(pl.multiple_of(c * tk, tk), tk), :], NT)

    def absorb(c, s, carry):
        m, l, acc = carry
        m_new = jnp.maximum(m, jnp.max(s, axis=1, keepdims=True))
        pt = jnp.exp(s - m_new)
        alpha = jnp.exp(m - m_new)
        l = alpha * l + jnp.sum(pt, axis=1, keepdims=True)
        acc = alpha * acc + _dot(pt.astype(BF16), vs_ref[0, pl.ds(pl.multiple_of(c * tk, tk), tk), :])
        return m_new, l, acc

    def tile(c, carry):
        s = s_buf[...]
        s_buf[...] = scores(c + 1)
        return absorb(c, s, carry)

    n_tiles = (start + tq + tk - 1) // tk
    carry = (jnp.full((4 * tq, 1), NEG, F32), jnp.zeros((4 * tq, 1), F32), jnp.zeros((4 * tq, LANES), F32))
    s_buf[...] = scores(0)
    carry = lax.fori_loop(0, n_tiles - 1, tile, carry)
    last = n_tiles - 1
    s_diag = jnp.where(last * tk + _iota((1, tk), 1) <= t4, s_buf[...], NEG)
    _, l, acc = absorb(last, s_diag, carry)
    o_s = acc / jnp.maximum(l, 1e-30)
    wl = WINDOW + tq
    w0 = pl.multiple_of(jnp.maximum(start - WINDOW, 0), tq)
    sw = _dot(q128b, win_ref[0, pl.ds(w0, wl), 0:LANES], NT)
    kposw = w0 + _iota((1, wl), 1)
    validw = (kposw <= t4) & (kposw > t4 - WINDOW)
    o_w = _dot(_masked_softmax(sw, validw).astype(BF16), win_ref[0, pl.ds(w0, wl), LANES:2 * LANES])
    heads = []
    for hh in range(N_HEADS):
        col = 2 * N_HEADS + hh * 3
        rows = slice(hh * tq, (hh + 1) * tq)
        heads.append(o_c[rows] * gl[:, col:col + 1] + o_s[rows] * gl[:, col + 1:col + 2]
                     + o_w[rows] * gl[:, col + 2:col + 3])
    y_ref[0] = jnp.concatenate([jnp.where(lo_half, heads[0], pltpu.roll(heads[1], HEAD_DIM, 1)),
                                jnp.where(lo_half, pltpu.roll(heads[2], HEAD_DIM, 1), heads[3])], axis=1)


def _nsa_prompt(oq3, os3, cb, okaug3, ovsb3, owinb3, tq):
    b, s, _ = oq3.shape
    ns = s // SEL_BLOCK
    assert ns <= LANES
    n_top = min(N_SELECT, ns)
    tk = min(512, s)
    full = lambda i, j: (i, 0, 0)
    return pl.pallas_call(
        functools.partial(_nsa_prompt_kernel, tq=tq, s_len=s, n_top=n_top, tk=tk),
        grid=(b, s // tq),
        in_specs=[pl.BlockSpec((1, tq, MIX_W), lambda i, j: (i, j, 0)),
                  pl.BlockSpec((1, tq, LANES), lambda i, j: (i, j, 0)),
                  pl.BlockSpec((1, s // CMP_STRIDE, 2 * LANES), full),
                  pl.BlockSpec((1, s, 2 * LANES), full),
                  pl.BlockSpec((1, s, LANES), full),
                  pl.BlockSpec((1, s, 2 * LANES), full)],
        out_specs=pl.BlockSpec((1, tq, MIX_W), lambda i, j: (i, j, 0)),
        out_shape=jax.ShapeDtypeStruct((b, s, MIX_W), F32),
        scratch_shapes=[pltpu.VMEM((N_HEADS * tq, tk), F32)],
        compiler_params=_cparams(("parallel", "arbitrary")), name="nsa_prompt",
    )(oq3, os3, cb, okaug3, ovsb3, owinb3)


def _nsa_sample_kernel(pt_ref, q4_ref, gq_ref, newkv_ref, newwin_ref, newcol_ref, win_ref, wall_ref, *rest,
                       pp, npages, n_top):
    pool_refs = rest[:pp]
    o_ref, wo_ref, lo_s, hi_s, m_s, l_s, a_s, a_stage = rest[pp:]
    i = pl.program_id(1)
    nb = 2 * npages
    nbp = m_s.shape[1]
    ncp = npages * (LANES // CMP_STRIDE)
    past = npages * LANES
    q4 = q4_ref[0]
    q4b = q4.astype(BF16)

    @pl.when(i == 0)
    def _():
        m_s[...] = jnp.zeros(m_s.shape, F32)
        l_s[...] = jnp.zeros(l_s.shape, F32)
        if nb < nbp:
            a_s[...] = jnp.zeros(a_s.shape, F32)

    lane8 = _iota((8, LANES), 1)
    lanej = _iota((8, nbp), 1)
    set_of_lane = _iota((8, 2 * LANES), 1) // HEAD_DIM
    bpp = LANES // SEL_BLOCK
    wall = wall_ref[...].astype(BF16)
    m_acc = m_s[...]
    l_acc = l_s[...]
    los, his = [], []
    for kk in range(pp):
        page = pool_refs[kk][0]
        pooled = _dot(wall, page[0:2 * LANES, :].astype(BF16), NT)
        lo8 = jnp.zeros((8, 2 * LANES), F32)
        hi8 = jnp.zeros((8, 2 * LANES), F32)
        for st in range(4):
            lo8 = jnp.where(set_of_lane == st, pooled[st * 16:st * 16 + 8], lo8)
            hi8 = jnp.where(set_of_lane == st, pooled[st * 16 + 8:st * 16 + 16], hi8)
        los.append(lo8)
        his.append(hi8)
        ks_t = page[2 * LANES:3 * LANES, :].astype(BF16)
        vs_t = page[3 * LANES:4 * LANES, :].astype(BF16)
        s = _dot(q4b, ks_t)
        probs = []
        for jb in range(bpp):
            inb = (lane8 >= jb * SEL_BLOCK) & (lane8 < (jb + 1) * SEL_BLOCK)
            sm = jnp.where(inb, s, NEG)
            m = jnp.max(sm, axis=1, keepdims=True)
            p = jnp.where(inb, jnp.exp(sm - m), 0.0)
            probs.append(p)
            j = i * (pp * bpp) + kk * bpp + jb
            m_acc = jnp.where(lanej == j, m, m_acc)
            l_acc = jnp.where(lanej == j, jnp.sum(p, axis=1, keepdims=True), l_acc)
        a = _dot(jnp.concatenate(probs, axis=0).astype(BF16), vs_t, NT)
        for jb in range(bpp):
            for hh in range(N_HEADS):
                a_stage[hh, kk * bpp + jb:kk * bpp + jb + 1, :] = a[jb * 8 + hh:jb * 8 + hh + 1, :]
    m_s[...] = m_acc
    l_s[...] = l_acc
    r0 = pl.multiple_of(i * (pp * 8), pp * 8)
    lo_s[pl.ds(r0, pp * 8), :] = jnp.concatenate(los, axis=0)
    hi_s[pl.ds(r0, pp * 8), :] = jnp.concatenate(his, axis=0)
    b0 = pl.multiple_of(i * (pp * bpp), pp * bpp)
    for hh in range(N_HEADS):
        a_s[hh, pl.ds(b0, pp * bpp), :] = a_stage[hh]

    @pl.when(i == pl.num_programs(1) - 1)
    def _():
        row8 = _iota((8, 1), 0)
        cbm = _combine_lo_hi(lo_s[...], hi_s[...])
        kcb = cbm[:, 0:LANES].astype(BF16)
        vcb = cbm[:, LANES:2 * LANES].astype(BF16)
        sc = _dot(q4b, kcb, NT)
        validc = (_iota((1, ncp), 1) * CMP_STRIDE + CMP_LEN) <= past + 1
        p = _masked_softmax(sc, validc)
        o_c = _dot(p.astype(BF16), vcb)
        pg = jnp.where(row8 == 0, p[0:1] + p[1:2], jnp.where(row8 == 1, p[2:3] + p[3:4], 0.0))
        pool4 = (_iota((ncp, nbp), 0) // (SEL_BLOCK // CMP_STRIDE) == _iota((ncp, nbp), 1)).astype(F32)
        imp2 = _dot(pg, pool4, precision=HIGHEST)
        ii = _iota((nbp, nbp), 0)
        jj = _iota((nbp, nbp), 1)
        sels = []
        for g in range(NSA_KV):
            mx = jnp.broadcast_to(imp2[g:g + 1, :], (nbp, nbp))
            mt = mx.T
            beats = ((mt > mx) | ((mt == mx) & (ii < jj))) & (ii < nb)
            rank = jnp.sum(jnp.where(beats, 1.0, 0.0), axis=0, keepdims=True)
            sels.append(jnp.where((rank < n_top - 1) & (_iota((1, nbp), 1) < nb), 1.0, 0.0))
        sel8 = jnp.where(row8 < 2, sels[0], sels[1]) > 0.5
        newkv = newkv_ref[0]
        ksn = newkv[:, 2 * LANES:3 * LANES]
        vsn = newkv[:, 3 * LANES:4 * LANES]
        s_new = jnp.sum(q4 * ksn, axis=1, keepdims=True)
        mrow = m_s[...]
        m_all = jnp.maximum(jnp.max(jnp.where(sel8, mrow, NEG), axis=1, keepdims=True), s_new)
        wj = jnp.where(sel8, jnp.exp(mrow - m_all), 0.0)
        w_new = jnp.exp(s_new - m_all)
        ltot = jnp.sum(wj * l_s[...], axis=1, keepdims=True) + w_new
        osum = w_new * vsn
        for hh in range(N_HEADS):
            osum = osum + jnp.where(row8 == hh, _dot(wj, a_s[hh], precision=HIGHEST), 0.0)
        o_s = osum / jnp.maximum(ltot, 1e-30)
        win_t = win_ref[0]
        nw = win_t.shape[1]
        lane_w = _iota((1, nw), 1)
        sw = _dot(q4b, win_t[0:LANES, :].astype(BF16))
        validw = (past - nw + lane_w) > past - WINDOW
        neww = newwin_ref[0]
        sw_new = jnp.sum(q4 * neww[:, 0:LANES], axis=1, keepdims=True)
        mw = jnp.maximum(jnp.max(jnp.where(validw, sw, NEG), axis=1, keepdims=True), sw_new)
        pw = jnp.where(validw, jnp.exp(sw - mw), 0.0)
        pn = jnp.exp(sw_new - mw)
        zw = jnp.sum(pw, axis=1, keepdims=True) + pn
        o_w = (_dot(pw.astype(BF16), win_t[LANES:2 * LANES, :].astype(BF16), NT) + pn * neww[:, LANES:2 * LANES]) / zw
        gg = _sigmoid(gq_ref[0])
        o_ref[0] = o_c * gg[:, 0:1] + o_s * gg[:, 1:2] + o_w * gg[:, 2:3]
        wo_ref[0] = jnp.where(lane_w == nw - 1, newcol_ref[0], pltpu.roll(win_t, nw - 1, 1))


def _nsa_sample(pt, q4, gq, newkv, newwin, newcol, win_t, win_off, wall, pool_t, pp):
    bd, npages = pt.shape
    nw = win_t.shape[2]
    nb = 2 * npages
    nbp = -(-nb // LANES) * LANES
    n_top = min(N_SELECT, nb + 1)
    per_b = lambda b, i, pt_ref: (b, 0, 0)
    const = lambda b, i, pt_ref: (0, 0)

    def page_map(kk):
        return lambda b, i, pt_ref: (pt_ref[b, i * pp + kk], 0, 0)

    grid_spec = pltpu.PrefetchScalarGridSpec(
        num_scalar_prefetch=1,
        grid=(bd, npages // pp),
        in_specs=[pl.BlockSpec((1, 8, LANES), per_b),
                  pl.BlockSpec((1, 8, LANES), per_b),
                  pl.BlockSpec((1, 1, 4 * LANES), per_b),
                  pl.BlockSpec((1, 1, 2 * LANES), per_b),
                  pl.BlockSpec((1, 2 * LANES, 1), per_b),
                  pl.BlockSpec((1, 2 * LANES, nw), lambda b, i, pt_ref: (win_off + b, 0, 0)),
                  pl.BlockSpec(wall.shape, const)]
                 + [pl.BlockSpec((1, 4 * LANES, LANES), page_map(kk)) for kk in range(pp)],
        out_specs=[pl.BlockSpec((1, 8, LANES), per_b),
                   pl.BlockSpec((1, 2 * LANES, nw), per_b)],
        scratch_shapes=[pltpu.VMEM((npages * 8, 2 * LANES), F32),
                        pltpu.VMEM((npages * 8, 2 * LANES), F32),
                        pltpu.VMEM((8, nbp), F32),
                        pltpu.VMEM((8, nbp), F32),
                        pltpu.VMEM((N_HEADS, nbp, LANES), F32),
                        pltpu.VMEM((N_HEADS, pp * (LANES // SEL_BLOCK), LANES), F32)])
    return pl.pallas_call(
        functools.partial(_nsa_sample_kernel, pp=pp, npages=npages, n_top=n_top),
        grid_spec=grid_spec,
        out_shape=[jax.ShapeDtypeStruct((bd, 8, LANES), F32),
                   jax.ShapeDtypeStruct((bd, 2 * LANES, nw), F32)],
        compiler_params=_cparams(("parallel", "arbitrary")), name="nsa_sample",
    )(pt, q4, gq, newkv, newwin, newcol, win_t, wall, *([pool_t] * pp))


def _split_bf16(x):
    hi = x.astype(BF16)
    return hi, (x - hi.astype(F32)).astype(BF16)


def _out_kernel(x_ref, ya_ref, yb_ref, yc_ref, yd_ref, wo_ref, g2_ref, wq_ref, k1_ref, k2_ref,
                xn_ref, ht_ref, s1_ref, s2_ref, *, nh, dq):
    acc = x_ref[...]
    for idx, y_ref in enumerate((ya_ref, yb_ref, yc_ref, yd_ref)):
        acc = acc + _dot(y_ref[...].astype(BF16), wo_ref[idx * MIX_W:(idx + 1) * MIX_W, :])
    xn_ref[...] = acc
    h2 = _rmsnorm(acc, g2_ref[...])
    ht_ref[...] = h2.T.astype(BF16)
    h_hi, h_lo = _split_bf16(h2)
    q = _dot(h_hi, wq_ref[0]) + _dot(h_lo, wq_ref[0]) + _dot(h_hi, wq_ref[1])
    q_hi, q_lo = _split_bf16(q)
    half = dq // 2
    for h in range(nh):
        for s_ref, k_ref, cols in ((s1_ref, k1_ref, slice(h * dq, h * dq + half)),
                                   (s2_ref, k2_ref, slice(h * dq + half, (h + 1) * dq))):
            s_ref[h] = (_dot(k_ref[0, h], q_hi[:, cols], NT) + _dot(k_ref[1, h], q_hi[:, cols], NT)
                        + _dot(k_ref[0, h], q_lo[:, cols], NT))


def _out_proj(x, ya, yb, yc, yd, wo, g2, wq, k1, k2, tm):
    n, d = x.shape
    _, nh, nk, half = k1.shape
    row = lambda i: (i, 0)
    c2 = lambda i: (0, 0)
    return pl.pallas_call(
        functools.partial(_out_kernel, nh=nh, dq=2 * half),
        grid=(n // tm,),
        in_specs=[pl.BlockSpec((tm, d), row)] + [pl.BlockSpec((tm, MIX_W), row)] * 4
                 + [pl.BlockSpec(wo.shape, c2), pl.BlockSpec((1, d), c2), pl.BlockSpec(wq.shape, lambda i: (0, 0, 0)),
                    pl.BlockSpec(k1.shape, lambda i: (0, 0, 0, 0)), pl.BlockSpec(k2.shape, lambda i: (0, 0, 0, 0))],
        out_specs=[pl.BlockSpec((tm, d), row),
                   pl.BlockSpec((d, tm), lambda i: (0, i)),
                   pl.BlockSpec((nh, nk, tm), lambda i: (0, 0, i)),
                   pl.BlockSpec((nh, nk, tm), lambda i: (0, 0, i))],
        out_shape=[jax.ShapeDtypeStruct((n, d), F32),
                   jax.ShapeDtypeStruct((d, n), BF16),
                   jax.ShapeDtypeStruct((nh, nk, n), F32),
                   jax.ShapeDtypeStruct((nh, nk, n), F32)],
        compiler_params=_cparams(("parallel",)), name="out_proj",
    )(x, ya, yb, yc, yd, wo, g2, wq, k1, k2)


_PAIR_COUNTS = tuple(PEER_TOPK // (i + 1) for i in range(PEER_TOPK))
_PAIR_ROWS = -(-sum(_PAIR_COUNTS) // 8) * 8


def _gate_kernel(s1_ref, s2_ref, rk2_ref, e2_ref, rr_ref, c_ref, v1_s, v2_s, c_s):
    x1 = s1_ref[0]
    x2 = s2_ref[0]
    for i in range(PEER_TOPK):
        m1 = jnp.max(x1, axis=0, keepdims=True)
        m2 = jnp.max(x2, axis=0, keepdims=True)
        v1_s[i:i + 1, :] = m1
        v2_s[i:i + 1, :] = m2
        marker = -RANK_MARK * (1.0 + i / PEER_TOPK)
        x1 = jnp.where(x1 == m1, marker, x1)
        x2 = jnp.where(x2 == m2, marker, x2)
    rank1 = jnp.where(x1 <= -RANK_MARK, (x1 * (-1.0 / RANK_MARK) - 1.0) * PEER_TOPK, float(PEER_TOPK))
    rank2 = jnp.where(x2 <= -RANK_MARK, (x2 * (-1.0 / RANK_MARK) - 1.0) * PEER_TOPK, float(PEER_TOPK))
    v1 = v1_s[...]
    v2 = v2_s[...]
    off = 0
    for i, cnt in enumerate(_PAIR_COUNTS):
        c_s[off:off + cnt, :] = v1[i:i + 1, :] + v2[0:cnt, :]
        off += cnt
    if off < _PAIR_ROWS:
        c_s[off:_PAIR_ROWS, :] = jnp.full((_PAIR_ROWS - off, v1.shape[1]), NEG, F32)
    c = c_s[...]
    mx = v1[0:1, :] + v2[0:1, :]
    z = jnp.zeros_like(mx)
    m = mx
    for i in range(PEER_TOPK):
        m = jnp.max(c, axis=0, keepdims=True)
        z = z + jnp.exp(m - mx)
        c = jnp.where(c == m, NEG, c)
    tau = m
    passing = jnp.zeros(v1.shape, F32)
    for jj in range(PEER_TOPK):
        passing = passing + jnp.where(v1 + v2[jj:jj + 1, :] >= tau, 1.0, 0.0)
    rr = jnp.zeros(rank1.shape, F32)
    for i in range(PEER_TOPK):
        rr = jnp.where(rank1 == float(i), passing[i:i + 1, :], rr)
    rk2_ref[0] = rank2.astype(BF16)
    rr_ref[0] = rr
    e2_ref[0] = jnp.exp(s2_ref[0] - v2[0:1, :]).astype(BF16)
    c_ref[0] = jnp.exp(s1_ref[0] - v1[0:1, :]) * (1.0 / z)


def _gate(s1t, s2t, tn):
    nh, nk, n = s1t.shape
    blk = lambda h, i: (h, 0, i)
    return pl.pallas_call(
        _gate_kernel,
        grid=(nh, n // tn),
        in_specs=[pl.BlockSpec((1, nk, tn), blk), pl.BlockSpec((1, nk, tn), blk)],
        out_specs=[pl.BlockSpec((1, nk, tn), blk)] * 4,
        out_shape=[jax.ShapeDtypeStruct((nh, nk, n), BF16), jax.ShapeDtypeStruct((nh, nk, n), BF16),
                   jax.ShapeDtypeStruct((nh, nk, n), F32), jax.ShapeDtypeStruct((nh, nk, n), F32)],
        scratch_shapes=[pltpu.VMEM((PEER_TOPK, tn), F32), pltpu.VMEM((PEER_TOPK, tn), F32),
                        pltpu.VMEM((_PAIR_ROWS, tn), F32)],
        compiler_params=_cparams(("parallel", "parallel")), name="peer_gate",
    )(s1t, s2t)


def _bf16_rows(row, n_rows):
    tile = jnp.broadcast_to(row, (16, row.shape[1])).astype(BF16)
    return jnp.concatenate([tile] * (n_rows // 16), axis=0)


def _peer_kernel(ht_ref, u_ref, vt_ref, rk2_ref, e2_ref, rr_ref, c_ref, xn_ref, fg_ref, o_ref,
                 acc_ref, f_ref, *, nh, nk, apc, final):
    j = pl.program_id(1)

    @pl.when(j == 0)
    def _():
        acc_ref[...] = jnp.zeros(acc_ref.shape, F32)

    ht = ht_ref[...]
    zero = jnp.zeros((), BF16)
    group = min(PEER_UP_ROWS, apc)
    for aa in range(apc):
        a = j * apc + aa
        if aa % group == 0:
            pre = _dot(u_ref[aa * nk:(aa + group) * nk, :], ht)
        k = aa % group
        act = _gelu(pre[k * nk:(k + 1) * nk, :]).astype(BF16)
        wgt = jnp.zeros(act.shape, BF16)
        for h in range(nh):
            limit = _bf16_rows(rr_ref[h, pl.ds(a, 1), :], nk)
            scale = _bf16_rows(c_ref[h, pl.ds(a, 1), :], nk)
            wgt = wgt + jnp.where(rk2_ref[h] < limit, e2_ref[h] * scale, zero)
        f_ref[aa * nk:(aa + 1) * nk, :] = wgt * act
    acc_ref[...] += _dot(vt_ref[...], f_ref[...])

    @pl.when(j == pl.num_programs(1) - 1)
    def _():
        out = xn_ref[...] + acc_ref[...].T
        if final:
            out = _rmsnorm(out, fg_ref[...])
        o_ref[...] = out


def _peer(ht, u, vt, layer, rk2, e2, rr, c, xn, fg, tm, apc, final):
    d, n = ht.shape
    nh, nk, _ = rk2.shape
    assert nk % 16 == 0
    ne = vt.shape[1]
    te = apc * nk
    nj = ne // te
    tok3 = lambda i, j: (0, 0, i)
    return pl.pallas_call(
        functools.partial(_peer_kernel, nh=nh, nk=nk, apc=apc, final=final),
        grid=(n // tm, nj),
        in_specs=[pl.BlockSpec((d, tm), lambda i, j: (0, i)),
                  pl.BlockSpec((te, d), lambda i, j: (layer * nj + j, 0)),
                  pl.BlockSpec((d, te), lambda i, j: (layer, j)),
                  pl.BlockSpec((nh, nk, tm), tok3),
                  pl.BlockSpec((nh, nk, tm), tok3),
                  pl.BlockSpec((nh, nk, tm), tok3),
                  pl.BlockSpec((nh, nk, tm), tok3),
                  pl.BlockSpec((tm, d), lambda i, j: (i, 0)),
                  pl.BlockSpec((1, d), lambda i, j: (0, 0))],
        out_specs=pl.BlockSpec((tm, d), lambda i, j: (i, 0)),
        out_shape=jax.ShapeDtypeStruct((n, d), F32),
        scratch_shapes=[pltpu.VMEM((d, tm), F32), pltpu.VMEM((te, tm), BF16)],
        compiler_params=_cparams(("parallel", "arbitrary")), name="peer",
    )(ht, u, vt, rk2, e2, rr, c, xn, fg)


def _rope_tables(pos):
    half = HEAD_DIM // 2
    inv_freq = ROPE_THETA ** (-jnp.arange(half, dtype=F32) / half)
    ang = pos.astype(F32)[:, None] * inv_freq[None, :]
    cos = jnp.cos(ang)
    sin = jnp.sin(ang)
    return jnp.tile(jnp.concatenate([cos, cos], axis=1), (1, 2)), jnp.tile(jnp.concatenate([-sin, sin], axis=1), (1, 2))


def _pad_rows(x, rows):
    return jnp.pad(x, ((0, rows - x.shape[0]),) + ((0, 0),) * (x.ndim - 1))


def _token_tile(n, pref):
    return pref if n % pref == 0 else n


def _page_pool_weights(wk, wv):
    wset = jnp.stack([wk[:, 0], wk[:, 1], wv[:, 0], wv[:, 1]]).reshape(4, 2, CMP_STRIDE)
    per_pos = jnp.tile(wset, (1, 1, LANES // CMP_STRIDE))
    chunk_of_pos = (jnp.arange(LANES) // CMP_STRIDE)[None, :] == jnp.arange(LANES // CMP_STRIDE)[:, None]
    return (per_pos[:, :, None, :] * chunk_of_pos[None, None].astype(F32)).reshape(64, LANES)


def kernel(x_prompt, x_sample, cache_nsa_kv, state_nsa_win, state_mlstm_C, state_mlstm_n, state_mlstm_m, state_conv, page_table, norm1_g, norm2_g, final_norm_g, w_in, w_out, mlstm_b_i, mlstm_b_f, mlstm_norm_g, conv_w, conv_b, conv_norm_g, conv_norm_b, nsa_cmp_wk, nsa_cmp_wv, gmlp_norm_g, gmlp_norm_b, gmlp_ws, gmlp_bs, peer_wq, peer_k1, peer_k2, peer_u, peer_v):
    depth = w_in.shape[0]
    bp, s_len, d = x_prompt.shape
    bd, t_dec, _ = x_sample.shape
    n_pool = cache_nsa_kv.shape[1]
    npages = page_table.shape[1]
    past = npages * cache_nsa_kv.shape[2]
    n_win = state_nsa_win.shape[2]
    assert t_dec == 1 and cache_nsa_kv.shape[2] == LANES and past % SEL_BLOCK == 0
    assert s_len >= WINDOW + 128 and n_win == WINDOW
    np_tok = bp * s_len
    ns_pad = LANES
    assert bd <= ns_pad

    xp = x_prompt.reshape(np_tok, d)
    xs = _pad_rows(x_sample.reshape(bd, d), ns_pad)
    cos_p, sin_p = _rope_tables(jnp.arange(s_len))
    cos_s, sin_s = _rope_tables(jnp.full((ns_pad,), past))
    pool_t = cache_nsa_kv.reshape(depth * n_pool, LANES, 4 * LANES).transpose(0, 2, 1)
    win_t = state_nsa_win.reshape(depth * bd, n_win, 2 * LANES).transpose(0, 2, 1)
    eye_h = jnp.eye(N_HEADS, dtype=F32)

    tm_in = _token_tile(s_len, 256)
    tm_out = _token_tile(np_tok, 256)
    tm_peer = _token_tile(np_tok, 512)
    nk = peer_k1.shape[2]
    apc = min(16, nk)
    w_in_t = w_in.transpose(2, 0, 1)
    ub = peer_u.astype(BF16).reshape(depth * peer_u.shape[1], d)
    vtb = peer_v.astype(BF16).transpose(0, 2, 1).reshape(depth * d, peer_v.shape[1])

    new_p, new_s = [], []
    for l in range(depth):
        wi = w_in_t[:, l, :]
        w_perm = jnp.concatenate([wi[0:1024], wi[1032:2568], wi[2580:3092]], axis=0).astype(BF16)
        w_small = jnp.concatenate([wi[1024:1032], wi[2568:2580], jnp.zeros((108, d), F32)], axis=0).astype(BF16)
        g1 = norm1_g[l].reshape(1, d)
        g2 = norm2_g[l].reshape(1, d)
        gate_bias = jnp.concatenate([mlstm_b_i[l], mlstm_b_f[l], jnp.zeros((LANES - 2 * N_HEADS,), F32)]).reshape(1, LANES)
        mng = mlstm_norm_g[l].reshape(1, MIX_W)
        cw = conv_w[l]
        cbias = conv_b[l].reshape(1, MIX_W)
        cg = conv_norm_g[l].reshape(1, MIX_W)
        cbeta = conv_norm_b[l].reshape(1, MIX_W)
        w32 = jnp.concatenate([jnp.repeat(nsa_cmp_wk[l], HEAD_DIM, axis=1), jnp.repeat(nsa_cmp_wv[l], HEAD_DIM, axis=1)], axis=1)
        wlo, whi = w32[:CMP_STRIDE], w32[CMP_STRIDE:]
        wall = _page_pool_weights(nsa_cmp_wk[l], nsa_cmp_wv[l])
        gg = gmlp_norm_g[l].reshape(1, MIX_W)
        gbeta = gmlp_norm_b[l].reshape(1, MIX_W)
        gws = gmlp_ws[l]
        gbsb = jnp.repeat(gmlp_bs[l].T, HEAD_DIM, axis=1)
        wo = w_out[l].astype(BF16)
        wq = jnp.stack(_split_bf16(peer_wq[l]))
        k1 = jnp.stack(_split_bf16(peer_k1[l]))
        k2 = jnp.stack(_split_bf16(peer_k2[l]))
        fg = final_norm_g.reshape(1, d)
        final = l == depth - 1

        oa, ob, oq, okv, owin, okaug, ovsb, owinb, od, osm = _in_proj(xp, g1, w_perm, w_small, cos_p, sin_p, tm_in)
        ya, c_p, n_p, m_p = _mlstm(oa.reshape(bp, s_len, 1024), osm.reshape(bp, s_len, LANES), gate_bias, mng,
                                   jnp.zeros((bp, MIX_W, MIX_W), F32), jnp.zeros((bp, 1, MIX_W), F32),
                                   jnp.zeros((bp, 1, MIX_W), F32), MLSTM_CHUNK, MLSTM_CHUNK, bp)
        yb, conv_p = _conv(ob.reshape(bp, s_len, 2 * MIX_W), jnp.zeros((bp, CONV_PAD, MIX_W), F32),
                           cw, cbias, cg, cbeta, 512, 512)
        cb = _cmp(okv.reshape(bp, s_len, 4 * LANES), wlo, whi)
        yc = _nsa_prompt(oq.reshape(bp, s_len, MIX_W), osm.reshape(bp, s_len, LANES), cb,
                         okaug.reshape(bp, s_len, 2 * LANES), ovsb.reshape(bp, s_len, LANES),
                         owinb.reshape(bp, s_len, 2 * LANES), 128)
        yd, _ = _gmlp(od, gg, gbeta, gws, gbsb, 4)
        xn, ht, s1t, s2t = _out_proj(xp, ya.reshape(np_tok, MIX_W), yb.reshape(np_tok, MIX_W),
                                     yc.reshape(np_tok, MIX_W), yd, wo, g2, wq, k1, k2, tm_out)
        rk2, e2, rr, cw8 = _gate(s1t, s2t, _token_tile(np_tok, 512))
        xp = _peer(ht, ub, vtb, l, rk2, e2, rr, cw8, xn, fg, tm_peer, apc, final)
        new_p.append((okv.reshape(bp, s_len, 4, NSA_KV, HEAD_DIM),
                      owin.reshape(bp, s_len, 2, NSA_KV, HEAD_DIM)[:, s_len - n_win:],
                      jnp.stack([c_p[:, h * HEAD_DIM:(h + 1) * HEAD_DIM, h * HEAD_DIM:(h + 1) * HEAD_DIM]
                                 for h in range(N_HEADS)], axis=1),
                      n_p.reshape(bp, N_HEADS, HEAD_DIM),
                      m_p[:, 0, ::HEAD_DIM],
                      conv_p[:, CONV_PAD - (CONV_W - 1):]))

        sa, sb, sq, skv, swin, _, _, _, sd, ssm = _in_proj(xs, g1, w_perm, w_small, cos_s, sin_s, ns_pad)
        rows8 = lambda t: jnp.pad(t[:bd, None, :], ((0, 0), (0, 7), (0, 0)))
        c0 = jnp.einsum('bhvk,hg->bhvgk', state_mlstm_C[l], eye_h).reshape(bd, MIX_W, MIX_W)
        n0 = state_mlstm_n[l].reshape(bd, 1, MIX_W)
        m0 = jnp.repeat(state_mlstm_m[l], HEAD_DIM, axis=-1).reshape(bd, 1, MIX_W)
        bb_s = 2 if bd % 2 == 0 else 1
        ya_s, c_s, n_s, m_s = _mlstm(rows8(sa), rows8(ssm), gate_bias, mng, c0, n0, m0, 8, 1, bb_s)
        prefix = jnp.pad(state_conv[l], ((0, 0), (CONV_PAD - (CONV_W - 1), 0), (0, 0)))
        yb_s, conv_s = _conv(rows8(sb), prefix, cw, cbias, cg, cbeta, 8, 1)
        zd_s = jnp.pad(sd[:bd, None, :], ((0, 0), (0, GMLP_CHUNK - 1), (0, 0))).reshape(bd * GMLP_CHUNK, 2 * MIX_W)
        yd_s, v_s = _gmlp(zd_s, gg, gbeta, gws, gbsb, 1)
        q_heads = sq[:bd].reshape(bd, NSA_KV, 2, 1, HEAD_DIM)
        q4 = (q_heads * jnp.eye(NSA_KV, dtype=F32)[None, :, None, :, None]).reshape(bd, N_HEADS, LANES)
        q4 = jnp.pad(q4, ((0, 0), (0, 8 - N_HEADS), (0, 0)))
        gq = jnp.pad(ssm[:bd, 2 * N_HEADS:2 * N_HEADS + 3 * N_HEADS].reshape(bd, N_HEADS, 3),
                     ((0, 0), (0, 8 - N_HEADS), (0, LANES - 3)))
        assert npages % 4 == 0
        pp = next(c for c in (32, 16, 8, 4) if npages % c == 0)
        o8, win_s = _nsa_sample(page_table + l * n_pool, q4, gq, skv[:bd, None, :], swin[:bd, None, :],
                                swin[:bd, :, None], win_t, l * bd, wall, pool_t, pp)
        yc_s = jnp.concatenate([o8[:, h, (h // 2) * HEAD_DIM:(h // 2 + 1) * HEAD_DIM] for h in range(N_HEADS)], axis=1)
        xn_s, ht_s, s1t_s, s2t_s = _out_proj(xs, _pad_rows(ya_s[:, 0], ns_pad), _pad_rows(yb_s[:, 0], ns_pad),
                                             _pad_rows(yc_s, ns_pad),
                                             _pad_rows(yd_s.reshape(bd, GMLP_CHUNK, MIX_W)[:, 0], ns_pad),
                                             wo, g2, wq, k1, k2, ns_pad)
        rk2_s, e2_s, rr_s, cw8_s = _gate(s1t_s, s2t_s, ns_pad)
        xs = _peer(ht_s, ub, vtb, l, rk2_s, e2_s, rr_s, cw8_s, xn_s, fg, ns_pad, apc, final)
        new_s.append((skv[:bd].reshape(bd, 1, 4, NSA_KV, HEAD_DIM),
                      win_s.transpose(0, 2, 1).reshape(bd, n_win, 2, NSA_KV, HEAD_DIM),
                      jnp.stack([c_s[:, h * HEAD_DIM:(h + 1) * HEAD_DIM, h * HEAD_DIM:(h + 1) * HEAD_DIM]
                                 for h in range(N_HEADS)], axis=1),
                      n_s.reshape(bd, N_HEADS, HEAD_DIM),
                      m_s[:, 0, ::HEAD_DIM],
                      conv_s[:, CONV_PAD - (CONV_W - 1):],
                      v_s.reshape(bd, GMLP_CHUNK, MIX_W)[:, 0:1]))

    def stack(states, i):
        return jnp.stack([st[i] for st in states])

    y_prompt = xp.reshape(bp, s_len, d)
    y_sample = xs[:bd].reshape(bd, 1, d)
    return (y_prompt, y_sample,
            stack(new_p, 0), stack(new_s, 0),
            stack(new_p, 1), stack(new_s, 1),
            stack(new_p, 2), stack(new_s, 2),
            stack(new_p, 3), stack(new_s, 3),
            stack(new_p, 4), stack(new_s, 4),
            stack(new_p, 5), stack(new_s, 5),
            stack(new_s, 6))
```

```python
import functools

import jax
import jax.numpy as jnp
from jax import lax
from jax.experimental import pallas as pl
from jax.experimental.pallas import tpu as pltpu

F32 = jnp.float32
BF16 = jnp.bfloat16
HIGHEST = lax.Precision.HIGHEST

HEAD_DIM = 64
N_HEADS = 4
MIX_W = 256
NSA_KV = 2
NORM_EPS = 1e-6
NEG = -1e30
MLSTM_CHUNK = 128
CONV_W = 31
CONV_PAD = 32
CMP_STRIDE = 16
CMP_LEN = 32
SEL_BLOCK = 64
N_SELECT = 16
SEL_FORCE = 1e9
WINDOW = 512
ROPE_THETA = 10000.0
ATTN_SCALE = HEAD_DIM ** -0.5
GMLP_CHUNK = 128
PEER_TOPK = 16
RANK_MARK = 2.0 ** 100
PEER_UP_ROWS = 1
LANES = 128
VMEM_LIMIT = 48 * 1024 * 1024

NT = (((1,), (1,)), ((), ()))
TN = (((0,), (0,)), ((), ()))


def _cparams(sem):
    return pltpu.CompilerParams(dimension_semantics=sem, vmem_limit_bytes=VMEM_LIMIT)


def _iota(shape, dim):
    return lax.broadcasted_iota(jnp.int32, shape, dim)


def _sigmoid(x):
    return 1.0 / (1.0 + jnp.exp(-x))


def _gelu(x):
    c1 = -2.0 * 0.7978845608028654 * 1.4426950408889634
    c2 = c1 * 0.044715
    return x / (1.0 + jnp.exp2(x * (c1 + c2 * (x * x))))


def _dot(a, b, dims=None, precision=None):
    if dims is None:
        return jnp.dot(a, b, preferred_element_type=F32, precision=precision)
    return lax.dot_general(a, b, dims, preferred_element_type=F32, precision=precision)


def _block_ones(n, blk):
    return (_iota((n, n), 0) // blk == _iota((n, n), 1) // blk).astype(F32)


def _group_layernorm(x, ee, width):
    mu = _dot(x, ee, precision=HIGHEST) * (1.0 / width)
    d = x - mu
    var = _dot(d * d, ee, precision=HIGHEST) * (1.0 / width)
    return d * lax.rsqrt(var + NORM_EPS)


def _rmsnorm(x, g):
    return x * lax.rsqrt(jnp.mean(x * x, axis=-1, keepdims=True) + NORM_EPS) * g


def _masked_softmax(s, valid):
    sm = jnp.where(valid, s, NEG)
    p = jnp.where(valid, jnp.exp(sm - jnp.max(sm, axis=-1, keepdims=True)), 0.0)
    return p / jnp.maximum(jnp.sum(p, axis=-1, keepdims=True), 1e-30)


def _rope(x, cos, sin_signed):
    w = x.shape[1]
    fwd = pltpu.roll(x, w - HEAD_DIM // 2, 1)
    bwd = pltpu.roll(x, HEAD_DIM // 2, 1)
    first = (_iota(x.shape, 1) % HEAD_DIM) < HEAD_DIM // 2
    return x * cos + jnp.where(first, fwd, bwd) * sin_signed


def _in_kernel(x_ref, g_ref, w_ref, wsmall_ref, cos_ref, sin_ref,
               oa_ref, ob_ref, oq_ref, okv_ref, owin_ref, okaug_ref, ovsb_ref, owinb_ref, od_ref, os_ref,
               okvt_ref, owint_ref, *, npb):
    tm = x_ref.shape[0]
    y = _rmsnorm(x_ref[...], g_ref[...])
    yb = y.astype(BF16)
    z = _dot(yb, w_ref[...], NT)
    os_ref[...] = _dot(yb, wsmall_ref[...], NT)
    oa_ref[...] = z[:, 0:1024]
    ob_ref[...] = z[:, 1024:1536]
    cos = cos_ref[...]
    sin = sin_ref[...]
    cos2 = jnp.concatenate([cos, cos], axis=1)
    sin2 = jnp.concatenate([sin, sin], axis=1)
    oq_ref[...] = _rope(z[:, 1536:1792], cos2, sin2) * ATTN_SCALE
    kc = _rope(z[:, 1792:1920], cos, sin)
    vc = z[:, 1920:2048]
    ks = _rope(z[:, 2048:2176], cos, sin)
    vs = z[:, 2176:2304]
    kw = _rope(z[:, 2304:2432], cos, sin)
    vw = z[:, 2432:2560]
    okv = jnp.concatenate([kc, vc, ks, vs], axis=1)
    okv_ref[...] = okv
    owin = jnp.concatenate([kw, vw], axis=1)
    owin_ref[...] = owin
    okvt_ref[0] = okv.T
    owint_ref[0] = owin.T
    pos = (pl.program_id(0) % npb) * tm + _iota((tm, LANES), 0)
    onehot = jnp.where(_iota((tm, LANES), 1) == pos // SEL_BLOCK, 1.0, 0.0)
    okaug_ref[...] = jnp.concatenate([ks, onehot], axis=1).astype(BF16)
    ovsb_ref[...] = vs.astype(BF16)
    owinb_ref[...] = owin.astype(BF16)
    od_ref[...] = z[:, 2560:3072]


def _in_proj(x, g, w, wsmall, cos_t, sin_t, tm):
    n, d = x.shape
    npb = cos_t.shape[0] // tm
    widths = (1024, 512, 256, 512, 256, 256, 128, 256, 512, 128)
    dtypes = (F32, F32, F32, F32, F32, BF16, BF16, BF16, F32, F32)
    row = lambda i: (i, 0)
    return pl.pallas_call(
        functools.partial(_in_kernel, npb=npb),
        grid=(n // tm,),
        in_specs=[pl.BlockSpec((tm, d), row),
                  pl.BlockSpec((1, d), lambda i: (0, 0)),
                  pl.BlockSpec(w.shape, lambda i: (0, 0)),
                  pl.BlockSpec(wsmall.shape, lambda i: (0, 0)),
                  pl.BlockSpec((tm, LANES), lambda i: (i % npb, 0)),
                  pl.BlockSpec((tm, LANES), lambda i: (i % npb, 0))],
        out_specs=[pl.BlockSpec((tm, wd), row) for wd in widths]
                  + [pl.BlockSpec((1, wd, tm), lambda i: (i // npb, 0, i % npb)) for wd in (4 * LANES, 2 * LANES)],
        out_shape=[jax.ShapeDtypeStruct((n, wd), dt) for wd, dt in zip(widths, dtypes)]
                  + [jax.ShapeDtypeStruct((n // (npb * tm), wd, npb * tm), F32) for wd in (4 * LANES, 2 * LANES)],
        compiler_params=_cparams(("parallel",)), name="in_proj",
    )(x, g, w, wsmall, cos_t, sin_t)


def _mlstm_kernel(za_ref, zs_ref, bias_ref, ng_ref, c0_ref, n0_ref, m0_ref,
                  y_ref, c_ref, n_ref, m_ref, *, L, t_valid, bb):
    @pl.when(pl.program_id(1) == 0)
    def _():
        c_ref[...] = c0_ref[...]
        n_ref[...] = n0_ref[...]
        m_ref[...] = m0_ref[...]

    head_of_lane = _iota((1, MIX_W), 1) // HEAD_DIM
    causal = _iota((L, L), 0) >= _iota((L, L), 1)
    tril = causal.astype(F32)
    ee = _block_ones(MIX_W, HEAD_DIM)
    bdiag = ee > 0.5
    row128 = _iota((L, LANES), 0)
    lane128 = _iota((L, LANES), 1)
    for bi in range(bb):
        za = za_ref[bi]
        q = za[:, 0:256]
        k = za[:, 256:512] * (HEAD_DIM ** -0.5)
        v = za[:, 512:768]
        o = za[:, 768:1024]
        gi = zs_ref[bi] + bias_ref[...]
        ls = jnp.minimum(gi, 0.0) - jnp.log(1.0 + jnp.exp(-jnp.abs(gi)))
        if t_valid < L:
            live = row128 < t_valid
            ig = jnp.where(live, gi, NEG)
            ls = jnp.where(live, ls, 0.0)
        else:
            ig = gi
        gmat = jnp.where(lane128 < N_HEADS, ig, 0.0)
        lfm = jnp.where((lane128 >= N_HEADS) & (lane128 < 2 * N_HEADS), ls, 0.0)
        fc = _dot(tril, lfm, precision=HIGHEST)
        g_t = gmat.T
        f_t = fc.T
        mrow = m_ref[bi]
        nrow = n_ref[bi]
        cm = c_ref[bi]
        qb = q.astype(BF16)
        kb = k.astype(BF16)
        vb = v.astype(BF16)
        num = jnp.zeros((L, MIX_W), F32)
        decay_b = jnp.zeros((L, MIX_W), F32)
        dens_b = jnp.zeros((L, MIX_W), F32)
        mt_b = jnp.zeros((L, MIX_W), F32)
        w_b = jnp.zeros((L, MIX_W), F32)
        cd_b = jnp.zeros((1, MIX_W), F32)
        mnew_b = jnp.zeros((1, MIX_W), F32)
        for h in range(N_HEADS):
            hm = head_of_lane == h
            f_col = fc[:, N_HEADS + h:N_HEADS + h + 1]
            ig_col = gmat[:, h:h + 1]
            f_row = f_t[N_HEADS + h:N_HEADS + h + 1, :]
            ig_row = g_t[h:h + 1, :]
            mp = mrow[:, h * HEAD_DIM:h * HEAD_DIM + 1]
            dmat = jnp.where(causal, (f_col - f_row) + ig_row, NEG)
            m_inter = mp + f_col
            m_t = jnp.maximum(m_inter, jnp.max(dmat, axis=1, keepdims=True))
            d_exp = jnp.exp(dmat - m_t)
            decay = jnp.exp(m_inter - m_t)
            qh = jnp.where(hm, q, 0.0).astype(BF16)
            s = _dot(qh, kb, NT) * d_exp
            num = jnp.where(hm, _dot(s.astype(BF16), vb), num)
            dens_b = jnp.where(hm, jnp.sum(s, axis=1, keepdims=True), dens_b)
            decay_b = jnp.where(hm, decay, decay_b)
            mt_b = jnp.where(hm, m_t, mt_b)
            f_last = f_col[L - 1:L, :]
            m_new = m_t[L - 1:L, :]
            w_b = jnp.where(hm, jnp.exp((f_last - f_col) + ig_col - m_new), w_b)
            cd_b = jnp.where(hm, jnp.exp(mp + f_last - m_new), cd_b)
            mnew_b = jnp.where(hm, m_new, mnew_b)
        inter = _dot(qb, cm.astype(BF16), NT)
        nq_b = _dot(q * nrow, ee, precision=HIGHEST)
        hnum = num + decay_b * inter
        den_b = dens_b + decay_b * nq_b
        hh = hnum / jnp.maximum(jnp.abs(den_b), jnp.exp(-mt_b))
        hn = _group_layernorm(hh, ee, HEAD_DIM) * ng_ref[...]
        y_ref[bi] = hn * _sigmoid(o)
        kw = k * w_b
        upd = _dot(vb, kw.astype(BF16), TN)
        c_ref[bi] = jnp.where(bdiag, cd_b * cm + upd, 0.0)
        n_ref[bi] = cd_b * nrow + jnp.sum(kw, axis=0, keepdims=True)
        m_ref[bi] = mnew_b


def _mlstm(za, zs, bias, ng, c0, n0, m0, L, t_valid, bb):
    b, t, _ = za.shape
    nc = t // L
    st = lambda i, c: (i, 0, 0)
    return pl.pallas_call(
        functools.partial(_mlstm_kernel, L=L, t_valid=t_valid, bb=bb),
        grid=(b // bb, nc),
        in_specs=[pl.BlockSpec((bb, L, 1024), lambda i, c: (i, c, 0)),
                  pl.BlockSpec((bb, L, LANES), lambda i, c: (i, c, 0)),
                  pl.BlockSpec((1, LANES), lambda i, c: (0, 0)),
                  pl.BlockSpec((1, MIX_W), lambda i, c: (0, 0)),
                  pl.BlockSpec((bb, MIX_W, MIX_W), st),
                  pl.BlockSpec((bb, 1, MIX_W), st),
                  pl.BlockSpec((bb, 1, MIX_W), st)],
        out_specs=[pl.BlockSpec((bb, L, MIX_W), lambda i, c: (i, c, 0)),
                   pl.BlockSpec((bb, MIX_W, MIX_W), st),
                   pl.BlockSpec((bb, 1, MIX_W), st),
                   pl.BlockSpec((bb, 1, MIX_W), st)],
        out_shape=[jax.ShapeDtypeStruct((b, t, MIX_W), F32),
                   jax.ShapeDtypeStruct((b, MIX_W, MIX_W), F32),
                   jax.ShapeDtypeStruct((b, 1, MIX_W), F32),
                   jax.ShapeDtypeStruct((b, 1, MIX_W), F32)],
        compiler_params=_cparams(("parallel", "arbitrary")), name="mlstm",
    )(za, zs, bias, ng, c0, n0, m0)


def _conv_kernel(zb_ref, pre_ref, w_ref, cb_ref, g_ref, beta_ref, y_ref, st_ref, ext_ref, *, tt, tv):
    @pl.when(pl.program_id(1) == 0)
    def _():
        ext_ref[0:CONV_PAD, :] = pre_ref[0]

    z = zb_ref[0]
    u = z[:, :MIX_W] * _sigmoid(z[:, MIX_W:])
    ext_ref[CONV_PAD:CONV_PAD + tt, :] = u
    off = CONV_PAD - (CONV_W - 1)
    acc = jnp.zeros((tt, MIX_W), F32)
    for j in range(CONV_W):
        acc = acc + w_ref[j:j + 1, :] * ext_ref[off + j:off + j + tt, :]
    ee = _block_ones(MIX_W, HEAD_DIM)
    y = _group_layernorm(acc + cb_ref[...], ee, HEAD_DIM) * g_ref[...] + beta_ref[...]
    y_ref[0] = y * _sigmoid(y)
    st_ref[0] = ext_ref[tv:tv + CONV_PAD, :]
    ext_ref[0:CONV_PAD, :] = ext_ref[tt:tt + CONV_PAD, :]


def _conv(zb, prefix, w, cb, g, beta, tt, tv):
    b, t, _ = zb.shape
    vec = lambda i, j: (0, 0)
    return pl.pallas_call(
        functools.partial(_conv_kernel, tt=tt, tv=tv),
        grid=(b, t // tt),
        in_specs=[pl.BlockSpec((1, tt, 2 * MIX_W), lambda i, j: (i, j, 0)),
                  pl.BlockSpec((1, CONV_PAD, MIX_W), lambda i, j: (i, 0, 0)),
                  pl.BlockSpec((CONV_W, MIX_W), vec),
                  pl.BlockSpec((1, MIX_W), vec),
                  pl.BlockSpec((1, MIX_W), vec),
                  pl.BlockSpec((1, MIX_W), vec)],
        out_specs=[pl.BlockSpec((1, tt, MIX_W), lambda i, j: (i, j, 0)),
                   pl.BlockSpec((1, CONV_PAD, MIX_W), lambda i, j: (i, 0, 0))],
        out_shape=[jax.ShapeDtypeStruct((b, t, MIX_W), F32),
                   jax.ShapeDtypeStruct((b, CONV_PAD, MIX_W), F32)],
        scratch_shapes=[pltpu.VMEM((CONV_PAD + tt, MIX_W), F32)],
        compiler_params=_cparams(("parallel", "arbitrary")), name="conv",
    )(zb, prefix, w, cb, g, beta)


def _gmlp_kernel(zd_ref, g_ref, beta_ref, ws_ref, bsb_ref, y_ref, v_ref, *, cpb):
    z = zd_ref[...]
    ee = _block_ones(MIX_W, HEAD_DIM)
    u = _gelu(z[:, :MIX_W])
    vv = _group_layernorm(_gelu(z[:, MIX_W:]), ee, HEAD_DIM) * g_ref[...] + beta_ref[...]
    v_ref[...] = vv
    tril = _iota((GMLP_CHUNK, GMLP_CHUNK), 0) >= _iota((GMLP_CHUNK, GMLP_CHUNK), 1)
    head_of_lane = _iota((1, MIX_W), 1) // HEAD_DIM
    wm = [jnp.where(tril, ws_ref[h], 0.0).astype(BF16) for h in range(N_HEADS)]
    for c in range(cpb):
        rows = slice(c * GMLP_CHUNK, (c + 1) * GMLP_CHUNK)
        vc = vv[rows].astype(BF16)
        mixed = jnp.zeros((GMLP_CHUNK, MIX_W), F32)
        for h in range(N_HEADS):
            mixed = jnp.where(head_of_lane == h, _dot(wm[h], vc), mixed)
        y_ref[rows, :] = u[rows] * (mixed + bsb_ref[...])


def _gmlp(zd, g, beta, ws, bsb, cpb):
    n = zd.shape[0]
    rows = cpb * GMLP_CHUNK
    vec = lambda i: (0, 0)
    return pl.pallas_call(
        functools.partial(_gmlp_kernel, cpb=cpb),
        grid=(n // rows,),
        in_specs=[pl.BlockSpec((rows, 2 * MIX_W), lambda i: (i, 0)),
                  pl.BlockSpec((1, MIX_W), vec),
                  pl.BlockSpec((1, MIX_W), vec),
                  pl.BlockSpec((N_HEADS, GMLP_CHUNK, GMLP_CHUNK), lambda i: (0, 0, 0)),
                  pl.BlockSpec((GMLP_CHUNK, MIX_W), vec)],
        out_specs=[pl.BlockSpec((rows, MIX_W), lambda i: (i, 0)),
                   pl.BlockSpec((rows, MIX_W), lambda i: (i, 0))],
        out_shape=[jax.ShapeDtypeStruct((n, MIX_W), F32),
                   jax.ShapeDtypeStruct((n, MIX_W), F32)],
        compiler_params=_cparams(("parallel",)), name="gmlp",
    )(zd, g, beta, ws, bsb)


def _combine_lo_hi(lo, hi):
    r = lo.shape[0]
    nxt = pltpu.roll(hi, r - 1, 0)
    return lo + jnp.where(_iota(lo.shape, 0) < r - 1, nxt, 0.0)


def _cmp_kernel(kv_ref, wlo_ref, whi_ref, o_ref):
    x = kv_ref[0]
    x3 = x.reshape(x.shape[0] // CMP_STRIDE, CMP_STRIDE, x.shape[1])
    lo = jnp.sum(x3 * wlo_ref[...][None], axis=1)
    hi = jnp.sum(x3 * whi_ref[...][None], axis=1)
    o_ref[0] = _combine_lo_hi(lo, hi)


def _cmp(okv3, wlo, whi):
    b, s, _ = okv3.shape
    return pl.pallas_call(
        _cmp_kernel,
        grid=(b,),
        in_specs=[pl.BlockSpec((1, s, 2 * LANES), lambda i: (i, 0, 0)),
                  pl.BlockSpec((CMP_STRIDE, 2 * LANES), lambda i: (0, 0)),
                  pl.BlockSpec((CMP_STRIDE, 2 * LANES), lambda i: (0, 0))],
        out_specs=pl.BlockSpec((1, s // CMP_STRIDE, 2 * LANES), lambda i: (i, 0, 0)),
        out_shape=jax.ShapeDtypeStruct((b, s // CMP_STRIDE, 2 * LANES), F32),
        compiler_params=_cparams(("parallel",)), name="nsa_cmp",
    )(okv3, wlo, whi)


def _nsa_prompt_kernel(q_ref, gs_ref, cb_ref, ka_ref, vs_ref, win_ref, y_ref, s_buf, *, tq, s_len, n_top, tk):
    start = pl.program_id(1) * tq
    nc = s_len // CMP_STRIDE
    ns = s_len // SEL_BLOCK
    q = q_ref[0]
    gl = _sigmoid(gs_ref[0])
    lo_half = _iota((tq, LANES), 1) < HEAD_DIM
    q_rows = []
    for g in range(NSA_KV):
        qpair = q[:, g * LANES:(g + 1) * LANES]
        swapped = pltpu.roll(qpair, HEAD_DIM, 1)
        if g == 0:
            q_rows += [jnp.where(lo_half, qpair, 0.0), jnp.where(lo_half, swapped, 0.0)]
        else:
            q_rows += [jnp.where(lo_half, 0.0, swapped), jnp.where(lo_half, 0.0, qpair)]
    q128 = jnp.concatenate(q_rows, axis=0)
    q128b = q128.astype(BF16)
    trow = start + _iota((tq, 1), 0)
    t2 = jnp.concatenate([trow, trow], axis=0)
    t4 = jnp.concatenate([t2, t2], axis=0)
    cb = cb_ref[0]
    sc = _dot(q128b, cb[:, 0:LANES].astype(BF16), NT)
    validc = (_iota((1, nc), 1) * CMP_STRIDE + CMP_LEN) <= (t4 + 1)
    p = _masked_softmax(sc, validc)
    o_c = _dot(p.astype(BF16), cb[:, LANES:2 * LANES].astype(BF16))
    psum = jnp.concatenate([p[0:tq] + p[tq:2 * tq], p[2 * tq:3 * tq] + p[3 * tq:4 * tq]], axis=0)
    pool4 = (_iota((nc, ns), 0) // (SEL_BLOCK // CMP_STRIDE) == _iota((nc, ns), 1)).astype(F32)
    imp = _dot(psum, pool4, precision=HIGHEST)
    jidx = _iota((1, ns), 1)
    imp = jnp.where(jidx == t2 // SEL_BLOCK, SEL_FORCE, imp)
    imp = jnp.where(jidx * SEL_BLOCK <= t2, imp, -1.0)
    if n_top < ns:
        x = jnp.concatenate([imp[0:tq].T, imp[tq:2 * tq].T], axis=1)
        jrow = _iota((ns, 2 * tq), 0).astype(F32)
        for _ in range(n_top):
            m = jnp.max(x, axis=0, keepdims=True)
            first = jnp.min(jnp.where(x == m, jrow, float(ns)), axis=0, keepdims=True)
            x = jnp.where(jrow == first, -3.0, x)
        sel_t = jnp.where(x == -3.0, 1.0, 0.0)
        sel = jnp.concatenate([sel_t[:, 0:tq].T, sel_t[:, tq:2 * tq].T], axis=0)
    else:
        sel = jnp.ones((2 * tq, ns), F32)
    selneg = (sel - 1.0) * (-NEG)
    if ns < LANES:
        selneg = jnp.concatenate([selneg, jnp.zeros((2 * tq, LANES - ns), F32)], axis=1)
    selneg4 = jnp.concatenate([selneg[0:tq], selneg[0:tq], selneg[tq:2 * tq], selneg[tq:2 * tq]], axis=0)
    qaug = jnp.concatenate([q128, selneg4], axis=1).astype(BF16)

    def scores(c):
        return _dot(qaug, ka_ref[0, pl.ds(pl.multiple_of(c * tk, tk), tk), :], NT)

    def absorb(c, s, carry):
        m, l, acc = carry
        m_new = jnp.maximum(m, jnp.max(s, axis=1, keepdims=True))
        pt = jnp.exp(s - m_new)
        alpha = jnp.exp(m - m_new)
        l = alpha * l + jnp.sum(pt, axis=1, keepdims=True)
        acc = alpha * acc + _dot(pt.astype(BF16), vs_ref[0, pl.ds(pl.multiple_of(c * tk, tk), tk), :])
        return m_new, l, acc

    def tile(c, carry):
        s = s_buf[...]
        s_buf[...] = scores(c + 1)
        return absorb(c, s, carry)

    n_tiles = (start + tq + tk - 1) // tk
    carry = (jnp.full((4 * tq, 1), NEG, F32), jnp.zeros((4 * tq, 1), F32), jnp.zeros((4 * tq, LANES), F32))
    s_buf[...] = scores(0)
    carry = lax.fori_loop(0, n_tiles - 1, tile, carry)
    last = n_tiles - 1
    s_diag = jnp.where(last * tk + _iota((1, tk), 1) <= t4, s_buf[...], NEG)
    _, l, acc = absorb(last, s_diag, carry)
    o_s = acc / jnp.maximum(l, 1e-30)
    wl = WINDOW + tq
    w0 = pl.multiple_of(jnp.maximum(start - WINDOW, 0), tq)
    sw = _dot(q128b, win_ref[0, pl.ds(w0, wl), 0:LANES], NT)
    kposw = w0 + _iota((1, wl), 1)
    validw = (kposw <= t4) & (kposw > t4 - WINDOW)
    o_w = _dot(_masked_softmax(sw, validw).astype(BF16), win_ref[0, pl.ds(w0, wl), LANES:2 * LANES])
    heads = []
    for hh in range(N_HEADS):
        col = 2 * N_HEADS + hh * 3
        rows = slice(hh * tq, (hh + 1) * tq)
        heads.append(o_c[rows] * gl[:, col:col + 1] + o_s[rows] * gl[:, col + 1:col + 2]
                     + o_w[rows] * gl[:, col + 2:col + 3])
    y_ref[0] = jnp.concatenate([jnp.where(lo_half, heads[0], pltpu.roll(heads[1], HEAD_DIM, 1)),
                                jnp.where(lo_half, pltpu.roll(heads[2], HEAD_DIM, 1), heads[3])], axis=1)


def _nsa_prompt(oq3, os3, cb, okaug3, ovsb3, owinb3, tq):
    b, s, _ = oq3.shape
    ns = s // SEL_BLOCK
    assert ns <= LANES
    n_top = min(N_SELECT, ns)
    tk = min(512, s)
    full = lambda i, j: (i, 0, 0)
    return pl.pallas_call(
        functools.partial(_nsa_prompt_kernel, tq=tq, s_len=s, n_top=n_top, tk=tk),
        grid=(b, s // tq),
        in_specs=[pl.BlockSpec((1, tq, MIX_W), lambda i, j: (i, j, 0)),
                  pl.BlockSpec((1, tq, LANES), lambda i, j: (i, j, 0)),
                  pl.BlockSpec((1, s // CMP_STRIDE, 2 * LANES), full),
                  pl.BlockSpec((1, s, 2 * LANES), full),
                  pl.BlockSpec((1, s, LANES), full),
                  pl.BlockSpec((1, s, 2 * LANES), full)],
        out_specs=pl.BlockSpec((1, tq, MIX_W), lambda i, j: (i, j, 0)),
        out_shape=jax.ShapeDtypeStruct((b, s, MIX_W), F32),
        scratch_shapes=[pltpu.VMEM((N_HEADS * tq, tk), F32)],
        compiler_params=_cparams(("parallel", "arbitrary")), name="nsa_prompt",
    )(oq3, os3, cb, okaug3, ovsb3, owinb3)


def _nsa_sample_kernel(pt_ref, q4_ref, gq_ref, newkv_ref, newwin_ref, newcol_ref, win_ref, wall_ref, *rest,
                       pp, npages, n_top):
    pool_refs = rest[:pp]
    o_ref, wo_ref, lo_s, hi_s, m_s, l_s, a_s, a_stage = rest[pp:]
    i = pl.program_id(1)
    nb = 2 * npages
    nbp = m_s.shape[1]
    ncp = npages * (LANES // CMP_STRIDE)
    past = npages * LANES
    q4 = q4_ref[0]
    q4b = q4.astype(BF16)

    @pl.when(i == 0)
    def _():
        m_s[...] = jnp.zeros(m_s.shape, F32)
        l_s[...] = jnp.zeros(l_s.shape, F32)
        if nb < nbp:
            a_s[...] = jnp.zeros(a_s.shape, F32)

    lane8 = _iota((8, LANES), 1)
    lanej = _iota((8, nbp), 1)
    set_of_lane = _iota((8, 2 * LANES), 1) // HEAD_DIM
    bpp = LANES // SEL_BLOCK
    wall = wall_ref[...].astype(BF16)
    m_acc = m_s[...]
    l_acc = l_s[...]
    los, his = [], []
    for kk in range(pp):
        page = pool_refs[kk][0]
        pooled = _dot(wall, page[0:2 * LANES, :].astype(BF16), NT)
        lo8 = jnp.zeros((8, 2 * LANES), F32)
        hi8 = jnp.zeros((8, 2 * LANES), F32)
        for st in range(4):
            lo8 = jnp.where(set_of_lane == st, pooled[st * 16:st * 16 + 8], lo8)
            hi8 = jnp.where(set_of_lane == st, pooled[st * 16 + 8:st * 16 + 16], hi8)
        los.append(lo8)
        his.append(hi8)
        ks_t = page[2 * LANES:3 * LANES, :].astype(BF16)
        vs_t = page[3 * LANES:4 * LANES, :].astype(BF16)
        s = _dot(q4b, ks_t)
        probs = []
        for jb in range(bpp):
            inb = (lane8 >= jb * SEL_BLOCK) & (lane8 < (jb + 1) * SEL_BLOCK)
            sm = jnp.where(inb, s, NEG)
            m = jnp.max(sm, axis=1, keepdims=True)
            p = jnp.where(inb, jnp.exp(sm - m), 0.0)
            probs.append(p)
            j = i * (pp * bpp) + kk * bpp + jb
            m_acc = jnp.where(lanej == j, m, m_acc)
            l_acc = jnp.where(lanej == j, jnp.sum(p, axis=1, keepdims=True), l_acc)
        a = _dot(jnp.concatenate(probs, axis=0).astype(BF16), vs_t, NT)
        for jb in range(bpp):
            for hh in range(N_HEADS):
                a_stage[hh, kk * bpp + jb:kk * bpp + jb + 1, :] = a[jb * 8 + hh:jb * 8 + hh + 1, :]
    m_s[...] = m_acc
    l_s[...] = l_acc
    r0 = pl.multiple_of(i * (pp * 8), pp * 8)
    lo_s[pl.ds(r0, pp * 8), :] = jnp.concatenate(los, axis=0)
    hi_s[pl.ds(r0, pp * 8), :] = jnp.concatenate(his, axis=0)
    b0 = pl.multiple_of(i * (pp * bpp), pp * bpp)
    for hh in range(N_HEADS):
        a_s[hh, pl.ds(b0, pp * bpp), :] = a_stage[hh]

    @pl.when(i == pl.num_programs(1) - 1)
    def _():
        row8 = _iota((8, 1), 0)
        cbm = _combine_lo_hi(lo_s[...], hi_s[...])
        kcb = cbm[:, 0:LANES].astype(BF16)
        vcb = cbm[:, LANES:2 * LANES].astype(BF16)
        sc = _dot(q4b, kcb, NT)
        validc = (_iota((1, ncp), 1) * CMP_STRIDE + CMP_LEN) <= past + 1
        p = _masked_softmax(sc, validc)
        o_c = _dot(p.astype(BF16), vcb)
        pg = jnp.where(row8 == 0, p[0:1] + p[1:2], jnp.where(row8 == 1, p[2:3] + p[3:4], 0.0))
        pool4 = (_iota((ncp, nbp), 0) // (SEL_BLOCK // CMP_STRIDE) == _iota((ncp, nbp), 1)).astype(F32)
        imp2 = _dot(pg, pool4, precision=HIGHEST)
        ii = _iota((nbp, nbp), 0)
        jj = _iota((nbp, nbp), 1)
        sels = []
        for g in range(NSA_KV):
            mx = jnp.broadcast_to(imp2[g:g + 1, :], (nbp, nbp))
            mt = mx.T
            beats = ((mt > mx) | ((mt == mx) & (ii < jj))) & (ii < nb)
            rank = jnp.sum(jnp.where(beats, 1.0, 0.0), axis=0, keepdims=True)
            sels.append(jnp.where((rank < n_top - 1) & (_iota((1, nbp), 1) < nb), 1.0, 0.0))
        sel8 = jnp.where(row8 < 2, sels[0], sels[1]) > 0.5
        newkv = newkv_ref[0]
        ksn = newkv[:, 2 * LANES:3 * LANES]
        vsn = newkv[:, 3 * LANES:4 * LANES]
        s_new = jnp.sum(q4 * ksn, axis=1, keepdims=True)
        mrow = m_s[...]
        m_all = jnp.maximum(jnp.max(jnp.where(sel8, mrow, NEG), axis=1, keepdims=True), s_new)
        wj = jnp.where(sel8, jnp.exp(mrow - m_all), 0.0)
        w_new = jnp.exp(s_new - m_all)
        ltot = jnp.sum(wj * l_s[...], axis=1, keepdims=True) + w_new
        osum = w_new * vsn
        for hh in range(N_HEADS):
            osum = osum + jnp.where(row8 == hh, _dot(wj, a_s[hh], precision=HIGHEST), 0.0)
        o_s = osum / jnp.maximum(ltot, 1e-30)
        win_t = win_ref[0]
        nw = win_t.shape[1]
        lane_w = _iota((1, nw), 1)
        sw = _dot(q4b, win_t[0:LANES, :].astype(BF16))
        validw = (past - nw + lane_w) > past - WINDOW
        neww = newwin_ref[0]
        sw_new = jnp.sum(q4 * neww[:, 0:LANES], axis=1, keepdims=True)
        mw = jnp.maximum(jnp.max(jnp.where(validw, sw, NEG), axis=1, keepdims=True), sw_new)
        pw = jnp.where(validw, jnp.exp(sw - mw), 0.0)
        pn = jnp.exp(sw_new - mw)
        zw = jnp.sum(pw, axis=1, keepdims=True) + pn
        o_w = (_dot(pw.astype(BF16), win_t[LANES:2 * LANES, :].astype(BF16), NT) + pn * neww[:, LANES:2 * LANES]) / zw
        gg = _sigmoid(gq_ref[0])
        o_ref[0] = o_c * gg[:, 0:1] + o_s * gg[:, 1:2] + o_w * gg[:, 2:3]
        wo_ref[0] = jnp.where(lane_w == nw - 1, newcol_ref[0], pltpu.roll(win_t, nw - 1, 1))


def _nsa_sample(pt, q4, gq, newkv, newwin, newcol, win_t, win_off, wall, pool_t, pp):
    bd, npages = pt.shape
    nw = win_t.shape[2]
    nb = 2 * npages
    nbp = -(-nb // LANES) * LANES
    n_top = min(N_SELECT, nb + 1)
    per_b = lambda b, i, pt_ref: (b, 0, 0)
    const = lambda b, i, pt_ref: (0, 0)

    def page_map(kk):
        return lambda b, i, pt_ref: (pt_ref[b, i * pp + kk], 0, 0)

    grid_spec = pltpu.PrefetchScalarGridSpec(
        num_scalar_prefetch=1,
        grid=(bd, npages // pp),
        in_specs=[pl.BlockSpec((1, 8, LANES), per_b),
                  pl.BlockSpec((1, 8, LANES), per_b),
                  pl.BlockSpec((1, 1, 4 * LANES), per_b),
                  pl.BlockSpec((1, 1, 2 * LANES), per_b),
                  pl.BlockSpec((1, 2 * LANES, 1), per_b),
                  pl.BlockSpec((1, 2 * LANES, nw), lambda b, i, pt_ref: (win_off + b, 0, 0)),
                  pl.BlockSpec(wall.shape, const)]
                 + [pl.BlockSpec((1, 4 * LANES, LANES), page_map(kk)) for kk in range(pp)],
        out_specs=[pl.BlockSpec((1, 8, LANES), per_b),
                   pl.BlockSpec((1, 2 * LANES, nw), per_b)],
        scratch_shapes=[pltpu.VMEM((npages * 8, 2 * LANES), F32),
                        pltpu.VMEM((npages * 8, 2 * LANES), F32),
                        pltpu.VMEM((8, nbp), F32),
                        pltpu.VMEM((8, nbp), F32),
                        pltpu.VMEM((N_HEADS, nbp, LANES), F32),
                        pltpu.VMEM((N_HEADS, pp * (LANES // SEL_BLOCK), LANES), F32)])
    return pl.pallas_call(
        functools.partial(_nsa_sample_kernel, pp=pp, npages=npages, n_top=n_top),
        grid_spec=grid_spec,
        out_shape=[jax.ShapeDtypeStruct((bd, 8, LANES), F32),
                   jax.ShapeDtypeStruct((bd, 2 * LANES, nw), F32)],
        compiler_params=_cparams(("parallel", "arbitrary")), name="nsa_sample",
    )(pt, q4, gq, newkv, newwin, newcol, win_t, wall, *([pool_t] * pp))


def _split_bf16(x):
    hi = x.astype(BF16)
    return hi, (x - hi.astype(F32)).astype(BF16)


def _out_kernel(x_ref, ya_ref, yb_ref, yc_ref, yd_ref, wo_ref, g2_ref, wq_ref, k1_ref, k2_ref,
                xn_ref, ht_ref, s1_ref, s2_ref, *, nh, dq):
    acc = x_ref[...]
    for idx, y_ref in enumerate((ya_ref, yb_ref, yc_ref, yd_ref)):
        acc = acc + _dot(y_ref[...].astype(BF16), wo_ref[idx * MIX_W:(idx + 1) * MIX_W, :])
    xn_ref[...] = acc
    h2 = _rmsnorm(acc, g2_ref[...])
    ht_ref[...] = h2.T.astype(BF16)
    h_hi, h_lo = _split_bf16(h2)
    q = _dot(h_hi, wq_ref[0]) + _dot(h_lo, wq_ref[0]) + _dot(h_hi, wq_ref[1])
    q_hi, q_lo = _split_bf16(q)
    half = dq // 2
    for h in range(nh):
        for s_ref, k_ref, cols in ((s1_ref, k1_ref, slice(h * dq, h * dq + half)),
                                   (s2_ref, k2_ref, slice(h * dq + half, (h + 1) * dq))):
            s_ref[h] = (_dot(k_ref[0, h], q_hi[:, cols], NT) + _dot(k_ref[1, h], q_hi[:, cols], NT)
                        + _dot(k_ref[0, h], q_lo[:, cols], NT))


def _out_proj(x, ya, yb, yc, yd, wo, g2, wq, k1, k2, tm):
    n, d = x.shape
    _, nh, nk, half = k1.shape
    row = lambda i: (i, 0)
    c2 = lambda i: (0, 0)
    return pl.pallas_call(
        functools.partial(_out_kernel, nh=nh, dq=2 * half),
        grid=(n // tm,),
        in_specs=[pl.BlockSpec((tm, d), row)] + [pl.BlockSpec((tm, MIX_W), row)] * 4
                 + [pl.BlockSpec(wo.shape, c2), pl.BlockSpec((1, d), c2), pl.BlockSpec(wq.shape, lambda i: (0, 0, 0)),
                    pl.BlockSpec(k1.shape, lambda i: (0, 0, 0, 0)), pl.BlockSpec(k2.shape, lambda i: (0, 0, 0, 0))],
        out_specs=[pl.BlockSpec((tm, d), row),
                   pl.BlockSpec((d, tm), lambda i: (0, i)),
                   pl.BlockSpec((nh, nk, tm), lambda i: (0, 0, i)),
                   pl.BlockSpec((nh, nk, tm), lambda i: (0, 0, i))],
        out_shape=[jax.ShapeDtypeStruct((n, d), F32),
                   jax.ShapeDtypeStruct((d, n), BF16),
                   jax.ShapeDtypeStruct((nh, nk, n), F32),
                   jax.ShapeDtypeStruct((nh, nk, n), F32)],
        compiler_params=_cparams(("parallel",)), name="out_proj",
    )(x, ya, yb, yc, yd, wo, g2, wq, k1, k2)


_PAIR_COUNTS = tuple(PEER_TOPK // (i + 1) for i in range(PEER_TOPK))
_PAIR_ROWS = -(-sum(_PAIR_COUNTS) // 8) * 8


def _gate_kernel(s1_ref, s2_ref, rk2_ref, e2_ref, rr_ref, c_ref, v1_s, v2_s, c_s):
    x1 = s1_ref[0]
    x2 = s2_ref[0]
    for i in range(PEER_TOPK):
        m1 = jnp.max(x1, axis=0, keepdims=True)
        m2 = jnp.max(x2, axis=0, keepdims=True)
        v1_s[i:i + 1, :] = m1
        v2_s[i:i + 1, :] = m2
        marker = -RANK_MARK * (1.0 + i / PEER_TOPK)
        x1 = jnp.where(x1 == m1, marker, x1)
        x2 = jnp.where(x2 == m2, marker, x2)
    rank1 = jnp.where(x1 <= -RANK_MARK, (x1 * (-1.0 / RANK_MARK) - 1.0) * PEER_TOPK, float(PEER_TOPK))
    rank2 = jnp.where(x2 <= -RANK_MARK, (x2 * (-1.0 / RANK_MARK) - 1.0) * PEER_TOPK, float(PEER_TOPK))
    v1 = v1_s[...]
    v2 = v2_s[...]
    off = 0
    for i, cnt in enumerate(_PAIR_COUNTS):
        c_s[off:off + cnt, :] = v1[i:i + 1, :] + v2[0:cnt, :]
        off += cnt
    if off < _PAIR_ROWS:
        c_s[off:_PAIR_ROWS, :] = jnp.full((_PAIR_ROWS - off, v1.shape[1]), NEG, F32)
    c = c_s[...]
    mx = v1[0:1, :] + v2[0:1, :]
    z = jnp.zeros_like(mx)
    m = mx
    for i in range(PEER_TOPK):
        m = jnp.max(c, axis=0, keepdims=True)
        z = z + jnp.exp(m - mx)
        c = jnp.where(c == m, NEG, c)
    tau = m
    passing = jnp.zeros(v1.shape, F32)
    for jj in range(PEER_TOPK):
        passing = passing + jnp.where(v1 + v2[jj:jj + 1, :] >= tau, 1.0, 0.0)
    rr = jnp.zeros(rank1.shape, F32)
    for i in range(PEER_TOPK):
        rr = jnp.where(rank1 == float(i), passing[i:i + 1, :], rr)
    rk2_ref[0] = rank2.astype(BF16)
    rr_ref[0] = rr
    e2_ref[0] = jnp.exp(s2_ref[0] - v2[0:1, :]).astype(BF16)
    c_ref[0] = jnp.exp(s1_ref[0] - v1[0:1, :]) * (1.0 / z)


def _gate(s1t, s2t, tn):
    nh, nk, n = s1t.shape
    blk = lambda h, i: (h, 0, i)
    return pl.pallas_call(
        _gate_kernel,
        grid=(nh, n // tn),
        in_specs=[pl.BlockSpec((1, nk, tn), blk), pl.BlockSpec((1, nk, tn), blk)],
        out_specs=[pl.BlockSpec((1, nk, tn), blk)] * 4,
        out_shape=[jax.ShapeDtypeStruct((nh, nk, n), BF16), jax.ShapeDtypeStruct((nh, nk, n), BF16),
                   jax.ShapeDtypeStruct((nh, nk, n), F32), jax.ShapeDtypeStruct((nh, nk, n), F32)],
        scratch_shapes=[pltpu.VMEM((PEER_TOPK, tn), F32), pltpu.VMEM((PEER_TOPK, tn), F32),
                        pltpu.VMEM((_PAIR_ROWS, tn), F32)],
        compiler_params=_cparams(("parallel", "parallel")), name="peer_gate",
    )(s1t, s2t)


def _bf16_rows(row, n_rows):
    tile = jnp.broadcast_to(row, (16, row.shape[1])).astype(BF16)
    return jnp.concatenate([tile] * (n_rows // 16), axis=0)


def _peer_kernel(ht_ref, u_ref, vt_ref, rk2_ref, e2_ref, rr_ref, c_ref, xn_ref, fg_ref, o_ref,
                 acc_ref, f_ref, *, nh, nk, apc, final):
    j = pl.program_id(1)

    @pl.when(j == 0)
    def _():
        acc_ref[...] = jnp.zeros(acc_ref.shape, F32)

    ht = ht_ref[...]
    zero = jnp.zeros((), BF16)
    group = min(PEER_UP_ROWS, apc)
    for aa in range(apc):
        a = j * apc + aa
        if aa % group == 0:
            pre = _dot(u_ref[aa * nk:(aa + group) * nk, :], ht)
        k = aa % group
        act = _gelu(pre[k * nk:(k + 1) * nk, :]).astype(BF16)
        wgt = jnp.zeros(act.shape, BF16)
        for h in range(nh):
            limit = _bf16_rows(rr_ref[h, pl.ds(a, 1), :], nk)
            scale = _bf16_rows(c_ref[h, pl.ds(a, 1), :], nk)
            wgt = wgt + jnp.where(rk2_ref[h] < limit, e2_ref[h] * scale, zero)
        f_ref[aa * nk:(aa + 1) * nk, :] = wgt * act
    acc_ref[...] += _dot(vt_ref[...], f_ref[...])

    @pl.when(j == pl.num_programs(1) - 1)
    def _():
        out = xn_ref[...] + acc_ref[...].T
        if final:
            out = _rmsnorm(out, fg_ref[...])
        o_ref[...] = out


def _peer(ht, u, vt, layer, rk2, e2, rr, c, xn, fg, tm, apc, final):
    d, n = ht.shape
    nh, nk, _ = rk2.shape
    assert nk % 16 == 0
    ne = vt.shape[1]
    te = apc * nk
    nj = ne // te
    tok3 = lambda i, j: (0, 0, i)
    return pl.pallas_call(
        functools.partial(_peer_kernel, nh=nh, nk=nk, apc=apc, final=final),
        grid=(n // tm, nj),
        in_specs=[pl.BlockSpec((d, tm), lambda i, j: (0, i)),
                  pl.BlockSpec((te, d), lambda i, j: (layer * nj + j, 0)),
                  pl.BlockSpec((d, te), lambda i, j: (layer, j)),
                  pl.BlockSpec((nh, nk, tm), tok3),
                  pl.BlockSpec((nh, nk, tm), tok3),
                  pl.BlockSpec((nh, nk, tm), tok3),
                  pl.BlockSpec((nh, nk, tm), tok3),
                  pl.BlockSpec((tm, d), lambda i, j: (i, 0)),
                  pl.BlockSpec((1, d), lambda i, j: (0, 0))],
        out_specs=pl.BlockSpec((tm, d), lambda i, j: (i, 0)),
        out_shape=jax.ShapeDtypeStruct((n, d), F32),
        scratch_shapes=[pltpu.VMEM((d, tm), F32), pltpu.VMEM((te, tm), BF16)],
        compiler_params=_cparams(("parallel", "arbitrary")), name="peer",
    )(ht, u, vt, rk2, e2, rr, c, xn, fg)


def _rope_tables(pos):
    half = HEAD_DIM // 2
    inv_freq = ROPE_THETA ** (-jnp.arange(half, dtype=F32) / half)
    ang = pos.astype(F32)[:, None] * inv_freq[None, :]
    cos = jnp.cos(ang)
    sin = jnp.sin(ang)
    return jnp.tile(jnp.concatenate([cos, cos], axis=1), (1, 2)), jnp.tile(jnp.concatenate([-sin, sin], axis=1), (1, 2))


def _pad_rows(x, rows):
    return jnp.pad(x, ((0, rows - x.shape[0]),) + ((0, 0),) * (x.ndim - 1))


def _token_tile(n, pref):
    return pref if n % pref == 0 else n


def _page_pool_weights(wk, wv):
    wset = jnp.stack([wk[:, 0], wk[:, 1], wv[:, 0], wv[:, 1]]).reshape(4, 2, CMP_STRIDE)
    per_pos = jnp.tile(wset, (1, 1, LANES // CMP_STRIDE))
    chunk_of_pos = (jnp.arange(LANES) // CMP_STRIDE)[None, :] == jnp.arange(LANES // CMP_STRIDE)[:, None]
    return (per_pos[:, :, None, :] * chunk_of_pos[None, None].astype(F32)).reshape(64, LANES)


def kernel(x_prompt, x_sample, cache_nsa_kv, state_nsa_win, state_mlstm_C, state_mlstm_n, state_mlstm_m, state_conv, page_table, norm1_g, norm2_g, final_norm_g, w_in, w_out, mlstm_b_i, mlstm_b_f, mlstm_norm_g, conv_w, conv_b, conv_norm_g, conv_norm_b, nsa_cmp_wk, nsa_cmp_wv, gmlp_norm_g, gmlp_norm_b, gmlp_ws, gmlp_bs, peer_wq, peer_k1, peer_k2, peer_u, peer_v):
    depth = w_in.shape[0]
    bp, s_len, d = x_prompt.shape
    bd, t_dec, _ = x_sample.shape
    n_pool = cache_nsa_kv.shape[1]
    npages = page_table.shape[1]
    past = npages * cache_nsa_kv.shape[2]
    n_win = state_nsa_win.shape[2]
    assert t_dec == 1 and cache_nsa_kv.shape[2] == LANES and past % SEL_BLOCK == 0
    assert s_len >= WINDOW + 128 and n_win == WINDOW
    np_tok = bp * s_len
    ns_pad = LANES
    assert bd <= ns_pad

    xp = x_prompt.reshape(np_tok, d)
    xs = _pad_rows(x_sample.reshape(bd, d), ns_pad)
    cos_p, sin_p = _rope_tables(jnp.arange(s_len))
    cos_s, sin_s = _rope_tables(jnp.full((ns_pad,), past))
    pool_t = cache_nsa_kv.reshape(depth * n_pool, LANES, 4 * LANES).transpose(0, 2, 1)
    win_t = state_nsa_win.reshape(depth * bd, n_win, 2 * LANES).transpose(0, 2, 1)
    eye_h = jnp.eye(N_HEADS, dtype=F32)

    tm_in = _token_tile(s_len, 256)
    tm_out = _token_tile(np_tok, 256)
    tm_peer = _token_tile(np_tok, 512)
    nk = peer_k1.shape[2]
    apc = min(16, nk)
    w_in_t = w_in.transpose(2, 0, 1)
    ub = peer_u.astype(BF16).reshape(depth * peer_u.shape[1], d)
    vtb = peer_v.astype(BF16).transpose(0, 2, 1).reshape(depth * d, peer_v.shape[1])

    new_p, new_s = [], []
    for l in range(depth):
        wi = w_in_t[:, l, :]
        w_perm = jnp.concatenate([wi[0:1024], wi[1032:2568], wi[2580:3092]], axis=0).astype(BF16)
        w_small = jnp.concatenate([wi[1024:1032], wi[2568:2580], jnp.zeros((108, d), F32)], axis=0).astype(BF16)
        g1 = norm1_g[l].reshape(1, d)
        g2 = norm2_g[l].reshape(1, d)
        gate_bias = jnp.concatenate([mlstm_b_i[l], mlstm_b_f[l], jnp.zeros((LANES - 2 * N_HEADS,), F32)]).reshape(1, LANES)
        mng = mlstm_norm_g[l].reshape(1, MIX_W)
        cw = conv_w[l]
        cbias = conv_b[l].reshape(1, MIX_W)
        cg = conv_norm_g[l].reshape(1, MIX_W)
        cbeta = conv_norm_b[l].reshape(1, MIX_W)
        w32 = jnp.concatenate([jnp.repeat(nsa_cmp_wk[l], HEAD_DIM, axis=1), jnp.repeat(nsa_cmp_wv[l], HEAD_DIM, axis=1)], axis=1)
        wlo, whi = w32[:CMP_STRIDE], w32[CMP_STRIDE:]
        wall = _page_pool_weights(nsa_cmp_wk[l], nsa_cmp_wv[l])
        gg = gmlp_norm_g[l].reshape(1, MIX_W)
        gbeta = gmlp_norm_b[l].reshape(1, MIX_W)
        gws = gmlp_ws[l]
        gbsb = jnp.repeat(gmlp_bs[l].T, HEAD_DIM, axis=1)
        wo = w_out[l].astype(BF16)
        wq = jnp.stack(_split_bf16(peer_wq[l]))
        k1 = jnp.stack(_split_bf16(peer_k1[l]))
        k2 = jnp.stack(_split_bf16(peer_k2[l]))
        fg = final_norm_g.reshape(1, d)
        final = l == depth - 1

        (oa, ob, oq, okv, _, okaug, ovsb, owinb, od, osm, okvt, owint) = _in_proj(xp, g1, w_perm, w_small, cos_p, sin_p, tm_in)
        ya, c_p, n_p, m_p = _mlstm(oa.reshape(bp, s_len, 1024), osm.reshape(bp, s_len, LANES), gate_bias, mng,
                                   jnp.zeros((bp, MIX_W, MIX_W), F32), jnp.zeros((bp, 1, MIX_W), F32),
                                   jnp.zeros((bp, 1, MIX_W), F32), MLSTM_CHUNK, MLSTM_CHUNK, bp)
        yb, conv_p = _conv(ob.reshape(bp, s_len, 2 * MIX_W), jnp.zeros((bp, CONV_PAD, MIX_W), F32),
                           cw, cbias, cg, cbeta, 512, 512)
        cb = _cmp(okv.reshape(bp, s_len, 4 * LANES), wlo, whi)
        yc = _nsa_prompt(oq.reshape(bp, s_len, MIX_W), osm.reshape(bp, s_len, LANES), cb,
                         okaug.reshape(bp, s_len, 2 * LANES), ovsb.reshape(bp, s_len, LANES),
                         owinb.reshape(bp, s_len, 2 * LANES), 128)
        yd, _ = _gmlp(od, gg, gbeta, gws, gbsb, 4)
        xn, ht, s1t, s2t = _out_proj(xp, ya.reshape(np_tok, MIX_W), yb.reshape(np_tok, MIX_W),
                                     yc.reshape(np_tok, MIX_W), yd, wo, g2, wq, k1, k2, tm_out)
        rk2, e2, rr, cw8 = _gate(s1t, s2t, _token_tile(np_tok, 512))
        xp = _peer(ht, ub, vtb, l, rk2, e2, rr, cw8, xn, fg, tm_peer, apc, final)
        new_p.append((okvt.reshape(bp, 4, NSA_KV, HEAD_DIM, s_len).transpose(0, 4, 1, 2, 3),
                      owint.reshape(bp, 2, NSA_KV, HEAD_DIM, s_len)[..., s_len - n_win:].transpose(0, 4, 1, 2, 3),
                      jnp.stack([c_p[:, h * HEAD_DIM:(h + 1) * HEAD_DIM, h * HEAD_DIM:(h + 1) * HEAD_DIM]
                                 for h in range(N_HEADS)], axis=1),
                      n_p.reshape(bp, N_HEADS, HEAD_DIM),
                      m_p[:, 0, ::HEAD_DIM],
                      conv_p[:, CONV_PAD - (CONV_W - 1):]))

        sa, sb, sq, skv, swin, _, _, _, sd, ssm, _, _ = _in_proj(xs, g1, w_perm, w_small, cos_s, sin_s, ns_pad)
        rows8 = lambda t: jnp.pad(t[:bd, None, :], ((0, 0), (0, 7), (0, 0)))
        c0 = jnp.einsum('bhvk,hg->bhvgk', state_mlstm_C[l], eye_h).reshape(bd, MIX_W, MIX_W)
        n0 = state_mlstm_n[l].reshape(bd, 1, MIX_W)
        m0 = jnp.repeat(state_mlstm_m[l], HEAD_DIM, axis=-1).reshape(bd, 1, MIX_W)
        bb_s = 2 if bd % 2 == 0 else 1
        ya_s, c_s, n_s, m_s = _mlstm(rows8(sa), rows8(ssm), gate_bias, mng, c0, n0, m0, 8, 1, bb_s)
        prefix = jnp.pad(state_conv[l], ((0, 0), (CONV_PAD - (CONV_W - 1), 0), (0, 0)))
        yb_s, conv_s = _conv(rows8(sb), prefix, cw, cbias, cg, cbeta, 8, 1)
        zd_s = jnp.pad(sd[:bd, None, :], ((0, 0), (0, GMLP_CHUNK - 1), (0, 0))).reshape(bd * GMLP_CHUNK, 2 * MIX_W)
        yd_s, v_s = _gmlp(zd_s, gg, gbeta, gws, gbsb, 1)
        q_heads = sq[:bd].reshape(bd, NSA_KV, 2, 1, HEAD_DIM)
        q4 = (q_heads * jnp.eye(NSA_KV, dtype=F32)[None, :, None, :, None]).reshape(bd, N_HEADS, LANES)
        q4 = jnp.pad(q4, ((0, 0), (0, 8 - N_HEADS), (0, 0)))
        gq = jnp.pad(ssm[:bd, 2 * N_HEADS:2 * N_HEADS + 3 * N_HEADS].reshape(bd, N_HEADS, 3),
                     ((0, 0), (0, 8 - N_HEADS), (0, LANES - 3)))
        assert npages % 4 == 0
        pp = next(c for c in (32, 16, 8, 4) if npages % c == 0)
        o8, win_s = _nsa_sample(page_table + l * n_pool, q4, gq, skv[:bd, None, :], swin[:bd, None, :],
                                swin[:bd, :, None], win_t, l * bd, wall, pool_t, pp)
        yc_s = jnp.concatenate([o8[:, h, (h // 2) * HEAD_DIM:(h // 2 + 1) * HEAD_DIM] for h in range(N_HEADS)], axis=1)
        xn_s, ht_s, s1t_s, s2t_s = _out_proj(xs, _pad_rows(ya_s[:, 0], ns_pad), _pad_rows(yb_s[:, 0], ns_pad),
                                             _pad_rows(yc_s, ns_pad),
                                             _pad_rows(yd_s.reshape(bd, GMLP_CHUNK, MIX_W)[:, 0], ns_pad),
                                             wo, g2, wq, k1, k2, ns_pad)
        rk2_s, e2_s, rr_s, cw8_s = _gate(s1t_s, s2t_s, ns_pad)
        xs = _peer(ht_s, ub, vtb, l, rk2_s, e2_s, rr_s, cw8_s, xn_s, fg, ns_pad, apc, final)
        new_s.append((skv[:bd].reshape(bd, 1, 4, NSA_KV, HEAD_DIM),
                      win_s.transpose(0, 2, 1).reshape(bd, n_win, 2, NSA_KV, HEAD_DIM),
                      jnp.stack([c_s[:, h * HEAD_DIM:(h + 1) * HEAD_DIM, h * HEAD_DIM:(h + 1) * HEAD_DIM]
                                 for h in range(N_HEADS)], axis=1),
                      n_s.reshape(bd, N_HEADS, HEAD_DIM),
                      m_s[:, 0, ::HEAD_DIM],
                      conv_s[:, CONV_PAD - (CONV_W - 1):],
                      v_s.reshape(bd, GMLP_CHUNK, MIX_W)[:, 0:1]))

    def stack(states, i):
        return jnp.stack([st[i] for st in states])

    y_prompt = xp.reshape(bp, s_len, d)
    y_sample = xs[:bd].reshape(bd, 1, d)
    return (y_prompt, y_sample,
            stack(new_p, 0), stack(new_s, 0),
            stack(new_p, 1), stack(new_s, 1),
            stack(new_p, 2), stack(new_s, 2),
            stack(new_p, 3), stack(new_s, 3),
            stack(new_p, 4), stack(new_s, 4),
            stack(new_p, 5), stack(new_s, 5),
            stack(new_s, 6))
```

```python
import functools

import jax
import jax.numpy as jnp
from jax import lax
from jax.experimental import pallas as pl
from jax.experimental.pallas import tpu as pltpu

F32 = jnp.float32
BF16 = jnp.bfloat16
HIGHEST = lax.Precision.HIGHEST

HEAD_DIM = 64
N_HEADS = 4
MIX_W = 256
NSA_KV = 2
NORM_EPS = 1e-6
NEG = -1e30
MLSTM_CHUNK = 128
CONV_W = 31
CONV_PAD = 32
CMP_STRIDE = 16
CMP_LEN = 32
SEL_BLOCK = 64
N_SELECT = 16
SEL_FORCE = 1e9
WINDOW = 512
ROPE_THETA = 10000.0
ATTN_SCALE = HEAD_DIM ** -0.5
GMLP_CHUNK = 128
PEER_TOPK = 16
RANK_MARK = 2.0 ** 100
PEER_UP_ROWS = 1
LANES = 128
VMEM_LIMIT = 48 * 1024 * 1024

NT = (((1,), (1,)), ((), ()))
TN = (((0,), (0,)), ((), ()))


def _cparams(sem):
    return pltpu.CompilerParams(dimension_semantics=sem, vmem_limit_bytes=VMEM_LIMIT)


def _iota(shape, dim):
    return lax.broadcasted_iota(jnp.int32, shape, dim)


def _sigmoid(x):
    return 1.0 / (1.0 + jnp.exp(-x))


def _gelu(x):
    c1 = -2.0 * 0.7978845608028654 * 1.4426950408889634
    c2 = c1 * 0.044715
    return x / (1.0 + jnp.exp2(x * (c1 + c2 * (x * x))))


def _dot(a, b, dims=None, precision=None):
    if dims is None:
        return jnp.dot(a, b, preferred_element_type=F32, precision=precision)
    return lax.dot_general(a, b, dims, preferred_element_type=F32, precision=precision)


def _block_ones(n, blk):
    return (_iota((n, n), 0) // blk == _iota((n, n), 1) // blk).astype(F32)


def _group_layernorm(x, ee, width):
    mu = _dot(x, ee, precision=HIGHEST) * (1.0 / width)
    d = x - mu
    var = _dot(d * d, ee, precision=HIGHEST) * (1.0 / width)
    return d * lax.rsqrt(var + NORM_EPS)


def _rmsnorm(x, g):
    return x * lax.rsqrt(jnp.mean(x * x, axis=-1, keepdims=True) + NORM_EPS) * g


def _masked_softmax(s, valid):
    sm = jnp.where(valid, s, NEG)
    p = jnp.where(valid, jnp.exp(sm - jnp.max(sm, axis=-1, keepdims=True)), 0.0)
    return p / jnp.maximum(jnp.sum(p, axis=-1, keepdims=True), 1e-30)


def _rope(x, cos, sin_signed):
    w = x.shape[1]
    fwd = pltpu.roll(x, w - HEAD_DIM // 2, 1)
    bwd = pltpu.roll(x, HEAD_DIM // 2, 1)
    first = (_iota(x.shape, 1) % HEAD_DIM) < HEAD_DIM // 2
    return x * cos + jnp.where(first, fwd, bwd) * sin_signed


def _in_kernel(x_ref, g_ref, w_ref, wsmall_ref, cos_ref, sin_ref,
               oa_ref, ob_ref, oq_ref, okv_ref, owin_ref, okaug_ref, ovsb_ref, owinb_ref, od_ref, os_ref,
               okvt_ref, owint_ref, *, npb):
    tm = x_ref.shape[0]
    y = _rmsnorm(x_ref[...], g_ref[...])
    yb = y.astype(BF16)
    z = _dot(yb, w_ref[...], NT)
    os_ref[...] = _dot(yb, wsmall_ref[...], NT)
    oa_ref[...] = z[:, 0:1024]
    ob_ref[...] = z[:, 1024:1536]
    cos = cos_ref[...]
    sin = sin_ref[...]
    cos2 = jnp.concatenate([cos, cos], axis=1)
    sin2 = jnp.concatenate([sin, sin], axis=1)
    oq_ref[...] = _rope(z[:, 1536:1792], cos2, sin2) * ATTN_SCALE
    kc = _rope(z[:, 1792:1920], cos, sin)
    vc = z[:, 1920:2048]
    ks = _rope(z[:, 2048:2176], cos, sin)
    vs = z[:, 2176:2304]
    kw = _rope(z[:, 2304:2432], cos, sin)
    vw = z[:, 2432:2560]
    okv = jnp.concatenate([kc, vc, ks, vs], axis=1)
    okv_ref[...] = okv
    owin = jnp.concatenate([kw, vw], axis=1)
    owin_ref[...] = owin
    okvt_ref[0] = okv.T
    owint_ref[0] = owin.T
    pos = (pl.program_id(0) % npb) * tm + _iota((tm, LANES), 0)
    onehot = jnp.where(_iota((tm, LANES), 1) == pos // SEL_BLOCK, 1.0, 0.0)
    okaug_ref[...] = jnp.concatenate([ks, onehot], axis=1).astype(BF16)
    ovsb_ref[...] = vs.astype(BF16)
    owinb_ref[...] = owin.astype(BF16)
    od_ref[...] = z[:, 2560:3072]


def _in_proj(x, g, w, wsmall, cos_t, sin_t, tm):
    n, d = x.shape
    npb = cos_t.shape[0] // tm
    widths = (1024, 512, 256, 512, 256, 256, 128, 256, 512, 128)
    dtypes = (F32, F32, F32, F32, F32, BF16, BF16, BF16, F32, F32)
    row = lambda i: (i, 0)
    return pl.pallas_call(
        functools.partial(_in_kernel, npb=npb),
        grid=(n // tm,),
        in_specs=[pl.BlockSpec((tm, d), row),
                  pl.BlockSpec((1, d), lambda i: (0, 0)),
                  pl.BlockSpec(w.shape, lambda i: (0, 0)),
                  pl.BlockSpec(wsmall.shape, lambda i: (0, 0)),
                  pl.BlockSpec((tm, LANES), lambda i: (i % npb, 0)),
                  pl.BlockSpec((tm, LANES), lambda i: (i % npb, 0))],
        out_specs=[pl.BlockSpec((tm, wd), row) for wd in widths]
                  + [pl.BlockSpec((1, wd, tm), lambda i: (i // npb, 0, i % npb)) for wd in (4 * LANES, 2 * LANES)],
        out_shape=[jax.ShapeDtypeStruct((n, wd), dt) for wd, dt in zip(widths, dtypes)]
                  + [jax.ShapeDtypeStruct((n // (npb * tm), wd, npb * tm), F32) for wd in (4 * LANES, 2 * LANES)],
        compiler_params=_cparams(("parallel",)), name="in_proj",
    )(x, g, w, wsmall, cos_t, sin_t)


def _mlstm_kernel(za_ref, zs_ref, bias_ref, ng_ref, c0_ref, n0_ref, m0_ref,
                  y_ref, c_ref, n_ref, m_ref, *, L, t_valid, bb):
    @pl.when(pl.program_id(1) == 0)
    def _():
        c_ref[...] = c0_ref[...]
        n_ref[...] = n0_ref[...]
        m_ref[...] = m0_ref[...]

    head_of_lane = _iota((1, MIX_W), 1) // HEAD_DIM
    causal = _iota((L, L), 0) >= _iota((L, L), 1)
    tril = causal.astype(F32)
    ee = _block_ones(MIX_W, HEAD_DIM)
    bdiag = ee > 0.5
    row128 = _iota((L, LANES), 0)
    lane128 = _iota((L, LANES), 1)
    for bi in range(bb):
        za = za_ref[bi]
        q = za[:, 0:256]
        k = za[:, 256:512] * (HEAD_DIM ** -0.5)
        v = za[:, 512:768]
        o = za[:, 768:1024]
        gi = zs_ref[bi] + bias_ref[...]
        ls = jnp.minimum(gi, 0.0) - jnp.log(1.0 + jnp.exp(-jnp.abs(gi)))
        if t_valid < L:
            live = row128 < t_valid
            ig = jnp.where(live, gi, NEG)
            ls = jnp.where(live, ls, 0.0)
        else:
            ig = gi
        gmat = jnp.where(lane128 < N_HEADS, ig, 0.0)
        lfm = jnp.where((lane128 >= N_HEADS) & (lane128 < 2 * N_HEADS), ls, 0.0)
        fc = _dot(tril, lfm, precision=HIGHEST)
        g_t = gmat.T
        f_t = fc.T
        mrow = m_ref[bi]
        nrow = n_ref[bi]
        cm = c_ref[bi]
        qb = q.astype(BF16)
        kb = k.astype(BF16)
        vb = v.astype(BF16)
        num = jnp.zeros((L, MIX_W), F32)
        decay_b = jnp.zeros((L, MIX_W), F32)
        dens_b = jnp.zeros((L, MIX_W), F32)
        mt_b = jnp.zeros((L, MIX_W), F32)
        w_b = jnp.zeros((L, MIX_W), F32)
        cd_b = jnp.zeros((1, MIX_W), F32)
        mnew_b = jnp.zeros((1, MIX_W), F32)
        for h in range(N_HEADS):
            hm = head_of_lane == h
            f_col = fc[:, N_HEADS + h:N_HEADS + h + 1]
            ig_col = gmat[:, h:h + 1]
            f_row = f_t[N_HEADS + h:N_HEADS + h + 1, :]
            ig_row = g_t[h:h + 1, :]
            mp = mrow[:, h * HEAD_DIM:h * HEAD_DIM + 1]
            dmat = jnp.where(causal, (f_col - f_row) + ig_row, NEG)
            m_inter = mp + f_col
            m_t = jnp.maximum(m_inter, jnp.max(dmat, axis=1, keepdims=True))
            d_exp = jnp.exp(dmat - m_t)
            decay = jnp.exp(m_inter - m_t)
            qh = jnp.where(hm, q, 0.0).astype(BF16)
            s = _dot(qh, kb, NT) * d_exp
            num = jnp.where(hm, _dot(s.astype(BF16), vb), num)
            dens_b = jnp.where(hm, jnp.sum(s, axis=1, keepdims=True), dens_b)
            decay_b = jnp.where(hm, decay, decay_b)
            mt_b = jnp.where(hm, m_t, mt_b)
            f_last = f_col[L - 1:L, :]
            m_new = m_t[L - 1:L, :]
            w_b = jnp.where(hm, jnp.exp((f_last - f_col) + ig_col - m_new), w_b)
            cd_b = jnp.where(hm, jnp.exp(mp + f_last - m_new), cd_b)
            mnew_b = jnp.where(hm, m_new, mnew_b)
        inter = _dot(qb, cm.astype(BF16), NT)
        nq_b = _dot(q * nrow, ee, precision=HIGHEST)
        hnum = num + decay_b * inter
        den_b = dens_b + decay_b * nq_b
        hh = hnum / jnp.maximum(jnp.abs(den_b), jnp.exp(-mt_b))
        hn = _group_layernorm(hh, ee, HEAD_DIM) * ng_ref[...]
        y_ref[bi] = hn * _sigmoid(o)
        kw = k * w_b
        upd = _dot(vb, kw.astype(BF16), TN)
        c_ref[bi] = jnp.where(bdiag, cd_b * cm + upd, 0.0)
        n_ref[bi] = cd_b * nrow + jnp.sum(kw, axis=0, keepdims=True)
        m_ref[bi] = mnew_b


def _mlstm(za, zs, bias, ng, c0, n0, m0, L, t_valid, bb):
    b, t, _ = za.shape
    nc = t // L
    st = lambda i, c: (i, 0, 0)
    return pl.pallas_call(
        functools.partial(_mlstm_kernel, L=L, t_valid=t_valid, bb=bb),
        grid=(b // bb, nc),
        in_specs=[pl.BlockSpec((bb, L, 1024), lambda i, c: (i, c, 0)),
                  pl.BlockSpec((bb, L, LANES), lambda i, c: (i, c, 0)),
                  pl.BlockSpec((1, LANES), lambda i, c: (0, 0)),
                  pl.BlockSpec((1, MIX_W), lambda i, c: (0, 0)),
                  pl.BlockSpec((bb, MIX_W, MIX_W), st),
                  pl.BlockSpec((bb, 1, MIX_W), st),
                  pl.BlockSpec((bb, 1, MIX_W), st)],
        out_specs=[pl.BlockSpec((bb, L, MIX_W), lambda i, c: (i, c, 0)),
                   pl.BlockSpec((bb, MIX_W, MIX_W), st),
                   pl.BlockSpec((bb, 1, MIX_W), st),
                   pl.BlockSpec((bb, 1, MIX_W), st)],
        out_shape=[jax.ShapeDtypeStruct((b, t, MIX_W), F32),
                   jax.ShapeDtypeStruct((b, MIX_W, MIX_W), F32),
                   jax.ShapeDtypeStruct((b, 1, MIX_W), F32),
                   jax.ShapeDtypeStruct((b, 1, MIX_W), F32)],
        compiler_params=_cparams(("parallel", "arbitrary")), name="mlstm",
    )(za, zs, bias, ng, c0, n0, m0)


def _conv_kernel(zb_ref, pre_ref, w_ref, cb_ref, g_ref, beta_ref, y_ref, st_ref, ext_ref, *, tt, tv):
    @pl.when(pl.program_id(1) == 0)
    def _():
        ext_ref[0:CONV_PAD, :] = pre_ref[0]

    z = zb_ref[0]
    u = z[:, :MIX_W] * _sigmoid(z[:, MIX_W:])
    ext_ref[CONV_PAD:CONV_PAD + tt, :] = u
    off = CONV_PAD - (CONV_W - 1)
    acc = jnp.zeros((tt, MIX_W), F32)
    for j in range(CONV_W):
        acc = acc + w_ref[j:j + 1, :] * ext_ref[off + j:off + j + tt, :]
    ee = _block_ones(MIX_W, HEAD_DIM)
    y = _group_layernorm(acc + cb_ref[...], ee, HEAD_DIM) * g_ref[...] + beta_ref[...]
    y_ref[0] = y * _sigmoid(y)
    st_ref[0] = ext_ref[tv:tv + CONV_PAD, :]
    ext_ref[0:CONV_PAD, :] = ext_ref[tt:tt + CONV_PAD, :]


def _conv(zb, prefix, w, cb, g, beta, tt, tv):
    b, t, _ = zb.shape
    vec = lambda i, j: (0, 0)
    return pl.pallas_call(
        functools.partial(_conv_kernel, tt=tt, tv=tv),
        grid=(b, t // tt),
        in_specs=[pl.BlockSpec((1, tt, 2 * MIX_W), lambda i, j: (i, j, 0)),
                  pl.BlockSpec((1, CONV_PAD, MIX_W), lambda i, j: (i, 0, 0)),
                  pl.BlockSpec((CONV_W, MIX_W), vec),
                  pl.BlockSpec((1, MIX_W), vec),
                  pl.BlockSpec((1, MIX_W), vec),
                  pl.BlockSpec((1, MIX_W), vec)],
        out_specs=[pl.BlockSpec((1, tt, MIX_W), lambda i, j: (i, j, 0)),
                   pl.BlockSpec((1, CONV_PAD, MIX_W), lambda i, j: (i, 0, 0))],
        out_shape=[jax.ShapeDtypeStruct((b, t, MIX_W), F32),
                   jax.ShapeDtypeStruct((b, CONV_PAD, MIX_W), F32)],
        scratch_shapes=[pltpu.VMEM((CONV_PAD + tt, MIX_W), F32)],
        compiler_params=_cparams(("parallel", "arbitrary")), name="conv",
    )(zb, prefix, w, cb, g, beta)


def _gmlp_kernel(zd_ref, g_ref, beta_ref, ws_ref, bsb_ref, y_ref, v_ref, *, cpb):
    z = zd_ref[...]
    ee = _block_ones(MIX_W, HEAD_DIM)
    u = _gelu(z[:, :MIX_W])
    vv = _group_layernorm(_gelu(z[:, MIX_W:]), ee, HEAD_DIM) * g_ref[...] + beta_ref[...]
    v_ref[...] = vv
    tril = _iota((GMLP_CHUNK, GMLP_CHUNK), 0) >= _iota((GMLP_CHUNK, GMLP_CHUNK), 1)
    head_of_lane = _iota((1, MIX_W), 1) // HEAD_DIM
    wm = [jnp.where(tril, ws_ref[h], 0.0).astype(BF16) for h in range(N_HEADS)]
    for c in range(cpb):
        rows = slice(c * GMLP_CHUNK, (c + 1) * GMLP_CHUNK)
        vc = vv[rows].astype(BF16)
        mixed = jnp.zeros((GMLP_CHUNK, MIX_W), F32)
        for h in range(N_HEADS):
            mixed = jnp.where(head_of_lane == h, _dot(wm[h], vc), mixed)
        y_ref[rows, :] = u[rows] * (mixed + bsb_ref[...])


def _gmlp(zd, g, beta, ws, bsb, cpb):
    n = zd.shape[0]
    rows = cpb * GMLP_CHUNK
    vec = lambda i: (0, 0)
    return pl.pallas_call(
        functools.partial(_gmlp_kernel, cpb=cpb),
        grid=(n // rows,),
        in_specs=[pl.BlockSpec((rows, 2 * MIX_W), lambda i: (i, 0)),
                  pl.BlockSpec((1, MIX_W), vec),
                  pl.BlockSpec((1, MIX_W), vec),
                  pl.BlockSpec((N_HEADS, GMLP_CHUNK, GMLP_CHUNK), lambda i: (0, 0, 0)),
                  pl.BlockSpec((GMLP_CHUNK, MIX_W), vec)],
        out_specs=[pl.BlockSpec((rows, MIX_W), lambda i: (i, 0)),
                   pl.BlockSpec((rows, MIX_W), lambda i: (i, 0))],
        out_shape=[jax.ShapeDtypeStruct((n, MIX_W), F32),
                   jax.ShapeDtypeStruct((n, MIX_W), F32)],
        compiler_params=_cparams(("parallel",)), name="gmlp",
    )(zd, g, beta, ws, bsb)


def _combine_lo_hi(lo, hi):
    r = lo.shape[0]
    nxt = pltpu.roll(hi, r - 1, 0)
    return lo + jnp.where(_iota(lo.shape, 0) < r - 1, nxt, 0.0)


def _cmp_kernel(kv_ref, wlo_ref, whi_ref, o_ref):
    x = kv_ref[0]
    x3 = x.reshape(x.shape[0] // CMP_STRIDE, CMP_STRIDE, x.shape[1])
    lo = jnp.sum(x3 * wlo_ref[...][None], axis=1)
    hi = jnp.sum(x3 * whi_ref[...][None], axis=1)
    o_ref[0] = _combine_lo_hi(lo, hi)


def _cmp(okv3, wlo, whi):
    b, s, _ = okv3.shape
    return pl.pallas_call(
        _cmp_kernel,
        grid=(b,),
        in_specs=[pl.BlockSpec((1, s, 2 * LANES), lambda i: (i, 0, 0)),
                  pl.BlockSpec((CMP_STRIDE, 2 * LANES), lambda i: (0, 0)),
                  pl.BlockSpec((CMP_STRIDE, 2 * LANES), lambda i: (0, 0))],
        out_specs=pl.BlockSpec((1, s // CMP_STRIDE, 2 * LANES), lambda i: (i, 0, 0)),
        out_shape=jax.ShapeDtypeStruct((b, s // CMP_STRIDE, 2 * LANES), F32),
        compiler_params=_cparams(("parallel",)), name="nsa_cmp",
    )(okv3, wlo, whi)


def _nsa_prompt_kernel(q_ref, gs_ref, cb_ref, ka_ref, vs_ref, win_ref, y_ref, s_buf, *, tq, s_len, n_top, tk):
    start = pl.program_id(1) * tq
    nc = s_len // CMP_STRIDE
    ns = s_len // SEL_BLOCK
    q = q_ref[0]
    gl = _sigmoid(gs_ref[0])
    lo_half = _iota((tq, LANES), 1) < HEAD_DIM
    q_rows = []
    for g in range(NSA_KV):
        qpair = q[:, g * LANES:(g + 1) * LANES]
        swapped = pltpu.roll(qpair, HEAD_DIM, 1)
        if g == 0:
            q_rows += [jnp.where(lo_half, qpair, 0.0), jnp.where(lo_half, swapped, 0.0)]
        else:
            q_rows += [jnp.where(lo_half, 0.0, swapped), jnp.where(lo_half, 0.0, qpair)]
    q128 = jnp.concatenate(q_rows, axis=0)
    q128b = q128.astype(BF16)
    trow = start + _iota((tq, 1), 0)
    t2 = jnp.concatenate([trow, trow], axis=0)
    t4 = jnp.concatenate([t2, t2], axis=0)
    cb = cb_ref[0]
    sc = _dot(q128b, cb[:, 0:LANES].astype(BF16), NT)
    validc = (_iota((1, nc), 1) * CMP_STRIDE + CMP_LEN) <= (t4 + 1)
    p = _masked_softmax(sc, validc)
    o_c = _dot(p.astype(BF16), cb[:, LANES:2 * LANES].astype(BF16))
    psum = jnp.concatenate([p[0:tq] + p[tq:2 * tq], p[2 * tq:3 * tq] + p[3 * tq:4 * tq]], axis=0)
    pool4 = (_iota((nc, ns), 0) // (SEL_BLOCK // CMP_STRIDE) == _iota((nc, ns), 1)).astype(F32)
    imp = _dot(psum, pool4, precision=HIGHEST)
    jidx = _iota((1, ns), 1)
    imp = jnp.where(jidx == t2 // SEL_BLOCK, SEL_FORCE, imp)
    imp = jnp.where(jidx * SEL_BLOCK <= t2, imp, -1.0)
    if n_top < ns:
        x = jnp.concatenate([imp[0:tq].T, imp[tq:2 * tq].T], axis=1)
        jrow = _iota((ns, 2 * tq), 0).astype(F32)
        for _ in range(n_top):
            m = jnp.max(x, axis=0, keepdims=True)
            first = jnp.min(jnp.where(x == m, jrow, float(ns)), axis=0, keepdims=True)
            x = jnp.where(jrow == first, -3.0, x)
        sel_t = jnp.where(x == -3.0, 1.0, 0.0)
        sel = jnp.concatenate([sel_t[:, 0:tq].T, sel_t[:, tq:2 * tq].T], axis=0)
    else:
        sel = jnp.ones((2 * tq, ns), F32)
    selneg = (sel - 1.0) * (-NEG)
    if ns < LANES:
        selneg = jnp.concatenate([selneg, jnp.zeros((2 * tq, LANES - ns), F32)], axis=1)
    selneg4 = jnp.concatenate([selneg[0:tq], selneg[0:tq], selneg[tq:2 * tq], selneg[tq:2 * tq]], axis=0)
    qaug = jnp.concatenate([q128, selneg4], axis=1).astype(BF16)

    def scores(c):
        return _dot(qaug, ka_ref[0, pl.ds(pl.multiple_of(c * tk, tk), tk), :], NT)

    def absorb(c, s, carry):
        m, l, acc = carry
        m_new = jnp.maximum(m, jnp.max(s, axis=1, keepdims=True))
        pt = jnp.exp(s - m_new)
        alpha = jnp.exp(m - m_new)
        l = alpha * l + jnp.sum(pt, axis=1, keepdims=True)
        acc = alpha * acc + _dot(pt.astype(BF16), vs_ref[0, pl.ds(pl.multiple_of(c * tk, tk), tk), :])
        return m_new, l, acc

    def tile(c, carry):
        s = s_buf[...]
        s_buf[...] = scores(c + 1)
        return absorb(c, s, carry)

    n_tiles = (start + tq + tk - 1) // tk
    carry = (jnp.full((4 * tq, 1), NEG, F32), jnp.zeros((4 * tq, 1), F32), jnp.zeros((4 * tq, LANES), F32))
    s_buf[...] = scores(0)
    carry = lax.fori_loop(0, n_tiles - 1, tile, carry)
    last = n_tiles - 1
    s_diag = jnp.where(last * tk + _iota((1, tk), 1) <= t4, s_buf[...], NEG)
    _, l, acc = absorb(last, s_diag, carry)
    o_s = acc / jnp.maximum(l, 1e-30)
    wl = WINDOW + tq
    w0 = pl.multiple_of(jnp.maximum(start - WINDOW, 0), tq)
    sw = _dot(q128b, win_ref[0, pl.ds(w0, wl), 0:LANES], NT)
    kposw = w0 + _iota((1, wl), 1)
    validw = (kposw <= t4) & (kposw > t4 - WINDOW)
    o_w = _dot(_masked_softmax(sw, validw).astype(BF16), win_ref[0, pl.ds(w0, wl), LANES:2 * LANES])
    heads = []
    for hh in range(N_HEADS):
        col = 2 * N_HEADS + hh * 3
        rows = slice(hh * tq, (hh + 1) * tq)
        heads.append(o_c[rows] * gl[:, col:col + 1] + o_s[rows] * gl[:, col + 1:col + 2]
                     + o_w[rows] * gl[:, col + 2:col + 3])
    y_ref[0] = jnp.concatenate([jnp.where(lo_half, heads[0], pltpu.roll(heads[1], HEAD_DIM, 1)),
                                jnp.where(lo_half, pltpu.roll(heads[2], HEAD_DIM, 1), heads[3])], axis=1)


def _nsa_prompt(oq3, os3, cb, okaug3, ovsb3, owinb3, tq):
    b, s, _ = oq3.shape
    ns = s // SEL_BLOCK
    assert ns <= LANES
    n_top = min(N_SELECT, ns)
    tk = min(512, s)
    full = lambda i, j: (i, 0, 0)
    return pl.pallas_call(
        functools.partial(_nsa_prompt_kernel, tq=tq, s_len=s, n_top=n_top, tk=tk),
        grid=(b, s // tq),
        in_specs=[pl.BlockSpec((1, tq, MIX_W), lambda i, j: (i, j, 0)),
                  pl.BlockSpec((1, tq, LANES), lambda i, j: (i, j, 0)),
                  pl.BlockSpec((1, s // CMP_STRIDE, 2 * LANES), full),
                  pl.BlockSpec((1, s, 2 * LANES), full),
                  pl.BlockSpec((1, s, LANES), full),
                  pl.BlockSpec((1, s, 2 * LANES), full)],
        out_specs=pl.BlockSpec((1, tq, MIX_W), lambda i, j: (i, j, 0)),
        out_shape=jax.ShapeDtypeStruct((b, s, MIX_W), F32),
        scratch_shapes=[pltpu.VMEM((N_HEADS * tq, tk), F32)],
        compiler_params=_cparams(("parallel", "arbitrary")), name="nsa_prompt",
    )(oq3, os3, cb, okaug3, ovsb3, owinb3)


def _nsa_sample_kernel(pt_ref, q4_ref, gq_ref, newkv_ref, newwin_ref, newcol_ref, win_ref, wall_ref, *rest,
                       pp, npages, n_top):
    pool_refs = rest[:pp]
    o_ref, wo_ref, lo_s, hi_s, m_s, l_s, a_s, a_stage = rest[pp:]
    i = pl.program_id(1)
    nb = 2 * npages
    nbp = m_s.shape[1]
    ncp = npages * (LANES // CMP_STRIDE)
    past = npages * LANES
    q4 = q4_ref[0]
    q4b = q4.astype(BF16)

    @pl.when(i == 0)
    def _():
        m_s[...] = jnp.zeros(m_s.shape, F32)
        l_s[...] = jnp.zeros(l_s.shape, F32)
        if nb < nbp:
            a_s[...] = jnp.zeros(a_s.shape, F32)

    lane8 = _iota((8, LANES), 1)
    lanej = _iota((8, nbp), 1)
    set_of_lane = _iota((8, 2 * LANES), 1) // HEAD_DIM
    bpp = LANES // SEL_BLOCK
    wall = wall_ref[...].astype(BF16)
    m_acc = m_s[...]
    l_acc = l_s[...]
    los, his = [], []
    for kk in range(pp):
        page = pool_refs[kk][0]
        pooled = _dot(wall, page[0:2 * LANES, :].astype(BF16), NT)
        lo8 = jnp.zeros((8, 2 * LANES), F32)
        hi8 = jnp.zeros((8, 2 * LANES), F32)
        for st in range(4):
            lo8 = jnp.where(set_of_lane == st, pooled[st * 16:st * 16 + 8], lo8)
            hi8 = jnp.where(set_of_lane == st, pooled[st * 16 + 8:st * 16 + 16], hi8)
        los.append(lo8)
        his.append(hi8)
        ks_t = page[2 * LANES:3 * LANES, :].astype(BF16)
        vs_t = page[3 * LANES:4 * LANES, :].astype(BF16)
        s = _dot(q4b, ks_t)
        probs = []
        for jb in range(bpp):
            inb = (lane8 >= jb * SEL_BLOCK) & (lane8 < (jb + 1) * SEL_BLOCK)
            sm = jnp.where(inb, s, NEG)
            m = jnp.max(sm, axis=1, keepdims=True)
            p = jnp.where(inb, jnp.exp(sm - m), 0.0)
            probs.append(p)
            j = i * (pp * bpp) + kk * bpp + jb
            m_acc = jnp.where(lanej == j, m, m_acc)
            l_acc = jnp.where(lanej == j, jnp.sum(p, axis=1, keepdims=True), l_acc)
        a = _dot(jnp.concatenate(probs, axis=0).astype(BF16), vs_t, NT)
        for jb in range(bpp):
            for hh in range(N_HEADS):
                a_stage[hh, kk * bpp + jb:kk * bpp + jb + 1, :] = a[jb * 8 + hh:jb * 8 + hh + 1, :]
    m_s[...] = m_acc
    l_s[...] = l_acc
    r0 = pl.multiple_of(i * (pp * 8), pp * 8)
    lo_s[pl.ds(r0, pp * 8), :] = jnp.concatenate(los, axis=0)
    hi_s[pl.ds(r0, pp * 8), :] = jnp.concatenate(his, axis=0)
    b0 = pl.multiple_of(i * (pp * bpp), pp * bpp)
    for hh in range(N_HEADS):
        a_s[hh, pl.ds(b0, pp * bpp), :] = a_stage[hh]

    @pl.when(i == pl.num_programs(1) - 1)
    def _():
        row8 = _iota((8, 1), 0)
        cbm = _combine_lo_hi(lo_s[...], hi_s[...])
        kcb = cbm[:, 0:LANES].astype(BF16)
        vcb = cbm[:, LANES:2 * LANES].astype(BF16)
        sc = _dot(q4b, kcb, NT)
        validc = (_iota((1, ncp), 1) * CMP_STRIDE + CMP_LEN) <= past + 1
        p = _masked_softmax(sc, validc)
        o_c = _dot(p.astype(BF16), vcb)
        pg = jnp.where(row8 == 0, p[0:1] + p[1:2], jnp.where(row8 == 1, p[2:3] + p[3:4], 0.0))
        pool4 = (_iota((ncp, nbp), 0) // (SEL_BLOCK // CMP_STRIDE) == _iota((ncp, nbp), 1)).astype(F32)
        imp2 = _dot(pg, pool4, precision=HIGHEST)
        ii = _iota((nbp, nbp), 0)
        jj = _iota((nbp, nbp), 1)
        sels = []
        for g in range(NSA_KV):
            mx = jnp.broadcast_to(imp2[g:g + 1, :], (nbp, nbp))
            mt = mx.T
            beats = ((mt > mx) | ((mt == mx) & (ii < jj))) & (ii < nb)
            rank = jnp.sum(jnp.where(beats, 1.0, 0.0), axis=0, keepdims=True)
            sels.append(jnp.where((rank < n_top - 1) & (_iota((1, nbp), 1) < nb), 1.0, 0.0))
        sel8 = jnp.where(row8 < 2, sels[0], sels[1]) > 0.5
        newkv = newkv_ref[0]
        ksn = newkv[:, 2 * LANES:3 * LANES]
        vsn = newkv[:, 3 * LANES:4 * LANES]
        s_new = jnp.sum(q4 * ksn, axis=1, keepdims=True)
        mrow = m_s[...]
        m_all = jnp.maximum(jnp.max(jnp.where(sel8, mrow, NEG), axis=1, keepdims=True), s_new)
        wj = jnp.where(sel8, jnp.exp(mrow - m_all), 0.0)
        w_new = jnp.exp(s_new - m_all)
        ltot = jnp.sum(wj * l_s[...], axis=1, keepdims=True) + w_new
        osum = w_new * vsn
        for hh in range(N_HEADS):
            osum = osum + jnp.where(row8 == hh, _dot(wj, a_s[hh], precision=HIGHEST), 0.0)
        o_s = osum / jnp.maximum(ltot, 1e-30)
        win_t = win_ref[0]
        nw = win_t.shape[1]
        lane_w = _iota((1, nw), 1)
        sw = _dot(q4b, win_t[0:LANES, :].astype(BF16))
        validw = (past - nw + lane_w) > past - WINDOW
        neww = newwin_ref[0]
        sw_new = jnp.sum(q4 * neww[:, 0:LANES], axis=1, keepdims=True)
        mw = jnp.maximum(jnp.max(jnp.where(validw, sw, NEG), axis=1, keepdims=True), sw_new)
        pw = jnp.where(validw, jnp.exp(sw - mw), 0.0)
        pn = jnp.exp(sw_new - mw)
        zw = jnp.sum(pw, axis=1, keepdims=True) + pn
        o_w = (_dot(pw.astype(BF16), win_t[LANES:2 * LANES, :].astype(BF16), NT) + pn * neww[:, LANES:2 * LANES]) / zw
        gg = _sigmoid(gq_ref[0])
        o_ref[0] = o_c * gg[:, 0:1] + o_s * gg[:, 1:2] + o_w * gg[:, 2:3]
        wo_ref[0] = jnp.where(lane_w == nw - 1, newcol_ref[0], pltpu.roll(win_t, nw - 1, 1))


def _nsa_sample(pt, q4, gq, newkv, newwin, newcol, win_t, win_off, wall, pool_t, pp):
    bd, npages = pt.shape
    nw = win_t.shape[2]
    nb = 2 * npages
    nbp = -(-nb // LANES) * LANES
    n_top = min(N_SELECT, nb + 1)
    per_b = lambda b, i, pt_ref: (b, 0, 0)
    const = lambda b, i, pt_ref: (0, 0)

    def page_map(kk):
        return lambda b, i, pt_ref: (pt_ref[b, i * pp + kk], 0, 0)

    grid_spec = pltpu.PrefetchScalarGridSpec(
        num_scalar_prefetch=1,
        grid=(bd, npages // pp),
        in_specs=[pl.BlockSpec((1, 8, LANES), per_b),
                  pl.BlockSpec((1, 8, LANES), per_b),
                  pl.BlockSpec((1, 1, 4 * LANES), per_b),
                  pl.BlockSpec((1, 1, 2 * LANES), per_b),
                  pl.BlockSpec((1, 2 * LANES, 1), per_b),
                  pl.BlockSpec((1, 2 * LANES, nw), lambda b, i, pt_ref: (win_off + b, 0, 0)),
                  pl.BlockSpec(wall.shape, const)]
                 + [pl.BlockSpec((1, 4 * LANES, LANES), page_map(kk)) for kk in range(pp)],
        out_specs=[pl.BlockSpec((1, 8, LANES), per_b),
                   pl.BlockSpec((1, 2 * LANES, nw), per_b)],
        scratch_shapes=[pltpu.VMEM((npages * 8, 2 * LANES), F32),
                        pltpu.VMEM((npages * 8, 2 * LANES), F32),
                        pltpu.VMEM((8, nbp), F32),
                        pltpu.VMEM((8, nbp), F32),
                        pltpu.VMEM((N_HEADS, nbp, LANES), F32),
                        pltpu.VMEM((N_HEADS, pp * (LANES // SEL_BLOCK), LANES), F32)])
    return pl.pallas_call(
        functools.partial(_nsa_sample_kernel, pp=pp, npages=npages, n_top=n_top),
        grid_spec=grid_spec,
        out_shape=[jax.ShapeDtypeStruct((bd, 8, LANES), F32),
                   jax.ShapeDtypeStruct((bd, 2 * LANES, nw), F32)],
        compiler_params=_cparams(("parallel", "arbitrary")), name="nsa_sample",
    )(pt, q4, gq, newkv, newwin, newcol, win_t, wall, *([pool_t] * pp))


def _split_bf16(x):
    hi = x.astype(BF16)
    return hi, (x - hi.astype(F32)).astype(BF16)


def _out_kernel(x_ref, ya_ref, yb_ref, yc_ref, yd_ref, wo_ref, g2_ref, wq_ref, k1_ref, k2_ref,
                xn_ref, ht_ref, s1_ref, s2_ref, *, nh, dq):
    acc = x_ref[...]
    for idx, y_ref in enumerate((ya_ref, yb_ref, yc_ref, yd_ref)):
        acc = acc + _dot(y_ref[...].astype(BF16), wo_ref[idx * MIX_W:(idx + 1) * MIX_W, :])
    xn_ref[...] = acc
    h2 = _rmsnorm(acc, g2_ref[...])
    ht_ref[...] = h2.T.astype(BF16)
    h_hi, h_lo = _split_bf16(h2)
    q = _dot(h_hi, wq_ref[0]) + _dot(h_lo, wq_ref[0]) + _dot(h_hi, wq_ref[1])
    q_hi, q_lo = _split_bf16(q)
    half = dq // 2
    for h in range(nh):
        for s_ref, k_ref, cols in ((s1_ref, k1_ref, slice(h * dq, h * dq + half)),
                                   (s2_ref, k2_ref, slice(h * dq + half, (h + 1) * dq))):
            s_ref[h] = (_dot(k_ref[0, h], q_hi[:, cols], NT) + _dot(k_ref[1, h], q_hi[:, cols], NT)
                        + _dot(k_ref[0, h], q_lo[:, cols], NT))


def _out_proj(x, ya, yb, yc, yd, wo, g2, wq, k1, k2, tm):
    n, d = x.shape
    _, nh, nk, half = k1.shape
    row = lambda i: (i, 0)
    c2 = lambda i: (0, 0)
    return pl.pallas_call(
        functools.partial(_out_kernel, nh=nh, dq=2 * half),
        grid=(n // tm,),
        in_specs=[pl.BlockSpec((tm, d), row)] + [pl.BlockSpec((tm, MIX_W), row)] * 4
                 + [pl.BlockSpec(wo.shape, c2), pl.BlockSpec((1, d), c2), pl.BlockSpec(wq.shape, lambda i: (0, 0, 0)),
                    pl.BlockSpec(k1.shape, lambda i: (0, 0, 0, 0)), pl.BlockSpec(k2.shape, lambda i: (0, 0, 0, 0))],
        out_specs=[pl.BlockSpec((tm, d), row),
                   pl.BlockSpec((d, tm), lambda i: (0, i)),
                   pl.BlockSpec((nh, nk, tm), lambda i: (0, 0, i)),
                   pl.BlockSpec((nh, nk, tm), lambda i: (0, 0, i))],
        out_shape=[jax.ShapeDtypeStruct((n, d), F32),
                   jax.ShapeDtypeStruct((d, n), BF16),
                   jax.ShapeDtypeStruct((nh, nk, n), F32),
                   jax.ShapeDtypeStruct((nh, nk, n), F32)],
        compiler_params=_cparams(("parallel",)), name="out_proj",
    )(x, ya, yb, yc, yd, wo, g2, wq, k1, k2)


_PAIR_COUNTS = tuple(PEER_TOPK // (i + 1) for i in range(PEER_TOPK))
_PAIR_ROWS = -(-sum(_PAIR_COUNTS) // 8) * 8


def _gate_kernel(s1_ref, s2_ref, rk2_ref, e2_ref, rr_ref, c_ref, v1_s, v2_s, c_s):
    x1 = s1_ref[0]
    x2 = s2_ref[0]
    for i in range(PEER_TOPK):
        m1 = jnp.max(x1, axis=0, keepdims=True)
        m2 = jnp.max(x2, axis=0, keepdims=True)
        v1_s[i:i + 1, :] = m1
        v2_s[i:i + 1, :] = m2
        marker = -RANK_MARK * (1.0 + i / PEER_TOPK)
        x1 = jnp.where(x1 == m1, marker, x1)
        x2 = jnp.where(x2 == m2, marker, x2)
    rank1 = jnp.where(x1 <= -RANK_MARK, (x1 * (-1.0 / RANK_MARK) - 1.0) * PEER_TOPK, float(PEER_TOPK))
    rank2 = jnp.where(x2 <= -RANK_MARK, (x2 * (-1.0 / RANK_MARK) - 1.0) * PEER_TOPK, float(PEER_TOPK))
    v1 = v1_s[...]
    v2 = v2_s[...]
    off = 0
    for i, cnt in enumerate(_PAIR_COUNTS):
        c_s[off:off + cnt, :] = v1[i:i + 1, :] + v2[0:cnt, :]
        off += cnt
    if off < _PAIR_ROWS:
        c_s[off:_PAIR_ROWS, :] = jnp.full((_PAIR_ROWS - off, v1.shape[1]), NEG, F32)
    c = c_s[...]
    mx = v1[0:1, :] + v2[0:1, :]
    z = jnp.zeros_like(mx)
    m = mx
    for i in range(PEER_TOPK):
        m = jnp.max(c, axis=0, keepdims=True)
        z = z + jnp.exp(m - mx)
        c = jnp.where(c == m, NEG, c)
    tau = m
    passing = jnp.zeros(v1.shape, F32)
    for jj in range(PEER_TOPK):
        passing = passing + jnp.where(v1 + v2[jj:jj + 1, :] >= tau, 1.0, 0.0)
    rr = jnp.zeros(rank1.shape, F32)
    for i in range(PEER_TOPK):
        rr = jnp.where(rank1 == float(i), passing[i:i + 1, :], rr)
    rk2_ref[0] = rank2.astype(BF16)
    rr_ref[0] = rr
    e2_ref[0] = jnp.exp(s2_ref[0] - v2[0:1, :]).astype(BF16)
    c_ref[0] = jnp.exp(s1_ref[0] - v1[0:1, :]) * (1.0 / z)


def _gate(s1t, s2t, tn):
    nh, nk, n = s1t.shape
    blk = lambda h, i: (h, 0, i)
    return pl.pallas_call(
        _gate_kernel,
        grid=(nh, n // tn),
        in_specs=[pl.BlockSpec((1, nk, tn), blk), pl.BlockSpec((1, nk, tn), blk)],
        out_specs=[pl.BlockSpec((1, nk, tn), blk)] * 4,
        out_shape=[jax.ShapeDtypeStruct((nh, nk, n), BF16), jax.ShapeDtypeStruct((nh, nk, n), BF16),
                   jax.ShapeDtypeStruct((nh, nk, n), F32), jax.ShapeDtypeStruct((nh, nk, n), F32)],
        scratch_shapes=[pltpu.VMEM((PEER_TOPK, tn), F32), pltpu.VMEM((PEER_TOPK, tn), F32),
                        pltpu.VMEM((_PAIR_ROWS, tn), F32)],
        compiler_params=_cparams(("parallel", "parallel")), name="peer_gate",
    )(s1t, s2t)


def _bf16_rows(row, n_rows):
    tile = jnp.broadcast_to(row, (16, row.shape[1])).astype(BF16)
    return jnp.concatenate([tile] * (n_rows // 16), axis=0)


def _peer_kernel(ht_ref, u_ref, vt_ref, rk2_ref, e2_ref, rr_ref, c_ref, xn_ref, fg_ref, o_ref,
                 acc_ref, f_ref, *, nh, nk, apc, final):
    j = pl.program_id(1)

    @pl.when(j == 0)
    def _():
        acc_ref[...] = jnp.zeros(acc_ref.shape, F32)

    ht = ht_ref[...]
    zero = jnp.zeros((), BF16)
    group = min(PEER_UP_ROWS, apc)
    for aa in range(apc):
        a = j * apc + aa
        if aa % group == 0:
            pre = _dot(u_ref[aa * nk:(aa + group) * nk, :], ht)
        k = aa % group
        act = _gelu(pre[k * nk:(k + 1) * nk, :]).astype(BF16)
        wgt = jnp.zeros(act.shape, BF16)
        for h in range(nh):
            limit = _bf16_rows(rr_ref[h, pl.ds(a, 1), :], nk)
            scale = _bf16_rows(c_ref[h, pl.ds(a, 1), :], nk)
            wgt = wgt + jnp.where(rk2_ref[h] < limit, e2_ref[h] * scale, zero)
        f_ref[aa * nk:(aa + 1) * nk, :] = wgt * act
    acc_ref[...] += _dot(vt_ref[...], f_ref[...])

    @pl.when(j == pl.num_programs(1) - 1)
    def _():
        out = xn_ref[...] + acc_ref[...].T
        if final:
            out = _rmsnorm(out, fg_ref[...])
        o_ref[...] = out


def _peer(ht, u, vt, layer, rk2, e2, rr, c, xn, fg, tm, apc, final):
    d, n = ht.shape
    nh, nk, _ = rk2.shape
    assert nk % 16 == 0
    ne = vt.shape[1]
    te = apc * nk
    nj = ne // te
    tok3 = lambda i, j: (0, 0, i)
    return pl.pallas_call(
        functools.partial(_peer_kernel, nh=nh, nk=nk, apc=apc, final=final),
        grid=(n // tm, nj),
        in_specs=[pl.BlockSpec((d, tm), lambda i, j: (0, i)),
                  pl.BlockSpec((te, d), lambda i, j: (layer * nj + j, 0)),
                  pl.BlockSpec((d, te), lambda i, j: (layer, j)),
                  pl.BlockSpec((nh, nk, tm), tok3),
                  pl.BlockSpec((nh, nk, tm), tok3),
                  pl.BlockSpec((nh, nk, tm), tok3),
                  pl.BlockSpec((nh, nk, tm), tok3),
                  pl.BlockSpec((tm, d), lambda i, j: (i, 0)),
                  pl.BlockSpec((1, d), lambda i, j: (0, 0))],
        out_specs=pl.BlockSpec((tm, d), lambda i, j: (i, 0)),
        out_shape=jax.ShapeDtypeStruct((n, d), F32),
        scratch_shapes=[pltpu.VMEM((d, tm), F32), pltpu.VMEM((te, tm), BF16)],
        compiler_params=_cparams(("parallel", "arbitrary")), name="peer",
    )(ht, u, vt, rk2, e2, rr, c, xn, fg)


def _rope_tables(pos):
    half = HEAD_DIM // 2
    inv_freq = ROPE_THETA ** (-jnp.arange(half, dtype=F32) / half)
    ang = pos.astype(F32)[:, None] * inv_freq[None, :]
    cos = jnp.cos(ang)
    sin = jnp.sin(ang)
    return jnp.tile(jnp.concatenate([cos, cos], axis=1), (1, 2)), jnp.tile(jnp.concatenate([-sin, sin], axis=1), (1, 2))


def _pad_rows(x, rows):
    return jnp.pad(x, ((0, rows - x.shape[0]),) + ((0, 0),) * (x.ndim - 1))


def _token_tile(n, pref):
    return pref if n % pref == 0 else n


def _page_pool_weights(wk, wv):
    wset = jnp.stack([wk[:, 0], wk[:, 1], wv[:, 0], wv[:, 1]]).reshape(4, 2, CMP_STRIDE)
    per_pos = jnp.tile(wset, (1, 1, LANES // CMP_STRIDE))
    chunk_of_pos = (jnp.arange(LANES) // CMP_STRIDE)[None, :] == jnp.arange(LANES // CMP_STRIDE)[:, None]
    return (per_pos[:, :, None, :] * chunk_of_pos[None, None].astype(F32)).reshape(64, LANES)


def kernel(x_prompt, x_sample, cache_nsa_kv, state_nsa_win, state_mlstm_C, state_mlstm_n, state_mlstm_m, state_conv, page_table, norm1_g, norm2_g, final_norm_g, w_in, w_out, mlstm_b_i, mlstm_b_f, mlstm_norm_g, conv_w, conv_b, conv_norm_g, conv_norm_b, nsa_cmp_wk, nsa_cmp_wv, gmlp_norm_g, gmlp_norm_b, gmlp_ws, gmlp_bs, peer_wq, peer_k1, peer_k2, peer_u, peer_v):
    depth = w_in.shape[0]
    bp, s_len, d = x_prompt.shape
    bd, t_dec, _ = x_sample.shape
    n_pool = cache_nsa_kv.shape[1]
    npages = page_table.shape[1]
    past = npages * cache_nsa_kv.shape[2]
    n_win = state_nsa_win.shape[2]
    assert t_dec == 1 and cache_nsa_kv.shape[2] == LANES and past % SEL_BLOCK == 0
    assert s_len >= WINDOW + 128 and n_win == WINDOW
    np_tok = bp * s_len
    ns_pad = LANES
    assert bd <= ns_pad

    xp = x_prompt.reshape(np_tok, d)
    xs = _pad_rows(x_sample.reshape(bd, d), ns_pad)
    cos_p, sin_p = _rope_tables(jnp.arange(s_len))
    cos_s, sin_s = _rope_tables(jnp.full((ns_pad,), past))
    pool_t = cache_nsa_kv.reshape(depth * n_pool, LANES, 4 * LANES).transpose(0, 2, 1)
    win_t = state_nsa_win.reshape(depth * bd, n_win, 2 * LANES).transpose(0, 2, 1)
    eye_h = jnp.eye(N_HEADS, dtype=F32)

    tm_in = _token_tile(s_len, 512)
    tm_out = _token_tile(np_tok, 256)
    tm_peer = _token_tile(np_tok, 512)
    nk = peer_k1.shape[2]
    apc = min(16, nk)
    w_in_t = w_in.transpose(2, 0, 1)
    ub = peer_u.astype(BF16).reshape(depth * peer_u.shape[1], d)
    vtb = peer_v.astype(BF16).transpose(0, 2, 1).reshape(depth * d, peer_v.shape[1])

    new_p, new_s = [], []
    for l in range(depth):
        wi = w_in_t[:, l, :]
        w_perm = jnp.concatenate([wi[0:1024], wi[1032:2568], wi[2580:3092]], axis=0).astype(BF16)
        w_small = jnp.concatenate([wi[1024:1032], wi[2568:2580], jnp.zeros((108, d), F32)], axis=0).astype(BF16)
        g1 = norm1_g[l].reshape(1, d)
        g2 = norm2_g[l].reshape(1, d)
        gate_bias = jnp.concatenate([mlstm_b_i[l], mlstm_b_f[l], jnp.zeros((LANES - 2 * N_HEADS,), F32)]).reshape(1, LANES)
        mng = mlstm_norm_g[l].reshape(1, MIX_W)
        cw = conv_w[l]
        cbias = conv_b[l].reshape(1, MIX_W)
        cg = conv_norm_g[l].reshape(1, MIX_W)
        cbeta = conv_norm_b[l].reshape(1, MIX_W)
        w32 = jnp.concatenate([jnp.repeat(nsa_cmp_wk[l], HEAD_DIM, axis=1), jnp.repeat(nsa_cmp_wv[l], HEAD_DIM, axis=1)], axis=1)
        wlo, whi = w32[:CMP_STRIDE], w32[CMP_STRIDE:]
        wall = _page_pool_weights(nsa_cmp_wk[l], nsa_cmp_wv[l])
        gg = gmlp_norm_g[l].reshape(1, MIX_W)
        gbeta = gmlp_norm_b[l].reshape(1, MIX_W)
        gws = gmlp_ws[l]
        gbsb = jnp.repeat(gmlp_bs[l].T, HEAD_DIM, axis=1)
        wo = w_out[l].astype(BF16)
        wq = jnp.stack(_split_bf16(peer_wq[l]))
        k1 = jnp.stack(_split_bf16(peer_k1[l]))
        k2 = jnp.stack(_split_bf16(peer_k2[l]))
        fg = final_norm_g.reshape(1, d)
        final = l == depth - 1

        (oa, ob, oq, okv, _, okaug, ovsb, owinb, od, osm, okvt, owint) = _in_proj(xp, g1, w_perm, w_small, cos_p, sin_p, tm_in)
        ya, c_p, n_p, m_p = _mlstm(oa.reshape(bp, s_len, 1024), osm.reshape(bp, s_len, LANES), gate_bias, mng,
                                   jnp.zeros((bp, MIX_W, MIX_W), F32), jnp.zeros((bp, 1, MIX_W), F32),
                                   jnp.zeros((bp, 1, MIX_W), F32), MLSTM_CHUNK, MLSTM_CHUNK, bp)
        yb, conv_p = _conv(ob.reshape(bp, s_len, 2 * MIX_W), jnp.zeros((bp, CONV_PAD, MIX_W), F32),
                           cw, cbias, cg, cbeta, 1024, 1024)
        cb = _cmp(okv.reshape(bp, s_len, 4 * LANES), wlo, whi)
        yc = _nsa_prompt(oq.reshape(bp, s_len, MIX_W), osm.reshape(bp, s_len, LANES), cb,
                         okaug.reshape(bp, s_len, 2 * LANES), ovsb.reshape(bp, s_len, LANES),
                         owinb.reshape(bp, s_len, 2 * LANES), 128)
        yd, _ = _gmlp(od, gg, gbeta, gws, gbsb, 4)
        xn, ht, s1t, s2t = _out_proj(xp, ya.reshape(np_tok, MIX_W), yb.reshape(np_tok, MIX_W),
                                     yc.reshape(np_tok, MIX_W), yd, wo, g2, wq, k1, k2, tm_out)
        rk2, e2, rr, cw8 = _gate(s1t, s2t, _token_tile(np_tok, 512))
        xp = _peer(ht, ub, vtb, l, rk2, e2, rr, cw8, xn, fg, tm_peer, apc, final)
        new_p.append((okvt.reshape(bp, 4, NSA_KV, HEAD_DIM, s_len).transpose(0, 4, 1, 2, 3),
                      owint.reshape(bp, 2, NSA_KV, HEAD_DIM, s_len)[..., s_len - n_win:].transpose(0, 4, 1, 2, 3),
                      jnp.stack([c_p[:, h * HEAD_DIM:(h + 1) * HEAD_DIM, h * HEAD_DIM:(h + 1) * HEAD_DIM]
                                 for h in range(N_HEADS)], axis=1),
                      n_p.reshape(bp, N_HEADS, HEAD_DIM),
                      m_p[:, 0, ::HEAD_DIM],
                      conv_p[:, CONV_PAD - (CONV_W - 1):]))

        sa, sb, sq, skv, swin, _, _, _, sd, ssm, _, _ = _in_proj(xs, g1, w_perm, w_small, cos_s, sin_s, ns_pad)
        rows8 = lambda t: jnp.pad(t[:bd, None, :], ((0, 0), (0, 7), (0, 0)))
        c0 = jnp.einsum('bhvk,hg->bhvgk', state_mlstm_C[l], eye_h).reshape(bd, MIX_W, MIX_W)
        n0 = state_mlstm_n[l].reshape(bd, 1, MIX_W)
        m0 = jnp.repeat(state_mlstm_m[l], HEAD_DIM, axis=-1).reshape(bd, 1, MIX_W)
        bb_s = 2 if bd % 2 == 0 else 1
        ya_s, c_s, n_s, m_s = _mlstm(rows8(sa), rows8(ssm), gate_bias, mng, c0, n0, m0, 8, 1, bb_s)
        prefix = jnp.pad(state_conv[l], ((0, 0), (CONV_PAD - (CONV_W - 1), 0), (0, 0)))
        yb_s, conv_s = _conv(rows8(sb), prefix, cw, cbias, cg, cbeta, 8, 1)
        zd_s = jnp.pad(sd[:bd, None, :], ((0, 0), (0, GMLP_CHUNK - 1), (0, 0))).reshape(bd * GMLP_CHUNK, 2 * MIX_W)
        yd_s, v_s = _gmlp(zd_s, gg, gbeta, gws, gbsb, 1)
        q_heads = sq[:bd].reshape(bd, NSA_KV, 2, 1, HEAD_DIM)
        q4 = (q_heads * jnp.eye(NSA_KV, dtype=F32)[None, :, None, :, None]).reshape(bd, N_HEADS, LANES)
        q4 = jnp.pad(q4, ((0, 0), (0, 8 - N_HEADS), (0, 0)))
        gq = jnp.pad(ssm[:bd, 2 * N_HEADS:2 * N_HEADS + 3 * N_HEADS].reshape(bd, N_HEADS, 3),
                     ((0, 0), (0, 8 - N_HEADS), (0, LANES - 3)))
        assert npages % 4 == 0
        pp = next(c for c in (32, 16, 8, 4) if npages % c == 0)
        o8, win_s = _nsa_sample(page_table + l * n_pool, q4, gq, skv[:bd, None, :], swin[:bd, None, :],
                                swin[:bd, :, None], win_t, l * bd, wall, pool_t, pp)
        yc_s = jnp.concatenate([o8[:, h, (h // 2) * HEAD_DIM:(h // 2 + 1) * HEAD_DIM] for h in range(N_HEADS)], axis=1)
        xn_s, ht_s, s1t_s, s2t_s = _out_proj(xs, _pad_rows(ya_s[:, 0], ns_pad), _pad_rows(yb_s[:, 0], ns_pad),
                                             _pad_rows(yc_s, ns_pad),
                                             _pad_rows(yd_s.reshape(bd, GMLP_CHUNK, MIX_W)[:, 0], ns_pad),
                                             wo, g2, wq, k1, k2, ns_pad)
        rk2_s, e2_s, rr_s, cw8_s = _gate(s1t_s, s2t_s, ns_pad)
        xs = _peer(ht_s, ub, vtb, l, rk2_s, e2_s, rr_s, cw8_s, xn_s, fg, ns_pad, apc, final)
        new_s.append((skv[:bd].reshape(bd, 1, 4, NSA_KV, HEAD_DIM),
                      win_s.transpose(0, 2, 1).reshape(bd, n_win, 2, NSA_KV, HEAD_DIM),
                      jnp.stack([c_s[:, h * HEAD_DIM:(h + 1) * HEAD_DIM, h * HEAD_DIM:(h + 1) * HEAD_DIM]
                                 for h in range(N_HEADS)], axis=1),
                      n_s.reshape(bd, N_HEADS, HEAD_DIM),
                      m_s[:, 0, ::HEAD_DIM],
                      conv_s[:, CONV_PAD - (CONV_W - 1):],
                      v_s.reshape(bd, GMLP_CHUNK, MIX_W)[:, 0:1]))

    def stack(states, i):
        return jnp.stack([st[i] for st in states])

    y_prompt = xp.reshape(bp, s_len, d)
    y_sample = xs[:bd].reshape(bd, 1, d)
    return (y_prompt, y_sample,
            stack(new_p, 0), stack(new_s, 0),
            stack(new_p, 1), stack(new_s, 1),
            stack(new_p, 2), stack(new_s, 2),
            stack(new_p, 3), stack(new_s, 3),
            stack(new_p, 4), stack(new_s, 4),
            stack(new_p, 5), stack(new_s, 5),
            stack(new_s, 6))
```

```python
import functools

import jax
import jax.numpy as jnp
from jax import lax
from jax.experimental import pallas as pl
from jax.experimental.pallas import tpu as pltpu

F32 = jnp.float32
BF16 = jnp.bfloat16
HIGHEST = lax.Precision.HIGHEST

HEAD_DIM = 64
N_HEADS = 4
MIX_W = 256
NSA_KV = 2
NORM_EPS = 1e-6
NEG = -1e30
MLSTM_CHUNK = 128
CONV_W = 31
CONV_PAD = 32
CMP_STRIDE = 16
CMP_LEN = 32
SEL_BLOCK = 64
N_SELECT = 16
SEL_FORCE = 1e9
WINDOW = 512
ROPE_THETA = 10000.0
ATTN_SCALE = HEAD_DIM ** -0.5
GMLP_CHUNK = 128
PEER_TOPK = 16
RANK_MARK = 2.0 ** 100
PEER_UP_ROWS = 1
LANES = 128
VMEM_LIMIT = 48 * 1024 * 1024
PEER_VMEM_LIMIT = 60 * 1024 * 1024

NT = (((1,), (1,)), ((), ()))
TN = (((0,), (0,)), ((), ()))


def _cparams(sem):
    return pltpu.CompilerParams(dimension_semantics=sem, vmem_limit_bytes=VMEM_LIMIT)


def _iota(shape, dim):
    return lax.broadcasted_iota(jnp.int32, shape, dim)


def _sigmoid(x):
    return 1.0 / (1.0 + jnp.exp(-x))


def _gelu(x):
    c1 = -2.0 * 0.7978845608028654 * 1.4426950408889634
    c2 = c1 * 0.044715
    return x / (1.0 + jnp.exp2(x * (c1 + c2 * (x * x))))


def _dot(a, b, dims=None, precision=None):
    if dims is None:
        return jnp.dot(a, b, preferred_element_type=F32, precision=precision)
    return lax.dot_general(a, b, dims, preferred_element_type=F32, precision=precision)


def _block_ones(n, blk):
    return (_iota((n, n), 0) // blk == _iota((n, n), 1) // blk).astype(F32)


def _group_layernorm(x, ee, width):
    mu = _dot(x, ee, precision=HIGHEST) * (1.0 / width)
    d = x - mu
    var = _dot(d * d, ee, precision=HIGHEST) * (1.0 / width)
    return d * lax.rsqrt(var + NORM_EPS)


def _rmsnorm(x, g):
    return x * lax.rsqrt(jnp.mean(x * x, axis=-1, keepdims=True) + NORM_EPS) * g


def _masked_softmax(s, valid):
    sm = jnp.where(valid, s, NEG)
    p = jnp.where(valid, jnp.exp(sm - jnp.max(sm, axis=-1, keepdims=True)), 0.0)
    return p / jnp.maximum(jnp.sum(p, axis=-1, keepdims=True), 1e-30)


def _rope(x, cos, sin_signed):
    w = x.shape[1]
    fwd = pltpu.roll(x, w - HEAD_DIM // 2, 1)
    bwd = pltpu.roll(x, HEAD_DIM // 2, 1)
    first = (_iota(x.shape, 1) % HEAD_DIM) < HEAD_DIM // 2
    return x * cos + jnp.where(first, fwd, bwd) * sin_signed


def _in_kernel(x_ref, g_ref, w_ref, wsmall_ref, cos_ref, sin_ref,
               oa_ref, ob_ref, oq_ref, okv_ref, owin_ref, okaug_ref, ovsb_ref, owinb_ref, od_ref, os_ref,
               okvt_ref, owint_ref, *, npb):
    tm = x_ref.shape[0]
    y = _rmsnorm(x_ref[...], g_ref[...])
    yb = y.astype(BF16)
    z = _dot(yb, w_ref[...], NT)
    os_ref[...] = _dot(yb, wsmall_ref[...], NT)
    oa_ref[...] = z[:, 0:1024]
    ob_ref[...] = z[:, 1024:1536]
    cos = cos_ref[...]
    sin = sin_ref[...]
    cos2 = jnp.concatenate([cos, cos], axis=1)
    sin2 = jnp.concatenate([sin, sin], axis=1)
    oq_ref[...] = _rope(z[:, 1536:1792], cos2, sin2) * ATTN_SCALE
    kc = _rope(z[:, 1792:1920], cos, sin)
    vc = z[:, 1920:2048]
    ks = _rope(z[:, 2048:2176], cos, sin)
    vs = z[:, 2176:2304]
    kw = _rope(z[:, 2304:2432], cos, sin)
    vw = z[:, 2432:2560]
    okv = jnp.concatenate([kc, vc, ks, vs], axis=1)
    okv_ref[...] = okv
    owin = jnp.concatenate([kw, vw], axis=1)
    owin_ref[...] = owin
    okvt_ref[0] = okv.T
    owint_ref[0] = owin.T
    pos = (pl.program_id(0) % npb) * tm + _iota((tm, LANES), 0)
    onehot = jnp.where(_iota((tm, LANES), 1) == pos // SEL_BLOCK, 1.0, 0.0)
    okaug_ref[...] = jnp.concatenate([ks, onehot], axis=1).astype(BF16)
    ovsb_ref[...] = vs.astype(BF16)
    owinb_ref[...] = owin.astype(BF16)
    od_ref[...] = z[:, 2560:3072]


def _in_proj(x, g, w, wsmall, cos_t, sin_t, tm):
    n, d = x.shape
    npb = cos_t.shape[0] // tm
    widths = (1024, 512, 256, 512, 256, 256, 128, 256, 512, 128)
    dtypes = (F32, F32, F32, F32, F32, BF16, BF16, BF16, F32, F32)
    row = lambda i: (i, 0)
    return pl.pallas_call(
        functools.partial(_in_kernel, npb=npb),
        grid=(n // tm,),
        in_specs=[pl.BlockSpec((tm, d), row),
                  pl.BlockSpec((1, d), lambda i: (0, 0)),
                  pl.BlockSpec(w.shape, lambda i: (0, 0)),
                  pl.BlockSpec(wsmall.shape, lambda i: (0, 0)),
                  pl.BlockSpec((tm, LANES), lambda i: (i % npb, 0)),
                  pl.BlockSpec((tm, LANES), lambda i: (i % npb, 0))],
        out_specs=[pl.BlockSpec((tm, wd), row) for wd in widths]
                  + [pl.BlockSpec((1, wd, tm), lambda i: (i // npb, 0, i % npb)) for wd in (4 * LANES, 2 * LANES)],
        out_shape=[jax.ShapeDtypeStruct((n, wd), dt) for wd, dt in zip(widths, dtypes)]
                  + [jax.ShapeDtypeStruct((n // (npb * tm), wd, npb * tm), F32) for wd in (4 * LANES, 2 * LANES)],
        compiler_params=_cparams(("parallel",)), name="in_proj",
    )(x, g, w, wsmall, cos_t, sin_t)


def _mlstm_kernel(za_ref, zs_ref, bias_ref, ng_ref, c0_ref, n0_ref, m0_ref,
                  y_ref, c_ref, n_ref, m_ref, *, L, t_valid, bb):
    @pl.when(pl.program_id(1) == 0)
    def _():
        c_ref[...] = c0_ref[...]
        n_ref[...] = n0_ref[...]
        m_ref[...] = m0_ref[...]

    head_of_lane = _iota((1, MIX_W), 1) // HEAD_DIM
    causal = _iota((L, L), 0) >= _iota((L, L), 1)
    tril = causal.astype(F32)
    ee = _block_ones(MIX_W, HEAD_DIM)
    bdiag = ee > 0.5
    row128 = _iota((L, LANES), 0)
    lane128 = _iota((L, LANES), 1)
    for bi in range(bb):
        za = za_ref[bi]
        q = za[:, 0:256]
        k = za[:, 256:512] * (HEAD_DIM ** -0.5)
        v = za[:, 512:768]
        o = za[:, 768:1024]
        gi = zs_ref[bi] + bias_ref[...]
        ls = jnp.minimum(gi, 0.0) - jnp.log(1.0 + jnp.exp(-jnp.abs(gi)))
        if t_valid < L:
            live = row128 < t_valid
            ig = jnp.where(live, gi, NEG)
            ls = jnp.where(live, ls, 0.0)
        else:
            ig = gi
        gmat = jnp.where(lane128 < N_HEADS, ig, 0.0)
        lfm = jnp.where((lane128 >= N_HEADS) & (lane128 < 2 * N_HEADS), ls, 0.0)
        fc = _dot(tril, lfm, precision=HIGHEST)
        g_t = gmat.T
        f_t = fc.T
        mrow = m_ref[bi]
        nrow = n_ref[bi]
        cm = c_ref[bi]
        qb = q.astype(BF16)
        kb = k.astype(BF16)
        vb = v.astype(BF16)
        num = jnp.zeros((L, MIX_W), F32)
        decay_b = jnp.zeros((L, MIX_W), F32)
        dens_b = jnp.zeros((L, MIX_W), F32)
        mt_b = jnp.zeros((L, MIX_W), F32)
        w_b = jnp.zeros((L, MIX_W), F32)
        cd_b = jnp.zeros((1, MIX_W), F32)
        mnew_b = jnp.zeros((1, MIX_W), F32)
        for h in range(N_HEADS):
            hm = head_of_lane == h
            f_col = fc[:, N_HEADS + h:N_HEADS + h + 1]
            ig_col = gmat[:, h:h + 1]
            f_row = f_t[N_HEADS + h:N_HEADS + h + 1, :]
            ig_row = g_t[h:h + 1, :]
            mp = mrow[:, h * HEAD_DIM:h * HEAD_DIM + 1]
            dmat = jnp.where(causal, (f_col - f_row) + ig_row, NEG)
            m_inter = mp + f_col
            m_t = jnp.maximum(m_inter, jnp.max(dmat, axis=1, keepdims=True))
            d_exp = jnp.exp(dmat - m_t)
            decay = jnp.exp(m_inter - m_t)
            qh = jnp.where(hm, q, 0.0).astype(BF16)
            s = _dot(qh, kb, NT) * d_exp
            num = jnp.where(hm, _dot(s.astype(BF16), vb), num)
            dens_b = jnp.where(hm, jnp.sum(s, axis=1, keepdims=True), dens_b)
            decay_b = jnp.where(hm, decay, decay_b)
            mt_b = jnp.where(hm, m_t, mt_b)
            f_last = f_col[L - 1:L, :]
            m_new = m_t[L - 1:L, :]
            w_b = jnp.where(hm, jnp.exp((f_last - f_col) + ig_col - m_new), w_b)
            cd_b = jnp.where(hm, jnp.exp(mp + f_last - m_new), cd_b)
            mnew_b = jnp.where(hm, m_new, mnew_b)
        inter = _dot(qb, cm.astype(BF16), NT)
        nq_b = _dot(q * nrow, ee, precision=HIGHEST)
        hnum = num + decay_b * inter
        den_b = dens_b + decay_b * nq_b
        hh = hnum / jnp.maximum(jnp.abs(den_b), jnp.exp(-mt_b))
        hn = _group_layernorm(hh, ee, HEAD_DIM) * ng_ref[...]
        y_ref[bi] = hn * _sigmoid(o)
        kw = k * w_b
        upd = _dot(vb, kw.astype(BF16), TN)
        c_ref[bi] = jnp.where(bdiag, cd_b * cm + upd, 0.0)
        n_ref[bi] = cd_b * nrow + jnp.sum(kw, axis=0, keepdims=True)
        m_ref[bi] = mnew_b


def _mlstm(za, zs, bias, ng, c0, n0, m0, L, t_valid, bb):
    b, t, _ = za.shape
    nc = t // L
    st = lambda i, c: (i, 0, 0)
    return pl.pallas_call(
        functools.partial(_mlstm_kernel, L=L, t_valid=t_valid, bb=bb),
        grid=(b // bb, nc),
        in_specs=[pl.BlockSpec((bb, L, 1024), lambda i, c: (i, c, 0)),
                  pl.BlockSpec((bb, L, LANES), lambda i, c: (i, c, 0)),
                  pl.BlockSpec((1, LANES), lambda i, c: (0, 0)),
                  pl.BlockSpec((1, MIX_W), lambda i, c: (0, 0)),
                  pl.BlockSpec((bb, MIX_W, MIX_W), st),
                  pl.BlockSpec((bb, 1, MIX_W), st),
                  pl.BlockSpec((bb, 1, MIX_W), st)],
        out_specs=[pl.BlockSpec((bb, L, MIX_W), lambda i, c: (i, c, 0)),
                   pl.BlockSpec((bb, MIX_W, MIX_W), st),
                   pl.BlockSpec((bb, 1, MIX_W), st),
                   pl.BlockSpec((bb, 1, MIX_W), st)],
        out_shape=[jax.ShapeDtypeStruct((b, t, MIX_W), F32),
                   jax.ShapeDtypeStruct((b, MIX_W, MIX_W), F32),
                   jax.ShapeDtypeStruct((b, 1, MIX_W), F32),
                   jax.ShapeDtypeStruct((b, 1, MIX_W), F32)],
        compiler_params=_cparams(("parallel", "arbitrary")), name="mlstm",
    )(za, zs, bias, ng, c0, n0, m0)


def _conv_kernel(zb_ref, pre_ref, w_ref, cb_ref, g_ref, beta_ref, y_ref, st_ref, ext_ref, *, tt, tv):
    @pl.when(pl.program_id(1) == 0)
    def _():
        ext_ref[0:CONV_PAD, :] = pre_ref[0]

    z = zb_ref[0]
    u = z[:, :MIX_W] * _sigmoid(z[:, MIX_W:])
    ext_ref[CONV_PAD:CONV_PAD + tt, :] = u
    off = CONV_PAD - (CONV_W - 1)
    acc = jnp.zeros((tt, MIX_W), F32)
    for j in range(CONV_W):
        acc = acc + w_ref[j:j + 1, :] * ext_ref[off + j:off + j + tt, :]
    ee = _block_ones(MIX_W, HEAD_DIM)
    y = _group_layernorm(acc + cb_ref[...], ee, HEAD_DIM) * g_ref[...] + beta_ref[...]
    y_ref[0] = y * _sigmoid(y)
    st_ref[0] = ext_ref[tv:tv + CONV_PAD, :]
    ext_ref[0:CONV_PAD, :] = ext_ref[tt:tt + CONV_PAD, :]


def _conv(zb, prefix, w, cb, g, beta, tt, tv):
    b, t, _ = zb.shape
    vec = lambda i, j: (0, 0)
    return pl.pallas_call(
        functools.partial(_conv_kernel, tt=tt, tv=tv),
        grid=(b, t // tt),
        in_specs=[pl.BlockSpec((1, tt, 2 * MIX_W), lambda i, j: (i, j, 0)),
                  pl.BlockSpec((1, CONV_PAD, MIX_W), lambda i, j: (i, 0, 0)),
                  pl.BlockSpec((CONV_W, MIX_W), vec),
                  pl.BlockSpec((1, MIX_W), vec),
                  pl.BlockSpec((1, MIX_W), vec),
                  pl.BlockSpec((1, MIX_W), vec)],
        out_specs=[pl.BlockSpec((1, tt, MIX_W), lambda i, j: (i, j, 0)),
                   pl.BlockSpec((1, CONV_PAD, MIX_W), lambda i, j: (i, 0, 0))],
        out_shape=[jax.ShapeDtypeStruct((b, t, MIX_W), F32),
                   jax.ShapeDtypeStruct((b, CONV_PAD, MIX_W), F32)],
        scratch_shapes=[pltpu.VMEM((CONV_PAD + tt, MIX_W), F32)],
        compiler_params=_cparams(("parallel", "arbitrary")), name="conv",
    )(zb, prefix, w, cb, g, beta)


def _gmlp_kernel(zd_ref, g_ref, beta_ref, ws_ref, bsb_ref, y_ref, v_ref, *, cpb):
    z = zd_ref[...]
    ee = _block_ones(MIX_W, HEAD_DIM)
    u = _gelu(z[:, :MIX_W])
    vv = _group_layernorm(_gelu(z[:, MIX_W:]), ee, HEAD_DIM) * g_ref[...] + beta_ref[...]
    v_ref[...] = vv
    tril = _iota((GMLP_CHUNK, GMLP_CHUNK), 0) >= _iota((GMLP_CHUNK, GMLP_CHUNK), 1)
    head_of_lane = _iota((1, MIX_W), 1) // HEAD_DIM
    wm = [jnp.where(tril, ws_ref[h], 0.0).astype(BF16) for h in range(N_HEADS)]
    for c in range(cpb):
        rows = slice(c * GMLP_CHUNK, (c + 1) * GMLP_CHUNK)
        vc = vv[rows].astype(BF16)
        mixed = jnp.zeros((GMLP_CHUNK, MIX_W), F32)
        for h in range(N_HEADS):
            mixed = jnp.where(head_of_lane == h, _dot(wm[h], vc), mixed)
        y_ref[rows, :] = u[rows] * (mixed + bsb_ref[...])


def _gmlp(zd, g, beta, ws, bsb, cpb):
    n = zd.shape[0]
    rows = cpb * GMLP_CHUNK
    vec = lambda i: (0, 0)
    return pl.pallas_call(
        functools.partial(_gmlp_kernel, cpb=cpb),
        grid=(n // rows,),
        in_specs=[pl.BlockSpec((rows, 2 * MIX_W), lambda i: (i, 0)),
                  pl.BlockSpec((1, MIX_W), vec),
                  pl.BlockSpec((1, MIX_W), vec),
                  pl.BlockSpec((N_HEADS, GMLP_CHUNK, GMLP_CHUNK), lambda i: (0, 0, 0)),
                  pl.BlockSpec((GMLP_CHUNK, MIX_W), vec)],
        out_specs=[pl.BlockSpec((rows, MIX_W), lambda i: (i, 0)),
                   pl.BlockSpec((rows, MIX_W), lambda i: (i, 0))],
        out_shape=[jax.ShapeDtypeStruct((n, MIX_W), F32),
                   jax.ShapeDtypeStruct((n, MIX_W), F32)],
        compiler_params=_cparams(("parallel",)), name="gmlp",
    )(zd, g, beta, ws, bsb)


def _combine_lo_hi(lo, hi):
    r = lo.shape[0]
    nxt = pltpu.roll(hi, r - 1, 0)
    return lo + jnp.where(_iota(lo.shape, 0) < r - 1, nxt, 0.0)


def _cmp_kernel(kv_ref, wlo_ref, whi_ref, o_ref):
    x = kv_ref[0]
    x3 = x.reshape(x.shape[0] // CMP_STRIDE, CMP_STRIDE, x.shape[1])
    lo = jnp.sum(x3 * wlo_ref[...][None], axis=1)
    hi = jnp.sum(x3 * whi_ref[...][None], axis=1)
    o_ref[0] = _combine_lo_hi(lo, hi)


def _cmp(okv3, wlo, whi):
    b, s, _ = okv3.shape
    return pl.pallas_call(
        _cmp_kernel,
        grid=(b,),
        in_specs=[pl.BlockSpec((1, s, 2 * LANES), lambda i: (i, 0, 0)),
                  pl.BlockSpec((CMP_STRIDE, 2 * LANES), lambda i: (0, 0)),
                  pl.BlockSpec((CMP_STRIDE, 2 * LANES), lambda i: (0, 0))],
        out_specs=pl.BlockSpec((1, s // CMP_STRIDE, 2 * LANES), lambda i: (i, 0, 0)),
        out_shape=jax.ShapeDtypeStruct((b, s // CMP_STRIDE, 2 * LANES), F32),
        compiler_params=_cparams(("parallel",)), name="nsa_cmp",
    )(okv3, wlo, whi)


def _nsa_prompt_kernel(q_ref, gs_ref, cb_ref, ka_ref, vs_ref, win_ref, y_ref, s_buf, *, tq, s_len, n_top, tk):
    start = pl.program_id(1) * tq
    nc = s_len // CMP_STRIDE
    ns = s_len // SEL_BLOCK
    q = q_ref[0]
    gl = _sigmoid(gs_ref[0])
    lo_half = _iota((tq, LANES), 1) < HEAD_DIM
    q_rows = []
    for g in range(NSA_KV):
        qpair = q[:, g * LANES:(g + 1) * LANES]
        swapped = pltpu.roll(qpair, HEAD_DIM, 1)
        if g == 0:
            q_rows += [jnp.where(lo_half, qpair, 0.0), jnp.where(lo_half, swapped, 0.0)]
        else:
            q_rows += [jnp.where(lo_half, 0.0, swapped), jnp.where(lo_half, 0.0, qpair)]
    q128 = jnp.concatenate(q_rows, axis=0)
    q128b = q128.astype(BF16)
    trow = start + _iota((tq, 1), 0)
    t2 = jnp.concatenate([trow, trow], axis=0)
    t4 = jnp.concatenate([t2, t2], axis=0)
    cb = cb_ref[0]
    sc = _dot(q128b, cb[:, 0:LANES].astype(BF16), NT)
    validc = (_iota((1, nc), 1) * CMP_STRIDE + CMP_LEN) <= (t4 + 1)
    p = _masked_softmax(sc, validc)
    o_c = _dot(p.astype(BF16), cb[:, LANES:2 * LANES].astype(BF16))
    psum = jnp.concatenate([p[0:tq] + p[tq:2 * tq], p[2 * tq:3 * tq] + p[3 * tq:4 * tq]], axis=0)
    pool4 = (_iota((nc, ns), 0) // (SEL_BLOCK // CMP_STRIDE) == _iota((nc, ns), 1)).astype(F32)
    imp = _dot(psum, pool4, precision=HIGHEST)
    jidx = _iota((1, ns), 1)
    imp = jnp.where(jidx == t2 // SEL_BLOCK, SEL_FORCE, imp)
    imp = jnp.where(jidx * SEL_BLOCK <= t2, imp, -1.0)
    if n_top < ns:
        x = jnp.concatenate([imp[0:tq].T, imp[tq:2 * tq].T], axis=1)
        jrow = _iota((ns, 2 * tq), 0).astype(F32)
        for _ in range(n_top):
            m = jnp.max(x, axis=0, keepdims=True)
            first = jnp.min(jnp.where(x == m, jrow, float(ns)), axis=0, keepdims=True)
            x = jnp.where(jrow == first, -3.0, x)
        sel_t = jnp.where(x == -3.0, 1.0, 0.0)
        sel = jnp.concatenate([sel_t[:, 0:tq].T, sel_t[:, tq:2 * tq].T], axis=0)
    else:
        sel = jnp.ones((2 * tq, ns), F32)
    selneg = (sel - 1.0) * (-NEG)
    if ns < LANES:
        selneg = jnp.concatenate([selneg, jnp.zeros((2 * tq, LANES - ns), F32)], axis=1)
    selneg4 = jnp.concatenate([selneg[0:tq], selneg[0:tq], selneg[tq:2 * tq], selneg[tq:2 * tq]], axis=0)
    qaug = jnp.concatenate([q128, selneg4], axis=1).astype(BF16)

    def scores(c):
        return _dot(qaug, ka_ref[0, pl.ds(pl.multiple_of(c * tk, tk), tk), :], NT)

    def absorb(c, s, carry):
        m, l, acc = carry
        m_new = jnp.maximum(m, jnp.max(s, axis=1, keepdims=True))
        pt = jnp.exp(s - m_new)
        alpha = jnp.exp(m - m_new)
        l = alpha * l + jnp.sum(pt, axis=1, keepdims=True)
        acc = alpha * acc + _dot(pt.astype(BF16), vs_ref[0, pl.ds(pl.multiple_of(c * tk, tk), tk), :])
        return m_new, l, acc

    def tile(c, carry):
        s = s_buf[...]
        s_buf[...] = scores(c + 1)
        return absorb(c, s, carry)

    n_tiles = (start + tq + tk - 1) // tk
    carry = (jnp.full((4 * tq, 1), NEG, F32), jnp.zeros((4 * tq, 1), F32), jnp.zeros((4 * tq, LANES), F32))
    s_buf[...] = scores(0)
    carry = lax.fori_loop(0, n_tiles - 1, tile, carry)
    last = n_tiles - 1
    s_diag = jnp.where(last * tk + _iota((1, tk), 1) <= t4, s_buf[...], NEG)
    _, l, acc = absorb(last, s_diag, carry)
    o_s = acc / jnp.maximum(l, 1e-30)
    wl = WINDOW + tq
    w0 = pl.multiple_of(jnp.maximum(start - WINDOW, 0), tq)
    sw = _dot(q128b, win_ref[0, pl.ds(w0, wl), 0:LANES], NT)
    kposw = w0 + _iota((1, wl), 1)
    validw = (kposw <= t4) & (kposw > t4 - WINDOW)
    o_w = _dot(_masked_softmax(sw, validw).astype(BF16), win_ref[0, pl.ds(w0, wl), LANES:2 * LANES])
    heads = []
    for hh in range(N_HEADS):
        col = 2 * N_HEADS + hh * 3
        rows = slice(hh * tq, (hh + 1) * tq)
        heads.append(o_c[rows] * gl[:, col:col + 1] + o_s[rows] * gl[:, col + 1:col + 2]
                     + o_w[rows] * gl[:, col + 2:col + 3])
    y_ref[0] = jnp.concatenate([jnp.where(lo_half, heads[0], pltpu.roll(heads[1], HEAD_DIM, 1)),
                                jnp.where(lo_half, pltpu.roll(heads[2], HEAD_DIM, 1), heads[3])], axis=1)


def _nsa_prompt(oq3, os3, cb, okaug3, ovsb3, owinb3, tq):
    b, s, _ = oq3.shape
    ns = s // SEL_BLOCK
    assert ns <= LANES
    n_top = min(N_SELECT, ns)
    tk = min(512, s)
    full = lambda i, j: (i, 0, 0)
    return pl.pallas_call(
        functools.partial(_nsa_prompt_kernel, tq=tq, s_len=s, n_top=n_top, tk=tk),
        grid=(b, s // tq),
        in_specs=[pl.BlockSpec((1, tq, MIX_W), lambda i, j: (i, j, 0)),
                  pl.BlockSpec((1, tq, LANES), lambda i, j: (i, j, 0)),
                  pl.BlockSpec((1, s // CMP_STRIDE, 2 * LANES), full),
                  pl.BlockSpec((1, s, 2 * LANES), full),
                  pl.BlockSpec((1, s, LANES), full),
                  pl.BlockSpec((1, s, 2 * LANES), full)],
        out_specs=pl.BlockSpec((1, tq, MIX_W), lambda i, j: (i, j, 0)),
        out_shape=jax.ShapeDtypeStruct((b, s, MIX_W), F32),
        scratch_shapes=[pltpu.VMEM((N_HEADS * tq, tk), F32)],
        compiler_params=_cparams(("parallel", "arbitrary")), name="nsa_prompt",
    )(oq3, os3, cb, okaug3, ovsb3, owinb3)


def _nsa_sample_kernel(pt_ref, q4_ref, gq_ref, newkv_ref, newwin_ref, newcol_ref, win_ref, wall_ref, *rest,
                       pp, npages, n_top):
    pool_refs = rest[:pp]
    o_ref, wo_ref, lo_s, hi_s, m_s, l_s, a_s, a_stage = rest[pp:]
    i = pl.program_id(1)
    nb = 2 * npages
    nbp = m_s.shape[1]
    ncp = npages * (LANES // CMP_STRIDE)
    past = npages * LANES
    q4 = q4_ref[0]
    q4b = q4.astype(BF16)

    @pl.when(i == 0)
    def _():
        m_s[...] = jnp.zeros(m_s.shape, F32)
        l_s[...] = jnp.zeros(l_s.shape, F32)
        if nb < nbp:
            a_s[...] = jnp.zeros(a_s.shape, F32)

    lane8 = _iota((8, LANES), 1)
    lanej = _iota((8, nbp), 1)
    set_of_lane = _iota((8, 2 * LANES), 1) // HEAD_DIM
    bpp = LANES // SEL_BLOCK
    wall = wall_ref[...].astype(BF16)
    m_acc = m_s[...]
    l_acc = l_s[...]
    los, his = [], []
    for kk in range(pp):
        page = pool_refs[kk][0]
        pooled = _dot(wall, page[0:2 * LANES, :].astype(BF16), NT)
        lo8 = jnp.zeros((8, 2 * LANES), F32)
        hi8 = jnp.zeros((8, 2 * LANES), F32)
        for st in range(4):
            lo8 = jnp.where(set_of_lane == st, pooled[st * 16:st * 16 + 8], lo8)
            hi8 = jnp.where(set_of_lane == st, pooled[st * 16 + 8:st * 16 + 16], hi8)
        los.append(lo8)
        his.append(hi8)
        ks_t = page[2 * LANES:3 * LANES, :].astype(BF16)
        vs_t = page[3 * LANES:4 * LANES, :].astype(BF16)
        s = _dot(q4b, ks_t)
        probs = []
        for jb in range(bpp):
            inb = (lane8 >= jb * SEL_BLOCK) & (lane8 < (jb + 1) * SEL_BLOCK)
            sm = jnp.where(inb, s, NEG)
            m = jnp.max(sm, axis=1, keepdims=True)
            p = jnp.where(inb, jnp.exp(sm - m), 0.0)
            probs.append(p)
            j = i * (pp * bpp) + kk * bpp + jb
            m_acc = jnp.where(lanej == j, m, m_acc)
            l_acc = jnp.where(lanej == j, jnp.sum(p, axis=1, keepdims=True), l_acc)
        a = _dot(jnp.concatenate(probs, axis=0).astype(BF16), vs_t, NT)
        for jb in range(bpp):
            for hh in range(N_HEADS):
                a_stage[hh, kk * bpp + jb:kk * bpp + jb + 1, :] = a[jb * 8 + hh:jb * 8 + hh + 1, :]
    m_s[...] = m_acc
    l_s[...] = l_acc
    r0 = pl.multiple_of(i * (pp * 8), pp * 8)
    lo_s[pl.ds(r0, pp * 8), :] = jnp.concatenate(los, axis=0)
    hi_s[pl.ds(r0, pp * 8), :] = jnp.concatenate(his, axis=0)
    b0 = pl.multiple_of(i * (pp * bpp), pp * bpp)
    for hh in range(N_HEADS):
        a_s[hh, pl.ds(b0, pp * bpp), :] = a_stage[hh]

    @pl.when(i == pl.num_programs(1) - 1)
    def _():
        row8 = _iota((8, 1), 0)
        cbm = _combine_lo_hi(lo_s[...], hi_s[...])
        kcb = cbm[:, 0:LANES].astype(BF16)
        vcb = cbm[:, LANES:2 * LANES].astype(BF16)
        sc = _dot(q4b, kcb, NT)
        validc = (_iota((1, ncp), 1) * CMP_STRIDE + CMP_LEN) <= past + 1
        p = _masked_softmax(sc, validc)
        o_c = _dot(p.astype(BF16), vcb)
        pg = jnp.where(row8 == 0, p[0:1] + p[1:2], jnp.where(row8 == 1, p[2:3] + p[3:4], 0.0))
        pool4 = (_iota((ncp, nbp), 0) // (SEL_BLOCK // CMP_STRIDE) == _iota((ncp, nbp), 1)).astype(F32)
        imp2 = _dot(pg, pool4, precision=HIGHEST)
        ii = _iota((nbp, nbp), 0)
        jj = _iota((nbp, nbp), 1)
        sels = []
        for g in range(NSA_KV):
            mx = jnp.broadcast_to(imp2[g:g + 1, :], (nbp, nbp))
            mt = mx.T
            beats = ((mt > mx) | ((mt == mx) & (ii < jj))) & (ii < nb)
            rank = jnp.sum(jnp.where(beats, 1.0, 0.0), axis=0, keepdims=True)
            sels.append(jnp.where((rank < n_top - 1) & (_iota((1, nbp), 1) < nb), 1.0, 0.0))
        sel8 = jnp.where(row8 < 2, sels[0], sels[1]) > 0.5
        newkv = newkv_ref[0]
        ksn = newkv[:, 2 * LANES:3 * LANES]
        vsn = newkv[:, 3 * LANES:4 * LANES]
        s_new = jnp.sum(q4 * ksn, axis=1, keepdims=True)
        mrow = m_s[...]
        m_all = jnp.maximum(jnp.max(jnp.where(sel8, mrow, NEG), axis=1, keepdims=True), s_new)
        wj = jnp.where(sel8, jnp.exp(mrow - m_all), 0.0)
        w_new = jnp.exp(s_new - m_all)
        ltot = jnp.sum(wj * l_s[...], axis=1, keepdims=True) + w_new
        osum = w_new * vsn
        for hh in range(N_HEADS):
            osum = osum + jnp.where(row8 == hh, _dot(wj, a_s[hh], precision=HIGHEST), 0.0)
        o_s = osum / jnp.maximum(ltot, 1e-30)
        win_t = win_ref[0]
        nw = win_t.shape[1]
        lane_w = _iota((1, nw), 1)
        sw = _dot(q4b, win_t[0:LANES, :].astype(BF16))
        validw = (past - nw + lane_w) > past - WINDOW
        neww = newwin_ref[0]
        sw_new = jnp.sum(q4 * neww[:, 0:LANES], axis=1, keepdims=True)
        mw = jnp.maximum(jnp.max(jnp.where(validw, sw, NEG), axis=1, keepdims=True), sw_new)
        pw = jnp.where(validw, jnp.exp(sw - mw), 0.0)
        pn = jnp.exp(sw_new - mw)
        zw = jnp.sum(pw, axis=1, keepdims=True) + pn
        o_w = (_dot(pw.astype(BF16), win_t[LANES:2 * LANES, :].astype(BF16), NT) + pn * neww[:, LANES:2 * LANES]) / zw
        gg = _sigmoid(gq_ref[0])
        o_ref[0] = o_c * gg[:, 0:1] + o_s * gg[:, 1:2] + o_w * gg[:, 2:3]
        wo_ref[0] = jnp.where(lane_w == nw - 1, newcol_ref[0], pltpu.roll(win_t, nw - 1, 1))


def _nsa_sample(pt, q4, gq, newkv, newwin, newcol, win_t, win_off, wall, pool_t, pp):
    bd, npages = pt.shape
    nw = win_t.shape[2]
    nb = 2 * npages
    nbp = -(-nb // LANES) * LANES
    n_top = min(N_SELECT, nb + 1)
    per_b = lambda b, i, pt_ref: (b, 0, 0)
    const = lambda b, i, pt_ref: (0, 0)

    def page_map(kk):
        return lambda b, i, pt_ref: (pt_ref[b, i * pp + kk], 0, 0)

    grid_spec = pltpu.PrefetchScalarGridSpec(
        num_scalar_prefetch=1,
        grid=(bd, npages // pp),
        in_specs=[pl.BlockSpec((1, 8, LANES), per_b),
                  pl.BlockSpec((1, 8, LANES), per_b),
                  pl.BlockSpec((1, 1, 4 * LANES), per_b),
                  pl.BlockSpec((1, 1, 2 * LANES), per_b),
                  pl.BlockSpec((1, 2 * LANES, 1), per_b),
                  pl.BlockSpec((1, 2 * LANES, nw), lambda b, i, pt_ref: (win_off + b, 0, 0)),
                  pl.BlockSpec(wall.shape, const)]
                 + [pl.BlockSpec((1, 4 * LANES, LANES), page_map(kk)) for kk in range(pp)],
        out_specs=[pl.BlockSpec((1, 8, LANES), per_b),
                   pl.BlockSpec((1, 2 * LANES, nw), per_b)],
        scratch_shapes=[pltpu.VMEM((npages * 8, 2 * LANES), F32),
                        pltpu.VMEM((npages * 8, 2 * LANES), F32),
                        pltpu.VMEM((8, nbp), F32),
                        pltpu.VMEM((8, nbp), F32),
                        pltpu.VMEM((N_HEADS, nbp, LANES), F32),
                        pltpu.VMEM((N_HEADS, pp * (LANES // SEL_BLOCK), LANES), F32)])
    return pl.pallas_call(
        functools.partial(_nsa_sample_kernel, pp=pp, npages=npages, n_top=n_top),
        grid_spec=grid_spec,
        out_shape=[jax.ShapeDtypeStruct((bd, 8, LANES), F32),
                   jax.ShapeDtypeStruct((bd, 2 * LANES, nw), F32)],
        compiler_params=_cparams(("parallel", "arbitrary")), name="nsa_sample",
    )(pt, q4, gq, newkv, newwin, newcol, win_t, wall, *([pool_t] * pp))


def _split_bf16(x):
    hi = x.astype(BF16)
    return hi, (x - hi.astype(F32)).astype(BF16)


def _out_kernel(x_ref, ya_ref, yb_ref, yc_ref, yd_ref, wo_ref, g2_ref, wq_ref, k1_ref, k2_ref,
                xn_ref, ht_ref, s1_ref, s2_ref, *, nh, dq):
    acc = x_ref[...]
    for idx, y_ref in enumerate((ya_ref, yb_ref, yc_ref, yd_ref)):
        acc = acc + _dot(y_ref[...].astype(BF16), wo_ref[idx * MIX_W:(idx + 1) * MIX_W, :])
    xn_ref[...] = acc
    h2 = _rmsnorm(acc, g2_ref[...])
    ht_ref[...] = h2.T.astype(BF16)
    h_hi, h_lo = _split_bf16(h2)
    q = _dot(h_hi, wq_ref[0]) + _dot(h_lo, wq_ref[0]) + _dot(h_hi, wq_ref[1])
    q_hi, q_lo = _split_bf16(q)
    half = dq // 2
    for h in range(nh):
        for s_ref, k_ref, cols in ((s1_ref, k1_ref, slice(h * dq, h * dq + half)),
                                   (s2_ref, k2_ref, slice(h * dq + half, (h + 1) * dq))):
            s_ref[h] = (_dot(k_ref[0, h], q_hi[:, cols], NT) + _dot(k_ref[1, h], q_hi[:, cols], NT)
                        + _dot(k_ref[0, h], q_lo[:, cols], NT))


def _out_proj(x, ya, yb, yc, yd, wo, g2, wq, k1, k2, tm):
    n, d = x.shape
    _, nh, nk, half = k1.shape
    row = lambda i: (i, 0)
    c2 = lambda i: (0, 0)
    return pl.pallas_call(
        functools.partial(_out_kernel, nh=nh, dq=2 * half),
        grid=(n // tm,),
        in_specs=[pl.BlockSpec((tm, d), row)] + [pl.BlockSpec((tm, MIX_W), row)] * 4
                 + [pl.BlockSpec(wo.shape, c2), pl.BlockSpec((1, d), c2), pl.BlockSpec(wq.shape, lambda i: (0, 0, 0)),
                    pl.BlockSpec(k1.shape, lambda i: (0, 0, 0, 0)), pl.BlockSpec(k2.shape, lambda i: (0, 0, 0, 0))],
        out_specs=[pl.BlockSpec((tm, d), row),
                   pl.BlockSpec((d, tm), lambda i: (0, i)),
                   pl.BlockSpec((nh, nk, tm), lambda i: (0, 0, i)),
                   pl.BlockSpec((nh, nk, tm), lambda i: (0, 0, i))],
        out_shape=[jax.ShapeDtypeStruct((n, d), F32),
                   jax.ShapeDtypeStruct((d, n), BF16),
                   jax.ShapeDtypeStruct((nh, nk, n), F32),
                   jax.ShapeDtypeStruct((nh, nk, n), F32)],
        compiler_params=_cparams(("parallel",)), name="out_proj",
    )(x, ya, yb, yc, yd, wo, g2, wq, k1, k2)


_PAIR_COUNTS = tuple(PEER_TOPK // (i + 1) for i in range(PEER_TOPK))
_PAIR_ROWS = -(-sum(_PAIR_COUNTS) // 8) * 8


def _gate_kernel(s1_ref, s2_ref, rk2_ref, e2_ref, rr_ref, c_ref, v1_s, v2_s, c_s):
    x1 = s1_ref[0]
    x2 = s2_ref[0]
    for i in range(PEER_TOPK):
        m1 = jnp.max(x1, axis=0, keepdims=True)
        m2 = jnp.max(x2, axis=0, keepdims=True)
        v1_s[i:i + 1, :] = m1
        v2_s[i:i + 1, :] = m2
        marker = -RANK_MARK * (1.0 + i / PEER_TOPK)
        x1 = jnp.where(x1 == m1, marker, x1)
        x2 = jnp.where(x2 == m2, marker, x2)
    rank1 = jnp.where(x1 <= -RANK_MARK, (x1 * (-1.0 / RANK_MARK) - 1.0) * PEER_TOPK, float(PEER_TOPK))
    rank2 = jnp.where(x2 <= -RANK_MARK, (x2 * (-1.0 / RANK_MARK) - 1.0) * PEER_TOPK, float(PEER_TOPK))
    v1 = v1_s[...]
    v2 = v2_s[...]
    off = 0
    for i, cnt in enumerate(_PAIR_COUNTS):
        c_s[off:off + cnt, :] = v1[i:i + 1, :] + v2[0:cnt, :]
        off += cnt
    if off < _PAIR_ROWS:
        c_s[off:_PAIR_ROWS, :] = jnp.full((_PAIR_ROWS - off, v1.shape[1]), NEG, F32)
    c = c_s[...]
    mx = v1[0:1, :] + v2[0:1, :]
    z = jnp.zeros_like(mx)
    m = mx
    for i in range(PEER_TOPK):
        m = jnp.max(c, axis=0, keepdims=True)
        z = z + jnp.exp(m - mx)
        c = jnp.where(c == m, NEG, c)
    tau = m
    passing = jnp.zeros(v1.shape, F32)
    for jj in range(PEER_TOPK):
        passing = passing + jnp.where(v1 + v2[jj:jj + 1, :] >= tau, 1.0, 0.0)
    rr = jnp.zeros(rank1.shape, F32)
    for i in range(PEER_TOPK):
        rr = jnp.where(rank1 == float(i), passing[i:i + 1, :], rr)
    rk2_ref[0] = rank2.astype(BF16)
    rr_ref[0] = rr
    e2_ref[0] = jnp.exp(s2_ref[0] - v2[0:1, :]).astype(BF16)
    c_ref[0] = jnp.exp(s1_ref[0] - v1[0:1, :]) * (1.0 / z)


def _gate(s1t, s2t, tn):
    nh, nk, n = s1t.shape
    blk = lambda h, i: (h, 0, i)
    return pl.pallas_call(
        _gate_kernel,
        grid=(nh, n // tn),
        in_specs=[pl.BlockSpec((1, nk, tn), blk), pl.BlockSpec((1, nk, tn), blk)],
        out_specs=[pl.BlockSpec((1, nk, tn), blk)] * 4,
        out_shape=[jax.ShapeDtypeStruct((nh, nk, n), BF16), jax.ShapeDtypeStruct((nh, nk, n), BF16),
                   jax.ShapeDtypeStruct((nh, nk, n), F32), jax.ShapeDtypeStruct((nh, nk, n), F32)],
        scratch_shapes=[pltpu.VMEM((PEER_TOPK, tn), F32), pltpu.VMEM((PEER_TOPK, tn), F32),
                        pltpu.VMEM((_PAIR_ROWS, tn), F32)],
        compiler_params=_cparams(("parallel", "parallel")), name="peer_gate",
    )(s1t, s2t)


def _bf16_rows(row, n_rows):
    tile = jnp.broadcast_to(row, (16, row.shape[1])).astype(BF16)
    return jnp.concatenate([tile] * (n_rows // 16), axis=0)


def _peer_kernel(ht_ref, u_ref, vt_ref, rk2_ref, e2_ref, rr_ref, c_ref, xn_ref, fg_ref, o_ref,
                 acc_ref, f_ref, *, nh, nk, apc, final):
    j = pl.program_id(1)

    @pl.when(j == 0)
    def _():
        acc_ref[...] = jnp.zeros(acc_ref.shape, F32)

    ht = ht_ref[...]
    zero = jnp.zeros((), BF16)
    group = min(PEER_UP_ROWS, apc)
    for aa in range(apc):
        a = j * apc + aa
        if aa % group == 0:
            pre = _dot(u_ref[aa * nk:(aa + group) * nk, :], ht)
        k = aa % group
        act = _gelu(pre[k * nk:(k + 1) * nk, :]).astype(BF16)
        wgt = jnp.zeros(act.shape, BF16)
        for h in range(nh):
            limit = _bf16_rows(rr_ref[h, pl.ds(a, 1), :], nk)
            scale = _bf16_rows(c_ref[h, pl.ds(a, 1), :], nk)
            wgt = wgt + jnp.where(rk2_ref[h] < limit, e2_ref[h] * scale, zero)
        f_ref[aa * nk:(aa + 1) * nk, :] = wgt * act
    acc_ref[...] += _dot(vt_ref[...], f_ref[...])

    @pl.when(j == pl.num_programs(1) - 1)
    def _():
        out = xn_ref[...] + acc_ref[...].T
        if final:
            out = _rmsnorm(out, fg_ref[...])
        o_ref[...] = out


def _peer(ht, u, vt, layer, rk2, e2, rr, c, xn, fg, tm, apc, final):
    d, n = ht.shape
    nh, nk, _ = rk2.shape
    assert nk % 16 == 0
    ne = vt.shape[1]
    te = apc * nk
    nj = ne // te
    tok3 = lambda i, j: (0, 0, i)
    once = pl.Buffered(1)
    return pl.pallas_call(
        functools.partial(_peer_kernel, nh=nh, nk=nk, apc=apc, final=final),
        grid=(n // tm, nj),
        in_specs=[pl.BlockSpec((d, tm), lambda i, j: (0, i), pipeline_mode=once),
                  pl.BlockSpec((te, d), lambda i, j: (layer * nj + j, 0)),
                  pl.BlockSpec((d, te), lambda i, j: (layer, j)),
                  pl.BlockSpec((nh, nk, tm), tok3, pipeline_mode=once),
                  pl.BlockSpec((nh, nk, tm), tok3, pipeline_mode=once),
                  pl.BlockSpec((nh, nk, tm), tok3, pipeline_mode=once),
                  pl.BlockSpec((nh, nk, tm), tok3, pipeline_mode=once),
                  pl.BlockSpec((tm, d), lambda i, j: (i, 0), pipeline_mode=once),
                  pl.BlockSpec((1, d), lambda i, j: (0, 0))],
        out_specs=pl.BlockSpec((tm, d), lambda i, j: (i, 0)),
        out_shape=jax.ShapeDtypeStruct((n, d), F32),
        scratch_shapes=[pltpu.VMEM((d, tm), F32), pltpu.VMEM((te, tm), BF16)],
        compiler_params=pltpu.CompilerParams(dimension_semantics=("parallel", "arbitrary"),
                                             vmem_limit_bytes=PEER_VMEM_LIMIT), name="peer",
    )(ht, u, vt, rk2, e2, rr, c, xn, fg)


def _rope_tables(pos):
    half = HEAD_DIM // 2
    inv_freq = ROPE_THETA ** (-jnp.arange(half, dtype=F32) / half)
    ang = pos.astype(F32)[:, None] * inv_freq[None, :]
    cos = jnp.cos(ang)
    sin = jnp.sin(ang)
    return jnp.tile(jnp.concatenate([cos, cos], axis=1), (1, 2)), jnp.tile(jnp.concatenate([-sin, sin], axis=1), (1, 2))


def _pad_rows(x, rows):
    return jnp.pad(x, ((0, rows - x.shape[0]),) + ((0, 0),) * (x.ndim - 1))


def _token_tile(n, pref):
    return pref if n % pref == 0 else n


def _page_pool_weights(wk, wv):
    wset = jnp.stack([wk[:, 0], wk[:, 1], wv[:, 0], wv[:, 1]]).reshape(4, 2, CMP_STRIDE)
    per_pos = jnp.tile(wset, (1, 1, LANES // CMP_STRIDE))
    chunk_of_pos = (jnp.arange(LANES) // CMP_STRIDE)[None, :] == jnp.arange(LANES // CMP_STRIDE)[:, None]
    return (per_pos[:, :, None, :] * chunk_of_pos[None, None].astype(F32)).reshape(64, LANES)


def kernel(x_prompt, x_sample, cache_nsa_kv, state_nsa_win, state_mlstm_C, state_mlstm_n, state_mlstm_m, state_conv, page_table, norm1_g, norm2_g, final_norm_g, w_in, w_out, mlstm_b_i, mlstm_b_f, mlstm_norm_g, conv_w, conv_b, conv_norm_g, conv_norm_b, nsa_cmp_wk, nsa_cmp_wv, gmlp_norm_g, gmlp_norm_b, gmlp_ws, gmlp_bs, peer_wq, peer_k1, peer_k2, peer_u, peer_v):
    depth = w_in.shape[0]
    bp, s_len, d = x_prompt.shape
    bd, t_dec, _ = x_sample.shape
    n_pool = cache_nsa_kv.shape[1]
    npages = page_table.shape[1]
    past = npages * cache_nsa_kv.shape[2]
    n_win = state_nsa_win.shape[2]
    assert t_dec == 1 and cache_nsa_kv.shape[2] == LANES and past % SEL_BLOCK == 0
    assert s_len >= WINDOW + 128 and n_win == WINDOW
    np_tok = bp * s_len
    ns_pad = LANES
    assert bd <= ns_pad

    xp = x_prompt.reshape(np_tok, d)
    xs = _pad_rows(x_sample.reshape(bd, d), ns_pad)
    cos_p, sin_p = _rope_tables(jnp.arange(s_len))
    cos_s, sin_s = _rope_tables(jnp.full((ns_pad,), past))
    pool_t = cache_nsa_kv.reshape(depth * n_pool, LANES, 4 * LANES).transpose(0, 2, 1)
    win_t = state_nsa_win.reshape(depth * bd, n_win, 2 * LANES).transpose(0, 2, 1)
    eye_h = jnp.eye(N_HEADS, dtype=F32)

    tm_in = _token_tile(s_len, 512)
    tm_out = _token_tile(np_tok, 256)
    tm_peer = _token_tile(np_tok, 512)
    nk = peer_k1.shape[2]
    apc = min(32, nk)
    w_in_t = w_in.transpose(2, 0, 1)
    ub = peer_u.astype(BF16).reshape(depth * peer_u.shape[1], d)
    vtb = peer_v.astype(BF16).transpose(0, 2, 1).reshape(depth * d, peer_v.shape[1])

    new_p, new_s = [], []
    for l in range(depth):
        wi = w_in_t[:, l, :]
        w_perm = jnp.concatenate([wi[0:1024], wi[1032:2568], wi[2580:3092]], axis=0).astype(BF16)
        w_small = jnp.concatenate([wi[1024:1032], wi[2568:2580], jnp.zeros((108, d), F32)], axis=0).astype(BF16)
        g1 = norm1_g[l].reshape(1, d)
        g2 = norm2_g[l].reshape(1, d)
        gate_bias = jnp.concatenate([mlstm_b_i[l], mlstm_b_f[l], jnp.zeros((LANES - 2 * N_HEADS,), F32)]).reshape(1, LANES)
        mng = mlstm_norm_g[l].reshape(1, MIX_W)
        cw = conv_w[l]
        cbias = conv_b[l].reshape(1, MIX_W)
        cg = conv_norm_g[l].reshape(1, MIX_W)
        cbeta = conv_norm_b[l].reshape(1, MIX_W)
        w32 = jnp.concatenate([jnp.repeat(nsa_cmp_wk[l], HEAD_DIM, axis=1), jnp.repeat(nsa_cmp_wv[l], HEAD_DIM, axis=1)], axis=1)
        wlo, whi = w32[:CMP_STRIDE], w32[CMP_STRIDE:]
        wall = _page_pool_weights(nsa_cmp_wk[l], nsa_cmp_wv[l])
        gg = gmlp_norm_g[l].reshape(1, MIX_W)
        gbeta = gmlp_norm_b[l].reshape(1, MIX_W)
        gws = gmlp_ws[l]
        gbsb = jnp.repeat(gmlp_bs[l].T, HEAD_DIM, axis=1)
        wo = w_out[l].astype(BF16)
        wq = jnp.stack(_split_bf16(peer_wq[l]))
        k1 = jnp.stack(_split_bf16(peer_k1[l]))
        k2 = jnp.stack(_split_bf16(peer_k2[l]))
        fg = final_norm_g.reshape(1, d)
        final = l == depth - 1

        (oa, ob, oq, okv, _, okaug, ovsb, owinb, od, osm, okvt, owint) = _in_proj(xp, g1, w_perm, w_small, cos_p, sin_p, tm_in)
        ya, c_p, n_p, m_p = _mlstm(oa.reshape(bp, s_len, 1024), osm.reshape(bp, s_len, LANES), gate_bias, mng,
                                   jnp.zeros((bp, MIX_W, MIX_W), F32), jnp.zeros((bp, 1, MIX_W), F32),
                                   jnp.zeros((bp, 1, MIX_W), F32), MLSTM_CHUNK, MLSTM_CHUNK, bp)
        yb, conv_p = _conv(ob.reshape(bp, s_len, 2 * MIX_W), jnp.zeros((bp, CONV_PAD, MIX_W), F32),
                           cw, cbias, cg, cbeta, 1024, 1024)
        cb = _cmp(okv.reshape(bp, s_len, 4 * LANES), wlo, whi)
        yc = _nsa_prompt(oq.reshape(bp, s_len, MIX_W), osm.reshape(bp, s_len, LANES), cb,
                         okaug.reshape(bp, s_len, 2 * LANES), ovsb.reshape(bp, s_len, LANES),
                         owinb.reshape(bp, s_len, 2 * LANES), 128)
        yd, _ = _gmlp(od, gg, gbeta, gws, gbsb, 4)
        xn, ht, s1t, s2t = _out_proj(xp, ya.reshape(np_tok, MIX_W), yb.reshape(np_tok, MIX_W),
                                     yc.reshape(np_tok, MIX_W), yd, wo, g2, wq, k1, k2, tm_out)
        rk2, e2, rr, cw8 = _gate(s1t, s2t, _token_tile(np_tok, 512))
        xp = _peer(ht, ub, vtb, l, rk2, e2, rr, cw8, xn, fg, tm_peer, apc, final)
        new_p.append((okvt.reshape(bp, 4, NSA_KV, HEAD_DIM, s_len).transpose(0, 4, 1, 2, 3),
                      owint.reshape(bp, 2, NSA_KV, HEAD_DIM, s_len)[..., s_len - n_win:].transpose(0, 4, 1, 2, 3),
                      jnp.stack([c_p[:, h * HEAD_DIM:(h + 1) * HEAD_DIM, h * HEAD_DIM:(h + 1) * HEAD_DIM]
                                 for h in range(N_HEADS)], axis=1),
                      n_p.reshape(bp, N_HEADS, HEAD_DIM),
                      m_p[:, 0, ::HEAD_DIM],
                      conv_p[:, CONV_PAD - (CONV_W - 1):]))

        sa, sb, sq, skv, swin, _, _, _, sd, ssm, _, _ = _in_proj(xs, g1, w_perm, w_small, cos_s, sin_s, ns_pad)
        rows8 = lambda t: jnp.pad(t[:bd, None, :], ((0, 0), (0, 7), (0, 0)))
        c0 = jnp.einsum('bhvk,hg->bhvgk', state_mlstm_C[l], eye_h).reshape(bd, MIX_W, MIX_W)
        n0 = state_mlstm_n[l].reshape(bd, 1, MIX_W)
        m0 = jnp.repeat(state_mlstm_m[l], HEAD_DIM, axis=-1).reshape(bd, 1, MIX_W)
        bb_s = 2 if bd % 2 == 0 else 1
        ya_s, c_s, n_s, m_s = _mlstm(rows8(sa), rows8(ssm), gate_bias, mng, c0, n0, m0, 8, 1, bb_s)
        prefix = jnp.pad(state_conv[l], ((0, 0), (CONV_PAD - (CONV_W - 1), 0), (0, 0)))
        yb_s, conv_s = _conv(rows8(sb), prefix, cw, cbias, cg, cbeta, 8, 1)
        zd_s = jnp.pad(sd[:bd, None, :], ((0, 0), (0, GMLP_CHUNK - 1), (0, 0))).reshape(bd * GMLP_CHUNK, 2 * MIX_W)
        yd_s, v_s = _gmlp(zd_s, gg, gbeta, gws, gbsb, 1)
        q_heads = sq[:bd].reshape(bd, NSA_KV, 2, 1, HEAD_DIM)
        q4 = (q_heads * jnp.eye(NSA_KV, dtype=F32)[None, :, None, :, None]).reshape(bd, N_HEADS, LANES)
        q4 = jnp.pad(q4, ((0, 0), (0, 8 - N_HEADS), (0, 0)))
        gq = jnp.pad(ssm[:bd, 2 * N_HEADS:2 * N_HEADS + 3 * N_HEADS].reshape(bd, N_HEADS, 3),
                     ((0, 0), (0, 8 - N_HEADS), (0, LANES - 3)))
        assert npages % 4 == 0
        pp = next(c for c in (32, 16, 8, 4) if npages % c == 0)
        o8, win_s = _nsa_sample(page_table + l * n_pool, q4, gq, skv[:bd, None, :], swin[:bd, None, :],
                                swin[:bd, :, None], win_t, l * bd, wall, pool_t, pp)
        yc_s = jnp.concatenate([o8[:, h, (h // 2) * HEAD_DIM:(h // 2 + 1) * HEAD_DIM] for h in range(N_HEADS)], axis=1)
        xn_s, ht_s, s1t_s, s2t_s = _out_proj(xs, _pad_rows(ya_s[:, 0], ns_pad), _pad_rows(yb_s[:, 0], ns_pad),
                                             _pad_rows(yc_s, ns_pad),
                                             _pad_rows(yd_s.reshape(bd, GMLP_CHUNK, MIX_W)[:, 0], ns_pad),
                                             wo, g2, wq, k1, k2, ns_pad)
        rk2_s, e2_s, rr_s, cw8_s = _gate(s1t_s, s2t_s, ns_pad)
        xs = _peer(ht_s, ub, vtb, l, rk2_s, e2_s, rr_s, cw8_s, xn_s, fg, ns_pad, apc, final)
        new_s.append((skv[:bd].reshape(bd, 1, 4, NSA_KV, HEAD_DIM),
                      win_s.transpose(0, 2, 1).reshape(bd, n_win, 2, NSA_KV, HEAD_DIM),
                      jnp.stack([c_s[:, h * HEAD_DIM:(h + 1) * HEAD_DIM, h * HEAD_DIM:(h + 1) * HEAD_DIM]
                                 for h in range(N_HEADS)], axis=1),
                      n_s.reshape(bd, N_HEADS, HEAD_DIM),
                      m_s[:, 0, ::HEAD_DIM],
                      conv_s[:, CONV_PAD - (CONV_W - 1):],
                      v_s.reshape(bd, GMLP_CHUNK, MIX_W)[:, 0:1]))

    def stack(states, i):
        return jnp.stack([st[i] for st in states])

    y_prompt = xp.reshape(bp, s_len, d)
    y_sample = xs[:bd].reshape(bd, 1, d)
    return (y_prompt, y_sample,
            stack(new_p, 0), stack(new_s, 0),
            stack(new_p, 1), stack(new_s, 1),
            stack(new_p, 2), stack(new_s, 2),
            stack(new_p, 3), stack(new_s, 3),
            stack(new_p, 4), stack(new_s, 4),
            stack(new_p, 5), stack(new_s, 5),
            stack(new_s, 6))
```
